```python
import math
import jax, jax.numpy as jnp
from jax import lax
import numpy as np

D_MODEL = 1024
BATCH = 1
SEQ = 16384
DEPTH = 1
DEC_BATCH = 32
DEC_SEQ = 16
PAST_LEN = 1024

CHUNK = 64
N_META = 16
N_HEADS = 8
HEAD_DIM = 64
ATT_W = N_HEADS * HEAD_DIM
N_IDX = 8
IDX_DIM = 64
TOPK_MAX = 256
Q_BLOCK = 128
SSM_GROUPS = 32
SSM_GC = 16
SSM_W = SSM_GROUPS * SSM_GC
SSM_P = 64
MIX_W = ATT_W + SSM_W
D_FF = 2816
NUM_BUCKETS = 32
MAX_DISTANCE = 128
EPS = 1e-6
SIZES = (ATT_W, ATT_W, ATT_W, N_IDX * IDX_DIM, IDX_DIM, N_IDX, SSM_W)
D_IN = 3 * ATT_W + N_IDX * IDX_DIM + IDX_DIM + N_IDX + SSM_W

kernel_name = 'hymba_dsa_s5_macaron_stream_step'


def rmsnorm(x, g):
    xf = x.astype(jnp.float32)
    y = xf * lax.rsqrt(jnp.mean(xf * xf, axis=-1, keepdims=True) + EPS)
    return (y * g.astype(jnp.float32)).astype(x.dtype)


def swiglu(x, w_gate, w_up, w_down):
    return (jax.nn.silu(x @ w_gate) * (x @ w_up)) @ w_down


def rel_bucket(rel):
    half = NUM_BUCKETS // 2
    max_exact = half // 2
    ret = jnp.where(rel > 0, half, 0)
    n = jnp.abs(rel)
    nf = jnp.maximum(n, 1).astype(jnp.float32)
    large = max_exact + (jnp.log(nf / max_exact) / math.log(MAX_DISTANCE / max_exact)
                         * (half - max_exact)).astype(jnp.int32)
    large = jnp.minimum(large, half - 1)
    return ret + jnp.where(n < max_exact, n, large)


def sparse_attention(q, k, v, qi, ki, wi, q_pos, q_chunk, k_pos, k_chunk, topk, rel_bias):
    bsz, tq, n_h, d_h = q.shape
    qb = min(Q_BLOCK, tq)
    nb = -(-tq // qb)
    pad = nb * qb - tq

    def to_blocks(a):
        a = jnp.pad(a, [(0, 0), (0, pad)] + [(0, 0)] * (a.ndim - 2))
        return jnp.moveaxis(a.reshape((bsz, nb, qb) + a.shape[2:]), 1, 0)

    qp_b = jnp.pad(q_pos, (0, pad), mode='edge').reshape(nb, qb)
    qc_b = jnp.pad(q_chunk, (0, pad), mode='edge').reshape(nb, qb)
    kf = ki.astype(jnp.float32)
    idx_scale = (IDX_DIM ** -0.5) * (N_IDX ** -0.5)
    att_scale = d_h ** -0.5
    gather = jax.vmap(lambda a, i: a[i])

    def one_block(args):
        q_b, qi_b, wi_b, qp, qc = args
        rel = jax.nn.relu(jnp.einsum('bqhd,bkd->bqhk', qi_b.astype(jnp.float32), kf))
        score = jnp.einsum('bqhk,bqh->bqk', rel, wi_b.astype(jnp.float32)) * idx_scale
        visible = k_chunk[None, :] <= qc[:, None]
        score = jnp.where(visible[None], score, -jnp.inf)
        top_val, top_idx = lax.top_k(score, topk)
        k_sel = gather(k, top_idx)
        v_sel = gather(v, top_idx)
        bias = rel_bias[rel_bucket(k_pos[top_idx] - qp[None, :, None])]
        logits = (jnp.einsum('bqhd,bqkhd->bqhk', q_b, k_sel).astype(jnp.float32) * att_scale
                  + jnp.swapaxes(bias, -1, -2).astype(jnp.float32))
        logits = jnp.where(jnp.isfinite(top_val)[:, :, None, :], logits, -jnp.inf)
        p = jax.nn.softmax(logits, axis=-1).astype(v.dtype)
        return jnp.einsum('bqhk,bqkhd->bqhd', p, v_sel)

    out = lax.map(one_block, (to_blocks(q), to_blocks(qi), to_blocks(wi), qp_b, qc_b))
    out = jnp.moveaxis(out, 0, 1).reshape(bsz, nb * qb, n_h, d_h)
    return out[:, :tq]


def s5_mixer(us, s_re0, s_im0, lam_re, lam_im, log_step, b_re, b_im, c_re, c_im, d_skip, w_glu, b_glu):
    bsz, t, _ = us.shape
    f32 = jnp.float32
    uf = us.astype(f32).reshape(bsz, t, SSM_GROUPS, SSM_GC)
    lr, li = lam_re.astype(f32), lam_im.astype(f32)
    dt = jnp.exp(log_step.astype(f32))[:, None]
    mag = jnp.exp(lr * dt)
    ab_re, ab_im = mag * jnp.cos(li * dt), mag * jnp.sin(li * dt)
    nr, ni = ab_re - 1.0, ab_im
    den = lr * lr + li * li
    f_re, f_im = (nr * lr + ni * li) / den, (ni * lr - nr * li) / den
    br, bi = b_re.astype(f32), b_im.astype(f32)
    bb_re = f_re[..., None] * br - f_im[..., None] * bi
    bb_im = f_re[..., None] * bi + f_im[..., None] * br
    bu_re = jnp.einsum('btgc,gpc->btgp', uf, bb_re)
    bu_im = jnp.einsum('btgc,gpc->btgp', uf, bb_im)
    a_re = jnp.broadcast_to(ab_re, bu_re.shape)
    a_im = jnp.broadcast_to(ab_im, bu_im.shape)

    def combine(e1, e2):
        a1r, a1i, b1r, b1i = e1
        a2r, a2i, b2r, b2i = e2
        return (a2r * a1r - a2i * a1i, a2r * a1i + a2i * a1r,
                a2r * b1r - a2i * b1i + b2r, a2r * b1i + a2i * b1r + b2i)

    A_r, A_i, X_r, X_i = lax.associative_scan(combine, (a_re, a_im, bu_re, bu_im), axis=1)
    s0r, s0i = s_re0.astype(f32)[:, None], s_im0.astype(f32)[:, None]
    x_re = X_r + A_r * s0r - A_i * s0i
    x_im = X_i + A_r * s0i + A_i * s0r
    y = (jnp.einsum('btgp,gcp->btgc', x_re, c_re.astype(f32))
         - jnp.einsum('btgp,gcp->btgc', x_im, c_im.astype(f32))
         + d_skip.astype(f32) * uf)
    g = jax.nn.gelu(y.reshape(bsz, t, SSM_W))
    out = g * jax.nn.sigmoid(g @ w_glu.astype(f32) + b_glu.astype(f32))
    return out.astype(us.dtype), x_re[:, -1], x_im[:, -1]


def mixer(u, pos, chunk, k_past, v_past, ki_past, pos_past, chunk_past, s_re0, s_im0, topk, rel_bias,
          w_in, w_out, lam_re, lam_im, log_step, b_re, b_im, c_re, c_im, d_skip, w_glu, b_glu):
    bsz, t, _ = u.shape
    z = u @ w_in
    parts = []
    off = 0
    for n in SIZES:
        parts.append(z[..., off:off + n])
        off += n
    q, k, v, qi, ki, wi, us = parts
    q = q.reshape(bsz, t, N_HEADS, HEAD_DIM)
    k = k.reshape(bsz, t, N_HEADS, HEAD_DIM)
    v = v.reshape(bsz, t, N_HEADS, HEAD_DIM)
    qi = qi.reshape(bsz, t, N_IDX, IDX_DIM)
    k_all = jnp.concatenate([k_past.astype(k.dtype), k], axis=1)
    v_all = jnp.concatenate([v_past.astype(v.dtype), v], axis=1)
    ki_all = jnp.concatenate([ki_past.astype(ki.dtype), ki], axis=1)
    pos_all = jnp.concatenate([pos_past, pos])
    chunk_all = jnp.concatenate([chunk_past, chunk])
    att = sparse_attention(q, k_all, v_all, qi, ki_all, wi, pos, chunk, pos_all, chunk_all, topk, rel_bias)
    ssm, s_re, s_im = s5_mixer(us, s_re0, s_im0, lam_re, lam_im, log_step, b_re, b_im, c_re, c_im,
                               d_skip, w_glu, b_glu)
    y = jnp.concatenate([att.reshape(bsz, t, ATT_W), ssm], axis=-1) @ w_out
    return y, k, v, ki, s_re, s_im


def layer(x, pos, chunk, k_past, v_past, ki_past, pos_past, chunk_past, s_re0, s_im0, topk, rel_bias,
          ffn1_g_pre, ffn1_w_gate, ffn1_w_up, ffn1_w_down, ffn1_g_post,
          mix_g_pre, w_in, w_out, mix_g_post,
          lam_re, lam_im, log_step, b_re, b_im, c_re, c_im, d_skip, w_glu, b_glu,
          ffn2_g_pre, ffn2_w_gate, ffn2_w_up, ffn2_w_down, ffn2_g_post):
    h = x + 0.5 * rmsnorm(swiglu(rmsnorm(x, ffn1_g_pre), ffn1_w_gate, ffn1_w_up, ffn1_w_down), ffn1_g_post)
    m, k, v, ki, s_re, s_im = mixer(rmsnorm(h, mix_g_pre), pos, chunk, k_past, v_past, ki_past,
                                    pos_past, chunk_past, s_re0, s_im0, topk, rel_bias, w_in, w_out,
                                    lam_re, lam_im, log_step, b_re, b_im, c_re, c_im, d_skip, w_glu, b_glu)
    h = h + rmsnorm(m, mix_g_post)
    y = h + 0.5 * rmsnorm(swiglu(rmsnorm(h, ffn2_g_pre), ffn2_w_gate, ffn2_w_up, ffn2_w_down), ffn2_g_post)
    return y, k, v, ki, s_re, s_im


def setup_inputs(seed: int = 0) -> dict:
    key = jax.random.key(seed)
    ks = iter(jax.random.split(key, 40))

    def nrm(shape, scale):
        return jax.random.normal(next(ks), shape, jnp.float32) * scale

    def gain(shape):
        return 1.0 + nrm(shape, 0.01)

    L, G, P, C = DEPTH, SSM_GROUPS, SSM_P, SSM_GC
    n_idx = jnp.arange(P, dtype=jnp.float32)
    return {
        'x_prompt': nrm((BATCH, SEQ, D_MODEL), 1.0),
        'x_sample': nrm((DEC_BATCH, DEC_SEQ, D_MODEL), 1.0),
        'cache_k': nrm((L, DEC_BATCH, PAST_LEN, N_HEADS, HEAD_DIM), 1.0),
        'cache_v': nrm((L, DEC_BATCH, PAST_LEN, N_HEADS, HEAD_DIM), 1.0),
        'cache_kidx': nrm((L, DEC_BATCH, PAST_LEN, IDX_DIM), 1.0),
        'state_ssm_re': nrm((L, DEC_BATCH, G, P), 0.3),
        'state_ssm_im': nrm((L, DEC_BATCH, G, P), 0.3),
        'meta_tokens': nrm((N_META, D_MODEL), 1.0),
        'rel_bias': nrm((NUM_BUCKETS, N_HEADS), 0.5),
        'ffn1_g_pre': gain((L, D_MODEL)),
        'ffn1_w_gate': nrm((L, D_MODEL, D_FF), D_MODEL ** -0.5),
        'ffn1_w_up': nrm((L, D_MODEL, D_FF), D_MODEL ** -0.5),
        'ffn1_w_down': nrm((L, D_FF, D_MODEL), D_FF ** -0.5),
        'ffn1_g_post': gain((L, D_MODEL)),
        'mix_g_pre': gain((L, D_MODEL)),
        'w_in': nrm((L, D_MODEL, D_IN), D_MODEL ** -0.5),
        'w_out': nrm((L, MIX_W, D_MODEL), MIX_W ** -0.5),
        'mix_g_post': gain((L, D_MODEL)),
        'lam_re': -0.5 + nrm((L, G, P), 0.01),
        'lam_im': jnp.broadcast_to(math.pi * n_idx, (L, G, P)) + nrm((L, G, P), 0.01),
        'log_step': jax.random.uniform(next(ks), (L, G), jnp.float32, math.log(1e-3), math.log(1e-1)),
        'b_re': nrm((L, G, P, C), (2 * C) ** -0.5),
        'b_im': nrm((L, G, P, C), (2 * C) ** -0.5),
        'c_re': nrm((L, G, C, P), P ** -0.5),
        'c_im': nrm((L, G, C, P), P ** -0.5),
        'd_skip': nrm((L, G, C), 1.0),
        'w_glu': nrm((L, SSM_W, SSM_W), SSM_W ** -0.5),
        'b_glu': nrm((L, SSM_W), 0.01),
        'ffn2_g_pre': gain((L, D_MODEL)),
        'ffn2_w_gate': nrm((L, D_MODEL, D_FF), D_MODEL ** -0.5),
        'ffn2_w_up': nrm((L, D_MODEL, D_FF), D_MODEL ** -0.5),
        'ffn2_w_down': nrm((L, D_FF, D_MODEL), D_FF ** -0.5),
        'ffn2_g_post': gain((L, D_MODEL)),
    }


def reference(x_prompt, x_sample, cache_k, cache_v, cache_kidx, state_ssm_re, state_ssm_im,
              meta_tokens, rel_bias,
              ffn1_g_pre, ffn1_w_gate, ffn1_w_up, ffn1_w_down, ffn1_g_post,
              mix_g_pre, w_in, w_out, mix_g_post,
              lam_re, lam_im, log_step, b_re, b_im, c_re, c_im, d_skip, w_glu, b_glu,
              ffn2_g_pre, ffn2_w_gate, ffn2_w_up, ffn2_w_down, ffn2_g_post):
    i32 = jnp.int32
    bp, seq = x_prompt.shape[0], x_prompt.shape[1]
    hp = jnp.concatenate([jnp.broadcast_to(meta_tokens[None].astype(x_prompt.dtype),
                                           (bp, N_META, D_MODEL)), x_prompt], axis=1)
    p_pos = jnp.arange(N_META + seq, dtype=i32)
    p_chunk = jnp.where(p_pos < N_META, -1, (p_pos - N_META) // CHUNK)
    topk_p = min(TOPK_MAX, seq // 4)
    empty_kv = jnp.zeros((bp, 0, N_HEADS, HEAD_DIM), x_prompt.dtype)
    empty_ki = jnp.zeros((bp, 0, IDX_DIM), x_prompt.dtype)
    empty_i = jnp.zeros((0,), i32)
    zero_state = jnp.zeros((bp, SSM_GROUPS, SSM_P), jnp.float32)
    ts, past = x_sample.shape[1], cache_k.shape[2]
    past_pos = jnp.arange(past, dtype=i32)
    s_pos = past + jnp.arange(ts, dtype=i32)
    topk_s = min(TOPK_MAX, (past + ts) // 4)
    hs = x_sample
    kp_l, vp_l, kip_l, srp_l, sip_l = [], [], [], [], []
    ks_l, vs_l, kis_l, srs_l, sis_l = [], [], [], [], []
    for l in range(DEPTH):
        lp = (ffn1_g_pre[l], ffn1_w_gate[l], ffn1_w_up[l], ffn1_w_down[l], ffn1_g_post[l],
              mix_g_pre[l], w_in[l], w_out[l], mix_g_post[l],
              lam_re[l], lam_im[l], log_step[l], b_re[l], b_im[l], c_re[l], c_im[l], d_skip[l],
              w_glu[l], b_glu[l],
              ffn2_g_pre[l], ffn2_w_gate[l], ffn2_w_up[l], ffn2_w_down[l], ffn2_g_post[l])
        hp, kp, vp, kip, srp, sip = layer(hp, p_pos, p_chunk, empty_kv, empty_kv, empty_ki, empty_i,
                                          empty_i, zero_state, zero_state, topk_p, rel_bias, *lp)
        hs, ks, vs, kis, srs, sis = layer(hs, s_pos, s_pos // CHUNK, cache_k[l], cache_v[l], cache_kidx[l],
                                          past_pos, past_pos // CHUNK, state_ssm_re[l], state_ssm_im[l],
                                          topk_s, rel_bias, *lp)
        kp_l.append(kp); vp_l.append(vp); kip_l.append(kip); srp_l.append(srp); sip_l.append(sip)
        ks_l.append(ks); vs_l.append(vs); kis_l.append(kis); srs_l.append(srs); sis_l.append(sis)
    y_prompt = hp[:, N_META:]
    return (y_prompt, hs,
            jnp.stack(kp_l), jnp.stack(vp_l), jnp.stack(kip_l), jnp.stack(srp_l), jnp.stack(sip_l),
            jnp.stack(ks_l), jnp.stack(vs_l), jnp.stack(kis_l), jnp.stack(srs_l), jnp.stack(sis_l))
```

```python
import functools
import math

import numpy as np
import jax
import jax.numpy as jnp
from jax import lax
from jax.experimental import pallas as pl
from jax.experimental.pallas import tpu as pltpu

F32, BF16, I32 = jnp.float32, jnp.bfloat16, jnp.int32

D_MODEL = 1024
CHUNK = 64
N_META = 16
N_HEADS = 8
HEAD_DIM = 64
ATT_W = N_HEADS * HEAD_DIM
N_IDX = 8
IDX_DIM = 64
TOPK_MAX = 256
SSM_GROUPS = 32
SSM_GC = 16
SSM_W = SSM_GROUPS * SSM_GC
SSM_P = 64
SSM_S = SSM_GROUPS * SSM_P
D_FF = 2816
NUM_BUCKETS = 32
EPS = 1e-6

LANE = 128
VMEM_LIMIT = 60000 * 1024
ROW_TILE = 512
TQ = LANE
KEY_CHUNK = 512
NEAR_TILES = 6
NEG = -1e30
F32_MAX = float(np.finfo(np.float32).max)
N_PAIRS = N_HEADS // 2


def _rms(x, g):
    return x * lax.rsqrt(jnp.mean(x * x, axis=-1, keepdims=True) + EPS) * g


def _swiglu(xn, wg_ref, wu_ref, wd_ref):
    xb = xn.astype(BF16)
    a = jnp.dot(xb, wg_ref[...], preferred_element_type=F32)
    b = jnp.dot(xb, wu_ref[...], preferred_element_type=F32)
    hidden = (a * jax.nn.sigmoid(a) * b).astype(BF16)
    return jnp.dot(hidden, wd_ref[...], preferred_element_type=F32)


def _const_spec(shape):
    nd = len(shape)
    return pl.BlockSpec(shape, lambda *_: (0,) * nd, pipeline_mode=pl.Buffered(1))


def _ffn_body(x_ref, gpre_ref, wg_ref, wu_ref, wd_ref, gpost_ref, o_ref):
    x = x_ref[...]
    y = _swiglu(_rms(x, gpre_ref[...]), wg_ref, wu_ref, wd_ref)
    o_ref[...] = x + 0.5 * _rms(y, gpost_ref[...])


def _ffn(x, gpre, wg, wu, wd, gpost):
    n = x.shape[0]
    row = pl.BlockSpec((ROW_TILE, D_MODEL), lambda i: (i, 0))
    return pl.pallas_call(
        _ffn_body,
        grid=(pl.cdiv(n, ROW_TILE),),
        in_specs=[row, _const_spec((1, D_MODEL)), _const_spec((D_MODEL, D_FF)), _const_spec((D_MODEL, D_FF)),
                  _const_spec((D_FF, D_MODEL)), _const_spec((1, D_MODEL))],
        out_specs=row,
        out_shape=jax.ShapeDtypeStruct((n, D_MODEL), F32),
        compiler_params=pltpu.CompilerParams(dimension_semantics=("arbitrary",), vmem_limit_bytes=VMEM_LIMIT),
        name="ffn1",
    )(x, gpre, wg, wu, wd, gpost)


ROW_COLS = 3 * ATT_W + IDX_DIM
ROW_COLS_PAD = 13 * LANE
COL_ROWS = 3 * ATT_W + N_IDX


def _mix_in_body(n_valid, h_ref, g_ref, wrow_ref, wcol_ref,
                 k_ref, v_ref, ki_ref, us_ref, kb_ref, kib_ref, vtb_ref, qt_ref, qit_ref, wt_ref):
    i = pl.program_id(0)
    u = _rms(h_ref[...], g_ref[...]).astype(BF16)
    z = jnp.dot(u, wrow_ref[...], preferred_element_type=F32)
    zt = lax.dot_general(wcol_ref[...], u, (((1,), (1,)), ((), ())), preferred_element_type=F32)
    k = z[:, 0:ATT_W]
    v = z[:, ATT_W:2 * ATT_W]
    ki = z[:, 3 * ATT_W:3 * ATT_W + IDX_DIM]
    k_ref[...] = k
    v_ref[...] = v
    us_ref[...] = z[:, 2 * ATT_W:3 * ATT_W]
    ki_ref[...] = ki
    row_ok = (i * ROW_TILE + lax.broadcasted_iota(I32, (ROW_TILE, 1), 0)) < n_valid
    col_ok = (i * ROW_TILE + lax.broadcasted_iota(I32, (1, ROW_TILE), 1)) < n_valid
    kb_ref[...] = jnp.where(row_ok, k, 0.0).astype(BF16)
    kib_ref[...] = jnp.where(row_ok, ki, 0.0).astype(BF16)
    zt = jnp.where(col_ok, zt, 0.0)
    qt_ref[...] = (zt[0:ATT_W] * (HEAD_DIM ** -0.5)).astype(BF16)
    qit_ref[...] = zt[ATT_W:2 * ATT_W].astype(BF16)
    vt = zt[2 * ATT_W:3 * ATT_W].astype(BF16)
    for b in range(ROW_TILE // LANE):
        vtb_ref[b] = vt[:, b * LANE:(b + 1) * LANE]
    wt_ref[...] = zt[3 * ATT_W:3 * ATT_W + N_IDX] * ((IDX_DIM ** -0.5) * (N_IDX ** -0.5))


def _mix_in(h, g, wrow, wcol, n_pad):
    n = h.shape[0]
    steps = n_pad // ROW_TILE
    assert steps == pl.cdiv(n, ROW_TILE)

    def row(width):
        return pl.BlockSpec((ROW_TILE, width), lambda i: (i, 0))

    col = lambda rows: pl.BlockSpec((rows, ROW_TILE), lambda i: (0, i))
    out_shape = (
        jax.ShapeDtypeStruct((n, ATT_W), F32), jax.ShapeDtypeStruct((n, ATT_W), F32),
        jax.ShapeDtypeStruct((n, IDX_DIM), F32), jax.ShapeDtypeStruct((n, SSM_W), F32),
        jax.ShapeDtypeStruct((n_pad, ATT_W), BF16), jax.ShapeDtypeStruct((n_pad, IDX_DIM), BF16),
        jax.ShapeDtypeStruct((n_pad // LANE, ATT_W, LANE), BF16),
        jax.ShapeDtypeStruct((ATT_W, n_pad), BF16), jax.ShapeDtypeStruct((ATT_W, n_pad), BF16),
        jax.ShapeDtypeStruct((N_IDX, n_pad), F32),
    )
    out_specs = (
        row(ATT_W), row(ATT_W), row(IDX_DIM), row(SSM_W), row(ATT_W), row(IDX_DIM),
        pl.BlockSpec((ROW_TILE // LANE, ATT_W, LANE), lambda i: (i, 0, 0)),
        col(ATT_W), col(ATT_W), col(N_IDX),
    )
    return pl.pallas_call(
        functools.partial(_mix_in_body, n),
        grid=(steps,),
        in_specs=[row(D_MODEL), _const_spec((1, D_MODEL)), _const_spec((D_MODEL, ROW_COLS_PAD)),
                  _const_spec((COL_ROWS, D_MODEL))],
        out_specs=out_specs,
        out_shape=out_shape,
        compiler_params=pltpu.CompilerParams(dimension_semantics=("arbitrary",), vmem_limit_bytes=VMEM_LIMIT),
        name="mix_in",
    )(h, g, wrow, wcol)


def _ordered_bits_to_f32(u):
    k = u ^ jnp.int32(-2 ** 31)
    b = k ^ ((k >> 31) & jnp.int32(0x7FFFFFFF))
    return lax.bitcast_convert_type(b, F32)


def _attn_body(topk, idx_bits,
               nch_ref, nfar_ref, nnear_ref, toff_ref,
               qit_ref, wt_ref, qt_ref, vis_ref, ki_ref, k_ref, vt_ref, bias_ref,
               o_ref,
               st_ref, qicat_ref, qbd_ref, m_ref, l_ref, acc_ref, cut_ref, ot_ref):
    j = pl.program_id(1)
    nch = nch_ref[j]
    nfar = nfar_ref[j]
    nnear = nnear_ref[j]
    toff = toff_ref[j]
    kf = float(topk)

    for h in range(N_IDX):
        qicat_ref[:, h * TQ:(h + 1) * TQ] = qit_ref[h * IDX_DIM:(h + 1) * IDX_DIM, :]
    qbd_ref[...] = jnp.zeros(qbd_ref.shape, BF16)
    for a in range(N_PAIRS):
        qbd_ref[a, 0:HEAD_DIM, 0:TQ] = qt_ref[2 * a * HEAD_DIM:(2 * a + 1) * HEAD_DIM, :]
        qbd_ref[a, HEAD_DIM:2 * HEAD_DIM, TQ:2 * TQ] = qt_ref[(2 * a + 1) * HEAD_DIM:(2 * a + 2) * HEAD_DIM, :]

    def chunk_start(c):
        return pl.multiple_of(c * KEY_CHUNK, KEY_CHUNK)

    def chunk_rows(r0):
        return r0 + lax.broadcasted_iota(I32, (KEY_CHUNK, TQ), 0)

    vis = vis_ref[...]

    def score_chunk(c, carry):
        r0 = chunk_start(c)
        rel = jnp.dot(ki_ref[pl.ds(r0, KEY_CHUNK), :], qicat_ref[...], preferred_element_type=F32)
        score = jnp.zeros((KEY_CHUNK, TQ), F32)
        for h in range(N_IDX):
            score = score + wt_ref[h:h + 1, :] * jnp.maximum(rel[:, h * TQ:(h + 1) * TQ], 0.0)
        st_ref[pl.ds(r0, KEY_CHUNK), :] = jnp.where(chunk_rows(r0) < vis, score, -jnp.inf)
        return carry

    lax.fori_loop(0, nch, score_chunk, 0)

    def count(indicator):
        def body(c, acc):
            r0 = chunk_start(c)
            ind = indicator(st_ref[pl.ds(r0, KEY_CHUNK), :], r0)
            return acc + jnp.sum(ind.reshape(KEY_CHUNK // 8, 8, TQ), axis=0)

        acc = lax.fori_loop(0, nch, body, jnp.zeros((8, TQ), F32))
        return jnp.sum(acc, axis=0, keepdims=True)

    def value_bit(i, res):
        cand = res | lax.shift_left(jnp.int32(1), 31 - i)
        thr_c = _ordered_bits_to_f32(cand)
        cnt = count(lambda blk, r0: jnp.where(blk >= thr_c, 1.0, 0.0))
        return jnp.where(cnt >= kf, cand, res)

    res = lax.fori_loop(0, 32, value_bit, jnp.zeros((1, TQ), I32))
    thr = jnp.where((res >> 23) == 0, -F32_MAX, _ordered_bits_to_f32(res))

    cnt_gt = count(lambda blk, r0: jnp.where(blk > thr, 1.0, 0.0))
    cnt_ge = count(lambda blk, r0: jnp.where(blk >= thr, 1.0, 0.0))
    need = kf - cnt_gt
    excess = cnt_ge - kf
    cut_ref[...] = jnp.full((1, TQ), 2 ** 30, I32)

    @pl.when(jnp.max(excess) > 0.0)
    def _():
        def index_bit(i, cut):
            cand = cut | lax.shift_left(jnp.int32(1), idx_bits - 1 - i)
            cnt = count(lambda blk, r0: jnp.where(blk == thr, jnp.where(chunk_rows(r0) < cand, 1.0, 0.0), 0.0))
            return jnp.where(cnt < need, cand, cut)

        cut = lax.fori_loop(0, idx_bits, index_bit, jnp.zeros((1, TQ), I32))
        cut_ref[...] = jnp.where(excess > 0.0, cut, 2 ** 30)

    cut = cut_ref[...]

    def mask_chunk(c, carry):
        r0 = chunk_start(c)
        blk = st_ref[pl.ds(r0, KEY_CHUNK), :]
        tie = jnp.where(chunk_rows(r0) <= cut, 0.0, NEG)
        st_ref[pl.ds(r0, KEY_CHUNK), :] = jnp.where(blk > thr, 0.0, jnp.where(blk == thr, tie, NEG))
        return carry

    lax.fori_loop(0, nch, mask_chunk, 0)

    m_ref[...] = jnp.full(m_ref.shape, NEG, F32)
    l_ref[...] = jnp.zeros(l_ref.shape, F32)
    acc_ref[...] = jnp.zeros(acc_ref.shape, F32)

    def attend(tile0, n_tiles, bias_row0):
        rows = n_tiles * LANE
        r0 = pl.multiple_of(tile0 * LANE, LANE)
        msk = st_ref[pl.ds(r0, rows), :]
        msk2 = jnp.concatenate([msk, msk], axis=1)
        kblk = k_ref[pl.ds(r0, rows), :]
        for a in range(N_PAIRS):
            s2 = jnp.dot(kblk[:, a * 2 * HEAD_DIM:(a + 1) * 2 * HEAD_DIM], qbd_ref[a], preferred_element_type=F32)
            s2 = s2 + msk2
            if bias_row0 is not None:
                s2 = s2 + bias_ref[a, pl.ds(bias_row0, rows), :]
            m_old = m_ref[a:a + 1, :]
            m_new = jnp.maximum(m_old, jnp.max(s2, axis=0, keepdims=True))
            alpha = jnp.exp(m_old - m_new)
            p = jnp.exp(s2 - m_new)
            l_ref[a:a + 1, :] = alpha * l_ref[a:a + 1, :] + jnp.sum(p, axis=0, keepdims=True)
            pb = p.astype(BF16)
            o2 = jnp.zeros((2 * HEAD_DIM, 2 * TQ), F32)
            for t in range(n_tiles):
                o2 = o2 + jnp.dot(vt_ref[tile0 + t, a * 2 * HEAD_DIM:(a + 1) * 2 * HEAD_DIM, :],
                                  pb[t * LANE:(t + 1) * LANE, :], preferred_element_type=F32)
            acc_ref[a] = alpha * acc_ref[a] + o2
            m_ref[a:a + 1, :] = m_new

    far_tiles = KEY_CHUNK // LANE

    def far_block(fb, carry):
        attend(fb * far_tiles, far_tiles, None)
        return carry

    lax.fori_loop(0, nfar, far_block, 0)

    def near_block(nb, carry):
        tile = nfar * far_tiles + nb
        attend(tile, 1, pl.multiple_of((tile + toff) * LANE, LANE))
        return carry

    lax.fori_loop(0, nnear, near_block, 0)

    for a in range(N_PAIRS):
        for e in range(2):
            h = 2 * a + e
            num = acc_ref[a, e * HEAD_DIM:(e + 1) * HEAD_DIM, e * TQ:(e + 1) * TQ]
            ot_ref[h * HEAD_DIM:(h + 1) * HEAD_DIM, :] = num / l_ref[a:a + 1, e * TQ:(e + 1) * TQ]
    o_ref[...] = ot_ref[...].T


def _attention(qit, wt, qt, vis, ki, k, vt, bias, sched, topk, resident):
    s_count, _, nq = qit.shape
    nk = k.shape[1]
    n_tiles = sched[0].shape[0]
    mode = dict(pipeline_mode=pl.Buffered(1)) if resident else {}
    idx_bits = max(1, int(nk - 1).bit_length())
    in_specs = [
        pl.BlockSpec((None, ATT_W, TQ), lambda s, j, *_: (s, 0, j)),
        pl.BlockSpec((None, N_IDX, TQ), lambda s, j, *_: (s, 0, j)),
        pl.BlockSpec((None, ATT_W, TQ), lambda s, j, *_: (s, 0, j)),
        pl.BlockSpec((1, TQ), lambda s, j, *_: (0, j)),
        pl.BlockSpec((None, nk, IDX_DIM), lambda s, j, *_: (s, 0, 0), **mode),
        pl.BlockSpec((None, nk, ATT_W), lambda s, j, *_: (s, 0, 0), **mode),
        pl.BlockSpec((None, nk // LANE, ATT_W, LANE), lambda s, j, *_: (s, 0, 0, 0), **mode),
        pl.BlockSpec(bias.shape, lambda s, j, *_: (0, 0, 0), pipeline_mode=pl.Buffered(1)),
    ]
    grid_spec = pltpu.PrefetchScalarGridSpec(
        num_scalar_prefetch=4,
        grid=(s_count, n_tiles),
        in_specs=in_specs,
        out_specs=pl.BlockSpec((None, TQ, ATT_W), lambda s, j, *_: (s, j, 0)),
        scratch_shapes=[
            pltpu.VMEM((nk, TQ), F32),
            pltpu.VMEM((IDX_DIM, N_IDX * TQ), BF16),
            pltpu.VMEM((N_PAIRS, 2 * HEAD_DIM, 2 * TQ), BF16),
            pltpu.VMEM((N_PAIRS, 2 * TQ), F32),
            pltpu.VMEM((N_PAIRS, 2 * TQ), F32),
            pltpu.VMEM((N_PAIRS, 2 * HEAD_DIM, 2 * TQ), F32),
            pltpu.VMEM((1, TQ), I32),
            pltpu.VMEM((ATT_W, TQ), F32),
        ],
    )
    return pl.pallas_call(
        functools.partial(_attn_body, topk, idx_bits),
        grid_spec=grid_spec,
        out_shape=jax.ShapeDtypeStruct((s_count, nq, ATT_W), F32),
        compiler_params=pltpu.CompilerParams(dimension_semantics=("arbitrary", "arbitrary"),
                                             vmem_limit_bytes=VMEM_LIMIT),
        name="attn",
    )(*sched, qit, wt, qt, vis, ki, k, vt, bias)


def _s5_body(tt, us_ref, s0r_ref, s0i_ref, ar_ref, ai_ref, bbr_ref, bbi_ref, cr_ref, ci_ref, d_ref, wglu_ref,
             bglu_ref, out_ref, sr_out_ref, si_out_ref, str_ref, sti_ref, bre_ref, bim_ref, xr_ref, xi_ref):
    @pl.when(pl.program_id(1) == 0)
    def _():
        str_ref[...] = s0r_ref[...]
        sti_ref[...] = s0i_ref[...]

    u = us_ref[...]
    ub = u.astype(BF16)
    bre_ref[...] = jnp.dot(ub, bbr_ref[...], preferred_element_type=F32)
    bim_ref[...] = jnp.dot(ub, bbi_ref[...], preferred_element_type=F32)
    ar = ar_ref[...]
    ai = ai_ref[...]

    def step(t, carry):
        sr, si = carry
        nr = ar * sr - ai * si + bre_ref[pl.ds(t, 1), :]
        ni = ar * si + ai * sr + bim_ref[pl.ds(t, 1), :]
        xr_ref[pl.ds(t, 1), :] = nr
        xi_ref[pl.ds(t, 1), :] = ni
        return nr, ni

    sr, si = lax.fori_loop(0, tt, step, (str_ref[...], sti_ref[...]))
    str_ref[...] = sr
    sti_ref[...] = si
    sr_out_ref[...] = sr
    si_out_ref[...] = si
    y = (jnp.dot(xr_ref[...].astype(BF16), cr_ref[...], preferred_element_type=F32)
         - jnp.dot(xi_ref[...].astype(BF16), ci_ref[...], preferred_element_type=F32)
         + d_ref[...] * u)
    g = jax.nn.gelu(y)
    gate = jnp.dot(g.astype(BF16), wglu_ref[...], preferred_element_type=F32) + bglu_ref[...]
    out_ref[...] = g * jax.nn.sigmoid(gate)


def _s5(us, s0r, s0i, ar, ai, bbr, bbi, cr, ci, d, wglu, bglu, tt):
    s_count, t_len, _ = us.shape
    assert t_len % tt == 0
    state = pl.BlockSpec((None, 1, SSM_S), lambda s, t: (s, 0, 0))
    seq = pl.BlockSpec((None, tt, SSM_W), lambda s, t: (s, t, 0))
    return pl.pallas_call(
        functools.partial(_s5_body, tt),
        grid=(s_count, t_len // tt),
        in_specs=[seq, state, state, _const_spec((1, SSM_S)), _const_spec((1, SSM_S)),
                  _const_spec((SSM_W, SSM_S)), _const_spec((SSM_W, SSM_S)),
                  _const_spec((SSM_S, SSM_W)), _const_spec((SSM_S, SSM_W)),
                  _const_spec((1, SSM_W)), _const_spec((SSM_W, SSM_W)), _const_spec((1, SSM_W))],
        out_specs=(seq, state, state),
        out_shape=(jax.ShapeDtypeStruct((s_count, t_len, SSM_W), F32),
                   jax.ShapeDtypeStruct((s_count, 1, SSM_S), F32),
                   jax.ShapeDtypeStruct((s_count, 1, SSM_S), F32)),
        scratch_shapes=[pltpu.VMEM((1, SSM_S), F32), pltpu.VMEM((1, SSM_S), F32),
                        pltpu.VMEM((tt, SSM_S), F32), pltpu.VMEM((tt, SSM_S), F32),
                        pltpu.VMEM((tt, SSM_S), F32), pltpu.VMEM((tt, SSM_S), F32)],
        compiler_params=pltpu.CompilerParams(dimension_semantics=("arbitrary", "arbitrary"),
                                             vmem_limit_bytes=VMEM_LIMIT),
        name="s5",
    )(us, s0r, s0i, ar, ai, bbr, bbi, cr, ci, d, wglu, bglu)


def _mix_out_body(att_ref, ssm_ref, h_ref, woa_ref, wos_ref, gmix_ref, gpre_ref, wg_ref, wu_ref, wd_ref, gpost_ref,
                  o_ref):
    m = (jnp.dot(att_ref[...].astype(BF16), woa_ref[...], preferred_element_type=F32)
         + jnp.dot(ssm_ref[...].astype(BF16), wos_ref[...], preferred_element_type=F32))
    h2 = h_ref[...] + _rms(m, gmix_ref[...])
    y = _swiglu(_rms(h2, gpre_ref[...]), wg_ref, wu_ref, wd_ref)
    o_ref[...] = h2 + 0.5 * _rms(y, gpost_ref[...])


def _mix_out(att, ssm, h, woa, wos, gmix, gpre, wg, wu, wd, gpost):
    n = h.shape[0]
    row = lambda width: pl.BlockSpec((ROW_TILE, width), lambda i: (i, 0))
    return pl.pallas_call(
        _mix_out_body,
        grid=(pl.cdiv(n, ROW_TILE),),
        in_specs=[row(ATT_W), row(SSM_W), row(D_MODEL), _const_spec((ATT_W, D_MODEL)),
                  _const_spec((SSM_W, D_MODEL)), _const_spec((1, D_MODEL)), _const_spec((1, D_MODEL)),
                  _const_spec((D_MODEL, D_FF)), _const_spec((D_MODEL, D_FF)), _const_spec((D_FF, D_MODEL)),
                  _const_spec((1, D_MODEL))],
        out_specs=row(D_MODEL),
        out_shape=jax.ShapeDtypeStruct((n, D_MODEL), F32),
        compiler_params=pltpu.CompilerParams(dimension_semantics=("arbitrary",), vmem_limit_bytes=VMEM_LIMIT),
        name="mix_out_ffn2",
    )(att, ssm, h, woa, wos, gmix, gpre, wg, wu, wd, gpost)


def _rel_bucket_np(rel):
    half = NUM_BUCKETS // 2
    max_exact = half // 2
    n = np.abs(rel).astype(np.int64)
    sq = np.maximum(n, max_exact) ** 2 // (max_exact * max_exact)
    large = max_exact + np.floor(np.log2(sq.astype(np.float64)) + 1e-9).astype(np.int64)
    large = np.minimum(large, half - 1)
    return np.where(rel > 0, half, 0) + np.where(n < max_exact, n, large)


def _pair_bias(rel_bias, rel):
    far = rel_bias[half_bucket_far()]
    table = rel_bias - far[None, :]
    b = table[_rel_bucket_np(rel)]
    b = jnp.transpose(b, (2, 0, 1)).reshape(N_PAIRS, 2, rel.shape[0], TQ)
    return jnp.concatenate([b[:, 0], b[:, 1]], axis=-1).astype(F32)


def half_bucket_far():
    return NUM_BUCKETS // 2 - 1


def _prompt_schedule(n, n_pad):
    pos = np.arange(n_pad)
    vis = np.where(pos < N_META, N_META, N_META + CHUNK * ((pos - N_META) // CHUNK + 1))
    vis = np.minimum(vis, n).astype(np.int32)
    n_tiles = -(-n // TQ)
    vmax = vis[:n_tiles * TQ].reshape(n_tiles, TQ).max(axis=1)
    j = np.arange(n_tiles)
    far_tiles = KEY_CHUNK // LANE
    nfar = np.maximum(j - 1, 0) // far_tiles
    ntile = -(-vmax // LANE)
    nnear = ntile - nfar * far_tiles
    assert nnear.max() <= NEAR_TILES and nnear.min() >= 1
    toff = (NEAR_TILES - 2) - j
    nch = -(-vmax // KEY_CHUNK)
    sched = tuple(jnp.asarray(a, I32) for a in (nch, nfar, nnear, toff))
    kk = np.arange(NEAR_TILES * LANE)[:, None]
    rel = kk - (NEAR_TILES - 2) * LANE - np.arange(TQ)[None, :]
    return sched, jnp.asarray(vis[None, :]), rel


def _sample_schedule(past, ts, n_pad):
    nk = past + ts
    ntile = -(-nk // LANE)
    sched = tuple(jnp.asarray([v], I32) for v in (-(-nk // KEY_CHUNK), 0, ntile, 0))
    vis = jnp.full((1, TQ), nk, I32)
    rel = np.arange(ntile * LANE)[:, None] - (past + np.arange(TQ))[None, :]
    return sched, vis, rel


def _s5_params(lam_re, lam_im, log_step, b_re, b_im, c_re, c_im, d_skip):
    dt = jnp.exp(log_step)[:, None]
    mag = jnp.exp(lam_re * dt)
    ab_re, ab_im = mag * jnp.cos(lam_im * dt), mag * jnp.sin(lam_im * dt)
    nr, ni = ab_re - 1.0, ab_im
    den = lam_re * lam_re + lam_im * lam_im
    f_re, f_im = (nr * lam_re + ni * lam_im) / den, (ni * lam_re - nr * lam_im) / den
    bb_re = f_re[..., None] * b_re - f_im[..., None] * b_im
    bb_im = f_re[..., None] * b_im + f_im[..., None] * b_re
    eye = jnp.eye(SSM_GROUPS, dtype=F32)
    bd_in = lambda w: jnp.einsum('gpc,gh->gchp', w, eye).reshape(SSM_W, SSM_S).astype(BF16)
    bd_out = lambda w: jnp.einsum('gcp,gh->gphc', w, eye).reshape(SSM_S, SSM_W).astype(BF16)
    return (ab_re.reshape(1, SSM_S), ab_im.reshape(1, SSM_S), bd_in(bb_re), bd_in(bb_im),
            bd_out(c_re), bd_out(c_im), d_skip.reshape(1, SSM_W))


def kernel(x_prompt, x_sample, cache_k, cache_v, cache_kidx, state_ssm_re, state_ssm_im, meta_tokens, rel_bias,
           ffn1_g_pre, ffn1_w_gate, ffn1_w_up, ffn1_w_down, ffn1_g_post, mix_g_pre, w_in, w_out, mix_g_post,
           lam_re, lam_im, log_step, b_re, b_im, c_re, c_im, d_skip, w_glu, b_glu,
           ffn2_g_pre, ffn2_w_gate, ffn2_w_up, ffn2_w_down, ffn2_g_post):
    depth = ffn1_g_pre.shape[0]
    assert depth == 1
    bp, seq, _ = x_prompt.shape
    assert bp == 1
    ds, ts, _ = x_sample.shape
    past = cache_k.shape[2]
    n_p = N_META + seq
    n_p_pad = pl.cdiv(n_p, ROW_TILE) * ROW_TILE
    n_s = ds * ts
    assert n_s % ROW_TILE == 0 and ts <= TQ
    nk_s_pad = pl.cdiv(past + ts, KEY_CHUNK) * KEY_CHUNK
    l = 0

    row2 = lambda g: g[l].reshape(1, -1)
    w1 = (row2(ffn1_g_pre), ffn1_w_gate[l].astype(BF16), ffn1_w_up[l].astype(BF16), ffn1_w_down[l].astype(BF16),
          row2(ffn1_g_post))
    w2 = (row2(ffn2_g_pre), ffn2_w_gate[l].astype(BF16), ffn2_w_up[l].astype(BF16), ffn2_w_down[l].astype(BF16),
          row2(ffn2_g_post))
    wi = w_in[l]
    o_q, o_k, o_v, o_qi, o_ki, o_wi, o_us = np.cumsum((0, ATT_W, ATT_W, ATT_W, N_IDX * IDX_DIM, IDX_DIM, N_IDX))
    sl = lambda o, w: wi[:, o:o + w]
    wrow = jnp.concatenate([sl(o_k, ATT_W), sl(o_v, ATT_W), sl(o_us, SSM_W), sl(o_ki, IDX_DIM),
                            jnp.zeros((D_MODEL, ROW_COLS_PAD - ROW_COLS), F32)], axis=1).astype(BF16)
    wcol = jnp.concatenate([sl(o_q, ATT_W), sl(o_qi, ATT_W), sl(o_v, ATT_W), sl(o_wi, N_IDX)], axis=1).T.astype(BF16)
    woa, wos = w_out[l][:ATT_W].astype(BF16), w_out[l][ATT_W:].astype(BF16)
    s5w = _s5_params(lam_re[l], lam_im[l], log_step[l], b_re[l], b_im[l], c_re[l], c_im[l], d_skip[l])
    s5w = s5w + (w_glu[l].astype(BF16), b_glu[l].reshape(1, SSM_W))
    bias_tab = rel_bias.astype(F32)

    xp = jnp.concatenate([meta_tokens.astype(F32), x_prompt[0]], axis=0)
    hp = _ffn(xp, *w1)
    kp, vp, kip, usp, kbp, kibp, vtbp, qtp, qitp, wtp = _mix_in(hp, row2(mix_g_pre), wrow, wcol, n_p_pad)
    sched_p, vis_p, rel_p = _prompt_schedule(n_p, n_p_pad)
    att_p = _attention(qitp[None], wtp[None], qtp[None], vis_p, kibp[None], kbp[None], vtbp[None],
                       _pair_bias(bias_tab, rel_p), sched_p, min(TOPK_MAX, seq // 4), True)[0]
    zero_state = jnp.zeros((1, 1, SSM_S), F32)
    tt_p = max(t for t in range(8, 513, 8) if n_p % t == 0)
    ssm_p, srp, sip = _s5(usp[None], zero_state, zero_state, *s5w, tt_p)
    yp = _mix_out(att_p, ssm_p[0], hp, woa, wos, row2(mix_g_post), *w2)

    hs = _ffn(x_sample.reshape(n_s, D_MODEL), *w1)
    ks, vs, kis, uss, _, _, _, qts, qits, wts = _mix_in(hs, row2(mix_g_pre), wrow, wcol, n_s)
    pad_k = lambda a: jnp.pad(a, ((0, 0), (0, nk_s_pad - past - ts), (0, 0)))
    k_all = pad_k(jnp.concatenate([cache_k[l].reshape(ds, past, ATT_W), ks.reshape(ds, ts, ATT_W)], 1).astype(BF16))
    v_all = pad_k(jnp.concatenate([cache_v[l].reshape(ds, past, ATT_W), vs.reshape(ds, ts, ATT_W)], 1).astype(BF16))
    ki_all = pad_k(jnp.concatenate([cache_kidx[l], kis.reshape(ds, ts, IDX_DIM)], 1).astype(BF16))
    vt_all = jnp.transpose(v_all.reshape(ds, nk_s_pad // LANE, LANE, ATT_W), (0, 1, 3, 2))
    lanes = lambda a: jnp.pad(jnp.transpose(a.reshape(a.shape[0], ds, ts), (1, 0, 2)), ((0, 0), (0, 0), (0, TQ - ts)))
    sched_s, vis_s, rel_s = _sample_schedule(past, ts, nk_s_pad)
    att_s = _attention(lanes(qits), lanes(wts), lanes(qts), vis_s, ki_all, k_all, vt_all,
                       _pair_bias(bias_tab, rel_s), sched_s, min(TOPK_MAX, (past + ts) // 4), False)
    att_s = att_s[:, :ts].reshape(n_s, ATT_W)
    ssm_s, srs, sis = _s5(uss.reshape(ds, ts, SSM_W), state_ssm_re[l].reshape(ds, 1, SSM_S),
                          state_ssm_im[l].reshape(ds, 1, SSM_S), *s5w, ts)
    ys = _mix_out(att_s, ssm_s.reshape(n_s, SSM_W), hs, woa, wos, row2(mix_g_post), *w2)

    heads = lambda a, b, t: a.reshape(1, b, t, N_HEADS, HEAD_DIM)
    state = lambda a, b: a.reshape(1, b, SSM_GROUPS, SSM_P)
    return (yp[N_META:][None], ys.reshape(ds, ts, D_MODEL),
            heads(kp, 1, n_p), heads(vp, 1, n_p), kip.reshape(1, 1, n_p, IDX_DIM), state(srp, 1), state(sip, 1),
            heads(ks, ds, ts), heads(vs, ds, ts), kis.reshape(1, ds, ts, IDX_DIM), state(srs, ds), state(sis, ds))
```

```python
import functools
import math

import numpy as np
import jax
import jax.numpy as jnp
from jax import lax
from jax.experimental import pallas as pl
from jax.experimental.pallas import tpu as pltpu

F32, BF16, I32 = jnp.float32, jnp.bfloat16, jnp.int32

D_MODEL = 1024
CHUNK = 64
N_META = 16
N_HEADS = 8
HEAD_DIM = 64
ATT_W = N_HEADS * HEAD_DIM
N_IDX = 8
IDX_DIM = 64
TOPK_MAX = 256
SSM_GROUPS = 32
SSM_GC = 16
SSM_W = SSM_GROUPS * SSM_GC
SSM_P = 64
SSM_S = SSM_GROUPS * SSM_P
D_FF = 2816
NUM_BUCKETS = 32
EPS = 1e-6

LANE = 128
VMEM_LIMIT = 60000 * 1024
ROW_TILE = 512
TQ = LANE
KEY_CHUNK = 512
PROMPT_BIAS_C0 = 512
PROMPT_BIAS_ROWS = 1280
NEG = -1e30
LOG2E = math.log2(math.e)
F32_MAX = float(np.finfo(np.float32).max)
N_PAIRS = N_HEADS // 2
CAND_DEPTH = 16
CAND_BATCH = 8 * CAND_DEPTH
CAND_GROUPS = 8


def _batcher_network(n):
    pairs, p = [], 1
    while p < n:
        k = p
        while k >= 1:
            for j in range(k % p, n - k, 2 * k):
                for i in range(min(k, n - j - k)):
                    if (i + j) // (2 * p) == (i + j + k) // (2 * p):
                        pairs.append((i + j, i + j + k))
            k //= 2
        p *= 2
    return pairs


_SORT_NETWORK = _batcher_network(CAND_DEPTH)


def _rms(x, g):
    return x * lax.rsqrt(jnp.mean(x * x, axis=-1, keepdims=True) + EPS) * g


def _swiglu(xn, wg_ref, wu_ref, wd_ref):
    xb = xn.astype(BF16)
    a = jnp.dot(xb, wg_ref[...], preferred_element_type=F32)
    b = jnp.dot(xb, wu_ref[...], preferred_element_type=F32)
    hidden = (a * jax.nn.sigmoid(a) * b).astype(BF16)
    return jnp.dot(hidden, wd_ref[...], preferred_element_type=F32)


def _const_spec(shape):
    nd = len(shape)
    return pl.BlockSpec(shape, lambda *_: (0,) * nd, pipeline_mode=pl.Buffered(1))


def _ffn_body(x_ref, gpre_ref, wg_ref, wu_ref, wd_ref, gpost_ref, o_ref):
    x = x_ref[...]
    y = _swiglu(_rms(x, gpre_ref[...]), wg_ref, wu_ref, wd_ref)
    o_ref[...] = x + 0.5 * _rms(y, gpost_ref[...])


def _ffn(x, gpre, wg, wu, wd, gpost):
    n = x.shape[0]
    row = pl.BlockSpec((ROW_TILE, D_MODEL), lambda i: (i, 0))
    return pl.pallas_call(
        _ffn_body,
        grid=(pl.cdiv(n, ROW_TILE),),
        in_specs=[row, _const_spec((1, D_MODEL)), _const_spec((D_MODEL, D_FF)), _const_spec((D_MODEL, D_FF)),
                  _const_spec((D_FF, D_MODEL)), _const_spec((1, D_MODEL))],
        out_specs=row,
        out_shape=jax.ShapeDtypeStruct((n, D_MODEL), F32),
        compiler_params=pltpu.CompilerParams(dimension_semantics=("arbitrary",), vmem_limit_bytes=VMEM_LIMIT),
        name="ffn1",
    )(x, gpre, wg, wu, wd, gpost)


ROW_COLS = 3 * ATT_W + IDX_DIM
ROW_COLS_PAD = 13 * LANE
COL_ROWS = 3 * ATT_W + N_IDX


def _mix_in_body(n_valid, h_ref, g_ref, wrow_ref, wcol_ref,
                 k_ref, v_ref, ki_ref, us_ref, kb_ref, kib_ref, vtb_ref, qt_ref, qit_ref, wt_ref):
    i = pl.program_id(0)
    u = _rms(h_ref[...], g_ref[...]).astype(BF16)
    z = jnp.dot(u, wrow_ref[...], preferred_element_type=F32)
    zt = lax.dot_general(wcol_ref[...], u, (((1,), (1,)), ((), ())), preferred_element_type=F32)
    k = z[:, 0:ATT_W]
    v = z[:, ATT_W:2 * ATT_W]
    ki = z[:, 3 * ATT_W:3 * ATT_W + IDX_DIM]
    k_ref[...] = k
    v_ref[...] = v
    us_ref[...] = z[:, 2 * ATT_W:3 * ATT_W]
    ki_ref[...] = ki
    row_ok = (i * ROW_TILE + lax.broadcasted_iota(I32, (ROW_TILE, 1), 0)) < n_valid
    col_ok = (i * ROW_TILE + lax.broadcasted_iota(I32, (1, ROW_TILE), 1)) < n_valid
    kb_ref[...] = jnp.where(row_ok, k, 0.0).astype(BF16)
    kib_ref[...] = jnp.where(row_ok, ki, 0.0).astype(BF16)
    zt = jnp.where(col_ok, zt, 0.0)
    qt_ref[...] = (zt[0:ATT_W] * (HEAD_DIM ** -0.5 * LOG2E)).astype(BF16)
    qit_ref[...] = zt[ATT_W:2 * ATT_W].astype(BF16)
    vtb_ref[...] = zt[2 * ATT_W:3 * ATT_W].astype(BF16)
    wt_ref[...] = zt[3 * ATT_W:3 * ATT_W + N_IDX] * ((IDX_DIM ** -0.5) * (N_IDX ** -0.5))


def _mix_in(h, g, wrow, wcol, n_pad):
    n = h.shape[0]
    steps = n_pad // ROW_TILE
    assert steps == pl.cdiv(n, ROW_TILE)

    def row(width):
        return pl.BlockSpec((ROW_TILE, width), lambda i: (i, 0))

    col = lambda rows: pl.BlockSpec((rows, ROW_TILE), lambda i: (0, i))
    out_shape = (
        jax.ShapeDtypeStruct((n, ATT_W), F32), jax.ShapeDtypeStruct((n, ATT_W), F32),
        jax.ShapeDtypeStruct((n, IDX_DIM), F32), jax.ShapeDtypeStruct((n, SSM_W), F32),
        jax.ShapeDtypeStruct((n_pad, ATT_W), BF16), jax.ShapeDtypeStruct((n_pad, IDX_DIM), BF16),
        jax.ShapeDtypeStruct((n_pad // ROW_TILE, ATT_W, ROW_TILE), BF16),
        jax.ShapeDtypeStruct((ATT_W, n_pad), BF16), jax.ShapeDtypeStruct((ATT_W, n_pad), BF16),
        jax.ShapeDtypeStruct((N_IDX, n_pad), F32),
    )
    out_specs = (
        row(ATT_W), row(ATT_W), row(IDX_DIM), row(SSM_W), row(ATT_W), row(IDX_DIM),
        pl.BlockSpec((None, ATT_W, ROW_TILE), lambda i: (i, 0, 0)),
        col(ATT_W), col(ATT_W), col(N_IDX),
    )
    return pl.pallas_call(
        functools.partial(_mix_in_body, n),
        grid=(steps,),
        in_specs=[row(D_MODEL), _const_spec((1, D_MODEL)), _const_spec((D_MODEL, ROW_COLS_PAD)),
                  _const_spec((COL_ROWS, D_MODEL))],
        out_specs=out_specs,
        out_shape=out_shape,
        compiler_params=pltpu.CompilerParams(dimension_semantics=("arbitrary",), vmem_limit_bytes=VMEM_LIMIT),
        name="mix_in",
    )(h, g, wrow, wcol)


def _ordered_bits_to_f32(u):
    k = u ^ jnp.int32(-2 ** 31)
    b = k ^ ((k >> 31) & jnp.int32(0x7FFFFFFF))
    return lax.bitcast_convert_type(b, F32)


def _attn_body(topk, idx_bits,
               nch_ref, toff_ref,
               qit_ref, wt_ref, qt_ref, vis_ref, ki_ref, k_ref, vt_ref, bias_ref,
               o_ref,
               st_ref, cand_ref, thr_ref, cgt_ref, cge_ref, qicat_ref, qbd_ref, sa_ref, sb_ref, m_ref, l_ref, acc_ref,
               cut_ref, ot_ref):
    j = pl.program_id(1)
    nch = nch_ref[j]
    toff = toff_ref[j]
    kf = float(topk)
    bias_tile_max = (bias_ref.shape[1] - KEY_CHUNK) // LANE - 1

    for h in range(N_IDX):
        qicat_ref[:, h * TQ:(h + 1) * TQ] = qit_ref[h * IDX_DIM:(h + 1) * IDX_DIM, :]
    qbd_ref[...] = jnp.zeros(qbd_ref.shape, BF16)
    for a in range(N_PAIRS):
        qbd_ref[a, 0:HEAD_DIM, 0:TQ] = qt_ref[2 * a * HEAD_DIM:(2 * a + 1) * HEAD_DIM, :]
        qbd_ref[a, HEAD_DIM:2 * HEAD_DIM, TQ:2 * TQ] = qt_ref[(2 * a + 1) * HEAD_DIM:(2 * a + 2) * HEAD_DIM, :]

    def chunk_start(c):
        return pl.multiple_of(c * KEY_CHUNK, KEY_CHUNK)

    def chunk_rows(r0):
        return r0 + lax.broadcasted_iota(I32, (KEY_CHUNK, TQ), 0)

    vis = vis_ref[...]
    cand_ref[...] = jnp.full(cand_ref.shape, -jnp.inf, F32)
    batches = KEY_CHUNK // CAND_BATCH

    def score_chunk(c, carry):
        r0 = chunk_start(c)
        rel = jnp.dot(ki_ref[pl.ds(r0, KEY_CHUNK), :], qicat_ref[...], preferred_element_type=F32)
        score = jnp.zeros((KEY_CHUNK, TQ), F32)
        for h in range(N_IDX):
            score = score + wt_ref[h:h + 1, :] * jnp.maximum(rel[:, h * TQ:(h + 1) * TQ], 0.0)
        score = jnp.where(chunk_rows(r0) < vis, score, -jnp.inf)
        st_ref[pl.ds(r0, KEY_CHUNK), :] = score
        group0 = lax.rem(c, CAND_GROUPS // batches) * batches
        for b in range(batches):
            x = [score[b * CAND_BATCH + 8 * r:b * CAND_BATCH + 8 * r + 8, :] for r in range(CAND_DEPTH)]
            for lo, hi in _SORT_NETWORK:
                x[lo], x[hi] = jnp.maximum(x[lo], x[hi]), jnp.minimum(x[lo], x[hi])
            g0 = pl.multiple_of((group0 + b) * CAND_BATCH, CAND_BATCH)
            t = [jnp.maximum(cand_ref[pl.ds(g0 + 8 * r, 8), :], x[CAND_DEPTH - 1 - r]) for r in range(CAND_DEPTH)]
            d = CAND_DEPTH // 2
            while d >= 1:
                for r in range(CAND_DEPTH):
                    if r & d == 0:
                        t[r], t[r + d] = jnp.maximum(t[r], t[r + d]), jnp.minimum(t[r], t[r + d])
                d //= 2
            for r in range(CAND_DEPTH):
                cand_ref[pl.ds(g0 + 8 * r, 8), :] = t[r]
        return carry

    lax.fori_loop(0, nch, score_chunk, 0)

    def count(ref, n_chunks, indicator):
        def body(c, acc):
            r0 = chunk_start(c)
            ind = indicator(ref[pl.ds(r0, KEY_CHUNK), :], r0)
            part = jnp.sum(ind.reshape(8, KEY_CHUNK // 64, 8, TQ), axis=1)
            return acc + jnp.sum(part, axis=0)

        acc = lax.fori_loop(0, n_chunks, body, jnp.zeros((8, TQ), F32))
        return jnp.sum(acc, axis=0, keepdims=True)

    def select(ref, n_chunks):
        def value_bit(i, res):
            trial = res | lax.shift_left(jnp.int32(1), 31 - i)
            thr_t = _ordered_bits_to_f32(trial)
            cnt = count(ref, n_chunks, lambda blk, r0: jnp.where(blk >= thr_t, 1.0, 0.0))
            return jnp.where(cnt >= kf, trial, res)

        res = lax.fori_loop(0, 32, value_bit, jnp.zeros((1, TQ), I32))
        thr_v = jnp.where((res >> 23) == 0, -F32_MAX, _ordered_bits_to_f32(res))
        thr_ref[...] = thr_v
        cgt_ref[...] = count(ref, n_chunks, lambda blk, r0: jnp.where(blk > thr_v, 1.0, 0.0))
        cge_ref[...] = count(ref, n_chunks, lambda blk, r0: jnp.where(blk >= thr_v, 1.0, 0.0))

    select(cand_ref, cand_ref.shape[0] // KEY_CHUNK)
    kept_min = cand_ref[CAND_BATCH - 8:CAND_BATCH, :]
    for g in range(1, CAND_GROUPS):
        kept_min = jnp.maximum(kept_min, cand_ref[(g + 1) * CAND_BATCH - 8:(g + 1) * CAND_BATCH, :])
    unsafe = jnp.max(kept_min, axis=0, keepdims=True) >= thr_ref[...]

    @pl.when(jnp.max(jnp.where(unsafe, 1.0, 0.0)) > 0.0)
    def _():
        select(st_ref, nch)

    thr = thr_ref[...]

    need = kf - cgt_ref[...]
    excess = cge_ref[...] - kf
    cut_ref[...] = jnp.full((1, TQ), 2 ** 30, I32)

    @pl.when(jnp.max(excess) > 0.0)
    def _():
        def index_bit(i, cut):
            trial = cut | lax.shift_left(jnp.int32(1), idx_bits - 1 - i)
            cnt = count(st_ref, nch, lambda blk, r0: jnp.where(
                blk == thr, jnp.where(chunk_rows(r0) < trial, 1.0, 0.0), 0.0))
            return jnp.where(cnt < need, trial, cut)

        cut = lax.fori_loop(0, idx_bits, index_bit, jnp.zeros((1, TQ), I32))
        cut_ref[...] = jnp.where(excess > 0.0, cut, 2 ** 30)

    cut = cut_ref[...]

    def mask_chunk(c, carry):
        r0 = chunk_start(c)
        blk = st_ref[pl.ds(r0, KEY_CHUNK), :]
        tie = jnp.where(chunk_rows(r0) <= cut, 0.0, NEG)
        st_ref[pl.ds(r0, KEY_CHUNK), :] = jnp.where(blk > thr, 0.0, jnp.where(blk == thr, tie, NEG))
        return carry

    lax.fori_loop(0, nch, mask_chunk, 0)
    st_ref[pl.ds(chunk_start(nch), KEY_CHUNK), :] = jnp.full((KEY_CHUNK, TQ), NEG, F32)

    m_ref[...] = jnp.full(m_ref.shape, NEG, F32)
    l_ref[...] = jnp.zeros(l_ref.shape, F32)
    acc_ref[...] = jnp.zeros(acc_ref.shape, F32)
    last = nch - 1

    def logits(blk, s_ref):
        kblk = k_ref[pl.ds(chunk_start(jnp.minimum(blk, last)), KEY_CHUNK), :]
        for a in range(N_PAIRS):
            s_ref[a] = jnp.dot(kblk[:, a * 2 * HEAD_DIM:(a + 1) * 2 * HEAD_DIM], qbd_ref[a],
                               preferred_element_type=F32)

    def softmax_pv(blk, s_ref):
        msk = st_ref[pl.ds(chunk_start(blk), KEY_CHUNK), :]
        msk2 = jnp.concatenate([msk, msk], axis=1)
        bias_row0 = pl.multiple_of((jnp.clip(4 * blk + toff, -1, bias_tile_max) + 1) * LANE, LANE)
        vblk = jnp.minimum(blk, last)
        for a in range(N_PAIRS):
            s2 = s_ref[a] + msk2 + bias_ref[a, pl.ds(bias_row0, KEY_CHUNK), :]
            m_old = m_ref[a:a + 1, :]
            m_new = jnp.maximum(m_old, jnp.max(s2, axis=0, keepdims=True))
            alpha = jnp.exp2(m_old - m_new)
            p = jnp.exp2(s2 - m_new)
            l_ref[a:a + 1, :] = alpha * l_ref[a:a + 1, :] + jnp.sum(p, axis=0, keepdims=True)
            o2 = jnp.dot(vt_ref[vblk, a * 2 * HEAD_DIM:(a + 1) * 2 * HEAD_DIM, :], p.astype(BF16),
                         preferred_element_type=F32)
            acc_ref[a] = alpha * acc_ref[a] + o2
            m_ref[a:a + 1, :] = m_new

    logits(0, sa_ref)

    def block_pair(i, carry):
        logits(2 * i + 1, sb_ref)
        softmax_pv(2 * i, sa_ref)
        logits(2 * i + 2, sa_ref)
        softmax_pv(2 * i + 1, sb_ref)
        return carry

    lax.fori_loop(0, (nch + 1) // 2, block_pair, 0)

    for a in range(N_PAIRS):
        for e in range(2):
            h = 2 * a + e
            num = acc_ref[a, e * HEAD_DIM:(e + 1) * HEAD_DIM, e * TQ:(e + 1) * TQ]
            ot_ref[h * HEAD_DIM:(h + 1) * HEAD_DIM, :] = num / l_ref[a:a + 1, e * TQ:(e + 1) * TQ]
    o_ref[...] = ot_ref[...].T


def _attention(qit, wt, qt, vis, ki, k, vt, bias, sched, topk, resident):
    s_count, _, nq = qit.shape
    nk = k.shape[1]
    n_tiles = sched[0].shape[0]
    mode = dict(pipeline_mode=pl.Buffered(1)) if resident else {}
    idx_bits = max(1, int(nk - 1).bit_length())
    logit_buf = pltpu.VMEM((N_PAIRS, KEY_CHUNK, 2 * TQ), F32)
    in_specs = [
        pl.BlockSpec((None, ATT_W, TQ), lambda s, j, *_: (s, 0, j)),
        pl.BlockSpec((None, N_IDX, TQ), lambda s, j, *_: (s, 0, j)),
        pl.BlockSpec((None, ATT_W, TQ), lambda s, j, *_: (s, 0, j)),
        pl.BlockSpec((1, TQ), lambda s, j, *_: (0, j)),
        pl.BlockSpec((None, nk, IDX_DIM), lambda s, j, *_: (s, 0, 0), **mode),
        pl.BlockSpec((None, nk, ATT_W), lambda s, j, *_: (s, 0, 0), **mode),
        pl.BlockSpec((None, nk // KEY_CHUNK, ATT_W, KEY_CHUNK), lambda s, j, *_: (s, 0, 0, 0), **mode),
        pl.BlockSpec(bias.shape, lambda s, j, *_: (0, 0, 0), pipeline_mode=pl.Buffered(1)),
    ]
    grid_spec = pltpu.PrefetchScalarGridSpec(
        num_scalar_prefetch=2,
        grid=(s_count, n_tiles),
        in_specs=in_specs,
        out_specs=pl.BlockSpec((None, TQ, ATT_W), lambda s, j, *_: (s, j, 0)),
        scratch_shapes=[
            pltpu.VMEM((nk + KEY_CHUNK, TQ), F32),
            pltpu.VMEM((CAND_GROUPS * CAND_BATCH, TQ), F32),
            pltpu.VMEM((1, TQ), F32),
            pltpu.VMEM((1, TQ), F32),
            pltpu.VMEM((1, TQ), F32),
            pltpu.VMEM((IDX_DIM, N_IDX * TQ), BF16),
            pltpu.VMEM((N_PAIRS, 2 * HEAD_DIM, 2 * TQ), BF16),
            logit_buf, logit_buf,
            pltpu.VMEM((N_PAIRS, 2 * TQ), F32),
            pltpu.VMEM((N_PAIRS, 2 * TQ), F32),
            pltpu.VMEM((N_PAIRS, 2 * HEAD_DIM, 2 * TQ), F32),
            pltpu.VMEM((1, TQ), I32),
            pltpu.VMEM((ATT_W, TQ), F32),
        ],
    )
    return pl.pallas_call(
        functools.partial(_attn_body, topk, idx_bits),
        grid_spec=grid_spec,
        out_shape=jax.ShapeDtypeStruct((s_count, nq, ATT_W), F32),
        compiler_params=pltpu.CompilerParams(dimension_semantics=("arbitrary", "arbitrary"),
                                             vmem_limit_bytes=VMEM_LIMIT),
        name="attn",
    )(*sched, qit, wt, qt, vis, ki, k, vt, bias)


def _s5_body(tt, us_ref, s0r_ref, s0i_ref, ar_ref, ai_ref, bbr_ref, bbi_ref, cr_ref, ci_ref, d_ref, wglu_ref,
             bglu_ref, out_ref, sr_out_ref, si_out_ref, str_ref, sti_ref, bre_ref, bim_ref, xr_ref, xi_ref):
    @pl.when(pl.program_id(1) == 0)
    def _():
        str_ref[...] = s0r_ref[...]
        sti_ref[...] = s0i_ref[...]

    u = us_ref[...]
    ub = u.astype(BF16)
    bre_ref[...] = jnp.dot(ub, bbr_ref[...], preferred_element_type=F32)
    bim_ref[...] = jnp.dot(ub, bbi_ref[...], preferred_element_type=F32)
    ar = ar_ref[...]
    ai = ai_ref[...]

    def step(t, carry):
        sr, si = carry
        nr = ar * sr - ai * si + bre_ref[pl.ds(t, 1), :]
        ni = ar * si + ai * sr + bim_ref[pl.ds(t, 1), :]
        xr_ref[pl.ds(t, 1), :] = nr
        xi_ref[pl.ds(t, 1), :] = ni
        return nr, ni

    sr, si = lax.fori_loop(0, tt, step, (str_ref[...], sti_ref[...]))
    str_ref[...] = sr
    sti_ref[...] = si
    sr_out_ref[...] = sr
    si_out_ref[...] = si
    y = (jnp.dot(xr_ref[...].astype(BF16), cr_ref[...], preferred_element_type=F32)
         - jnp.dot(xi_ref[...].astype(BF16), ci_ref[...], preferred_element_type=F32)
         + d_ref[...] * u)
    g = jax.nn.gelu(y)
    gate = jnp.dot(g.astype(BF16), wglu_ref[...], preferred_element_type=F32) + bglu_ref[...]
    out_ref[...] = g * jax.nn.sigmoid(gate)


def _s5(us, s0r, s0i, ar, ai, bbr, bbi, cr, ci, d, wglu, bglu, tt):
    s_count, t_len, _ = us.shape
    assert t_len % tt == 0
    state = pl.BlockSpec((None, 1, SSM_S), lambda s, t: (s, 0, 0))
    seq = pl.BlockSpec((None, tt, SSM_W), lambda s, t: (s, t, 0))
    return pl.pallas_call(
        functools.partial(_s5_body, tt),
        grid=(s_count, t_len // tt),
        in_specs=[seq, state, state, _const_spec((1, SSM_S)), _const_spec((1, SSM_S)),
                  _const_spec((SSM_W, SSM_S)), _const_spec((SSM_W, SSM_S)),
                  _const_spec((SSM_S, SSM_W)), _const_spec((SSM_S, SSM_W)),
                  _const_spec((1, SSM_W)), _const_spec((SSM_W, SSM_W)), _const_spec((1, SSM_W))],
        out_specs=(seq, state, state),
        out_shape=(jax.ShapeDtypeStruct((s_count, t_len, SSM_W), F32),
                   jax.ShapeDtypeStruct((s_count, 1, SSM_S), F32),
                   jax.ShapeDtypeStruct((s_count, 1, SSM_S), F32)),
        scratch_shapes=[pltpu.VMEM((1, SSM_S), F32), pltpu.VMEM((1, SSM_S), F32),
                        pltpu.VMEM((tt, SSM_S), F32), pltpu.VMEM((tt, SSM_S), F32),
                        pltpu.VMEM((tt, SSM_S), F32), pltpu.VMEM((tt, SSM_S), F32)],
        compiler_params=pltpu.CompilerParams(dimension_semantics=("arbitrary", "arbitrary"),
                                             vmem_limit_bytes=VMEM_LIMIT),
        name="s5",
    )(us, s0r, s0i, ar, ai, bbr, bbi, cr, ci, d, wglu, bglu)


def _mix_out_body(att_ref, ssm_ref, h_ref, woa_ref, wos_ref, gmix_ref, gpre_ref, wg_ref, wu_ref, wd_ref, gpost_ref,
                  o_ref):
    m = (jnp.dot(att_ref[...].astype(BF16), woa_ref[...], preferred_element_type=F32)
         + jnp.dot(ssm_ref[...].astype(BF16), wos_ref[...], preferred_element_type=F32))
    h2 = h_ref[...] + _rms(m, gmix_ref[...])
    y = _swiglu(_rms(h2, gpre_ref[...]), wg_ref, wu_ref, wd_ref)
    o_ref[...] = h2 + 0.5 * _rms(y, gpost_ref[...])


def _mix_out(att, ssm, h, woa, wos, gmix, gpre, wg, wu, wd, gpost):
    n = h.shape[0]
    row = lambda width: pl.BlockSpec((ROW_TILE, width), lambda i: (i, 0))
    return pl.pallas_call(
        _mix_out_body,
        grid=(pl.cdiv(n, ROW_TILE),),
        in_specs=[row(ATT_W), row(SSM_W), row(D_MODEL), _const_spec((ATT_W, D_MODEL)),
                  _const_spec((SSM_W, D_MODEL)), _const_spec((1, D_MODEL)), _const_spec((1, D_MODEL)),
                  _const_spec((D_MODEL, D_FF)), _const_spec((D_MODEL, D_FF)), _const_spec((D_FF, D_MODEL)),
                  _const_spec((1, D_MODEL))],
        out_specs=row(D_MODEL),
        out_shape=jax.ShapeDtypeStruct((n, D_MODEL), F32),
        compiler_params=pltpu.CompilerParams(dimension_semantics=("arbitrary",), vmem_limit_bytes=VMEM_LIMIT),
        name="mix_out_ffn2",
    )(att, ssm, h, woa, wos, gmix, gpre, wg, wu, wd, gpost)


def _rel_bucket_np(rel):
    half = NUM_BUCKETS // 2
    max_exact = half // 2
    n = np.abs(rel).astype(np.int64)
    sq = np.maximum(n, max_exact) ** 2 // (max_exact * max_exact)
    large = max_exact + np.floor(np.log2(sq.astype(np.float64)) + 1e-9).astype(np.int64)
    large = np.minimum(large, half - 1)
    return np.where(rel > 0, half, 0) + np.where(n < max_exact, n, large)


def _pair_bias(rel_bias, n_rows, c0):
    length = n_rows + TQ
    rel = np.arange(length) - (TQ - 1) - LANE - c0
    table = (rel_bias - rel_bias[NUM_BUCKETS // 2 - 1][None, :]) * LOG2E
    t1d = table[_rel_bucket_np(rel)].T
    x = jnp.tile(t1d, (1, TQ))[:, :TQ * (length - 1)].reshape(N_HEADS, TQ, length - 1)
    b = jnp.transpose(x[:, :, TQ - 1:TQ - 1 + n_rows], (0, 2, 1)).reshape(N_PAIRS, 2, n_rows, TQ)
    return jnp.concatenate([b[:, 0], b[:, 1]], axis=-1).astype(F32)


def _prompt_schedule(n, n_pad):
    pos = np.arange(n_pad)
    vis = np.where(pos < N_META, N_META, N_META + CHUNK * ((pos - N_META) // CHUNK + 1))
    vis = np.where(pos < n, np.minimum(vis, n), 0).astype(np.int32)
    n_tiles = -(-n // TQ)
    vmax = vis[:n_tiles * TQ].reshape(n_tiles, TQ).max(axis=1)
    nch = -(-vmax // KEY_CHUNK)
    toff = PROMPT_BIAS_C0 // LANE - np.arange(n_tiles)
    assert ((nch - 1) * (KEY_CHUNK // LANE) + toff).max() <= (PROMPT_BIAS_ROWS - KEY_CHUNK) // LANE - 1
    sched = tuple(jnp.asarray(a, I32) for a in (nch, toff))
    return sched, jnp.asarray(vis[None, :])


def _sample_schedule(past, ts):
    nk = past + ts
    sched = tuple(jnp.asarray([v], I32) for v in (-(-nk // KEY_CHUNK), 0))
    vis = jnp.asarray(np.where(np.arange(TQ) < ts, nk, 0)[None, :], I32)
    return sched, vis


def _s5_params(lam_re, lam_im, log_step, b_re, b_im, c_re, c_im, d_skip):
    dt = jnp.exp(log_step)[:, None]
    mag = jnp.exp(lam_re * dt)
    ab_re, ab_im = mag * jnp.cos(lam_im * dt), mag * jnp.sin(lam_im * dt)
    nr, ni = ab_re - 1.0, ab_im
    den = lam_re * lam_re + lam_im * lam_im
    f_re, f_im = (nr * lam_re + ni * lam_im) / den, (ni * lam_re - nr * lam_im) / den
    bb_re = f_re[..., None] * b_re - f_im[..., None] * b_im
    bb_im = f_re[..., None] * b_im + f_im[..., None] * b_re
    eye = jnp.eye(SSM_GROUPS, dtype=F32)
    bd_in = lambda w: jnp.einsum('gpc,gh->gchp', w, eye).reshape(SSM_W, SSM_S).astype(BF16)
    bd_out = lambda w: jnp.einsum('gcp,gh->gphc', w, eye).reshape(SSM_S, SSM_W).astype(BF16)
    return (ab_re.reshape(1, SSM_S), ab_im.reshape(1, SSM_S), bd_in(bb_re), bd_in(bb_im),
            bd_out(c_re), bd_out(c_im), d_skip.reshape(1, SSM_W))


def kernel(x_prompt, x_sample, cache_k, cache_v, cache_kidx, state_ssm_re, state_ssm_im, meta_tokens, rel_bias,
           ffn1_g_pre, ffn1_w_gate, ffn1_w_up, ffn1_w_down, ffn1_g_post, mix_g_pre, w_in, w_out, mix_g_post,
           lam_re, lam_im, log_step, b_re, b_im, c_re, c_im, d_skip, w_glu, b_glu,
           ffn2_g_pre, ffn2_w_gate, ffn2_w_up, ffn2_w_down, ffn2_g_post):
    depth = ffn1_g_pre.shape[0]
    assert depth == 1
    bp, seq, _ = x_prompt.shape
    assert bp == 1
    ds, ts, _ = x_sample.shape
    past = cache_k.shape[2]
    n_p = N_META + seq
    n_p_pad = pl.cdiv(n_p, ROW_TILE) * ROW_TILE
    n_s = ds * ts
    assert n_s % ROW_TILE == 0 and ts <= TQ
    nk_s_pad = pl.cdiv(past + ts, KEY_CHUNK) * KEY_CHUNK
    l = 0

    row2 = lambda g: g[l].reshape(1, -1)
    w1 = (row2(ffn1_g_pre), ffn1_w_gate[l].astype(BF16), ffn1_w_up[l].astype(BF16), ffn1_w_down[l].astype(BF16),
          row2(ffn1_g_post))
    w2 = (row2(ffn2_g_pre), ffn2_w_gate[l].astype(BF16), ffn2_w_up[l].astype(BF16), ffn2_w_down[l].astype(BF16),
          row2(ffn2_g_post))
    wi = w_in[l]
    o_q, o_k, o_v, o_qi, o_ki, o_wi, o_us = np.cumsum((0, ATT_W, ATT_W, ATT_W, N_IDX * IDX_DIM, IDX_DIM, N_IDX))
    sl = lambda o, w: wi[:, o:o + w]
    wrow = jnp.concatenate([sl(o_k, ATT_W), sl(o_v, ATT_W), sl(o_us, SSM_W), sl(o_ki, IDX_DIM),
                            jnp.zeros((D_MODEL, ROW_COLS_PAD - ROW_COLS), F32)], axis=1).astype(BF16)
    wcol = jnp.concatenate([sl(o_q, ATT_W), sl(o_qi, ATT_W), sl(o_v, ATT_W), sl(o_wi, N_IDX)], axis=1).T.astype(BF16)
    woa, wos = w_out[l][:ATT_W].astype(BF16), w_out[l][ATT_W:].astype(BF16)
    s5w = _s5_params(lam_re[l], lam_im[l], log_step[l], b_re[l], b_im[l], c_re[l], c_im[l], d_skip[l])
    s5w = s5w + (w_glu[l].astype(BF16), b_glu[l].reshape(1, SSM_W))
    bias_tab = rel_bias.astype(F32)

    xp = jnp.concatenate([meta_tokens.astype(F32), x_prompt[0]], axis=0)
    hp = _ffn(xp, *w1)
    kp, vp, kip, usp, kbp, kibp, vtbp, qtp, qitp, wtp = _mix_in(hp, row2(mix_g_pre), wrow, wcol, n_p_pad)
    sched_p, vis_p = _prompt_schedule(n_p, n_p_pad)
    att_p = _attention(qitp[None], wtp[None], qtp[None], vis_p, kibp[None], kbp[None], vtbp[None],
                       _pair_bias(bias_tab, PROMPT_BIAS_ROWS, PROMPT_BIAS_C0), sched_p,
                       min(TOPK_MAX, seq // 4), True)[0]
    zero_state = jnp.zeros((1, 1, SSM_S), F32)
    tt_p = max(t for t in range(8, 513, 8) if n_p % t == 0)
    ssm_p, srp, sip = _s5(usp[None], zero_state, zero_state, *s5w, tt_p)
    yp = _mix_out(att_p, ssm_p[0], hp, woa, wos, row2(mix_g_post), *w2)

    hs = _ffn(x_sample.reshape(n_s, D_MODEL), *w1)
    ks, vs, kis, uss, _, _, _, qts, qits, wts = _mix_in(hs, row2(mix_g_pre), wrow, wcol, n_s)
    pad_k = lambda a: jnp.pad(a, ((0, 0), (0, nk_s_pad - past - ts), (0, 0)))
    k_all = pad_k(jnp.concatenate([cache_k[l].reshape(ds, past, ATT_W), ks.reshape(ds, ts, ATT_W)], 1).astype(BF16))
    v_all = pad_k(jnp.concatenate([cache_v[l].reshape(ds, past, ATT_W), vs.reshape(ds, ts, ATT_W)], 1).astype(BF16))
    ki_all = pad_k(jnp.concatenate([cache_kidx[l], kis.reshape(ds, ts, IDX_DIM)], 1).astype(BF16))
    vt_all = jnp.transpose(v_all.reshape(ds, nk_s_pad // KEY_CHUNK, KEY_CHUNK, ATT_W), (0, 1, 3, 2))
    lanes = lambda a: jnp.pad(jnp.transpose(a.reshape(a.shape[0], ds, ts), (1, 0, 2)), ((0, 0), (0, 0), (0, TQ - ts)))
    sched_s, vis_s = _sample_schedule(past, ts)
    att_s = _attention(lanes(qits), lanes(wts), lanes(qts), vis_s, ki_all, k_all, vt_all,
                       _pair_bias(bias_tab, LANE + nk_s_pad, past), sched_s,
                       min(TOPK_MAX, (past + ts) // 4), False)
    att_s = att_s[:, :ts].reshape(n_s, ATT_W)
    ssm_s, srs, sis = _s5(uss.reshape(ds, ts, SSM_W), state_ssm_re[l].reshape(ds, 1, SSM_S),
                          state_ssm_im[l].reshape(ds, 1, SSM_S), *s5w, ts)
    ys = _mix_out(att_s, ssm_s.reshape(n_s, SSM_W), hs, woa, wos, row2(mix_g_post), *w2)

    heads = lambda a, b, t: a.reshape(1, b, t, N_HEADS, HEAD_DIM)
    state = lambda a, b: a.reshape(1, b, SSM_GROUPS, SSM_P)
    return (yp[N_META:][None], ys.reshape(ds, ts, D_MODEL),
            heads(kp, 1, n_p), heads(vp, 1, n_p), kip.reshape(1, 1, n_p, IDX_DIM), state(srp, 1), state(sip, 1),
            heads(ks, ds, ts), heads(vs, ds, ts), kis.reshape(1, ds, ts, IDX_DIM), state(srs, ds), state(sis, ds))
```

```python
import functools
import math

import numpy as np
import jax
import jax.numpy as jnp
from jax import lax
from jax.experimental import pallas as pl
from jax.experimental.pallas import tpu as pltpu

F32, BF16, I32 = jnp.float32, jnp.bfloat16, jnp.int32

D_MODEL = 1024
CHUNK = 64
N_META = 16
N_HEADS = 8
HEAD_DIM = 64
ATT_W = N_HEADS * HEAD_DIM
N_IDX = 8
IDX_DIM = 64
TOPK_MAX = 256
SSM_GROUPS = 32
SSM_GC = 16
SSM_W = SSM_GROUPS * SSM_GC
SSM_P = 64
SSM_S = SSM_GROUPS * SSM_P
D_FF = 2816
NUM_BUCKETS = 32
EPS = 1e-6

LANE = 128
VMEM_LIMIT = 62 * 1024 * 1024
ROW_TILE = 512
TQ = LANE
KEY_CHUNK = 512
PROMPT_BIAS_C0 = 512
PROMPT_BIAS_ROWS = 1280
NEG = -1e30
LOG2E = math.log2(math.e)
F32_MAX = float(np.finfo(np.float32).max)
N_PAIRS = N_HEADS // 2
SUM_ROWS = 16
CAND_DEPTH = 16
CAND_BATCH = 8 * CAND_DEPTH
CAND_GROUPS = 8


def _batcher_network(n):
    pairs, p = [], 1
    while p < n:
        k = p
        while k >= 1:
            for j in range(k % p, n - k, 2 * k):
                for i in range(min(k, n - j - k)):
                    if (i + j) // (2 * p) == (i + j + k) // (2 * p):
                        pairs.append((i + j, i + j + k))
            k //= 2
        p *= 2
    return pairs


_SORT_NETWORK = _batcher_network(CAND_DEPTH)


def _rms(x, g):
    return x * lax.rsqrt(jnp.mean(x * x, axis=-1, keepdims=True) + EPS) * g


def _swiglu(xn, wg_ref, wu_ref, wd_ref):
    xb = xn.astype(BF16)
    a = jnp.dot(xb, wg_ref[...], preferred_element_type=F32)
    b = jnp.dot(xb, wu_ref[...], preferred_element_type=F32)
    hidden = (a * jax.nn.sigmoid(a) * b).astype(BF16)
    return jnp.dot(hidden, wd_ref[...], preferred_element_type=F32)


def _const_spec(shape):
    nd = len(shape)
    return pl.BlockSpec(shape, lambda *_: (0,) * nd, pipeline_mode=pl.Buffered(1))


def _ffn_body(x_ref, gpre_ref, wg_ref, wu_ref, wd_ref, gpost_ref, o_ref):
    x = x_ref[...]
    y = _swiglu(_rms(x, gpre_ref[...]), wg_ref, wu_ref, wd_ref)
    o_ref[...] = x + 0.5 * _rms(y, gpost_ref[...])


def _ffn(x, gpre, wg, wu, wd, gpost):
    n = x.shape[0]
    row = pl.BlockSpec((ROW_TILE, D_MODEL), lambda i: (i, 0))
    return pl.pallas_call(
        _ffn_body,
        grid=(pl.cdiv(n, ROW_TILE),),
        in_specs=[row, _const_spec((1, D_MODEL)), _const_spec((D_MODEL, D_FF)), _const_spec((D_MODEL, D_FF)),
                  _const_spec((D_FF, D_MODEL)), _const_spec((1, D_MODEL))],
        out_specs=row,
        out_shape=jax.ShapeDtypeStruct((n, D_MODEL), F32),
        compiler_params=pltpu.CompilerParams(dimension_semantics=("arbitrary",), vmem_limit_bytes=VMEM_LIMIT),
        name="ffn1",
    )(x, gpre, wg, wu, wd, gpost)


ROW_COLS = 3 * ATT_W + IDX_DIM
ROW_COLS_PAD = 13 * LANE
COL_ROWS = 3 * ATT_W + N_IDX


def _mix_in_body(n_valid, h_ref, g_ref, wrow_ref, wcol_ref,
                 k_ref, v_ref, ki_ref, us_ref, kb_ref, kib_ref, vtb_ref, qt_ref, qit_ref, wt_ref):
    i = pl.program_id(0)
    u = _rms(h_ref[...], g_ref[...]).astype(BF16)
    z = jnp.dot(u, wrow_ref[...], preferred_element_type=F32)
    zt = lax.dot_general(wcol_ref[...], u, (((1,), (1,)), ((), ())), preferred_element_type=F32)
    k = z[:, 0:ATT_W]
    v = z[:, ATT_W:2 * ATT_W]
    ki = z[:, 3 * ATT_W:3 * ATT_W + IDX_DIM]
    k_ref[...] = k
    v_ref[...] = v
    us_ref[...] = z[:, 2 * ATT_W:3 * ATT_W]
    ki_ref[...] = ki
    row_ok = (i * ROW_TILE + lax.broadcasted_iota(I32, (ROW_TILE, 1), 0)) < n_valid
    col_ok = (i * ROW_TILE + lax.broadcasted_iota(I32, (1, ROW_TILE), 1)) < n_valid
    kb_ref[...] = jnp.where(row_ok, k, 0.0).astype(BF16)
    kib_ref[...] = jnp.where(row_ok, ki, 0.0).astype(BF16)
    zt = jnp.where(col_ok, zt, 0.0)
    qt_ref[...] = (zt[0:ATT_W] * (HEAD_DIM ** -0.5 * LOG2E)).astype(BF16)
    qit_ref[...] = zt[ATT_W:2 * ATT_W].astype(BF16)
    vtb_ref[...] = zt[2 * ATT_W:3 * ATT_W].astype(BF16)
    wt_ref[...] = zt[3 * ATT_W:3 * ATT_W + N_IDX] * ((IDX_DIM ** -0.5) * (N_IDX ** -0.5))


def _mix_in(h, g, wrow, wcol, n_pad):
    n = h.shape[0]
    steps = n_pad // ROW_TILE
    assert steps == pl.cdiv(n, ROW_TILE)

    def row(width):
        return pl.BlockSpec((ROW_TILE, width), lambda i: (i, 0))

    col = lambda rows: pl.BlockSpec((rows, ROW_TILE), lambda i: (0, i))
    out_shape = (
        jax.ShapeDtypeStruct((n, ATT_W), F32), jax.ShapeDtypeStruct((n, ATT_W), F32),
        jax.ShapeDtypeStruct((n, IDX_DIM), F32), jax.ShapeDtypeStruct((n, SSM_W), F32),
        jax.ShapeDtypeStruct((n_pad, ATT_W), BF16), jax.ShapeDtypeStruct((n_pad, IDX_DIM), BF16),
        jax.ShapeDtypeStruct((n_pad // ROW_TILE, ATT_W, ROW_TILE), BF16),
        jax.ShapeDtypeStruct((ATT_W, n_pad), BF16), jax.ShapeDtypeStruct((ATT_W, n_pad), BF16),
        jax.ShapeDtypeStruct((N_IDX, n_pad), F32),
    )
    out_specs = (
        row(ATT_W), row(ATT_W), row(IDX_DIM), row(SSM_W), row(ATT_W), row(IDX_DIM),
        pl.BlockSpec((None, ATT_W, ROW_TILE), lambda i: (i, 0, 0)),
        col(ATT_W), col(ATT_W), col(N_IDX),
    )
    return pl.pallas_call(
        functools.partial(_mix_in_body, n),
        grid=(steps,),
        in_specs=[row(D_MODEL), _const_spec((1, D_MODEL)), _const_spec((D_MODEL, ROW_COLS_PAD)),
                  _const_spec((COL_ROWS, D_MODEL))],
        out_specs=out_specs,
        out_shape=out_shape,
        compiler_params=pltpu.CompilerParams(dimension_semantics=("arbitrary",), vmem_limit_bytes=VMEM_LIMIT),
        name="mix_in",
    )(h, g, wrow, wcol)


def _ordered_bits_to_f32(u):
    k = u ^ jnp.int32(-2 ** 31)
    b = k ^ ((k >> 31) & jnp.int32(0x7FFFFFFF))
    return lax.bitcast_convert_type(b, F32)


def _attn_body(topk, idx_bits,
               nch_ref, toff_ref,
               qit_ref, wt_ref, qt_ref, vis_ref, ki_ref, k_ref, vt_ref, bias_ref,
               o_ref,
               st_ref, cand_ref, thr_ref, cgt_ref, cge_ref, qicat_ref, qbd_ref, sa_ref, sb_ref, m_ref, acc_ref,
               cut_ref, ot_ref):
    j = pl.program_id(1)
    nch = nch_ref[j]
    toff = toff_ref[j]
    kf = float(topk)
    bias_tile_max = (bias_ref.shape[1] - KEY_CHUNK) // LANE - 1

    for h in range(N_IDX):
        qicat_ref[:, h * TQ:(h + 1) * TQ] = qit_ref[h * IDX_DIM:(h + 1) * IDX_DIM, :]
    qbd_ref[...] = jnp.zeros(qbd_ref.shape, BF16)
    for a in range(N_PAIRS):
        qbd_ref[a, 0:HEAD_DIM, 0:TQ] = qt_ref[2 * a * HEAD_DIM:(2 * a + 1) * HEAD_DIM, :]
        qbd_ref[a, HEAD_DIM:2 * HEAD_DIM, TQ:2 * TQ] = qt_ref[(2 * a + 1) * HEAD_DIM:(2 * a + 2) * HEAD_DIM, :]

    def chunk_start(c):
        return pl.multiple_of(c * KEY_CHUNK, KEY_CHUNK)

    def chunk_rows(r0):
        return r0 + lax.broadcasted_iota(I32, (KEY_CHUNK, TQ), 0)

    vis = vis_ref[...]
    cand_ref[...] = jnp.full(cand_ref.shape, -jnp.inf, F32)
    batches = KEY_CHUNK // CAND_BATCH

    def score_chunk(c, carry):
        r0 = chunk_start(c)
        rel = jnp.dot(ki_ref[pl.ds(r0, KEY_CHUNK), :], qicat_ref[...], preferred_element_type=F32)
        score = jnp.zeros((KEY_CHUNK, TQ), F32)
        for h in range(N_IDX):
            score = score + wt_ref[h:h + 1, :] * jnp.maximum(rel[:, h * TQ:(h + 1) * TQ], 0.0)
        score = jnp.where(chunk_rows(r0) < vis, score, -jnp.inf)
        st_ref[pl.ds(r0, KEY_CHUNK), :] = score
        group0 = lax.rem(c, CAND_GROUPS // batches) * batches
        for b in range(batches):
            x = [score[b * CAND_BATCH + 8 * r:b * CAND_BATCH + 8 * r + 8, :] for r in range(CAND_DEPTH)]
            for lo, hi in _SORT_NETWORK:
                x[lo], x[hi] = jnp.maximum(x[lo], x[hi]), jnp.minimum(x[lo], x[hi])
            g0 = pl.multiple_of((group0 + b) * CAND_BATCH, CAND_BATCH)
            t = [jnp.maximum(cand_ref[pl.ds(g0 + 8 * r, 8), :], x[CAND_DEPTH - 1 - r]) for r in range(CAND_DEPTH)]
            d = CAND_DEPTH // 2
            while d >= 1:
                for r in range(CAND_DEPTH):
                    if r & d == 0:
                        t[r], t[r + d] = jnp.maximum(t[r], t[r + d]), jnp.minimum(t[r], t[r + d])
                d //= 2
            for r in range(CAND_DEPTH):
                cand_ref[pl.ds(g0 + 8 * r, 8), :] = t[r]
        return carry

    lax.fori_loop(0, nch, score_chunk, 0)

    def count(ref, n_chunks, indicator):
        def body(c, acc):
            r0 = chunk_start(c)
            ind = indicator(ref[pl.ds(r0, KEY_CHUNK), :], r0)
            part = jnp.sum(ind.reshape(8, KEY_CHUNK // 64, 8, TQ), axis=1)
            return acc + jnp.sum(part, axis=0)

        acc = lax.fori_loop(0, n_chunks, body, jnp.zeros((8, TQ), F32))
        return jnp.sum(acc, axis=0, keepdims=True)

    def select(ref, n_chunks):
        def value_bit(i, res):
            trial = res | lax.shift_left(jnp.int32(1), 31 - i)
            thr_t = _ordered_bits_to_f32(trial)
            cnt = count(ref, n_chunks, lambda blk, r0: jnp.where(blk >= thr_t, 1.0, 0.0))
            return jnp.where(cnt >= kf, trial, res)

        res = lax.fori_loop(0, 32, value_bit, jnp.zeros((1, TQ), I32))
        thr_v = jnp.where((res >> 23) == 0, -F32_MAX, _ordered_bits_to_f32(res))
        thr_ref[...] = thr_v
        cgt_ref[...] = count(ref, n_chunks, lambda blk, r0: jnp.where(blk > thr_v, 1.0, 0.0))
        cge_ref[...] = count(ref, n_chunks, lambda blk, r0: jnp.where(blk >= thr_v, 1.0, 0.0))

    select(cand_ref, cand_ref.shape[0] // KEY_CHUNK)
    kept_min = cand_ref[CAND_BATCH - 8:CAND_BATCH, :]
    for g in range(1, CAND_GROUPS):
        kept_min = jnp.maximum(kept_min, cand_ref[(g + 1) * CAND_BATCH - 8:(g + 1) * CAND_BATCH, :])
    unsafe = jnp.max(kept_min, axis=0, keepdims=True) >= thr_ref[...]

    @pl.when(jnp.max(jnp.where(unsafe, 1.0, 0.0)) > 0.0)
    def _():
        select(st_ref, nch)

    thr = thr_ref[...]

    need = kf - cgt_ref[...]
    excess = cge_ref[...] - kf
    cut_ref[...] = jnp.full((1, TQ), 2 ** 30, I32)

    @pl.when(jnp.max(excess) > 0.0)
    def _():
        def index_bit(i, cut):
            trial = cut | lax.shift_left(jnp.int32(1), idx_bits - 1 - i)
            cnt = count(st_ref, nch, lambda blk, r0: jnp.where(
                blk == thr, jnp.where(chunk_rows(r0) < trial, 1.0, 0.0), 0.0))
            return jnp.where(cnt < need, trial, cut)

        cut = lax.fori_loop(0, idx_bits, index_bit, jnp.zeros((1, TQ), I32))
        cut_ref[...] = jnp.where(excess > 0.0, cut, 2 ** 30)

    cut = cut_ref[...]

    def mask_chunk(c, carry):
        r0 = chunk_start(c)
        blk = st_ref[pl.ds(r0, KEY_CHUNK), :]
        tie = jnp.where(chunk_rows(r0) <= cut, 0.0, NEG)
        st_ref[pl.ds(r0, KEY_CHUNK), :] = jnp.where(blk > thr, 0.0, jnp.where(blk == thr, tie, NEG))
        return carry

    lax.fori_loop(0, nch, mask_chunk, 0)
    st_ref[pl.ds(chunk_start(nch), KEY_CHUNK), :] = jnp.full((KEY_CHUNK, TQ), NEG, F32)

    m_ref[...] = jnp.full(m_ref.shape, NEG, F32)
    acc_ref[...] = jnp.zeros(acc_ref.shape, F32)
    last = nch - 1
    ones_rows = jnp.ones((SUM_ROWS, KEY_CHUNK), BF16)

    def logits(blk, s_ref):
        kblk = k_ref[pl.ds(chunk_start(jnp.minimum(blk, last)), KEY_CHUNK), :]
        for a in range(N_PAIRS):
            s_ref[a] = jnp.dot(kblk[:, a * 2 * HEAD_DIM:(a + 1) * 2 * HEAD_DIM], qbd_ref[a],
                               preferred_element_type=F32)

    def softmax_pv(blk, s_ref, with_bias):
        msk = st_ref[pl.ds(chunk_start(blk), KEY_CHUNK), :]
        msk2 = jnp.concatenate([msk, msk], axis=1)
        bias_row0 = pl.multiple_of((jnp.clip(4 * blk + toff, -1, bias_tile_max) + 1) * LANE, LANE)
        vblk = jnp.minimum(blk, last)
        for a in range(N_PAIRS):
            s2 = s_ref[a] + msk2
            if with_bias:
                s2 = s2 + bias_ref[a, pl.ds(bias_row0, KEY_CHUNK), :]
            m_old = m_ref[a:a + 1, :]
            m_new = jnp.maximum(m_old, jnp.max(s2, axis=0, keepdims=True))
            alpha = jnp.exp2(m_old - m_new)
            p = jnp.exp2(s2 - m_new).astype(BF16)
            v_ones = jnp.concatenate([vt_ref[vblk, a * 2 * HEAD_DIM:(a + 1) * 2 * HEAD_DIM, :], ones_rows], axis=0)
            acc_ref[a] = alpha * acc_ref[a] + jnp.dot(v_ones, p, preferred_element_type=F32)
            m_ref[a:a + 1, :] = m_new

    logits(0, sa_ref)

    def block_pair(with_bias, i, carry):
        logits(2 * i + 1, sb_ref)
        softmax_pv(2 * i, sa_ref, with_bias)
        logits(2 * i + 2, sa_ref)
        softmax_pv(2 * i + 1, sb_ref, with_bias)
        return carry

    far_pairs = jnp.minimum(jnp.maximum(3 - toff, 0) // 4, nch) // 2
    lax.fori_loop(0, far_pairs, functools.partial(block_pair, False), 0)
    lax.fori_loop(far_pairs, (nch + 1) // 2, functools.partial(block_pair, True), 0)

    for a in range(N_PAIRS):
        for e in range(2):
            h = 2 * a + e
            num = acc_ref[a, e * HEAD_DIM:(e + 1) * HEAD_DIM, e * TQ:(e + 1) * TQ]
            den = acc_ref[a, 2 * HEAD_DIM:2 * HEAD_DIM + 1, e * TQ:(e + 1) * TQ]
            ot_ref[h * HEAD_DIM:(h + 1) * HEAD_DIM, :] = num / den
    o_ref[...] = ot_ref[...].T


def _attention(qit, wt, qt, vis, ki, k, vt, bias, sched, topk, resident):
    s_count, _, nq = qit.shape
    nk = k.shape[1]
    n_tiles = sched[0].shape[0]
    mode = dict(pipeline_mode=pl.Buffered(1)) if resident else {}
    idx_bits = max(1, int(nk - 1).bit_length())
    logit_buf = pltpu.VMEM((N_PAIRS, KEY_CHUNK, 2 * TQ), F32)
    in_specs = [
        pl.BlockSpec((None, ATT_W, TQ), lambda s, j, *_: (s, 0, j)),
        pl.BlockSpec((None, N_IDX, TQ), lambda s, j, *_: (s, 0, j)),
        pl.BlockSpec((None, ATT_W, TQ), lambda s, j, *_: (s, 0, j)),
        pl.BlockSpec((1, TQ), lambda s, j, *_: (0, j)),
        pl.BlockSpec((None, nk, IDX_DIM), lambda s, j, *_: (s, 0, 0), **mode),
        pl.BlockSpec((None, nk, ATT_W), lambda s, j, *_: (s, 0, 0), **mode),
        pl.BlockSpec((None, nk // KEY_CHUNK, ATT_W, KEY_CHUNK), lambda s, j, *_: (s, 0, 0, 0), **mode),
        pl.BlockSpec(bias.shape, lambda s, j, *_: (0, 0, 0), pipeline_mode=pl.Buffered(1)),
    ]
    grid_spec = pltpu.PrefetchScalarGridSpec(
        num_scalar_prefetch=2,
        grid=(s_count, n_tiles),
        in_specs=in_specs,
        out_specs=pl.BlockSpec((None, TQ, ATT_W), lambda s, j, *_: (s, j, 0)),
        scratch_shapes=[
            pltpu.VMEM((nk + KEY_CHUNK, TQ), F32),
            pltpu.VMEM((CAND_GROUPS * CAND_BATCH, TQ), F32),
            pltpu.VMEM((1, TQ), F32),
            pltpu.VMEM((1, TQ), F32),
            pltpu.VMEM((1, TQ), F32),
            pltpu.VMEM((IDX_DIM, N_IDX * TQ), BF16),
            pltpu.VMEM((N_PAIRS, 2 * HEAD_DIM, 2 * TQ), BF16),
            logit_buf, logit_buf,
            pltpu.VMEM((N_PAIRS, 2 * TQ), F32),
            pltpu.VMEM((N_PAIRS, 2 * HEAD_DIM + SUM_ROWS, 2 * TQ), F32),
            pltpu.VMEM((1, TQ), I32),
            pltpu.VMEM((ATT_W, TQ), F32),
        ],
    )
    return pl.pallas_call(
        functools.partial(_attn_body, topk, idx_bits),
        grid_spec=grid_spec,
        out_shape=jax.ShapeDtypeStruct((s_count, nq, ATT_W), F32),
        compiler_params=pltpu.CompilerParams(dimension_semantics=("arbitrary", "arbitrary"),
                                             vmem_limit_bytes=VMEM_LIMIT),
        name="attn",
    )(*sched, qit, wt, qt, vis, ki, k, vt, bias)


def _s5_body(tt, us_ref, s0r_ref, s0i_ref, ar_ref, ai_ref, bbr_ref, bbi_ref, cr_ref, ci_ref, d_ref, wglu_ref,
             bglu_ref, out_ref, sr_out_ref, si_out_ref, str_ref, sti_ref, bre_ref, bim_ref, xr_ref, xi_ref):
    @pl.when(pl.program_id(1) == 0)
    def _():
        str_ref[...] = s0r_ref[...]
        sti_ref[...] = s0i_ref[...]

    u = us_ref[...]
    ub = u.astype(BF16)
    bre_ref[...] = jnp.dot(ub, bbr_ref[...], preferred_element_type=F32)
    bim_ref[...] = jnp.dot(ub, bbi_ref[...], preferred_element_type=F32)
    ar = ar_ref[...]
    ai = ai_ref[...]

    def step(t, carry):
        sr, si = carry
        nr = ar * sr - ai * si + bre_ref[pl.ds(t, 1), :]
        ni = ar * si + ai * sr + bim_ref[pl.ds(t, 1), :]
        xr_ref[pl.ds(t, 1), :] = nr
        xi_ref[pl.ds(t, 1), :] = ni
        return nr, ni

    sr, si = lax.fori_loop(0, tt, step, (str_ref[...], sti_ref[...]))
    str_ref[...] = sr
    sti_ref[...] = si
    sr_out_ref[...] = sr
    si_out_ref[...] = si
    y = (jnp.dot(xr_ref[...].astype(BF16), cr_ref[...], preferred_element_type=F32)
         - jnp.dot(xi_ref[...].astype(BF16), ci_ref[...], preferred_element_type=F32)
         + d_ref[...] * u)
    g = jax.nn.gelu(y)
    gate = jnp.dot(g.astype(BF16), wglu_ref[...], preferred_element_type=F32) + bglu_ref[...]
    out_ref[...] = g * jax.nn.sigmoid(gate)


def _s5(us, s0r, s0i, ar, ai, bbr, bbi, cr, ci, d, wglu, bglu, tt):
    s_count, t_len, _ = us.shape
    assert t_len % tt == 0
    state = pl.BlockSpec((None, 1, SSM_S), lambda s, t: (s, 0, 0))
    seq = pl.BlockSpec((None, tt, SSM_W), lambda s, t: (s, t, 0))
    return pl.pallas_call(
        functools.partial(_s5_body, tt),
        grid=(s_count, t_len // tt),
        in_specs=[seq, state, state, _const_spec((1, SSM_S)), _const_spec((1, SSM_S)),
                  _const_spec((SSM_W, SSM_S)), _const_spec((SSM_W, SSM_S)),
                  _const_spec((SSM_S, SSM_W)), _const_spec((SSM_S, SSM_W)),
                  _const_spec((1, SSM_W)), _const_spec((SSM_W, SSM_W)), _const_spec((1, SSM_W))],
        out_specs=(seq, state, state),
        out_shape=(jax.ShapeDtypeStruct((s_count, t_len, SSM_W), F32),
                   jax.ShapeDtypeStruct((s_count, 1, SSM_S), F32),
                   jax.ShapeDtypeStruct((s_count, 1, SSM_S), F32)),
        scratch_shapes=[pltpu.VMEM((1, SSM_S), F32), pltpu.VMEM((1, SSM_S), F32),
                        pltpu.VMEM((tt, SSM_S), F32), pltpu.VMEM((tt, SSM_S), F32),
                        pltpu.VMEM((tt, SSM_S), F32), pltpu.VMEM((tt, SSM_S), F32)],
        compiler_params=pltpu.CompilerParams(dimension_semantics=("arbitrary", "arbitrary"),
                                             vmem_limit_bytes=VMEM_LIMIT),
        name="s5",
    )(us, s0r, s0i, ar, ai, bbr, bbi, cr, ci, d, wglu, bglu)


def _mix_out_body(att_ref, ssm_ref, h_ref, woa_ref, wos_ref, gmix_ref, gpre_ref, wg_ref, wu_ref, wd_ref, gpost_ref,
                  o_ref):
    m = (jnp.dot(att_ref[...].astype(BF16), woa_ref[...], preferred_element_type=F32)
         + jnp.dot(ssm_ref[...].astype(BF16), wos_ref[...], preferred_element_type=F32))
    h2 = h_ref[...] + _rms(m, gmix_ref[...])
    y = _swiglu(_rms(h2, gpre_ref[...]), wg_ref, wu_ref, wd_ref)
    o_ref[...] = h2 + 0.5 * _rms(y, gpost_ref[...])


def _mix_out(att, ssm, h, woa, wos, gmix, gpre, wg, wu, wd, gpost):
    n = h.shape[0]
    row = lambda width: pl.BlockSpec((ROW_TILE, width), lambda i: (i, 0))
    return pl.pallas_call(
        _mix_out_body,
        grid=(pl.cdiv(n, ROW_TILE),),
        in_specs=[row(ATT_W), row(SSM_W), row(D_MODEL), _const_spec((ATT_W, D_MODEL)),
                  _const_spec((SSM_W, D_MODEL)), _const_spec((1, D_MODEL)), _const_spec((1, D_MODEL)),
                  _const_spec((D_MODEL, D_FF)), _const_spec((D_MODEL, D_FF)), _const_spec((D_FF, D_MODEL)),
                  _const_spec((1, D_MODEL))],
        out_specs=row(D_MODEL),
        out_shape=jax.ShapeDtypeStruct((n, D_MODEL), F32),
        compiler_params=pltpu.CompilerParams(dimension_semantics=("arbitrary",), vmem_limit_bytes=VMEM_LIMIT),
        name="mix_out_ffn2",
    )(att, ssm, h, woa, wos, gmix, gpre, wg, wu, wd, gpost)


def _rel_bucket_np(rel):
    half = NUM_BUCKETS // 2
    max_exact = half // 2
    n = np.abs(rel).astype(np.int64)
    sq = np.maximum(n, max_exact) ** 2 // (max_exact * max_exact)
    large = max_exact + np.floor(np.log2(sq.astype(np.float64)) + 1e-9).astype(np.int64)
    large = np.minimum(large, half - 1)
    return np.where(rel > 0, half, 0) + np.where(n < max_exact, n, large)


def _pair_bias(rel_bias, n_rows, c0):
    length = n_rows + TQ
    rel = np.arange(length) - (TQ - 1) - LANE - c0
    table = (rel_bias - rel_bias[NUM_BUCKETS // 2 - 1][None, :]) * LOG2E
    t1d = table[_rel_bucket_np(rel)].T
    x = jnp.tile(t1d, (1, TQ))[:, :TQ * (length - 1)].reshape(N_HEADS, TQ, length - 1)
    b = jnp.transpose(x[:, :, TQ - 1:TQ - 1 + n_rows], (0, 2, 1)).reshape(N_PAIRS, 2, n_rows, TQ)
    return jnp.concatenate([b[:, 0], b[:, 1]], axis=-1).astype(F32)


def _prompt_schedule(n, n_pad):
    pos = np.arange(n_pad)
    vis = np.where(pos < N_META, N_META, N_META + CHUNK * ((pos - N_META) // CHUNK + 1))
    vis = np.where(pos < n, np.minimum(vis, n), 0).astype(np.int32)
    n_tiles = -(-n // TQ)
    vmax = vis[:n_tiles * TQ].reshape(n_tiles, TQ).max(axis=1)
    nch = -(-vmax // KEY_CHUNK)
    toff = PROMPT_BIAS_C0 // LANE - np.arange(n_tiles)
    assert ((nch - 1) * (KEY_CHUNK // LANE) + toff).max() <= (PROMPT_BIAS_ROWS - KEY_CHUNK) // LANE - 1
    sched = tuple(jnp.asarray(a, I32) for a in (nch, toff))
    return sched, jnp.asarray(vis[None, :])


def _sample_schedule(past, ts):
    nk = past + ts
    sched = tuple(jnp.asarray([v], I32) for v in (-(-nk // KEY_CHUNK), 0))
    vis = jnp.asarray(np.where(np.arange(TQ) < ts, nk, 0)[None, :], I32)
    return sched, vis


def _s5_params(lam_re, lam_im, log_step, b_re, b_im, c_re, c_im, d_skip):
    dt = jnp.exp(log_step)[:, None]
    mag = jnp.exp(lam_re * dt)
    ab_re, ab_im = mag * jnp.cos(lam_im * dt), mag * jnp.sin(lam_im * dt)
    nr, ni = ab_re - 1.0, ab_im
    den = lam_re * lam_re + lam_im * lam_im
    f_re, f_im = (nr * lam_re + ni * lam_im) / den, (ni * lam_re - nr * lam_im) / den
    bb_re = f_re[..., None] * b_re - f_im[..., None] * b_im
    bb_im = f_re[..., None] * b_im + f_im[..., None] * b_re
    eye = jnp.eye(SSM_GROUPS, dtype=F32)
    bd_in = lambda w: jnp.einsum('gpc,gh->gchp', w, eye).reshape(SSM_W, SSM_S).astype(BF16)
    bd_out = lambda w: jnp.einsum('gcp,gh->gphc', w, eye).reshape(SSM_S, SSM_W).astype(BF16)
    return (ab_re.reshape(1, SSM_S), ab_im.reshape(1, SSM_S), bd_in(bb_re), bd_in(bb_im),
            bd_out(c_re), bd_out(c_im), d_skip.reshape(1, SSM_W))


def kernel(x_prompt, x_sample, cache_k, cache_v, cache_kidx, state_ssm_re, state_ssm_im, meta_tokens, rel_bias,
           ffn1_g_pre, ffn1_w_gate, ffn1_w_up, ffn1_w_down, ffn1_g_post, mix_g_pre, w_in, w_out, mix_g_post,
           lam_re, lam_im, log_step, b_re, b_im, c_re, c_im, d_skip, w_glu, b_glu,
           ffn2_g_pre, ffn2_w_gate, ffn2_w_up, ffn2_w_down, ffn2_g_post):
    depth = ffn1_g_pre.shape[0]
    assert depth == 1
    bp, seq, _ = x_prompt.shape
    assert bp == 1
    ds, ts, _ = x_sample.shape
    past = cache_k.shape[2]
    n_p = N_META + seq
    n_p_pad = pl.cdiv(n_p, ROW_TILE) * ROW_TILE
    n_s = ds * ts
    assert n_s % ROW_TILE == 0 and ts <= TQ
    nk_s_pad = pl.cdiv(past + ts, KEY_CHUNK) * KEY_CHUNK
    l = 0

    row2 = lambda g: g[l].reshape(1, -1)
    w1 = (row2(ffn1_g_pre), ffn1_w_gate[l].astype(BF16), ffn1_w_up[l].astype(BF16), ffn1_w_down[l].astype(BF16),
          row2(ffn1_g_post))
    w2 = (row2(ffn2_g_pre), ffn2_w_gate[l].astype(BF16), ffn2_w_up[l].astype(BF16), ffn2_w_down[l].astype(BF16),
          row2(ffn2_g_post))
    wi = w_in[l]
    o_q, o_k, o_v, o_qi, o_ki, o_wi, o_us = np.cumsum((0, ATT_W, ATT_W, ATT_W, N_IDX * IDX_DIM, IDX_DIM, N_IDX))
    sl = lambda o, w: wi[:, o:o + w]
    wrow = jnp.concatenate([sl(o_k, ATT_W), sl(o_v, ATT_W), sl(o_us, SSM_W), sl(o_ki, IDX_DIM),
                            jnp.zeros((D_MODEL, ROW_COLS_PAD - ROW_COLS), F32)], axis=1).astype(BF16)
    wcol = jnp.concatenate([sl(o_q, ATT_W), sl(o_qi, ATT_W), sl(o_v, ATT_W), sl(o_wi, N_IDX)], axis=1).T.astype(BF16)
    woa, wos = w_out[l][:ATT_W].astype(BF16), w_out[l][ATT_W:].astype(BF16)
    s5w = _s5_params(lam_re[l], lam_im[l], log_step[l], b_re[l], b_im[l], c_re[l], c_im[l], d_skip[l])
    s5w = s5w + (w_glu[l].astype(BF16), b_glu[l].reshape(1, SSM_W))
    bias_tab = rel_bias.astype(F32)

    xp = jnp.concatenate([meta_tokens.astype(F32), x_prompt[0]], axis=0)
    hp = _ffn(xp, *w1)
    kp, vp, kip, usp, kbp, kibp, vtbp, qtp, qitp, wtp = _mix_in(hp, row2(mix_g_pre), wrow, wcol, n_p_pad)
    sched_p, vis_p = _prompt_schedule(n_p, n_p_pad)
    att_p = _attention(qitp[None], wtp[None], qtp[None], vis_p, kibp[None], kbp[None], vtbp[None],
                       _pair_bias(bias_tab, PROMPT_BIAS_ROWS, PROMPT_BIAS_C0), sched_p,
                       min(TOPK_MAX, seq // 4), True)[0]
    zero_state = jnp.zeros((1, 1, SSM_S), F32)
    tt_p = max(t for t in range(8, 513, 8) if n_p % t == 0)
    ssm_p, srp, sip = _s5(usp[None], zero_state, zero_state, *s5w, tt_p)
    yp = _mix_out(att_p, ssm_p[0], hp, woa, wos, row2(mix_g_post), *w2)

    hs = _ffn(x_sample.reshape(n_s, D_MODEL), *w1)
    ks, vs, kis, uss, _, _, _, qts, qits, wts = _mix_in(hs, row2(mix_g_pre), wrow, wcol, n_s)
    pad_k = lambda a: jnp.pad(a, ((0, 0), (0, nk_s_pad - past - ts), (0, 0)))
    k_all = pad_k(jnp.concatenate([cache_k[l].reshape(ds, past, ATT_W), ks.reshape(ds, ts, ATT_W)], 1).astype(BF16))
    v_all = pad_k(jnp.concatenate([cache_v[l].reshape(ds, past, ATT_W), vs.reshape(ds, ts, ATT_W)], 1).astype(BF16))
    ki_all = pad_k(jnp.concatenate([cache_kidx[l], kis.reshape(ds, ts, IDX_DIM)], 1).astype(BF16))
    vt_all = jnp.transpose(v_all.reshape(ds, nk_s_pad // KEY_CHUNK, KEY_CHUNK, ATT_W), (0, 1, 3, 2))
    lanes = lambda a: jnp.pad(jnp.transpose(a.reshape(a.shape[0], ds, ts), (1, 0, 2)), ((0, 0), (0, 0), (0, TQ - ts)))
    sched_s, vis_s = _sample_schedule(past, ts)
    att_s = _attention(lanes(qits), lanes(wts), lanes(qts), vis_s, ki_all, k_all, vt_all,
                       _pair_bias(bias_tab, LANE + nk_s_pad, past), sched_s,
                       min(TOPK_MAX, (past + ts) // 4), False)
    att_s = att_s[:, :ts].reshape(n_s, ATT_W)
    ssm_s, srs, sis = _s5(uss.reshape(ds, ts, SSM_W), state_ssm_re[l].reshape(ds, 1, SSM_S),
                          state_ssm_im[l].reshape(ds, 1, SSM_S), *s5w, ts)
    ys = _mix_out(att_s, ssm_s.reshape(n_s, SSM_W), hs, woa, wos, row2(mix_g_post), *w2)

    heads = lambda a, b, t: a.reshape(1, b, t, N_HEADS, HEAD_DIM)
    state = lambda a, b: a.reshape(1, b, SSM_GROUPS, SSM_P)
    return (yp[N_META:][None], ys.reshape(ds, ts, D_MODEL),
            heads(kp, 1, n_p), heads(vp, 1, n_p), kip.reshape(1, 1, n_p, IDX_DIM), state(srp, 1), state(sip, 1),
            heads(ks, ds, ts), heads(vs, ds, ts), kis.reshape(1, ds, ts, IDX_DIM), state(srs, ds), state(sis, ds))
```

```python
import functools
import math

import numpy as np
import jax
import jax.numpy as jnp
from jax import lax
from jax.experimental import pallas as pl
from jax.experimental.pallas import tpu as pltpu

F32, BF16, I32 = jnp.float32, jnp.bfloat16, jnp.int32

D_MODEL = 1024
CHUNK = 64
N_META = 16
N_HEADS = 8
HEAD_DIM = 64
ATT_W = N_HEADS * HEAD_DIM
N_IDX = 8
IDX_DIM = 64
TOPK_MAX = 256
SSM_GROUPS = 32
SSM_GC = 16
SSM_W = SSM_GROUPS * SSM_GC
SSM_P = 64
SSM_S = SSM_GROUPS * SSM_P
D_FF = 2816
NUM_BUCKETS = 32
EPS = 1e-6

LANE = 128
VMEM_LIMIT = 62 * 1024 * 1024
ROW_TILE = 512
TQ = LANE
KEY_CHUNK = 512
PROMPT_BIAS_C0 = 512
PROMPT_BIAS_ROWS = 1280
NEG = -1e30
LOG2E = math.log2(math.e)
F32_MAX = float(np.finfo(np.float32).max)
N_PAIRS = N_HEADS // 2
SUM_ROWS = 16
CAND_DEPTH = 16
CAND_BATCH = 8 * CAND_DEPTH
CAND_GROUPS = 8


def _batcher_network(n):
    pairs, p = [], 1
    while p < n:
        k = p
        while k >= 1:
            for j in range(k % p, n - k, 2 * k):
                for i in range(min(k, n - j - k)):
                    if (i + j) // (2 * p) == (i + j + k) // (2 * p):
                        pairs.append((i + j, i + j + k))
            k //= 2
        p *= 2
    return pairs


_SORT_NETWORK = _batcher_network(CAND_DEPTH)


def _rms(x, g):
    return x * lax.rsqrt(jnp.mean(x * x, axis=-1, keepdims=True) + EPS) * g


def _swiglu(xn, wg_ref, wu_ref, wd_ref):
    xb = xn.astype(BF16)
    a = jnp.dot(xb, wg_ref[...], preferred_element_type=F32)
    b = jnp.dot(xb, wu_ref[...], preferred_element_type=F32)
    hidden = (a * jax.nn.sigmoid(a) * b).astype(BF16)
    return jnp.dot(hidden, wd_ref[...], preferred_element_type=F32)


def _const_spec(shape):
    nd = len(shape)
    return pl.BlockSpec(shape, lambda *_: (0,) * nd, pipeline_mode=pl.Buffered(1))


def _ffn_body(x_ref, gpre_ref, wg_ref, wu_ref, wd_ref, gpost_ref, o_ref):
    x = x_ref[...]
    y = _swiglu(_rms(x, gpre_ref[...]), wg_ref, wu_ref, wd_ref)
    o_ref[...] = x + 0.5 * _rms(y, gpost_ref[...])


def _ffn(x, gpre, wg, wu, wd, gpost):
    n = x.shape[0]
    row = pl.BlockSpec((ROW_TILE, D_MODEL), lambda i: (i, 0))
    return pl.pallas_call(
        _ffn_body,
        grid=(pl.cdiv(n, ROW_TILE),),
        in_specs=[row, _const_spec((1, D_MODEL)), _const_spec((D_MODEL, D_FF)), _const_spec((D_MODEL, D_FF)),
                  _const_spec((D_FF, D_MODEL)), _const_spec((1, D_MODEL))],
        out_specs=row,
        out_shape=jax.ShapeDtypeStruct((n, D_MODEL), F32),
        compiler_params=pltpu.CompilerParams(dimension_semantics=("arbitrary",), vmem_limit_bytes=VMEM_LIMIT),
        name="ffn1",
    )(x, gpre, wg, wu, wd, gpost)


ROW_COLS = 3 * ATT_W + IDX_DIM
ROW_COLS_PAD = 13 * LANE
COL_ROWS = 3 * ATT_W + N_IDX


def _mix_in_body(n_valid, h_ref, g_ref, wrow_ref, wcol_ref,
                 k_ref, v_ref, ki_ref, us_ref, kb_ref, kib_ref, vtb_ref, qt_ref, qit_ref, wt_ref):
    i = pl.program_id(0)
    u = _rms(h_ref[...], g_ref[...]).astype(BF16)
    z = jnp.dot(u, wrow_ref[...], preferred_element_type=F32)
    zt = lax.dot_general(wcol_ref[...], u, (((1,), (1,)), ((), ())), preferred_element_type=F32)
    k = z[:, 0:ATT_W]
    v = z[:, ATT_W:2 * ATT_W]
    ki = z[:, 3 * ATT_W:3 * ATT_W + IDX_DIM]
    k_ref[...] = k
    v_ref[...] = v
    us_ref[...] = z[:, 2 * ATT_W:3 * ATT_W]
    ki_ref[...] = ki
    row_ok = (i * ROW_TILE + lax.broadcasted_iota(I32, (ROW_TILE, 1), 0)) < n_valid
    col_ok = (i * ROW_TILE + lax.broadcasted_iota(I32, (1, ROW_TILE), 1)) < n_valid
    kb_ref[...] = jnp.where(row_ok, k, 0.0).astype(BF16)
    kib_ref[...] = jnp.where(row_ok, ki, 0.0).astype(BF16)
    zt = jnp.where(col_ok, zt, 0.0)
    qt_ref[...] = (zt[0:ATT_W] * (HEAD_DIM ** -0.5 * LOG2E)).astype(BF16)
    vtb_ref[...] = zt[2 * ATT_W:3 * ATT_W].astype(BF16)
    wt = zt[3 * ATT_W:3 * ATT_W + N_IDX] * ((IDX_DIM ** -0.5) * (N_IDX ** -0.5))
    wt_ref[...] = wt
    for h in range(N_IDX):
        qi_h = zt[ATT_W + h * IDX_DIM:ATT_W + (h + 1) * IDX_DIM]
        qit_ref[h * IDX_DIM:(h + 1) * IDX_DIM, :] = (qi_h * wt[h:h + 1, :]).astype(BF16)


def _mix_in(h, g, wrow, wcol, n_pad):
    n = h.shape[0]
    steps = n_pad // ROW_TILE
    assert steps == pl.cdiv(n, ROW_TILE)

    def row(width):
        return pl.BlockSpec((ROW_TILE, width), lambda i: (i, 0))

    col = lambda rows: pl.BlockSpec((rows, ROW_TILE), lambda i: (0, i))
    out_shape = (
        jax.ShapeDtypeStruct((n, ATT_W), F32), jax.ShapeDtypeStruct((n, ATT_W), F32),
        jax.ShapeDtypeStruct((n, IDX_DIM), F32), jax.ShapeDtypeStruct((n, SSM_W), F32),
        jax.ShapeDtypeStruct((n_pad, ATT_W), BF16), jax.ShapeDtypeStruct((n_pad, IDX_DIM), BF16),
        jax.ShapeDtypeStruct((n_pad // ROW_TILE, ATT_W, ROW_TILE), BF16),
        jax.ShapeDtypeStruct((ATT_W, n_pad), BF16), jax.ShapeDtypeStruct((ATT_W, n_pad), BF16),
        jax.ShapeDtypeStruct((N_IDX, n_pad), F32),
    )
    out_specs = (
        row(ATT_W), row(ATT_W), row(IDX_DIM), row(SSM_W), row(ATT_W), row(IDX_DIM),
        pl.BlockSpec((None, ATT_W, ROW_TILE), lambda i: (i, 0, 0)),
        col(ATT_W), col(ATT_W), col(N_IDX),
    )
    return pl.pallas_call(
        functools.partial(_mix_in_body, n),
        grid=(steps,),
        in_specs=[row(D_MODEL), _const_spec((1, D_MODEL)), _const_spec((D_MODEL, ROW_COLS_PAD)),
                  _const_spec((COL_ROWS, D_MODEL))],
        out_specs=out_specs,
        out_shape=out_shape,
        compiler_params=pltpu.CompilerParams(dimension_semantics=("arbitrary",), vmem_limit_bytes=VMEM_LIMIT),
        name="mix_in",
    )(h, g, wrow, wcol)


def _ordered_bits_to_f32(u):
    k = u ^ jnp.int32(-2 ** 31)
    b = k ^ ((k >> 31) & jnp.int32(0x7FFFFFFF))
    return lax.bitcast_convert_type(b, F32)


def _attn_body(topk, idx_bits,
               nch_ref, nfull_ref, toff_ref,
               qit_ref, wt_ref, qt_ref, vis_ref, ki_ref, k_ref, vt_ref, bias_ref,
               o_ref,
               st_ref, cand_ref, thr_ref, cgt_ref, cge_ref, clip_ref, qicat_ref, qbd_ref, sa_ref, sb_ref,
               bma_ref, bmb_ref,
               m_ref, acc_ref, cut_ref, ot_ref):
    j = pl.program_id(1)
    nch = nch_ref[j]
    toff = toff_ref[j]
    kf = float(topk)
    bias_tile_max = (bias_ref.shape[1] - KEY_CHUNK) // LANE - 1

    for h in range(N_IDX):
        qicat_ref[:, h * TQ:(h + 1) * TQ] = qit_ref[h * IDX_DIM:(h + 1) * IDX_DIM, :]
    qbd_ref[...] = jnp.zeros(qbd_ref.shape, BF16)
    for a in range(N_PAIRS):
        qbd_ref[a, 0:HEAD_DIM, 0:TQ] = qt_ref[2 * a * HEAD_DIM:(2 * a + 1) * HEAD_DIM, :]
        qbd_ref[a, HEAD_DIM:2 * HEAD_DIM, TQ:2 * TQ] = qt_ref[(2 * a + 1) * HEAD_DIM:(2 * a + 2) * HEAD_DIM, :]

    def chunk_start(c):
        return pl.multiple_of(c * KEY_CHUNK, KEY_CHUNK)

    def chunk_rows(r0):
        return r0 + lax.broadcasted_iota(I32, (KEY_CHUNK, TQ), 0)

    vis = vis_ref[...]
    cand_ref[...] = jnp.full(cand_ref.shape, -jnp.inf, F32)
    w_pos = wt_ref[...] >= 0.0
    clip_ref[0:N_IDX, :] = jnp.where(w_pos, 0.0, -jnp.inf)
    clip_ref[N_IDX:2 * N_IDX, :] = jnp.where(w_pos, jnp.inf, 0.0)

    last_batch_row = ki_ref.shape[0] - CAND_BATCH

    def score_round(masked, i, carry):
        for g in range(CAND_GROUPS):
            r0 = pl.multiple_of((i * CAND_GROUPS + g) * CAND_BATCH, CAND_BATCH)
            rel = jnp.dot(ki_ref[pl.ds(jnp.minimum(r0, last_batch_row), CAND_BATCH), :], qicat_ref[...],
                          preferred_element_type=F32)
            score = jnp.zeros((CAND_BATCH, TQ), F32)
            for h in range(N_IDX):
                score = score + jnp.clip(rel[:, h * TQ:(h + 1) * TQ], clip_ref[h:h + 1, :],
                                         clip_ref[N_IDX + h:N_IDX + h + 1, :])
            if masked:
                score = jnp.where(r0 + lax.broadcasted_iota(I32, (CAND_BATCH, TQ), 0) < vis, score, -jnp.inf)
            st_ref[pl.ds(r0, CAND_BATCH), :] = score
            x = [score[8 * r:8 * r + 8, :] for r in range(CAND_DEPTH)]
            for lo, hi in _SORT_NETWORK:
                x[lo], x[hi] = jnp.maximum(x[lo], x[hi]), jnp.minimum(x[lo], x[hi])
            g0 = g * CAND_BATCH
            t = [jnp.maximum(cand_ref[g0 + 8 * r:g0 + 8 * r + 8, :], x[CAND_DEPTH - 1 - r]) for r in range(CAND_DEPTH)]
            d = CAND_DEPTH // 2
            while d >= 1:
                for r in range(CAND_DEPTH):
                    if r & d == 0:
                        t[r], t[r + d] = jnp.maximum(t[r], t[r + d]), jnp.minimum(t[r], t[r + d])
                d //= 2
            for r in range(CAND_DEPTH):
                cand_ref[g0 + 8 * r:g0 + 8 * r + 8, :] = t[r]
        return carry

    round_chunks = CAND_GROUPS * CAND_BATCH // KEY_CHUNK
    full_rounds = nfull_ref[j] // round_chunks
    lax.fori_loop(0, full_rounds, functools.partial(score_round, False), 0)
    lax.fori_loop(full_rounds, (nch + round_chunks - 1) // round_chunks, functools.partial(score_round, True), 0)

    def count(ref, n_chunks, indicator):
        def body(c, acc):
            r0 = chunk_start(c)
            ind = indicator(ref[pl.ds(r0, KEY_CHUNK), :], r0)
            part = jnp.sum(ind.reshape(8, KEY_CHUNK // 64, 8, TQ), axis=1)
            return acc + jnp.sum(part, axis=0)

        acc = lax.fori_loop(0, n_chunks, body, jnp.zeros((8, TQ), F32))
        return jnp.sum(acc, axis=0, keepdims=True)

    def select(ref, n_chunks):
        def value_bit(i, res):
            trial = res | lax.shift_left(jnp.int32(1), 31 - i)
            thr_t = _ordered_bits_to_f32(trial)
            cnt = count(ref, n_chunks, lambda blk, r0: jnp.where(blk >= thr_t, 1.0, 0.0))
            return jnp.where(cnt >= kf, trial, res)

        res = lax.fori_loop(0, 32, value_bit, jnp.zeros((1, TQ), I32))
        thr_v = jnp.where((res >> 23) == 0, -F32_MAX, _ordered_bits_to_f32(res))
        thr_ref[...] = thr_v
        cgt_ref[...] = count(ref, n_chunks, lambda blk, r0: jnp.where(blk > thr_v, 1.0, 0.0))
        cge_ref[...] = count(ref, n_chunks, lambda blk, r0: jnp.where(blk >= thr_v, 1.0, 0.0))

    select(cand_ref, cand_ref.shape[0] // KEY_CHUNK)
    kept_min = cand_ref[CAND_BATCH - 8:CAND_BATCH, :]
    for g in range(1, CAND_GROUPS):
        kept_min = jnp.maximum(kept_min, cand_ref[(g + 1) * CAND_BATCH - 8:(g + 1) * CAND_BATCH, :])
    unsafe = jnp.max(kept_min, axis=0, keepdims=True) >= thr_ref[...]

    @pl.when(jnp.max(jnp.where(unsafe, 1.0, 0.0)) > 0.0)
    def _():
        select(st_ref, nch)

    thr = thr_ref[...]

    need = kf - cgt_ref[...]
    excess = cge_ref[...] - kf
    cut_ref[...] = jnp.full((1, TQ), 2 ** 30, I32)

    @pl.when(jnp.max(excess) > 0.0)
    def _():
        def index_bit(i, cut):
            trial = cut | lax.shift_left(jnp.int32(1), idx_bits - 1 - i)
            cnt = count(st_ref, nch, lambda blk, r0: jnp.where(
                blk == thr, jnp.where(chunk_rows(r0) < trial, 1.0, 0.0), 0.0))
            return jnp.where(cnt < need, trial, cut)

        cut = lax.fori_loop(0, idx_bits, index_bit, jnp.zeros((1, TQ), I32))
        cut_ref[...] = jnp.where(excess > 0.0, cut, 2 ** 30)

    cut = cut_ref[...]

    def mask_chunk(c, carry):
        r0 = chunk_start(c)
        blk = st_ref[pl.ds(r0, KEY_CHUNK), :]
        tie = jnp.where(chunk_rows(r0) <= cut, 0.0, NEG)
        st_ref[pl.ds(r0, KEY_CHUNK), :] = jnp.where(blk > thr, 0.0, jnp.where(blk == thr, tie, NEG))
        return carry

    lax.fori_loop(0, nch, mask_chunk, 0)
    st_ref[pl.ds(chunk_start(nch), KEY_CHUNK), :] = jnp.full((KEY_CHUNK, TQ), NEG, F32)

    m_ref[...] = jnp.full(m_ref.shape, NEG, F32)
    acc_ref[...] = jnp.zeros(acc_ref.shape, F32)
    last = nch - 1
    ones_rows = jnp.ones((SUM_ROWS, KEY_CHUNK), BF16)

    def logits(blk, s_ref, bmax_ref, with_bias):
        kblk = k_ref[pl.ds(chunk_start(jnp.minimum(blk, last)), KEY_CHUNK), :]
        msk = st_ref[pl.ds(chunk_start(jnp.minimum(blk, nch)), KEY_CHUNK), :]
        msk2 = jnp.concatenate([msk, msk], axis=1)
        bias_row0 = pl.multiple_of((jnp.clip(4 * blk + toff, -1, bias_tile_max) + 1) * LANE, LANE)
        for a in range(N_PAIRS):
            s2 = jnp.dot(kblk[:, a * 2 * HEAD_DIM:(a + 1) * 2 * HEAD_DIM], qbd_ref[a],
                         preferred_element_type=F32) + msk2
            if with_bias:
                s2 = s2 + bias_ref[a, pl.ds(bias_row0, KEY_CHUNK), :]
            s_ref[a] = s2
            bmax_ref[a:a + 1, :] = jnp.max(s2, axis=0, keepdims=True)

    def softmax_pv(blk, s_ref, bmax_ref):
        vblk = jnp.minimum(blk, last)
        for a in range(N_PAIRS):
            m_old = m_ref[a:a + 1, :]
            m_new = jnp.maximum(m_old, bmax_ref[a:a + 1, :])
            alpha = jnp.exp2(m_old - m_new)
            p = jnp.exp2(s_ref[a] - m_new).astype(BF16)
            v_ones = jnp.concatenate([vt_ref[vblk, a * 2 * HEAD_DIM:(a + 1) * 2 * HEAD_DIM, :], ones_rows], axis=0)
            acc_ref[a] = alpha * acc_ref[a] + jnp.dot(v_ones, p, preferred_element_type=F32)
            m_ref[a:a + 1, :] = m_new

    def block_pair(with_bias, i, carry):
        logits(2 * i + 1, sb_ref, bmb_ref, with_bias)
        softmax_pv(2 * i, sa_ref, bma_ref)
        logits(2 * i + 2, sa_ref, bma_ref, with_bias)
        softmax_pv(2 * i + 1, sb_ref, bmb_ref)
        return carry

    far_pairs = jnp.minimum(jnp.maximum(3 - toff, 0) // 4, nch) // 2
    logits(0, sa_ref, bma_ref, False)
    lax.fori_loop(0, far_pairs, functools.partial(block_pair, False), 0)
    logits(2 * far_pairs, sa_ref, bma_ref, True)
    lax.fori_loop(far_pairs, (nch + 1) // 2, functools.partial(block_pair, True), 0)

    for a in range(N_PAIRS):
        for e in range(2):
            h = 2 * a + e
            num = acc_ref[a, e * HEAD_DIM:(e + 1) * HEAD_DIM, e * TQ:(e + 1) * TQ]
            den = acc_ref[a, 2 * HEAD_DIM:2 * HEAD_DIM + 1, e * TQ:(e + 1) * TQ]
            ot_ref[h * HEAD_DIM:(h + 1) * HEAD_DIM, :] = num / den
    o_ref[...] = ot_ref[...].T


def _attention(qit, wt, qt, vis, ki, k, vt, bias, sched, topk, resident):
    s_count, _, nq = qit.shape
    nk = k.shape[1]
    n_tiles = sched[0].shape[0]
    mode = dict(pipeline_mode=pl.Buffered(1)) if resident else {}
    idx_bits = max(1, int(nk - 1).bit_length())
    logit_buf = pltpu.VMEM((N_PAIRS, KEY_CHUNK, 2 * TQ), F32)
    in_specs = [
        pl.BlockSpec((None, ATT_W, TQ), lambda s, j, *_: (s, 0, j)),
        pl.BlockSpec((None, N_IDX, TQ), lambda s, j, *_: (s, 0, j)),
        pl.BlockSpec((None, ATT_W, TQ), lambda s, j, *_: (s, 0, j)),
        pl.BlockSpec((1, TQ), lambda s, j, *_: (0, j)),
        pl.BlockSpec((None, nk, IDX_DIM), lambda s, j, *_: (s, 0, 0), **mode),
        pl.BlockSpec((None, nk, ATT_W), lambda s, j, *_: (s, 0, 0), **mode),
        pl.BlockSpec((None, nk // KEY_CHUNK, ATT_W, KEY_CHUNK), lambda s, j, *_: (s, 0, 0, 0), **mode),
        pl.BlockSpec(bias.shape, lambda s, j, *_: (0, 0, 0), pipeline_mode=pl.Buffered(1)),
    ]
    grid_spec = pltpu.PrefetchScalarGridSpec(
        num_scalar_prefetch=3,
        grid=(s_count, n_tiles),
        in_specs=in_specs,
        out_specs=pl.BlockSpec((None, TQ, ATT_W), lambda s, j, *_: (s, j, 0)),
        scratch_shapes=[
            pltpu.VMEM((nk + KEY_CHUNK, TQ), F32),
            pltpu.VMEM((CAND_GROUPS * CAND_BATCH, TQ), F32),
            pltpu.VMEM((1, TQ), F32),
            pltpu.VMEM((1, TQ), F32),
            pltpu.VMEM((1, TQ), F32),
            pltpu.VMEM((2 * N_IDX, TQ), F32),
            pltpu.VMEM((IDX_DIM, N_IDX * TQ), BF16),
            pltpu.VMEM((N_PAIRS, 2 * HEAD_DIM, 2 * TQ), BF16),
            logit_buf, logit_buf,
            pltpu.VMEM((N_PAIRS, 2 * TQ), F32),
            pltpu.VMEM((N_PAIRS, 2 * TQ), F32),
            pltpu.VMEM((N_PAIRS, 2 * TQ), F32),
            pltpu.VMEM((N_PAIRS, 2 * HEAD_DIM + SUM_ROWS, 2 * TQ), F32),
            pltpu.VMEM((1, TQ), I32),
            pltpu.VMEM((ATT_W, TQ), F32),
        ],
    )
    return pl.pallas_call(
        functools.partial(_attn_body, topk, idx_bits),
        grid_spec=grid_spec,
        out_shape=jax.ShapeDtypeStruct((s_count, nq, ATT_W), F32),
        compiler_params=pltpu.CompilerParams(dimension_semantics=("arbitrary", "arbitrary"),
                                             vmem_limit_bytes=VMEM_LIMIT),
        name="attn",
    )(*sched, qit, wt, qt, vis, ki, k, vt, bias)


def _s5_body(tt, us_ref, s0r_ref, s0i_ref, ar_ref, ai_ref, bbr_ref, bbi_ref, cr_ref, ci_ref, d_ref, wglu_ref,
             bglu_ref, out_ref, sr_out_ref, si_out_ref, str_ref, sti_ref, bre_ref, bim_ref, xr_ref, xi_ref):
    @pl.when(pl.program_id(1) == 0)
    def _():
        str_ref[...] = s0r_ref[...]
        sti_ref[...] = s0i_ref[...]

    u = us_ref[...]
    ub = u.astype(BF16)
    bre_ref[...] = jnp.dot(ub, bbr_ref[...], preferred_element_type=F32)
    bim_ref[...] = jnp.dot(ub, bbi_ref[...], preferred_element_type=F32)
    ar = ar_ref[...]
    ai = ai_ref[...]

    def step(t, carry):
        sr, si = carry
        nr = ar * sr - ai * si + bre_ref[pl.ds(t, 1), :]
        ni = ar * si + ai * sr + bim_ref[pl.ds(t, 1), :]
        xr_ref[pl.ds(t, 1), :] = nr
        xi_ref[pl.ds(t, 1), :] = ni
        return nr, ni

    sr, si = lax.fori_loop(0, tt, step, (str_ref[...], sti_ref[...]))
    str_ref[...] = sr
    sti_ref[...] = si
    sr_out_ref[...] = sr
    si_out_ref[...] = si
    y = (jnp.dot(xr_ref[...].astype(BF16), cr_ref[...], preferred_element_type=F32)
         - jnp.dot(xi_ref[...].astype(BF16), ci_ref[...], preferred_element_type=F32)
         + d_ref[...] * u)
    g = jax.nn.gelu(y)
    gate = jnp.dot(g.astype(BF16), wglu_ref[...], preferred_element_type=F32) + bglu_ref[...]
    out_ref[...] = g * jax.nn.sigmoid(gate)


def _s5(us, s0r, s0i, ar, ai, bbr, bbi, cr, ci, d, wglu, bglu, tt):
    s_count, t_len, _ = us.shape
    assert t_len % tt == 0
    state = pl.BlockSpec((None, 1, SSM_S), lambda s, t: (s, 0, 0))
    seq = pl.BlockSpec((None, tt, SSM_W), lambda s, t: (s, t, 0))
    return pl.pallas_call(
        functools.partial(_s5_body, tt),
        grid=(s_count, t_len // tt),
        in_specs=[seq, state, state, _const_spec((1, SSM_S)), _const_spec((1, SSM_S)),
                  _const_spec((SSM_W, SSM_S)), _const_spec((SSM_W, SSM_S)),
                  _const_spec((SSM_S, SSM_W)), _const_spec((SSM_S, SSM_W)),
                  _const_spec((1, SSM_W)), _const_spec((SSM_W, SSM_W)), _const_spec((1, SSM_W))],
        out_specs=(seq, state, state),
        out_shape=(jax.ShapeDtypeStruct((s_count, t_len, SSM_W), F32),
                   jax.ShapeDtypeStruct((s_count, 1, SSM_S), F32),
                   jax.ShapeDtypeStruct((s_count, 1, SSM_S), F32)),
        scratch_shapes=[pltpu.VMEM((1, SSM_S), F32), pltpu.VMEM((1, SSM_S), F32),
                        pltpu.VMEM((tt, SSM_S), F32), pltpu.VMEM((tt, SSM_S), F32),
                        pltpu.VMEM((tt, SSM_S), F32), pltpu.VMEM((tt, SSM_S), F32)],
        compiler_params=pltpu.CompilerParams(dimension_semantics=("arbitrary", "arbitrary"),
                                             vmem_limit_bytes=VMEM_LIMIT),
        name="s5",
    )(us, s0r, s0i, ar, ai, bbr, bbi, cr, ci, d, wglu, bglu)


def _mix_out_body(att_ref, ssm_ref, h_ref, woa_ref, wos_ref, gmix_ref, gpre_ref, wg_ref, wu_ref, wd_ref, gpost_ref,
                  o_ref):
    m = (jnp.dot(att_ref[...].astype(BF16), woa_ref[...], preferred_element_type=F32)
         + jnp.dot(ssm_ref[...].astype(BF16), wos_ref[...], preferred_element_type=F32))
    h2 = h_ref[...] + _rms(m, gmix_ref[...])
    y = _swiglu(_rms(h2, gpre_ref[...]), wg_ref, wu_ref, wd_ref)
    o_ref[...] = h2 + 0.5 * _rms(y, gpost_ref[...])


def _mix_out(att, ssm, h, woa, wos, gmix, gpre, wg, wu, wd, gpost):
    n = h.shape[0]
    row = lambda width: pl.BlockSpec((ROW_TILE, width), lambda i: (i, 0))
    return pl.pallas_call(
        _mix_out_body,
        grid=(pl.cdiv(n, ROW_TILE),),
        in_specs=[row(ATT_W), row(SSM_W), row(D_MODEL), _const_spec((ATT_W, D_MODEL)),
                  _const_spec((SSM_W, D_MODEL)), _const_spec((1, D_MODEL)), _const_spec((1, D_MODEL)),
                  _const_spec((D_MODEL, D_FF)), _const_spec((D_MODEL, D_FF)), _const_spec((D_FF, D_MODEL)),
                  _const_spec((1, D_MODEL))],
        out_specs=row(D_MODEL),
        out_shape=jax.ShapeDtypeStruct((n, D_MODEL), F32),
        compiler_params=pltpu.CompilerParams(dimension_semantics=("arbitrary",), vmem_limit_bytes=VMEM_LIMIT),
        name="mix_out_ffn2",
    )(att, ssm, h, woa, wos, gmix, gpre, wg, wu, wd, gpost)


def _rel_bucket_np(rel):
    half = NUM_BUCKETS // 2
    max_exact = half // 2
    n = np.abs(rel).astype(np.int64)
    sq = np.maximum(n, max_exact) ** 2 // (max_exact * max_exact)
    large = max_exact + np.floor(np.log2(sq.astype(np.float64)) + 1e-9).astype(np.int64)
    large = np.minimum(large, half - 1)
    return np.where(rel > 0, half, 0) + np.where(n < max_exact, n, large)


def _pair_bias(rel_bias, n_rows, c0):
    length = n_rows + TQ
    rel = np.arange(length) - (TQ - 1) - LANE - c0
    table = (rel_bias - rel_bias[NUM_BUCKETS // 2 - 1][None, :]) * LOG2E
    t1d = table[_rel_bucket_np(rel)].T
    x = jnp.tile(t1d, (1, TQ))[:, :TQ * (length - 1)].reshape(N_HEADS, TQ, length - 1)
    b = jnp.transpose(x[:, :, TQ - 1:TQ - 1 + n_rows], (0, 2, 1)).reshape(N_PAIRS, 2, n_rows, TQ)
    return jnp.concatenate([b[:, 0], b[:, 1]], axis=-1).astype(F32)


def _prompt_schedule(n, n_pad):
    pos = np.arange(n_pad)
    vis = np.where(pos < N_META, N_META, N_META + CHUNK * ((pos - N_META) // CHUNK + 1))
    vis = np.where(pos < n, np.minimum(vis, n), 0).astype(np.int32)
    n_tiles = -(-n // TQ)
    vmax = vis[:n_tiles * TQ].reshape(n_tiles, TQ).max(axis=1)
    nch = -(-vmax // KEY_CHUNK)
    nfull = vis[:n_tiles * TQ].reshape(n_tiles, TQ).min(axis=1) // KEY_CHUNK
    toff = PROMPT_BIAS_C0 // LANE - np.arange(n_tiles)
    assert ((nch - 1) * (KEY_CHUNK // LANE) + toff).max() <= (PROMPT_BIAS_ROWS - KEY_CHUNK) // LANE - 1
    sched = tuple(jnp.asarray(a, I32) for a in (nch, nfull, toff))
    return sched, jnp.asarray(vis[None, :])


def _sample_schedule(past, ts):
    nk = past + ts
    sched = tuple(jnp.asarray([v], I32) for v in (-(-nk // KEY_CHUNK), 0, 0))
    vis = jnp.asarray(np.where(np.arange(TQ) < ts, nk, 0)[None, :], I32)
    return sched, vis


def _s5_params(lam_re, lam_im, log_step, b_re, b_im, c_re, c_im, d_skip):
    dt = jnp.exp(log_step)[:, None]
    mag = jnp.exp(lam_re * dt)
    ab_re, ab_im = mag * jnp.cos(lam_im * dt), mag * jnp.sin(lam_im * dt)
    nr, ni = ab_re - 1.0, ab_im
    den = lam_re * lam_re + lam_im * lam_im
    f_re, f_im = (nr * lam_re + ni * lam_im) / den, (ni * lam_re - nr * lam_im) / den
    bb_re = f_re[..., None] * b_re - f_im[..., None] * b_im
    bb_im = f_re[..., None] * b_im + f_im[..., None] * b_re
    eye = jnp.eye(SSM_GROUPS, dtype=F32)
    bd_in = lambda w: jnp.einsum('gpc,gh->gchp', w, eye).reshape(SSM_W, SSM_S).astype(BF16)
    bd_out = lambda w: jnp.einsum('gcp,gh->gphc', w, eye).reshape(SSM_S, SSM_W).astype(BF16)
    return (ab_re.reshape(1, SSM_S), ab_im.reshape(1, SSM_S), bd_in(bb_re), bd_in(bb_im),
            bd_out(c_re), bd_out(c_im), d_skip.reshape(1, SSM_W))


def kernel(x_prompt, x_sample, cache_k, cache_v, cache_kidx, state_ssm_re, state_ssm_im, meta_tokens, rel_bias,
           ffn1_g_pre, ffn1_w_gate, ffn1_w_up, ffn1_w_down, ffn1_g_post, mix_g_pre, w_in, w_out, mix_g_post,
           lam_re, lam_im, log_step, b_re, b_im, c_re, c_im, d_skip, w_glu, b_glu,
           ffn2_g_pre, ffn2_w_gate, ffn2_w_up, ffn2_w_down, ffn2_g_post):
    depth = ffn1_g_pre.shape[0]
    assert depth == 1
    bp, seq, _ = x_prompt.shape
    assert bp == 1
    ds, ts, _ = x_sample.shape
    past = cache_k.shape[2]
    n_p = N_META + seq
    n_p_pad = pl.cdiv(n_p, ROW_TILE) * ROW_TILE
    n_s = ds * ts
    assert n_s % ROW_TILE == 0 and ts <= TQ
    nk_s_pad = pl.cdiv(past + ts, KEY_CHUNK) * KEY_CHUNK
    l = 0

    row2 = lambda g: g[l].reshape(1, -1)
    w1 = (row2(ffn1_g_pre), ffn1_w_gate[l].astype(BF16), ffn1_w_up[l].astype(BF16), ffn1_w_down[l].astype(BF16),
          row2(ffn1_g_post))
    w2 = (row2(ffn2_g_pre), ffn2_w_gate[l].astype(BF16), ffn2_w_up[l].astype(BF16), ffn2_w_down[l].astype(BF16),
          row2(ffn2_g_post))
    wi = w_in[l]
    o_q, o_k, o_v, o_qi, o_ki, o_wi, o_us = np.cumsum((0, ATT_W, ATT_W, ATT_W, N_IDX * IDX_DIM, IDX_DIM, N_IDX))
    sl = lambda o, w: wi[:, o:o + w]
    wrow = jnp.concatenate([sl(o_k, ATT_W), sl(o_v, ATT_W), sl(o_us, SSM_W), sl(o_ki, IDX_DIM),
                            jnp.zeros((D_MODEL, ROW_COLS_PAD - ROW_COLS), F32)], axis=1).astype(BF16)
    wcol = jnp.concatenate([sl(o_q, ATT_W), sl(o_qi, ATT_W), sl(o_v, ATT_W), sl(o_wi, N_IDX)], axis=1).T.astype(BF16)
    woa, wos = w_out[l][:ATT_W].astype(BF16), w_out[l][ATT_W:].astype(BF16)
    s5w = _s5_params(lam_re[l], lam_im[l], log_step[l], b_re[l], b_im[l], c_re[l], c_im[l], d_skip[l])
    s5w = s5w + (w_glu[l].astype(BF16), b_glu[l].reshape(1, SSM_W))
    bias_tab = rel_bias.astype(F32)

    xp = jnp.concatenate([meta_tokens.astype(F32), x_prompt[0]], axis=0)
    hp = _ffn(xp, *w1)
    kp, vp, kip, usp, kbp, kibp, vtbp, qtp, qitp, wtp = _mix_in(hp, row2(mix_g_pre), wrow, wcol, n_p_pad)
    sched_p, vis_p = _prompt_schedule(n_p, n_p_pad)
    att_p = _attention(qitp[None], wtp[None], qtp[None], vis_p, kibp[None], kbp[None], vtbp[None],
                       _pair_bias(bias_tab, PROMPT_BIAS_ROWS, PROMPT_BIAS_C0), sched_p,
                       min(TOPK_MAX, seq // 4), True)[0]
    zero_state = jnp.zeros((1, 1, SSM_S), F32)
    tt_p = max(t for t in range(8, 513, 8) if n_p % t == 0)
    ssm_p, srp, sip = _s5(usp[None], zero_state, zero_state, *s5w, tt_p)
    yp = _mix_out(att_p, ssm_p[0], hp, woa, wos, row2(mix_g_post), *w2)

    hs = _ffn(x_sample.reshape(n_s, D_MODEL), *w1)
    ks, vs, kis, uss, _, _, _, qts, qits, wts = _mix_in(hs, row2(mix_g_pre), wrow, wcol, n_s)
    pad_k = lambda a: jnp.pad(a, ((0, 0), (0, nk_s_pad - past - ts), (0, 0)))
    k_all = pad_k(jnp.concatenate([cache_k[l].reshape(ds, past, ATT_W), ks.reshape(ds, ts, ATT_W)], 1).astype(BF16))
    v_all = pad_k(jnp.concatenate([cache_v[l].reshape(ds, past, ATT_W), vs.reshape(ds, ts, ATT_W)], 1).astype(BF16))
    ki_all = pad_k(jnp.concatenate([cache_kidx[l], kis.reshape(ds, ts, IDX_DIM)], 1).astype(BF16))
    vt_all = jnp.transpose(v_all.reshape(ds, nk_s_pad // KEY_CHUNK, KEY_CHUNK, ATT_W), (0, 1, 3, 2))
    lanes = lambda a: jnp.pad(jnp.transpose(a.reshape(a.shape[0], ds, ts), (1, 0, 2)), ((0, 0), (0, 0), (0, TQ - ts)))
    sched_s, vis_s = _sample_schedule(past, ts)
    att_s = _attention(lanes(qits), lanes(wts), lanes(qts), vis_s, ki_all, k_all, vt_all,
                       _pair_bias(bias_tab, LANE + nk_s_pad, past), sched_s,
                       min(TOPK_MAX, (past + ts) // 4), False)
    att_s = att_s[:, :ts].reshape(n_s, ATT_W)
    ssm_s, srs, sis = _s5(uss.reshape(ds, ts, SSM_W), state_ssm_re[l].reshape(ds, 1, SSM_S),
                          state_ssm_im[l].reshape(ds, 1, SSM_S), *s5w, ts)
    ys = _mix_out(att_s, ssm_s.reshape(n_s, SSM_W), hs, woa, wos, row2(mix_g_post), *w2)

    heads = lambda a, b, t: a.reshape(1, b, t, N_HEADS, HEAD_DIM)
    state = lambda a, b: a.reshape(1, b, SSM_GROUPS, SSM_P)
    return (yp[N_META:][None], ys.reshape(ds, ts, D_MODEL),
            heads(kp, 1, n_p), heads(vp, 1, n_p), kip.reshape(1, 1, n_p, IDX_DIM), state(srp, 1), state(sip, 1),
            heads(ks, ds, ts), heads(vs, ds, ts), kis.reshape(1, ds, ts, IDX_DIM), state(srs, ds), state(sis, ds))
```

```python
import functools
import math

import numpy as np
import jax
import jax.numpy as jnp
from jax import lax
from jax.experimental import pallas as pl
from jax.experimental.pallas import tpu as pltpu

F32, BF16, I32 = jnp.float32, jnp.bfloat16, jnp.int32

D_MODEL = 1024
CHUNK = 64
N_META = 16
N_HEADS = 8
HEAD_DIM = 64
ATT_W = N_HEADS * HEAD_DIM
N_IDX = 8
IDX_DIM = 64
TOPK_MAX = 256
SSM_GROUPS = 32
SSM_GC = 16
SSM_W = SSM_GROUPS * SSM_GC
SSM_P = 64
SSM_S = SSM_GROUPS * SSM_P
D_FF = 2816
NUM_BUCKETS = 32
EPS = 1e-6

LANE = 128
VMEM_LIMIT = 62 * 1024 * 1024
ROW_TILE = 512
TQ = LANE
KEY_CHUNK = 512
PROMPT_BIAS_C0 = 512
PROMPT_BIAS_ROWS = 1280
NEG = -1e30
LOG2E = math.log2(math.e)
F32_MAX = float(np.finfo(np.float32).max)
N_PAIRS = N_HEADS // 2
SUM_ROWS = 16
CAND_DEPTH = 16
CAND_BATCH = 8 * CAND_DEPTH
CAND_GROUPS = 8
TIE_SCAN_MAX = 8


def _batcher_network(n):
    pairs, p = [], 1
    while p < n:
        k = p
        while k >= 1:
            for j in range(k % p, n - k, 2 * k):
                for i in range(min(k, n - j - k)):
                    if (i + j) // (2 * p) == (i + j + k) // (2 * p):
                        pairs.append((i + j, i + j + k))
            k //= 2
        p *= 2
    return pairs


_SORT_NETWORK = _batcher_network(CAND_DEPTH)


def _rms(x, g):
    return x * lax.rsqrt(jnp.mean(x * x, axis=-1, keepdims=True) + EPS) * g


def _swiglu(xn, wg_ref, wu_ref, wd_ref):
    xb = xn.astype(BF16)
    a = jnp.dot(xb, wg_ref[...], preferred_element_type=F32)
    b = jnp.dot(xb, wu_ref[...], preferred_element_type=F32)
    hidden = (a * jax.nn.sigmoid(a) * b).astype(BF16)
    return jnp.dot(hidden, wd_ref[...], preferred_element_type=F32)


def _const_spec(shape):
    nd = len(shape)
    return pl.BlockSpec(shape, lambda *_: (0,) * nd, pipeline_mode=pl.Buffered(1))


def _ffn_body(x_ref, gpre_ref, wg_ref, wu_ref, wd_ref, gpost_ref, o_ref):
    x = x_ref[...]
    y = _swiglu(_rms(x, gpre_ref[...]), wg_ref, wu_ref, wd_ref)
    o_ref[...] = x + 0.5 * _rms(y, gpost_ref[...])


def _ffn(x, gpre, wg, wu, wd, gpost):
    n = x.shape[0]
    row = pl.BlockSpec((ROW_TILE, D_MODEL), lambda i: (i, 0))
    return pl.pallas_call(
        _ffn_body,
        grid=(pl.cdiv(n, ROW_TILE),),
        in_specs=[row, _const_spec((1, D_MODEL)), _const_spec((D_MODEL, D_FF)), _const_spec((D_MODEL, D_FF)),
                  _const_spec((D_FF, D_MODEL)), _const_spec((1, D_MODEL))],
        out_specs=row,
        out_shape=jax.ShapeDtypeStruct((n, D_MODEL), F32),
        compiler_params=pltpu.CompilerParams(dimension_semantics=("arbitrary",), vmem_limit_bytes=VMEM_LIMIT),
        name="ffn1",
    )(x, gpre, wg, wu, wd, gpost)


ROW_COLS = 3 * ATT_W + IDX_DIM
ROW_COLS_PAD = 13 * LANE
COL_ROWS = 3 * ATT_W + N_IDX


def _mix_in_body(n_valid, h_ref, g_ref, wrow_ref, wcol_ref,
                 k_ref, v_ref, ki_ref, us_ref, kb_ref, kib_ref, vtb_ref, qt_ref, qit_ref, wt_ref):
    i = pl.program_id(0)
    u = _rms(h_ref[...], g_ref[...]).astype(BF16)
    z = jnp.dot(u, wrow_ref[...], preferred_element_type=F32)
    zt = lax.dot_general(wcol_ref[...], u, (((1,), (1,)), ((), ())), preferred_element_type=F32)
    k = z[:, 0:ATT_W]
    v = z[:, ATT_W:2 * ATT_W]
    ki = z[:, 3 * ATT_W:3 * ATT_W + IDX_DIM]
    k_ref[...] = k
    v_ref[...] = v
    us_ref[...] = z[:, 2 * ATT_W:3 * ATT_W]
    ki_ref[...] = ki
    row_ok = (i * ROW_TILE + lax.broadcasted_iota(I32, (ROW_TILE, 1), 0)) < n_valid
    col_ok = (i * ROW_TILE + lax.broadcasted_iota(I32, (1, ROW_TILE), 1)) < n_valid
    kb_ref[...] = jnp.where(row_ok, k, 0.0).astype(BF16)
    kib_ref[...] = jnp.where(row_ok, ki, 0.0).astype(BF16)
    zt = jnp.where(col_ok, zt, 0.0)
    qt_ref[...] = (zt[0:ATT_W] * (HEAD_DIM ** -0.5 * LOG2E)).astype(BF16)
    vtb_ref[...] = zt[2 * ATT_W:3 * ATT_W].astype(BF16)
    wt = zt[3 * ATT_W:3 * ATT_W + N_IDX] * ((IDX_DIM ** -0.5) * (N_IDX ** -0.5))
    wt_ref[...] = wt
    for h in range(N_IDX):
        qi_h = zt[ATT_W + h * IDX_DIM:ATT_W + (h + 1) * IDX_DIM]
        qit_ref[h * IDX_DIM:(h + 1) * IDX_DIM, :] = (qi_h * wt[h:h + 1, :]).astype(BF16)


def _mix_in(h, g, wrow, wcol, n_pad):
    n = h.shape[0]
    steps = n_pad // ROW_TILE
    assert steps == pl.cdiv(n, ROW_TILE)

    def row(width):
        return pl.BlockSpec((ROW_TILE, width), lambda i: (i, 0))

    col = lambda rows: pl.BlockSpec((rows, ROW_TILE), lambda i: (0, i))
    out_shape = (
        jax.ShapeDtypeStruct((n, ATT_W), F32), jax.ShapeDtypeStruct((n, ATT_W), F32),
        jax.ShapeDtypeStruct((n, IDX_DIM), F32), jax.ShapeDtypeStruct((n, SSM_W), F32),
        jax.ShapeDtypeStruct((n_pad, ATT_W), BF16), jax.ShapeDtypeStruct((n_pad, IDX_DIM), BF16),
        jax.ShapeDtypeStruct((n_pad // ROW_TILE, ATT_W, ROW_TILE), BF16),
        jax.ShapeDtypeStruct((ATT_W, n_pad), BF16), jax.ShapeDtypeStruct((ATT_W, n_pad), BF16),
        jax.ShapeDtypeStruct((N_IDX, n_pad), F32),
    )
    out_specs = (
        row(ATT_W), row(ATT_W), row(IDX_DIM), row(SSM_W), row(ATT_W), row(IDX_DIM),
        pl.BlockSpec((None, ATT_W, ROW_TILE), lambda i: (i, 0, 0)),
        col(ATT_W), col(ATT_W), col(N_IDX),
    )
    return pl.pallas_call(
        functools.partial(_mix_in_body, n),
        grid=(steps,),
        in_specs=[row(D_MODEL), _const_spec((1, D_MODEL)), _const_spec((D_MODEL, ROW_COLS_PAD)),
                  _const_spec((COL_ROWS, D_MODEL))],
        out_specs=out_specs,
        out_shape=out_shape,
        compiler_params=pltpu.CompilerParams(dimension_semantics=("arbitrary",), vmem_limit_bytes=VMEM_LIMIT),
        name="mix_in",
    )(h, g, wrow, wcol)


def _ordered_bits_to_f32(u):
    k = u ^ jnp.int32(-2 ** 31)
    b = k ^ ((k >> 31) & jnp.int32(0x7FFFFFFF))
    return lax.bitcast_convert_type(b, F32)


def _attn_body(topk, idx_bits,
               nch_ref, nfull_ref, toff_ref,
               qit_ref, wt_ref, qt_ref, vis_ref, ki_ref, k_ref, vt_ref, bias_ref,
               o_ref,
               st_ref, cand_ref, thr_ref, cgt_ref, cge_ref, clip_ref, qicat_ref, qbd_ref, sa_ref, sb_ref,
               bma_ref, bmb_ref,
               m_ref, acc_ref, cut_ref, ot_ref):
    j = pl.program_id(1)
    nch = nch_ref[j]
    toff = toff_ref[j]
    kf = float(topk)
    bias_tile_max = (bias_ref.shape[1] - KEY_CHUNK) // LANE - 1

    for h in range(N_IDX):
        qicat_ref[:, h * TQ:(h + 1) * TQ] = qit_ref[h * IDX_DIM:(h + 1) * IDX_DIM, :]
    qbd_ref[...] = jnp.zeros(qbd_ref.shape, BF16)
    for a in range(N_PAIRS):
        qbd_ref[a, 0:HEAD_DIM, 0:TQ] = qt_ref[2 * a * HEAD_DIM:(2 * a + 1) * HEAD_DIM, :]
        qbd_ref[a, HEAD_DIM:2 * HEAD_DIM, TQ:2 * TQ] = qt_ref[(2 * a + 1) * HEAD_DIM:(2 * a + 2) * HEAD_DIM, :]

    def chunk_start(c):
        return pl.multiple_of(c * KEY_CHUNK, KEY_CHUNK)

    def chunk_rows(r0):
        return r0 + lax.broadcasted_iota(I32, (KEY_CHUNK, TQ), 0)

    vis = vis_ref[...]
    cand_ref[...] = jnp.full(cand_ref.shape, -jnp.inf, F32)
    w_pos = wt_ref[...] >= 0.0
    clip_ref[0:N_IDX, :] = jnp.where(w_pos, 0.0, -jnp.inf)
    clip_ref[N_IDX:2 * N_IDX, :] = jnp.where(w_pos, jnp.inf, 0.0)

    last_batch_row = ki_ref.shape[0] - CAND_BATCH

    def score_round(masked, i, carry):
        for g in range(CAND_GROUPS):
            r0 = pl.multiple_of((i * CAND_GROUPS + g) * CAND_BATCH, CAND_BATCH)
            rel = jnp.dot(ki_ref[pl.ds(jnp.minimum(r0, last_batch_row), CAND_BATCH), :], qicat_ref[...],
                          preferred_element_type=F32)
            score = jnp.zeros((CAND_BATCH, TQ), F32)
            for h in range(N_IDX):
                score = score + jnp.clip(rel[:, h * TQ:(h + 1) * TQ], clip_ref[h:h + 1, :],
                                         clip_ref[N_IDX + h:N_IDX + h + 1, :])
            if masked:
                score = jnp.where(r0 + lax.broadcasted_iota(I32, (CAND_BATCH, TQ), 0) < vis, score, -jnp.inf)
            st_ref[pl.ds(r0, CAND_BATCH), :] = score
            x = [score[8 * r:8 * r + 8, :] for r in range(CAND_DEPTH)]
            for lo, hi in _SORT_NETWORK:
                x[lo], x[hi] = jnp.maximum(x[lo], x[hi]), jnp.minimum(x[lo], x[hi])
            g0 = g * CAND_BATCH
            t = [jnp.maximum(cand_ref[g0 + 8 * r:g0 + 8 * r + 8, :], x[CAND_DEPTH - 1 - r]) for r in range(CAND_DEPTH)]
            d = CAND_DEPTH // 2
            while d >= 1:
                for r in range(CAND_DEPTH):
                    if r & d == 0:
                        t[r], t[r + d] = jnp.maximum(t[r], t[r + d]), jnp.minimum(t[r], t[r + d])
                d //= 2
            for r in range(CAND_DEPTH):
                cand_ref[g0 + 8 * r:g0 + 8 * r + 8, :] = t[r]
        return carry

    round_chunks = CAND_GROUPS * CAND_BATCH // KEY_CHUNK
    full_rounds = nfull_ref[j] // round_chunks
    lax.fori_loop(0, full_rounds, functools.partial(score_round, False), 0)
    lax.fori_loop(full_rounds, (nch + round_chunks - 1) // round_chunks, functools.partial(score_round, True), 0)

    def count(ref, n_chunks, indicator):
        def body(c, acc):
            r0 = chunk_start(c)
            ind = indicator(ref[pl.ds(r0, KEY_CHUNK), :], r0)
            part = jnp.sum(ind.reshape(8, KEY_CHUNK // 64, 8, TQ), axis=1)
            return acc + jnp.sum(part, axis=0)

        acc = lax.fori_loop(0, n_chunks, body, jnp.zeros((8, TQ), F32))
        return jnp.sum(acc, axis=0, keepdims=True)

    def select(ref, n_chunks):
        def value_bit(i, res):
            trial = res | lax.shift_left(jnp.int32(1), 31 - i)
            thr_t = _ordered_bits_to_f32(trial)
            cnt = count(ref, n_chunks, lambda blk, r0: jnp.where(blk >= thr_t, 1.0, 0.0))
            return jnp.where(cnt >= kf, trial, res)

        res = lax.fori_loop(0, 32, value_bit, jnp.zeros((1, TQ), I32))
        thr_v = jnp.where((res >> 23) == 0, -F32_MAX, _ordered_bits_to_f32(res))
        thr_ref[...] = thr_v
        cgt_ref[...] = count(ref, n_chunks, lambda blk, r0: jnp.where(blk > thr_v, 1.0, 0.0))
        cge_ref[...] = count(ref, n_chunks, lambda blk, r0: jnp.where(blk >= thr_v, 1.0, 0.0))

    select(cand_ref, cand_ref.shape[0] // KEY_CHUNK)
    kept_min = cand_ref[CAND_BATCH - 8:CAND_BATCH, :]
    for g in range(1, CAND_GROUPS):
        kept_min = jnp.maximum(kept_min, cand_ref[(g + 1) * CAND_BATCH - 8:(g + 1) * CAND_BATCH, :])
    unsafe = jnp.max(kept_min, axis=0, keepdims=True) >= thr_ref[...]

    @pl.when(jnp.max(jnp.where(unsafe, 1.0, 0.0)) > 0.0)
    def _():
        select(st_ref, nch)

    thr = thr_ref[...]

    need = kf - cgt_ref[...]
    excess = cge_ref[...] - kf
    cut_ref[...] = jnp.full((1, TQ), 2 ** 30, I32)

    tie_need = jnp.max(jnp.where(excess > 0.0, need, 0.0)).astype(I32)

    @pl.when(jnp.logical_and(tie_need > 0, tie_need <= TIE_SCAN_MAX))
    def _():
        def next_tied(cut_f):
            def body(c, acc):
                r0 = chunk_start(c)
                rows_f = chunk_rows(r0).astype(F32)
                v = jnp.where(st_ref[pl.ds(r0, KEY_CHUNK), :] == thr, jnp.where(rows_f > cut_f, rows_f, F32_MAX),
                              F32_MAX)
                part = jnp.min(v.reshape(8, KEY_CHUNK // 64, 8, TQ), axis=1)
                return jnp.minimum(acc, jnp.min(part, axis=0))

            acc = lax.fori_loop(0, nch, body, jnp.full((8, TQ), F32_MAX, F32))
            return jnp.min(acc, axis=0, keepdims=True)

        def take(t, cut_f):
            return jnp.where(t.astype(F32) < need, next_tied(cut_f), cut_f)

        cut_f = lax.fori_loop(0, tie_need, take, jnp.full((1, TQ), -1.0, F32))
        cut_ref[...] = jnp.where(excess > 0.0, cut_f.astype(I32), 2 ** 30)

    @pl.when(tie_need > TIE_SCAN_MAX)
    def _():
        def index_bit(i, cut):
            trial = cut | lax.shift_left(jnp.int32(1), idx_bits - 1 - i)
            cnt = count(st_ref, nch, lambda blk, r0: jnp.where(
                blk == thr, jnp.where(chunk_rows(r0) < trial, 1.0, 0.0), 0.0))
            return jnp.where(cnt < need, trial, cut)

        cut = lax.fori_loop(0, idx_bits, index_bit, jnp.zeros((1, TQ), I32))
        cut_ref[...] = jnp.where(excess > 0.0, cut, 2 ** 30)

    cut = cut_ref[...]

    def mask_chunk(c, carry):
        r0 = chunk_start(c)
        blk = st_ref[pl.ds(r0, KEY_CHUNK), :]
        tie = jnp.where(chunk_rows(r0) <= cut, 0.0, NEG)
        st_ref[pl.ds(r0, KEY_CHUNK), :] = jnp.where(blk > thr, 0.0, jnp.where(blk == thr, tie, NEG))
        return carry

    lax.fori_loop(0, nch, mask_chunk, 0)
    st_ref[pl.ds(chunk_start(nch), KEY_CHUNK), :] = jnp.full((KEY_CHUNK, TQ), NEG, F32)

    m_ref[...] = jnp.full(m_ref.shape, NEG, F32)
    acc_ref[...] = jnp.zeros(acc_ref.shape, F32)
    last = nch - 1
    ones_rows = jnp.ones((SUM_ROWS, KEY_CHUNK), BF16)

    def logits(blk, s_ref, bmax_ref, with_bias):
        kblk = k_ref[pl.ds(chunk_start(jnp.minimum(blk, last)), KEY_CHUNK), :]
        msk = st_ref[pl.ds(chunk_start(jnp.minimum(blk, nch)), KEY_CHUNK), :]
        msk2 = jnp.concatenate([msk, msk], axis=1)
        bias_row0 = pl.multiple_of((jnp.clip(4 * blk + toff, -1, bias_tile_max) + 1) * LANE, LANE)
        for a in range(N_PAIRS):
            s2 = jnp.dot(kblk[:, a * 2 * HEAD_DIM:(a + 1) * 2 * HEAD_DIM], qbd_ref[a],
                         preferred_element_type=F32) + msk2
            if with_bias:
                s2 = s2 + bias_ref[a, pl.ds(bias_row0, KEY_CHUNK), :]
            s_ref[a] = s2
            bmax_ref[a:a + 1, :] = jnp.max(s2, axis=0, keepdims=True)

    def softmax_pv(blk, s_ref, bmax_ref):
        vblk = jnp.minimum(blk, last)
        for a in range(N_PAIRS):
            m_old = m_ref[a:a + 1, :]
            m_new = jnp.maximum(m_old, bmax_ref[a:a + 1, :])
            alpha = jnp.exp2(m_old - m_new)
            p = jnp.exp2(s_ref[a] - m_new).astype(BF16)
            v_ones = jnp.concatenate([vt_ref[vblk, a * 2 * HEAD_DIM:(a + 1) * 2 * HEAD_DIM, :], ones_rows], axis=0)
            acc_ref[a] = alpha * acc_ref[a] + jnp.dot(v_ones, p, preferred_element_type=F32)
            m_ref[a:a + 1, :] = m_new

    def block_pair(with_bias, i, carry):
        logits(2 * i + 1, sb_ref, bmb_ref, with_bias)
        softmax_pv(2 * i, sa_ref, bma_ref)
        logits(2 * i + 2, sa_ref, bma_ref, with_bias)
        softmax_pv(2 * i + 1, sb_ref, bmb_ref)
        return carry

    far_pairs = jnp.minimum(jnp.maximum(3 - toff, 0) // 4, nch) // 2
    logits(0, sa_ref, bma_ref, False)
    lax.fori_loop(0, far_pairs, functools.partial(block_pair, False), 0)
    logits(2 * far_pairs, sa_ref, bma_ref, True)
    lax.fori_loop(far_pairs, (nch + 1) // 2, functools.partial(block_pair, True), 0)

    for a in range(N_PAIRS):
        for e in range(2):
            h = 2 * a + e
            num = acc_ref[a, e * HEAD_DIM:(e + 1) * HEAD_DIM, e * TQ:(e + 1) * TQ]
            den = acc_ref[a, 2 * HEAD_DIM:2 * HEAD_DIM + 1, e * TQ:(e + 1) * TQ]
            ot_ref[h * HEAD_DIM:(h + 1) * HEAD_DIM, :] = num / den
    o_ref[...] = ot_ref[...].T


def _attention(qit, wt, qt, vis, ki, k, vt, bias, sched, topk, resident):
    s_count, _, nq = qit.shape
    nk = k.shape[1]
    n_tiles = sched[0].shape[0]
    mode = dict(pipeline_mode=pl.Buffered(1)) if resident else {}
    idx_bits = max(1, int(nk - 1).bit_length())
    logit_buf = pltpu.VMEM((N_PAIRS, KEY_CHUNK, 2 * TQ), F32)
    in_specs = [
        pl.BlockSpec((None, ATT_W, TQ), lambda s, j, *_: (s, 0, j)),
        pl.BlockSpec((None, N_IDX, TQ), lambda s, j, *_: (s, 0, j)),
        pl.BlockSpec((None, ATT_W, TQ), lambda s, j, *_: (s, 0, j)),
        pl.BlockSpec((1, TQ), lambda s, j, *_: (0, j)),
        pl.BlockSpec((None, nk, IDX_DIM), lambda s, j, *_: (s, 0, 0), **mode),
        pl.BlockSpec((None, nk, ATT_W), lambda s, j, *_: (s, 0, 0), **mode),
        pl.BlockSpec((None, nk // KEY_CHUNK, ATT_W, KEY_CHUNK), lambda s, j, *_: (s, 0, 0, 0), **mode),
        pl.BlockSpec(bias.shape, lambda s, j, *_: (0, 0, 0), pipeline_mode=pl.Buffered(1)),
    ]
    grid_spec = pltpu.PrefetchScalarGridSpec(
        num_scalar_prefetch=3,
        grid=(s_count, n_tiles),
        in_specs=in_specs,
        out_specs=pl.BlockSpec((None, TQ, ATT_W), lambda s, j, *_: (s, j, 0)),
        scratch_shapes=[
            pltpu.VMEM((nk + KEY_CHUNK, TQ), F32),
            pltpu.VMEM((CAND_GROUPS * CAND_BATCH, TQ), F32),
            pltpu.VMEM((1, TQ), F32),
            pltpu.VMEM((1, TQ), F32),
            pltpu.VMEM((1, TQ), F32),
            pltpu.VMEM((2 * N_IDX, TQ), F32),
            pltpu.VMEM((IDX_DIM, N_IDX * TQ), BF16),
            pltpu.VMEM((N_PAIRS, 2 * HEAD_DIM, 2 * TQ), BF16),
            logit_buf, logit_buf,
            pltpu.VMEM((N_PAIRS, 2 * TQ), F32),
            pltpu.VMEM((N_PAIRS, 2 * TQ), F32),
            pltpu.VMEM((N_PAIRS, 2 * TQ), F32),
            pltpu.VMEM((N_PAIRS, 2 * HEAD_DIM + SUM_ROWS, 2 * TQ), F32),
            pltpu.VMEM((1, TQ), I32),
            pltpu.VMEM((ATT_W, TQ), F32),
        ],
    )
    return pl.pallas_call(
        functools.partial(_attn_body, topk, idx_bits),
        grid_spec=grid_spec,
        out_shape=jax.ShapeDtypeStruct((s_count, nq, ATT_W), F32),
        compiler_params=pltpu.CompilerParams(dimension_semantics=("arbitrary", "arbitrary"),
                                             vmem_limit_bytes=VMEM_LIMIT),
        name="attn",
    )(*sched, qit, wt, qt, vis, ki, k, vt, bias)


def _s5_body(tt, us_ref, s0r_ref, s0i_ref, ar_ref, ai_ref, bbr_ref, bbi_ref, cr_ref, ci_ref, d_ref, wglu_ref,
             bglu_ref, out_ref, sr_out_ref, si_out_ref, str_ref, sti_ref, bre_ref, bim_ref, xr_ref, xi_ref):
    @pl.when(pl.program_id(1) == 0)
    def _():
        str_ref[...] = s0r_ref[...]
        sti_ref[...] = s0i_ref[...]

    u = us_ref[...]
    ub = u.astype(BF16)
    bre_ref[...] = jnp.dot(ub, bbr_ref[...], preferred_element_type=F32)
    bim_ref[...] = jnp.dot(ub, bbi_ref[...], preferred_element_type=F32)
    ar = ar_ref[...]
    ai = ai_ref[...]

    def step(t, carry):
        sr, si = carry
        nr = ar * sr - ai * si + bre_ref[pl.ds(t, 1), :]
        ni = ar * si + ai * sr + bim_ref[pl.ds(t, 1), :]
        xr_ref[pl.ds(t, 1), :] = nr
        xi_ref[pl.ds(t, 1), :] = ni
        return nr, ni

    sr, si = lax.fori_loop(0, tt, step, (str_ref[...], sti_ref[...]))
    str_ref[...] = sr
    sti_ref[...] = si
    sr_out_ref[...] = sr
    si_out_ref[...] = si
    y = (jnp.dot(xr_ref[...].astype(BF16), cr_ref[...], preferred_element_type=F32)
         - jnp.dot(xi_ref[...].astype(BF16), ci_ref[...], preferred_element_type=F32)
         + d_ref[...] * u)
    g = jax.nn.gelu(y)
    gate = jnp.dot(g.astype(BF16), wglu_ref[...], preferred_element_type=F32) + bglu_ref[...]
    out_ref[...] = g * jax.nn.sigmoid(gate)


def _s5(us, s0r, s0i, ar, ai, bbr, bbi, cr, ci, d, wglu, bglu, tt):
    s_count, t_len, _ = us.shape
    assert t_len % tt == 0
    state = pl.BlockSpec((None, 1, SSM_S), lambda s, t: (s, 0, 0))
    seq = pl.BlockSpec((None, tt, SSM_W), lambda s, t: (s, t, 0))
    return pl.pallas_call(
        functools.partial(_s5_body, tt),
        grid=(s_count, t_len // tt),
        in_specs=[seq, state, state, _const_spec((1, SSM_S)), _const_spec((1, SSM_S)),
                  _const_spec((SSM_W, SSM_S)), _const_spec((SSM_W, SSM_S)),
                  _const_spec((SSM_S, SSM_W)), _const_spec((SSM_S, SSM_W)),
                  _const_spec((1, SSM_W)), _const_spec((SSM_W, SSM_W)), _const_spec((1, SSM_W))],
        out_specs=(seq, state, state),
        out_shape=(jax.ShapeDtypeStruct((s_count, t_len, SSM_W), F32),
                   jax.ShapeDtypeStruct((s_count, 1, SSM_S), F32),
                   jax.ShapeDtypeStruct((s_count, 1, SSM_S), F32)),
        scratch_shapes=[pltpu.VMEM((1, SSM_S), F32), pltpu.VMEM((1, SSM_S), F32),
                        pltpu.VMEM((tt, SSM_S), F32), pltpu.VMEM((tt, SSM_S), F32),
                        pltpu.VMEM((tt, SSM_S), F32), pltpu.VMEM((tt, SSM_S), F32)],
        compiler_params=pltpu.CompilerParams(dimension_semantics=("arbitrary", "arbitrary"),
                                             vmem_limit_bytes=VMEM_LIMIT),
        name="s5",
    )(us, s0r, s0i, ar, ai, bbr, bbi, cr, ci, d, wglu, bglu)


def _mix_out_body(att_ref, ssm_ref, h_ref, woa_ref, wos_ref, gmix_ref, gpre_ref, wg_ref, wu_ref, wd_ref, gpost_ref,
                  o_ref):
    m = (jnp.dot(att_ref[...].astype(BF16), woa_ref[...], preferred_element_type=F32)
         + jnp.dot(ssm_ref[...].astype(BF16), wos_ref[...], preferred_element_type=F32))
    h2 = h_ref[...] + _rms(m, gmix_ref[...])
    y = _swiglu(_rms(h2, gpre_ref[...]), wg_ref, wu_ref, wd_ref)
    o_ref[...] = h2 + 0.5 * _rms(y, gpost_ref[...])


def _mix_out(att, ssm, h, woa, wos, gmix, gpre, wg, wu, wd, gpost):
    n = h.shape[0]
    row = lambda width: pl.BlockSpec((ROW_TILE, width), lambda i: (i, 0))
    return pl.pallas_call(
        _mix_out_body,
        grid=(pl.cdiv(n, ROW_TILE),),
        in_specs=[row(ATT_W), row(SSM_W), row(D_MODEL), _const_spec((ATT_W, D_MODEL)),
                  _const_spec((SSM_W, D_MODEL)), _const_spec((1, D_MODEL)), _const_spec((1, D_MODEL)),
                  _const_spec((D_MODEL, D_FF)), _const_spec((D_MODEL, D_FF)), _const_spec((D_FF, D_MODEL)),
                  _const_spec((1, D_MODEL))],
        out_specs=row(D_MODEL),
        out_shape=jax.ShapeDtypeStruct((n, D_MODEL), F32),
        compiler_params=pltpu.CompilerParams(dimension_semantics=("arbitrary",), vmem_limit_bytes=VMEM_LIMIT),
        name="mix_out_ffn2",
    )(att, ssm, h, woa, wos, gmix, gpre, wg, wu, wd, gpost)


def _rel_bucket_np(rel):
    half = NUM_BUCKETS // 2
    max_exact = half // 2
    n = np.abs(rel).astype(np.int64)
    sq = np.maximum(n, max_exact) ** 2 // (max_exact * max_exact)
    large = max_exact + np.floor(np.log2(sq.astype(np.float64)) + 1e-9).astype(np.int64)
    large = np.minimum(large, half - 1)
    return np.where(rel > 0, half, 0) + np.where(n < max_exact, n, large)


def _pair_bias(rel_bias, n_rows, c0):
    length = n_rows + TQ
    rel = np.arange(length) - (TQ - 1) - LANE - c0
    table = (rel_bias - rel_bias[NUM_BUCKETS // 2 - 1][None, :]) * LOG2E
    t1d = table[_rel_bucket_np(rel)].T
    x = jnp.tile(t1d, (1, TQ))[:, :TQ * (length - 1)].reshape(N_HEADS, TQ, length - 1)
    b = jnp.transpose(x[:, :, TQ - 1:TQ - 1 + n_rows], (0, 2, 1)).reshape(N_PAIRS, 2, n_rows, TQ)
    return jnp.concatenate([b[:, 0], b[:, 1]], axis=-1).astype(F32)


def _prompt_schedule(n, n_pad):
    pos = np.arange(n_pad)
    vis = np.where(pos < N_META, N_META, N_META + CHUNK * ((pos - N_META) // CHUNK + 1))
    vis = np.where(pos < n, np.minimum(vis, n), 0).astype(np.int32)
    n_tiles = -(-n // TQ)
    vmax = vis[:n_tiles * TQ].reshape(n_tiles, TQ).max(axis=1)
    nch = -(-vmax // KEY_CHUNK)
    nfull = vis[:n_tiles * TQ].reshape(n_tiles, TQ).min(axis=1) // KEY_CHUNK
    toff = PROMPT_BIAS_C0 // LANE - np.arange(n_tiles)
    assert ((nch - 1) * (KEY_CHUNK // LANE) + toff).max() <= (PROMPT_BIAS_ROWS - KEY_CHUNK) // LANE - 1
    sched = tuple(jnp.asarray(a, I32) for a in (nch, nfull, toff))
    return sched, jnp.asarray(vis[None, :])


def _sample_schedule(past, ts):
    nk = past + ts
    sched = tuple(jnp.asarray([v], I32) for v in (-(-nk // KEY_CHUNK), 0, 0))
    vis = jnp.asarray(np.where(np.arange(TQ) < ts, nk, 0)[None, :], I32)
    return sched, vis


def _s5_params(lam_re, lam_im, log_step, b_re, b_im, c_re, c_im, d_skip):
    dt = jnp.exp(log_step)[:, None]
    mag = jnp.exp(lam_re * dt)
    ab_re, ab_im = mag * jnp.cos(lam_im * dt), mag * jnp.sin(lam_im * dt)
    nr, ni = ab_re - 1.0, ab_im
    den = lam_re * lam_re + lam_im * lam_im
    f_re, f_im = (nr * lam_re + ni * lam_im) / den, (ni * lam_re - nr * lam_im) / den
    bb_re = f_re[..., None] * b_re - f_im[..., None] * b_im
    bb_im = f_re[..., None] * b_im + f_im[..., None] * b_re
    eye = jnp.eye(SSM_GROUPS, dtype=F32)
    bd_in = lambda w: jnp.einsum('gpc,gh->gchp', w, eye).reshape(SSM_W, SSM_S).astype(BF16)
    bd_out = lambda w: jnp.einsum('gcp,gh->gphc', w, eye).reshape(SSM_S, SSM_W).astype(BF16)
    return (ab_re.reshape(1, SSM_S), ab_im.reshape(1, SSM_S), bd_in(bb_re), bd_in(bb_im),
            bd_out(c_re), bd_out(c_im), d_skip.reshape(1, SSM_W))


def kernel(x_prompt, x_sample, cache_k, cache_v, cache_kidx, state_ssm_re, state_ssm_im, meta_tokens, rel_bias,
           ffn1_g_pre, ffn1_w_gate, ffn1_w_up, ffn1_w_down, ffn1_g_post, mix_g_pre, w_in, w_out, mix_g_post,
           lam_re, lam_im, log_step, b_re, b_im, c_re, c_im, d_skip, w_glu, b_glu,
           ffn2_g_pre, ffn2_w_gate, ffn2_w_up, ffn2_w_down, ffn2_g_post):
    depth = ffn1_g_pre.shape[0]
    assert depth == 1
    bp, seq, _ = x_prompt.shape
    assert bp == 1
    ds, ts, _ = x_sample.shape
    past = cache_k.shape[2]
    n_p = N_META + seq
    n_p_pad = pl.cdiv(n_p, ROW_TILE) * ROW_TILE
    n_s = ds * ts
    assert n_s % ROW_TILE == 0 and ts <= TQ
    nk_s_pad = pl.cdiv(past + ts, KEY_CHUNK) * KEY_CHUNK
    l = 0

    row2 = lambda g: g[l].reshape(1, -1)
    w1 = (row2(ffn1_g_pre), ffn1_w_gate[l].astype(BF16), ffn1_w_up[l].astype(BF16), ffn1_w_down[l].astype(BF16),
          row2(ffn1_g_post))
    w2 = (row2(ffn2_g_pre), ffn2_w_gate[l].astype(BF16), ffn2_w_up[l].astype(BF16), ffn2_w_down[l].astype(BF16),
          row2(ffn2_g_post))
    wi = w_in[l]
    o_q, o_k, o_v, o_qi, o_ki, o_wi, o_us = np.cumsum((0, ATT_W, ATT_W, ATT_W, N_IDX * IDX_DIM, IDX_DIM, N_IDX))
    sl = lambda o, w: wi[:, o:o + w]
    wrow = jnp.concatenate([sl(o_k, ATT_W), sl(o_v, ATT_W), sl(o_us, SSM_W), sl(o_ki, IDX_DIM),
                            jnp.zeros((D_MODEL, ROW_COLS_PAD - ROW_COLS), F32)], axis=1).astype(BF16)
    wcol = jnp.concatenate([sl(o_q, ATT_W), sl(o_qi, ATT_W), sl(o_v, ATT_W), sl(o_wi, N_IDX)], axis=1).T.astype(BF16)
    woa, wos = w_out[l][:ATT_W].astype(BF16), w_out[l][ATT_W:].astype(BF16)
    s5w = _s5_params(lam_re[l], lam_im[l], log_step[l], b_re[l], b_im[l], c_re[l], c_im[l], d_skip[l])
    s5w = s5w + (w_glu[l].astype(BF16), b_glu[l].reshape(1, SSM_W))
    bias_tab = rel_bias.astype(F32)

    xp = jnp.concatenate([meta_tokens.astype(F32), x_prompt[0]], axis=0)
    hp = _ffn(xp, *w1)
    kp, vp, kip, usp, kbp, kibp, vtbp, qtp, qitp, wtp = _mix_in(hp, row2(mix_g_pre), wrow, wcol, n_p_pad)
    sched_p, vis_p = _prompt_schedule(n_p, n_p_pad)
    att_p = _attention(qitp[None], wtp[None], qtp[None], vis_p, kibp[None], kbp[None], vtbp[None],
                       _pair_bias(bias_tab, PROMPT_BIAS_ROWS, PROMPT_BIAS_C0), sched_p,
                       min(TOPK_MAX, seq // 4), True)[0]
    zero_state = jnp.zeros((1, 1, SSM_S), F32)
    tt_p = max(t for t in range(8, 513, 8) if n_p % t == 0)
    ssm_p, srp, sip = _s5(usp[None], zero_state, zero_state, *s5w, tt_p)
    yp = _mix_out(att_p, ssm_p[0], hp, woa, wos, row2(mix_g_post), *w2)

    hs = _ffn(x_sample.reshape(n_s, D_MODEL), *w1)
    ks, vs, kis, uss, _, _, _, qts, qits, wts = _mix_in(hs, row2(mix_g_pre), wrow, wcol, n_s)
    pad_k = lambda a: jnp.pad(a, ((0, 0), (0, nk_s_pad - past - ts), (0, 0)))
    k_all = pad_k(jnp.concatenate([cache_k[l].reshape(ds, past, ATT_W), ks.reshape(ds, ts, ATT_W)], 1).astype(BF16))
    v_all = pad_k(jnp.concatenate([cache_v[l].reshape(ds, past, ATT_W), vs.reshape(ds, ts, ATT_W)], 1).astype(BF16))
    ki_all = pad_k(jnp.concatenate([cache_kidx[l], kis.reshape(ds, ts, IDX_DIM)], 1).astype(BF16))
    vt_all = jnp.transpose(v_all.reshape(ds, nk_s_pad // KEY_CHUNK, KEY_CHUNK, ATT_W), (0, 1, 3, 2))
    lanes = lambda a: jnp.pad(jnp.transpose(a.reshape(a.shape[0], ds, ts), (1, 0, 2)), ((0, 0), (0, 0), (0, TQ - ts)))
    sched_s, vis_s = _sample_schedule(past, ts)
    att_s = _attention(lanes(qits), lanes(wts), lanes(qts), vis_s, ki_all, k_all, vt_all,
                       _pair_bias(bias_tab, LANE + nk_s_pad, past), sched_s,
                       min(TOPK_MAX, (past + ts) // 4), False)
    att_s = att_s[:, :ts].reshape(n_s, ATT_W)
    ssm_s, srs, sis = _s5(uss.reshape(ds, ts, SSM_W), state_ssm_re[l].reshape(ds, 1, SSM_S),
                          state_ssm_im[l].reshape(ds, 1, SSM_S), *s5w, ts)
    ys = _mix_out(att_s, ssm_s.reshape(n_s, SSM_W), hs, woa, wos, row2(mix_g_post), *w2)

    heads = lambda a, b, t: a.reshape(1, b, t, N_HEADS, HEAD_DIM)
    state = lambda a, b: a.reshape(1, b, SSM_GROUPS, SSM_P)
    return (yp[N_META:][None], ys.reshape(ds, ts, D_MODEL),
            heads(kp, 1, n_p), heads(vp, 1, n_p), kip.reshape(1, 1, n_p, IDX_DIM), state(srp, 1), state(sip, 1),
            heads(ks, ds, ts), heads(vs, ds, ts), kis.reshape(1, ds, ts, IDX_DIM), state(srs, ds), state(sis, ds))
```

```python
import functools
import math

import numpy as np
import jax
import jax.numpy as jnp
from jax import lax
from jax.experimental import pallas as pl
from jax.experimental.pallas import tpu as pltpu

F32, BF16, I32 = jnp.float32, jnp.bfloat16, jnp.int32

D_MODEL = 1024
CHUNK = 64
N_META = 16
N_HEADS = 8
HEAD_DIM = 64
ATT_W = N_HEADS * HEAD_DIM
N_IDX = 8
IDX_DIM = 64
TOPK_MAX = 256
SSM_GROUPS = 32
SSM_GC = 16
SSM_W = SSM_GROUPS * SSM_GC
SSM_P = 64
SSM_S = SSM_GROUPS * SSM_P
D_FF = 2816
NUM_BUCKETS = 32
MAX_DISTANCE = 128
EPS = 1e-6

LANE = 128
VMEM_LIMIT = 62 * 1024 * 1024
ROW_TILE = 512
TQ = LANE
KEY_CHUNK = 512
PROMPT_BIAS_C0 = 512
PROMPT_BIAS_ROWS = 1280
NEG = -1e30
LOG2E = math.log2(math.e)
F32_MAX = float(np.finfo(np.float32).max)
N_PAIRS = N_HEADS // 2
SUM_ROWS = 16
CAND_DEPTH = 16
CAND_BATCH = 8 * CAND_DEPTH
CAND_GROUPS = 8
SCORE_ROUND = 8
TIE_SCAN_MAX = 8


def _batcher_network(n):
    pairs, p = [], 1
    while p < n:
        k = p
        while k >= 1:
            for j in range(k % p, n - k, 2 * k):
                for i in range(min(k, n - j - k)):
                    if (i + j) // (2 * p) == (i + j + k) // (2 * p):
                        pairs.append((i + j, i + j + k))
            k //= 2
        p *= 2
    return pairs


_SORT_NETWORK = _batcher_network(CAND_DEPTH)


def _rms(x, g):
    return x * lax.rsqrt(jnp.mean(x * x, axis=-1, keepdims=True) + EPS) * g


def _swiglu(xn, wg_ref, wu_ref, wd_ref):
    xb = xn.astype(BF16)
    a = jnp.dot(xb, wg_ref[...], preferred_element_type=F32)
    b = jnp.dot(xb, wu_ref[...], preferred_element_type=F32)
    hidden = (a * jax.nn.sigmoid(a) * b).astype(BF16)
    return jnp.dot(hidden, wd_ref[...], preferred_element_type=F32)


def _const_spec(shape):
    nd = len(shape)
    return pl.BlockSpec(shape, lambda *_: (0,) * nd, pipeline_mode=pl.Buffered(1))


def _ffn_body(x_ref, gpre_ref, wg_ref, wu_ref, wd_ref, gpost_ref, o_ref):
    x = x_ref[...]
    y = _swiglu(_rms(x, gpre_ref[...]), wg_ref, wu_ref, wd_ref)
    o_ref[...] = x + 0.5 * _rms(y, gpost_ref[...])


def _ffn(x, gpre, wg, wu, wd, gpost):
    n = x.shape[0]
    row = pl.BlockSpec((ROW_TILE, D_MODEL), lambda i: (i, 0))
    return pl.pallas_call(
        _ffn_body,
        grid=(pl.cdiv(n, ROW_TILE),),
        in_specs=[row, _const_spec((1, D_MODEL)), _const_spec((D_MODEL, D_FF)), _const_spec((D_MODEL, D_FF)),
                  _const_spec((D_FF, D_MODEL)), _const_spec((1, D_MODEL))],
        out_specs=row,
        out_shape=jax.ShapeDtypeStruct((n, D_MODEL), F32),
        compiler_params=pltpu.CompilerParams(dimension_semantics=("arbitrary",), vmem_limit_bytes=VMEM_LIMIT),
        name="ffn1",
    )(x, gpre, wg, wu, wd, gpost)


ROW_COLS = 3 * ATT_W + IDX_DIM
ROW_COLS_PAD = 13 * LANE
COL_ROWS = 3 * ATT_W + N_IDX


def _mix_in_body(n_valid, h_ref, g_ref, wrow_ref, wcol_ref,
                 k_ref, v_ref, ki_ref, us_ref, kb_ref, kib_ref, vtb_ref, qt_ref, qit_ref, wt_ref):
    i = pl.program_id(0)
    u = _rms(h_ref[...], g_ref[...]).astype(BF16)
    z = jnp.dot(u, wrow_ref[...], preferred_element_type=F32)
    zt = lax.dot_general(wcol_ref[...], u, (((1,), (1,)), ((), ())), preferred_element_type=F32)
    k = z[:, 0:ATT_W]
    v = z[:, ATT_W:2 * ATT_W]
    ki = z[:, 3 * ATT_W:3 * ATT_W + IDX_DIM]
    k_ref[...] = k
    v_ref[...] = v
    us_ref[...] = z[:, 2 * ATT_W:3 * ATT_W]
    ki_ref[...] = ki
    row_ok = (i * ROW_TILE + lax.broadcasted_iota(I32, (ROW_TILE, 1), 0)) < n_valid
    col_ok = (i * ROW_TILE + lax.broadcasted_iota(I32, (1, ROW_TILE), 1)) < n_valid
    kb_ref[...] = jnp.where(row_ok, k, 0.0).astype(BF16)
    kib_ref[...] = jnp.where(row_ok, ki, 0.0).astype(BF16)
    zt = jnp.where(col_ok, zt, 0.0)
    qt_ref[...] = (zt[0:ATT_W] * (HEAD_DIM ** -0.5 * LOG2E)).astype(BF16)
    vtb_ref[...] = zt[2 * ATT_W:3 * ATT_W].astype(BF16)
    wt = zt[3 * ATT_W:3 * ATT_W + N_IDX] * ((IDX_DIM ** -0.5) * (N_IDX ** -0.5))
    wt_ref[...] = wt
    for h in range(N_IDX):
        qi_h = zt[ATT_W + h * IDX_DIM:ATT_W + (h + 1) * IDX_DIM]
        qit_ref[h * IDX_DIM:(h + 1) * IDX_DIM, :] = (qi_h * wt[h:h + 1, :]).astype(BF16)


def _mix_in(h, g, wrow, wcol, n_pad):
    n = h.shape[0]
    steps = n_pad // ROW_TILE
    assert steps == pl.cdiv(n, ROW_TILE)

    def row(width):
        return pl.BlockSpec((ROW_TILE, width), lambda i: (i, 0))

    col = lambda rows: pl.BlockSpec((rows, ROW_TILE), lambda i: (0, i))
    out_shape = (
        jax.ShapeDtypeStruct((n, ATT_W), F32), jax.ShapeDtypeStruct((n, ATT_W), F32),
        jax.ShapeDtypeStruct((n, IDX_DIM), F32), jax.ShapeDtypeStruct((n, SSM_W), F32),
        jax.ShapeDtypeStruct((n_pad, ATT_W), BF16), jax.ShapeDtypeStruct((n_pad, IDX_DIM), BF16),
        jax.ShapeDtypeStruct((n_pad // ROW_TILE, ATT_W, ROW_TILE), BF16),
        jax.ShapeDtypeStruct((ATT_W, n_pad), BF16), jax.ShapeDtypeStruct((ATT_W, n_pad), BF16),
        jax.ShapeDtypeStruct((N_IDX, n_pad), F32),
    )
    out_specs = (
        row(ATT_W), row(ATT_W), row(IDX_DIM), row(SSM_W), row(ATT_W), row(IDX_DIM),
        pl.BlockSpec((None, ATT_W, ROW_TILE), lambda i: (i, 0, 0)),
        col(ATT_W), col(ATT_W), col(N_IDX),
    )
    return pl.pallas_call(
        functools.partial(_mix_in_body, n),
        grid=(steps,),
        in_specs=[row(D_MODEL), _const_spec((1, D_MODEL)), _const_spec((D_MODEL, ROW_COLS_PAD)),
                  _const_spec((COL_ROWS, D_MODEL))],
        out_specs=out_specs,
        out_shape=out_shape,
        compiler_params=pltpu.CompilerParams(dimension_semantics=("arbitrary",), vmem_limit_bytes=VMEM_LIMIT),
        name="mix_in",
    )(h, g, wrow, wcol)


def _ordered_bits_to_f32(u):
    k = u ^ jnp.int32(-2 ** 31)
    b = k ^ ((k >> 31) & jnp.int32(0x7FFFFFFF))
    return lax.bitcast_convert_type(b, F32)


def _attn_body(topk, idx_bits,
               nch_ref, nfull_ref, toff_ref,
               qit_ref, wt_ref, qt_ref, vis_ref, ki_ref, k_ref, vt_ref, bias_ref,
               o_ref,
               st_ref, cand_ref, thr_ref, cgt_ref, cge_ref, clip_ref, qicat_ref, qbd_ref, sa_ref, sb_ref,
               bma_ref, bmb_ref,
               m_ref, acc_ref, cut_ref, rowf_ref, ot_ref):
    j = pl.program_id(1)
    nch = nch_ref[j]
    toff = toff_ref[j]
    kf = float(topk)
    bias_tile_max = (bias_ref.shape[1] - KEY_CHUNK) // LANE - 1

    for h in range(N_IDX):
        qicat_ref[:, h * TQ:(h + 1) * TQ] = qit_ref[h * IDX_DIM:(h + 1) * IDX_DIM, :]
    qbd_ref[...] = jnp.zeros(qbd_ref.shape, BF16)
    for a in range(N_PAIRS):
        qbd_ref[a, 0:HEAD_DIM, 0:TQ] = qt_ref[2 * a * HEAD_DIM:(2 * a + 1) * HEAD_DIM, :]
        qbd_ref[a, HEAD_DIM:2 * HEAD_DIM, TQ:2 * TQ] = qt_ref[(2 * a + 1) * HEAD_DIM:(2 * a + 2) * HEAD_DIM, :]

    def chunk_start(c):
        return pl.multiple_of(c * KEY_CHUNK, KEY_CHUNK)

    def chunk_rows(r0):
        return r0 + lax.broadcasted_iota(I32, (KEY_CHUNK, TQ), 0)

    vis = vis_ref[...]
    cand_ref[...] = jnp.full(cand_ref.shape, -jnp.inf, F32)
    w_pos = wt_ref[...] >= 0.0
    clip_ref[0:N_IDX, :] = jnp.where(w_pos, 0.0, -jnp.inf)
    clip_ref[N_IDX:2 * N_IDX, :] = jnp.where(w_pos, jnp.inf, 0.0)

    last_batch_row = ki_ref.shape[0] - CAND_BATCH

    def score_round(masked, i, carry):
        for g in range(SCORE_ROUND):
            r0 = pl.multiple_of((i * SCORE_ROUND + g) * CAND_BATCH, CAND_BATCH)
            rel = jnp.dot(ki_ref[pl.ds(jnp.minimum(r0, last_batch_row), CAND_BATCH), :], qicat_ref[...],
                          preferred_element_type=F32)
            score = jnp.zeros((CAND_BATCH, TQ), F32)
            for h in range(N_IDX):
                score = score + jnp.clip(rel[:, h * TQ:(h + 1) * TQ], clip_ref[h:h + 1, :],
                                         clip_ref[N_IDX + h:N_IDX + h + 1, :])
            if masked:
                score = jnp.where(r0 + lax.broadcasted_iota(I32, (CAND_BATCH, TQ), 0) < vis, score, -jnp.inf)
            st_ref[pl.ds(r0, CAND_BATCH), :] = score
        return carry

    round_chunks = SCORE_ROUND * CAND_BATCH // KEY_CHUNK
    full_rounds = nfull_ref[j] // round_chunks
    lax.fori_loop(0, full_rounds, functools.partial(score_round, False), 0)
    lax.fori_loop(full_rounds, (nch + round_chunks - 1) // round_chunks, functools.partial(score_round, True), 0)

    def merge_batch(b, carry):
        r0 = pl.multiple_of(b * CAND_BATCH, CAND_BATCH)
        x = [st_ref[pl.ds(r0 + 8 * r, 8), :] for r in range(CAND_DEPTH)]
        for lo, hi in _SORT_NETWORK:
            x[lo], x[hi] = jnp.maximum(x[lo], x[hi]), jnp.minimum(x[lo], x[hi])
        g0 = pl.multiple_of(lax.rem(b, CAND_GROUPS) * CAND_BATCH, CAND_BATCH)
        t = [jnp.maximum(cand_ref[pl.ds(g0 + 8 * r, 8), :], x[CAND_DEPTH - 1 - r]) for r in range(CAND_DEPTH)]
        d = CAND_DEPTH // 2
        while d >= 1:
            for r in range(CAND_DEPTH):
                if r & d == 0:
                    t[r], t[r + d] = jnp.maximum(t[r], t[r + d]), jnp.minimum(t[r], t[r + d])
            d //= 2
        for r in range(CAND_DEPTH):
            cand_ref[pl.ds(g0 + 8 * r, 8), :] = t[r]
        return carry

    lax.fori_loop(0, nch * (KEY_CHUNK // CAND_BATCH), merge_batch, 0)

    def reduce_rows(ref, n_chunks, fn, combine, init):
        reducer = jnp.sum if combine is jnp.add else jnp.max

        def body(c, acc):
            r0 = chunk_start(c)
            val = fn(ref[pl.ds(r0, KEY_CHUNK), :], r0)
            part = reducer(val.reshape(8, KEY_CHUNK // 64, 8, TQ), axis=1)
            return combine(acc, reducer(part, axis=0))

        acc = lax.fori_loop(0, n_chunks, body, jnp.full((8, TQ), init, F32))
        return reducer(acc, axis=0, keepdims=True)

    def count(ref, n_chunks, indicator):
        return reduce_rows(ref, n_chunks, indicator, jnp.add, 0.0)

    def select(ref, n_chunks):
        def value_bit(i, res):
            trial = res | lax.shift_left(jnp.int32(1), 31 - i)
            thr_t = _ordered_bits_to_f32(trial)
            cnt = count(ref, n_chunks, lambda blk, r0: jnp.where(blk >= thr_t, 1.0, 0.0))
            return jnp.where(cnt >= kf, trial, res)

        res = lax.fori_loop(0, 32, value_bit, jnp.zeros((1, TQ), I32))
        res = jnp.where((res >> 23) == 0, jnp.int32(0x00800000), res)
        lo = _ordered_bits_to_f32(res)
        above = _ordered_bits_to_f32(res + 1)
        c_above = count(ref, n_chunks, lambda blk, r0: jnp.where(blk >= above, 1.0, 0.0))
        c_lo = count(ref, n_chunks, lambda blk, r0: jnp.where(blk >= lo, 1.0, 0.0))
        thr_ref[...] = lo
        cgt_ref[...] = c_above
        cge_ref[...] = c_lo

        def walking(state):
            return jnp.max(state[0]) > 0.0

        def walk(state):
            active, hi, c_hi = state
            v = reduce_rows(ref, n_chunks, lambda blk, r0: jnp.where(blk < hi, blk, -jnp.inf), jnp.maximum,
                            -jnp.inf)
            c_v = count(ref, n_chunks, lambda blk, r0: jnp.where(blk >= v, 1.0, 0.0))
            done = jnp.logical_and(active > 0.0, c_v >= kf)
            thr_ref[...] = jnp.where(done, v, thr_ref[...])
            cgt_ref[...] = jnp.where(done, c_hi, cgt_ref[...])
            cge_ref[...] = jnp.where(done, c_v, cge_ref[...])
            return jnp.where(done, 0.0, active), v, c_v

        surplus = jnp.where(jnp.logical_and(c_lo > kf, c_above < kf), 1.0, 0.0)
        lax.while_loop(walking, walk, (surplus, above, c_above))
        return lo

    cand_lo = select(cand_ref, cand_ref.shape[0] // KEY_CHUNK)
    kept_min = cand_ref[CAND_BATCH - 8:CAND_BATCH, :]
    for g in range(1, CAND_GROUPS):
        kept_min = jnp.maximum(kept_min, cand_ref[(g + 1) * CAND_BATCH - 8:(g + 1) * CAND_BATCH, :])
    unsafe = jnp.max(kept_min, axis=0, keepdims=True) >= cand_lo

    @pl.when(jnp.max(jnp.where(unsafe, 1.0, 0.0)) > 0.0)
    def _():
        select(st_ref, nch)

    thr = thr_ref[...]

    need = kf - cgt_ref[...]
    excess = cge_ref[...] - kf
    cut_ref[...] = jnp.full((1, TQ), F32_MAX, F32)
    rowf_ref[...] = lax.broadcasted_iota(I32, (KEY_CHUNK, TQ), 0).astype(F32)

    def chunk_rows_f(r0):
        return rowf_ref[...] + r0.astype(F32)

    tie_need = jnp.max(jnp.where(excess > 0.0, need, 0.0)).astype(I32)

    @pl.when(jnp.logical_and(tie_need > 0, tie_need <= TIE_SCAN_MAX))
    def _():
        def next_tied(cut_f):
            def body(c, acc):
                r0 = chunk_start(c)
                tied_row = jnp.where(st_ref[pl.ds(r0, KEY_CHUNK), :] == thr, chunk_rows_f(r0), F32_MAX)
                v = jnp.where(tied_row > cut_f, tied_row, F32_MAX)
                part = jnp.min(v.reshape(8, KEY_CHUNK // 64, 8, TQ), axis=1)
                return jnp.minimum(acc, jnp.min(part, axis=0))

            acc = lax.fori_loop(0, nch, body, jnp.full((8, TQ), F32_MAX, F32))
            return jnp.min(acc, axis=0, keepdims=True)

        def take(t, cut_f):
            return jnp.where(t.astype(F32) < need, next_tied(cut_f), cut_f)

        cut_f = lax.fori_loop(0, tie_need, take, jnp.full((1, TQ), -1.0, F32))
        cut_ref[...] = jnp.where(excess > 0.0, cut_f, F32_MAX)

    @pl.when(tie_need > TIE_SCAN_MAX)
    def _():
        def index_bit(i, cut):
            trial = cut | lax.shift_left(jnp.int32(1), idx_bits - 1 - i)
            cnt = count(st_ref, nch, lambda blk, r0: jnp.where(
                blk == thr, jnp.where(chunk_rows(r0) < trial, 1.0, 0.0), 0.0))
            return jnp.where(cnt < need, trial, cut)

        cut = lax.fori_loop(0, idx_bits, index_bit, jnp.zeros((1, TQ), I32))
        cut_ref[...] = jnp.where(excess > 0.0, cut.astype(F32), F32_MAX)

    cut_f = cut_ref[...]

    def mask_chunk(c, carry):
        r0 = chunk_start(c)
        blk = st_ref[pl.ds(r0, KEY_CHUNK), :]
        tie = jnp.where(chunk_rows_f(r0) <= cut_f, 0.0, NEG)
        st_ref[pl.ds(r0, KEY_CHUNK), :] = jnp.where(blk > thr, 0.0, jnp.where(blk == thr, tie, NEG))
        return carry

    lax.fori_loop(0, nch, mask_chunk, 0)
    st_ref[pl.ds(chunk_start(nch), KEY_CHUNK), :] = jnp.full((KEY_CHUNK, TQ), NEG, F32)

    m_ref[...] = jnp.full(m_ref.shape, NEG, F32)
    acc_ref[...] = jnp.zeros(acc_ref.shape, F32)
    last = nch - 1
    ones_rows = jnp.ones((SUM_ROWS, KEY_CHUNK), BF16)

    def logits(blk, s_ref, bmax_ref, with_bias):
        kblk = k_ref[pl.ds(chunk_start(jnp.minimum(blk, last)), KEY_CHUNK), :]
        msk = st_ref[pl.ds(chunk_start(jnp.minimum(blk, nch)), KEY_CHUNK), :]
        msk2 = jnp.concatenate([msk, msk], axis=1)
        bias_row0 = pl.multiple_of((jnp.clip(4 * blk + toff, -1, bias_tile_max) + 1) * LANE, LANE)
        for a in range(N_PAIRS):
            s2 = jnp.dot(kblk[:, a * 2 * HEAD_DIM:(a + 1) * 2 * HEAD_DIM], qbd_ref[a],
                         preferred_element_type=F32) + msk2
            if with_bias:
                s2 = s2 + bias_ref[a, pl.ds(bias_row0, KEY_CHUNK), :]
            s_ref[a] = s2
            bmax_ref[a:a + 1, :] = jnp.max(s2, axis=0, keepdims=True)

    def softmax_pv(blk, s_ref, bmax_ref):
        vblk = jnp.minimum(blk, last)
        for a in range(N_PAIRS):
            m_old = m_ref[a:a + 1, :]
            m_new = jnp.maximum(m_old, bmax_ref[a:a + 1, :])
            alpha = jnp.exp2(m_old - m_new)
            p = jnp.exp2(s_ref[a] - m_new).astype(BF16)
            v_ones = jnp.concatenate([vt_ref[vblk, a * 2 * HEAD_DIM:(a + 1) * 2 * HEAD_DIM, :], ones_rows], axis=0)
            acc_ref[a] = alpha * acc_ref[a] + jnp.dot(v_ones, p, preferred_element_type=F32)
            m_ref[a:a + 1, :] = m_new

    def block_pair(with_bias, i, carry):
        logits(2 * i + 1, sb_ref, bmb_ref, with_bias)
        softmax_pv(2 * i, sa_ref, bma_ref)
        logits(2 * i + 2, sa_ref, bma_ref, with_bias)
        softmax_pv(2 * i + 1, sb_ref, bmb_ref)
        return carry

    far_pairs = jnp.minimum(jnp.maximum(3 - toff, 0) // 4, nch) // 2
    logits(0, sa_ref, bma_ref, False)
    lax.fori_loop(0, far_pairs, functools.partial(block_pair, False), 0)
    logits(2 * far_pairs, sa_ref, bma_ref, True)
    lax.fori_loop(far_pairs, (nch + 1) // 2, functools.partial(block_pair, True), 0)

    for a in range(N_PAIRS):
        for e in range(2):
            h = 2 * a + e
            num = acc_ref[a, e * HEAD_DIM:(e + 1) * HEAD_DIM, e * TQ:(e + 1) * TQ]
            den = acc_ref[a, 2 * HEAD_DIM:2 * HEAD_DIM + 1, e * TQ:(e + 1) * TQ]
            ot_ref[h * HEAD_DIM:(h + 1) * HEAD_DIM, :] = num / den
    o_ref[...] = ot_ref[...].T


def _attention(qit, wt, qt, vis, ki, k, vt, bias, sched, topk, resident):
    s_count, _, nq = qit.shape
    nk = k.shape[1]
    n_tiles = sched[0].shape[0]
    mode = dict(pipeline_mode=pl.Buffered(1)) if resident else {}
    idx_bits = max(1, int(nk - 1).bit_length())
    logit_buf = pltpu.VMEM((N_PAIRS, KEY_CHUNK, 2 * TQ), F32)
    in_specs = [
        pl.BlockSpec((None, ATT_W, TQ), lambda s, j, *_: (s, 0, j)),
        pl.BlockSpec((None, N_IDX, TQ), lambda s, j, *_: (s, 0, j)),
        pl.BlockSpec((None, ATT_W, TQ), lambda s, j, *_: (s, 0, j)),
        pl.BlockSpec((1, TQ), lambda s, j, *_: (0, j)),
        pl.BlockSpec((None, nk, IDX_DIM), lambda s, j, *_: (s, 0, 0), **mode),
        pl.BlockSpec((None, nk, ATT_W), lambda s, j, *_: (s, 0, 0), **mode),
        pl.BlockSpec((None, nk // KEY_CHUNK, ATT_W, KEY_CHUNK), lambda s, j, *_: (s, 0, 0, 0), **mode),
        pl.BlockSpec(bias.shape, lambda s, j, *_: (0, 0, 0), pipeline_mode=pl.Buffered(1)),
    ]
    grid_spec = pltpu.PrefetchScalarGridSpec(
        num_scalar_prefetch=3,
        grid=(s_count, n_tiles),
        in_specs=in_specs,
        out_specs=pl.BlockSpec((None, TQ, ATT_W), lambda s, j, *_: (s, j, 0)),
        scratch_shapes=[
            pltpu.VMEM((nk + KEY_CHUNK, TQ), F32),
            pltpu.VMEM((CAND_GROUPS * CAND_BATCH, TQ), F32),
            pltpu.VMEM((1, TQ), F32),
            pltpu.VMEM((1, TQ), F32),
            pltpu.VMEM((1, TQ), F32),
            pltpu.VMEM((2 * N_IDX, TQ), F32),
            pltpu.VMEM((IDX_DIM, N_IDX * TQ), BF16),
            pltpu.VMEM((N_PAIRS, 2 * HEAD_DIM, 2 * TQ), BF16),
            logit_buf, logit_buf,
            pltpu.VMEM((N_PAIRS, 2 * TQ), F32),
            pltpu.VMEM((N_PAIRS, 2 * TQ), F32),
            pltpu.VMEM((N_PAIRS, 2 * TQ), F32),
            pltpu.VMEM((N_PAIRS, 2 * HEAD_DIM + SUM_ROWS, 2 * TQ), F32),
            pltpu.VMEM((1, TQ), F32),
            pltpu.VMEM((KEY_CHUNK, TQ), F32),
            pltpu.VMEM((ATT_W, TQ), F32),
        ],
    )
    return pl.pallas_call(
        functools.partial(_attn_body, topk, idx_bits),
        grid_spec=grid_spec,
        out_shape=jax.ShapeDtypeStruct((s_count, nq, ATT_W), F32),
        compiler_params=pltpu.CompilerParams(dimension_semantics=("arbitrary", "arbitrary"),
                                             vmem_limit_bytes=VMEM_LIMIT),
        name="attn",
    )(*sched, qit, wt, qt, vis, ki, k, vt, bias)


def _s5_body(tt, us_ref, s0r_ref, s0i_ref, ar_ref, ai_ref, bbr_ref, bbi_ref, cr_ref, ci_ref, d_ref, wglu_ref,
             bglu_ref, out_ref, sr_out_ref, si_out_ref, str_ref, sti_ref, bre_ref, bim_ref, xr_ref, xi_ref):
    @pl.when(pl.program_id(1) == 0)
    def _():
        str_ref[...] = s0r_ref[...]
        sti_ref[...] = s0i_ref[...]

    u = us_ref[...]
    ub = u.astype(BF16)
    bre_ref[...] = jnp.dot(ub, bbr_ref[...], preferred_element_type=F32)
    bim_ref[...] = jnp.dot(ub, bbi_ref[...], preferred_element_type=F32)
    ar = ar_ref[...]
    ai = ai_ref[...]

    def step(t, carry):
        sr, si = carry
        nr = ar * sr - ai * si + bre_ref[pl.ds(t, 1), :]
        ni = ar * si + ai * sr + bim_ref[pl.ds(t, 1), :]
        xr_ref[pl.ds(t, 1), :] = nr
        xi_ref[pl.ds(t, 1), :] = ni
        return nr, ni

    sr, si = lax.fori_loop(0, tt, step, (str_ref[...], sti_ref[...]))
    str_ref[...] = sr
    sti_ref[...] = si
    sr_out_ref[...] = sr
    si_out_ref[...] = si
    y = (jnp.dot(xr_ref[...].astype(BF16), cr_ref[...], preferred_element_type=F32)
         - jnp.dot(xi_ref[...].astype(BF16), ci_ref[...], preferred_element_type=F32)
         + d_ref[...] * u)
    g = jax.nn.gelu(y)
    gate = jnp.dot(g.astype(BF16), wglu_ref[...], preferred_element_type=F32) + bglu_ref[...]
    out_ref[...] = g * jax.nn.sigmoid(gate)


def _s5(us, s0r, s0i, ar, ai, bbr, bbi, cr, ci, d, wglu, bglu, tt):
    s_count, t_len, _ = us.shape
    assert t_len % tt == 0
    state = pl.BlockSpec((None, 1, SSM_S), lambda s, t: (s, 0, 0))
    seq = pl.BlockSpec((None, tt, SSM_W), lambda s, t: (s, t, 0))
    return pl.pallas_call(
        functools.partial(_s5_body, tt),
        grid=(s_count, t_len // tt),
        in_specs=[seq, state, state, _const_spec((1, SSM_S)), _const_spec((1, SSM_S)),
                  _const_spec((SSM_W, SSM_S)), _const_spec((SSM_W, SSM_S)),
                  _const_spec((SSM_S, SSM_W)), _const_spec((SSM_S, SSM_W)),
                  _const_spec((1, SSM_W)), _const_spec((SSM_W, SSM_W)), _const_spec((1, SSM_W))],
        out_specs=(seq, state, state),
        out_shape=(jax.ShapeDtypeStruct((s_count, t_len, SSM_W), F32),
                   jax.ShapeDtypeStruct((s_count, 1, SSM_S), F32),
                   jax.ShapeDtypeStruct((s_count, 1, SSM_S), F32)),
        scratch_shapes=[pltpu.VMEM((1, SSM_S), F32), pltpu.VMEM((1, SSM_S), F32),
                        pltpu.VMEM((tt, SSM_S), F32), pltpu.VMEM((tt, SSM_S), F32),
                        pltpu.VMEM((tt, SSM_S), F32), pltpu.VMEM((tt, SSM_S), F32)],
        compiler_params=pltpu.CompilerParams(dimension_semantics=("arbitrary", "arbitrary"),
                                             vmem_limit_bytes=VMEM_LIMIT),
        name="s5",
    )(us, s0r, s0i, ar, ai, bbr, bbi, cr, ci, d, wglu, bglu)


def _mix_out_body(att_ref, ssm_ref, h_ref, woa_ref, wos_ref, gmix_ref, gpre_ref, wg_ref, wu_ref, wd_ref, gpost_ref,
                  o_ref):
    m = (jnp.dot(att_ref[...].astype(BF16), woa_ref[...], preferred_element_type=F32)
         + jnp.dot(ssm_ref[...].astype(BF16), wos_ref[...], preferred_element_type=F32))
    h2 = h_ref[...] + _rms(m, gmix_ref[...])
    y = _swiglu(_rms(h2, gpre_ref[...]), wg_ref, wu_ref, wd_ref)
    o_ref[...] = h2 + 0.5 * _rms(y, gpost_ref[...])


def _mix_out(att, ssm, h, woa, wos, gmix, gpre, wg, wu, wd, gpost):
    n = h.shape[0]
    row = lambda width: pl.BlockSpec((ROW_TILE, width), lambda i: (i, 0))
    return pl.pallas_call(
        _mix_out_body,
        grid=(pl.cdiv(n, ROW_TILE),),
        in_specs=[row(ATT_W), row(SSM_W), row(D_MODEL), _const_spec((ATT_W, D_MODEL)),
                  _const_spec((SSM_W, D_MODEL)), _const_spec((1, D_MODEL)), _const_spec((1, D_MODEL)),
                  _const_spec((D_MODEL, D_FF)), _const_spec((D_MODEL, D_FF)), _const_spec((D_FF, D_MODEL)),
                  _const_spec((1, D_MODEL))],
        out_specs=row(D_MODEL),
        out_shape=jax.ShapeDtypeStruct((n, D_MODEL), F32),
        compiler_params=pltpu.CompilerParams(dimension_semantics=("arbitrary",), vmem_limit_bytes=VMEM_LIMIT),
        name="mix_out_ffn2",
    )(att, ssm, h, woa, wos, gmix, gpre, wg, wu, wd, gpost)


def _rel_bucket_np(rel):
    half = NUM_BUCKETS // 2
    max_exact = half // 2
    n = np.abs(rel).astype(np.int64)
    nf = np.maximum(n, 1).astype(np.float64)
    large = max_exact + (np.log(nf / max_exact) / math.log(MAX_DISTANCE / max_exact)
                         * (half - max_exact)).astype(np.int64)
    large = np.minimum(large, half - 1)
    return np.where(rel > 0, half, 0) + np.where(n < max_exact, n, large)


def _pair_bias(rel_bias, n_rows, c0):
    length = n_rows + TQ
    rel = np.arange(length) - (TQ - 1) - LANE - c0
    table = (rel_bias - rel_bias[NUM_BUCKETS // 2 - 1][None, :]) * LOG2E
    t1d = table[_rel_bucket_np(rel)].T
    x = jnp.tile(t1d, (1, TQ))[:, :TQ * (length - 1)].reshape(N_HEADS, TQ, length - 1)
    b = jnp.transpose(x[:, :, TQ - 1:TQ - 1 + n_rows], (0, 2, 1)).reshape(N_PAIRS, 2, n_rows, TQ)
    return jnp.concatenate([b[:, 0], b[:, 1]], axis=-1).astype(F32)


def _prompt_schedule(n, n_pad):
    pos = np.arange(n_pad)
    vis = np.where(pos < N_META, N_META, N_META + CHUNK * ((pos - N_META) // CHUNK + 1))
    vis = np.where(pos < n, np.minimum(vis, n), 0).astype(np.int32)
    n_tiles = -(-n // TQ)
    vmax = vis[:n_tiles * TQ].reshape(n_tiles, TQ).max(axis=1)
    nch = -(-vmax // KEY_CHUNK)
    nfull = vis[:n_tiles * TQ].reshape(n_tiles, TQ).min(axis=1) // KEY_CHUNK
    toff = PROMPT_BIAS_C0 // LANE - np.arange(n_tiles)
    assert ((nch - 1) * (KEY_CHUNK // LANE) + toff).max() <= (PROMPT_BIAS_ROWS - KEY_CHUNK) // LANE - 1
    sched = tuple(jnp.asarray(a, I32) for a in (nch, nfull, toff))
    return sched, jnp.asarray(vis[None, :])


def _sample_schedule(past, ts):
    nk = past + ts
    sched = tuple(jnp.asarray([v], I32) for v in (-(-nk // KEY_CHUNK), 0, 0))
    vis = jnp.asarray(np.where(np.arange(TQ) < ts, nk, 0)[None, :], I32)
    return sched, vis


def _s5_params(lam_re, lam_im, log_step, b_re, b_im, c_re, c_im, d_skip):
    dt = jnp.exp(log_step)[:, None]
    mag = jnp.exp(lam_re * dt)
    ab_re, ab_im = mag * jnp.cos(lam_im * dt), mag * jnp.sin(lam_im * dt)
    nr, ni = ab_re - 1.0, ab_im
    den = lam_re * lam_re + lam_im * lam_im
    f_re, f_im = (nr * lam_re + ni * lam_im) / den, (ni * lam_re - nr * lam_im) / den
    bb_re = f_re[..., None] * b_re - f_im[..., None] * b_im
    bb_im = f_re[..., None] * b_im + f_im[..., None] * b_re
    eye = jnp.eye(SSM_GROUPS, dtype=F32)
    bd_in = lambda w: jnp.einsum('gpc,gh->gchp', w, eye).reshape(SSM_W, SSM_S).astype(BF16)
    bd_out = lambda w: jnp.einsum('gcp,gh->gphc', w, eye).reshape(SSM_S, SSM_W).astype(BF16)
    return (ab_re.reshape(1, SSM_S), ab_im.reshape(1, SSM_S), bd_in(bb_re), bd_in(bb_im),
            bd_out(c_re), bd_out(c_im), d_skip.reshape(1, SSM_W))


def kernel(x_prompt, x_sample, cache_k, cache_v, cache_kidx, state_ssm_re, state_ssm_im, meta_tokens, rel_bias,
           ffn1_g_pre, ffn1_w_gate, ffn1_w_up, ffn1_w_down, ffn1_g_post, mix_g_pre, w_in, w_out, mix_g_post,
           lam_re, lam_im, log_step, b_re, b_im, c_re, c_im, d_skip, w_glu, b_glu,
           ffn2_g_pre, ffn2_w_gate, ffn2_w_up, ffn2_w_down, ffn2_g_post):
    depth = ffn1_g_pre.shape[0]
    assert depth == 1
    bp, seq, _ = x_prompt.shape
    assert bp == 1
    ds, ts, _ = x_sample.shape
    past = cache_k.shape[2]
    n_p = N_META + seq
    n_p_pad = pl.cdiv(n_p, ROW_TILE) * ROW_TILE
    n_s = ds * ts
    assert n_s % ROW_TILE == 0 and ts <= TQ
    nk_s_pad = pl.cdiv(past + ts, KEY_CHUNK) * KEY_CHUNK
    l = 0

    row2 = lambda g: g[l].reshape(1, -1)
    w1 = (row2(ffn1_g_pre), ffn1_w_gate[l].astype(BF16), ffn1_w_up[l].astype(BF16), ffn1_w_down[l].astype(BF16),
          row2(ffn1_g_post))
    w2 = (row2(ffn2_g_pre), ffn2_w_gate[l].astype(BF16), ffn2_w_up[l].astype(BF16), ffn2_w_down[l].astype(BF16),
          row2(ffn2_g_post))
    wi = w_in[l]
    o_q, o_k, o_v, o_qi, o_ki, o_wi, o_us = np.cumsum((0, ATT_W, ATT_W, ATT_W, N_IDX * IDX_DIM, IDX_DIM, N_IDX))
    sl = lambda o, w: wi[:, o:o + w]
    wrow = jnp.concatenate([sl(o_k, ATT_W), sl(o_v, ATT_W), sl(o_us, SSM_W), sl(o_ki, IDX_DIM),
                            jnp.zeros((D_MODEL, ROW_COLS_PAD - ROW_COLS), F32)], axis=1).astype(BF16)
    wcol = jnp.concatenate([sl(o_q, ATT_W), sl(o_qi, ATT_W), sl(o_v, ATT_W), sl(o_wi, N_IDX)], axis=1).T.astype(BF16)
    woa, wos = w_out[l][:ATT_W].astype(BF16), w_out[l][ATT_W:].astype(BF16)
    s5w = _s5_params(lam_re[l], lam_im[l], log_step[l], b_re[l], b_im[l], c_re[l], c_im[l], d_skip[l])
    s5w = s5w + (w_glu[l].astype(BF16), b_glu[l].reshape(1, SSM_W))
    bias_tab = rel_bias.astype(F32)

    xp = jnp.concatenate([meta_tokens.astype(F32), x_prompt[0]], axis=0)
    hp = _ffn(xp, *w1)
    kp, vp, kip, usp, kbp, kibp, vtbp, qtp, qitp, wtp = _mix_in(hp, row2(mix_g_pre), wrow, wcol, n_p_pad)
    sched_p, vis_p = _prompt_schedule(n_p, n_p_pad)
    att_p = _attention(qitp[None], wtp[None], qtp[None], vis_p, kibp[None], kbp[None], vtbp[None],
                       _pair_bias(bias_tab, PROMPT_BIAS_ROWS, PROMPT_BIAS_C0), sched_p,
                       min(TOPK_MAX, seq // 4), True)[0]
    zero_state = jnp.zeros((1, 1, SSM_S), F32)
    tt_p = max(t for t in range(8, 513, 8) if n_p % t == 0)
    ssm_p, srp, sip = _s5(usp[None], zero_state, zero_state, *s5w, tt_p)
    yp = _mix_out(att_p, ssm_p[0], hp, woa, wos, row2(mix_g_post), *w2)

    hs = _ffn(x_sample.reshape(n_s, D_MODEL), *w1)
    ks, vs, kis, uss, _, _, _, qts, qits, wts = _mix_in(hs, row2(mix_g_pre), wrow, wcol, n_s)
    pad_k = lambda a: jnp.pad(a, ((0, 0), (0, nk_s_pad - past - ts), (0, 0)))
    k_all = pad_k(jnp.concatenate([cache_k[l].reshape(ds, past, ATT_W), ks.reshape(ds, ts, ATT_W)], 1).astype(BF16))
    v_all = pad_k(jnp.concatenate([cache_v[l].reshape(ds, past, ATT_W), vs.reshape(ds, ts, ATT_W)], 1).astype(BF16))
    ki_all = pad_k(jnp.concatenate([cache_kidx[l], kis.reshape(ds, ts, IDX_DIM)], 1).astype(BF16))
    vt_all = jnp.transpose(v_all.reshape(ds, nk_s_pad // KEY_CHUNK, KEY_CHUNK, ATT_W), (0, 1, 3, 2))
    lanes = lambda a: jnp.pad(jnp.transpose(a.reshape(a.shape[0], ds, ts), (1, 0, 2)), ((0, 0), (0, 0), (0, TQ - ts)))
    sched_s, vis_s = _sample_schedule(past, ts)
    att_s = _attention(lanes(qits), lanes(wts), lanes(qts), vis_s, ki_all, k_all, vt_all,
                       _pair_bias(bias_tab, LANE + nk_s_pad, past), sched_s,
                       min(TOPK_MAX, (past + ts) // 4), False)
    att_s = att_s[:, :ts].reshape(n_s, ATT_W)
    ssm_s, srs, sis = _s5(uss.reshape(ds, ts, SSM_W), state_ssm_re[l].reshape(ds, 1, SSM_S),
                          state_ssm_im[l].reshape(ds, 1, SSM_S), *s5w, ts)
    ys = _mix_out(att_s, ssm_s.reshape(n_s, SSM_W), hs, woa, wos, row2(mix_g_post), *w2)

    heads = lambda a, b, t: a.reshape(1, b, t, N_HEADS, HEAD_DIM)
    state = lambda a, b: a.reshape(1, b, SSM_GROUPS, SSM_P)
    return (yp[N_META:][None], ys.reshape(ds, ts, D_MODEL),
            heads(kp, 1, n_p), heads(vp, 1, n_p), kip.reshape(1, 1, n_p, IDX_DIM), state(srp, 1), state(sip, 1),
            heads(ks, ds, ts), heads(vs, ds, ts), kis.reshape(1, ds, ts, IDX_DIM), state(srs, ds), state(sis, ds))
```

```python
import functools
import math

import numpy as np
import jax
import jax.numpy as jnp
from jax import lax
from jax.experimental import pallas as pl
from jax.experimental.pallas import tpu as pltpu

F32, BF16, I32 = jnp.float32, jnp.bfloat16, jnp.int32

D_MODEL = 1024
CHUNK = 64
N_META = 16
N_HEADS = 8
HEAD_DIM = 64
ATT_W = N_HEADS * HEAD_DIM
N_IDX = 8
IDX_DIM = 64
TOPK_MAX = 256
SSM_GROUPS = 32
SSM_GC = 16
SSM_W = SSM_GROUPS * SSM_GC
SSM_P = 64
SSM_S = SSM_GROUPS * SSM_P
D_FF = 2816
NUM_BUCKETS = 32
MAX_DISTANCE = 128
EPS = 1e-6

LANE = 128
VMEM_LIMIT = 62 * 1024 * 1024
ROW_TILE = 512
TQ = LANE
KEY_CHUNK = 512
PROMPT_BIAS_C0 = 512
PROMPT_BIAS_ROWS = 1280
NEG = -1e30
LOG2E = math.log2(math.e)
F32_MAX = float(np.finfo(np.float32).max)
N_PAIRS = N_HEADS // 2
SUM_ROWS = 16
CAND_DEPTH = 16
CAND_BATCH = 8 * CAND_DEPTH
CAND_GROUPS = 8
S5_SEGS = 8
S5_CHUNKS = SSM_S // LANE
SCORE_ROUND = 8
TIE_SCAN_MAX = 8


def _batcher_network(n):
    pairs, p = [], 1
    while p < n:
        k = p
        while k >= 1:
            for j in range(k % p, n - k, 2 * k):
                for i in range(min(k, n - j - k)):
                    if (i + j) // (2 * p) == (i + j + k) // (2 * p):
                        pairs.append((i + j, i + j + k))
            k //= 2
        p *= 2
    return pairs


_SORT_NETWORK = _batcher_network(CAND_DEPTH)


def _rms(x, g):
    return x * lax.rsqrt(jnp.mean(x * x, axis=-1, keepdims=True) + EPS) * g


def _swiglu(xn, wg_ref, wu_ref, wd_ref):
    xb = xn.astype(BF16)
    a = jnp.dot(xb, wg_ref[...], preferred_element_type=F32)
    b = jnp.dot(xb, wu_ref[...], preferred_element_type=F32)
    hidden = (a * jax.nn.sigmoid(a) * b).astype(BF16)
    return jnp.dot(hidden, wd_ref[...], preferred_element_type=F32)


def _const_spec(shape):
    nd = len(shape)
    return pl.BlockSpec(shape, lambda *_: (0,) * nd, pipeline_mode=pl.Buffered(1))


def _ffn_body(x_ref, gpre_ref, wg_ref, wu_ref, wd_ref, gpost_ref, o_ref):
    x = x_ref[...]
    y = _swiglu(_rms(x, gpre_ref[...]), wg_ref, wu_ref, wd_ref)
    o_ref[...] = x + 0.5 * _rms(y, gpost_ref[...])


def _ffn(x, gpre, wg, wu, wd, gpost):
    n = x.shape[0]
    row = pl.BlockSpec((ROW_TILE, D_MODEL), lambda i: (i, 0))
    return pl.pallas_call(
        _ffn_body,
        grid=(pl.cdiv(n, ROW_TILE),),
        in_specs=[row, _const_spec((1, D_MODEL)), _const_spec((D_MODEL, D_FF)), _const_spec((D_MODEL, D_FF)),
                  _const_spec((D_FF, D_MODEL)), _const_spec((1, D_MODEL))],
        out_specs=row,
        out_shape=jax.ShapeDtypeStruct((n, D_MODEL), F32),
        compiler_params=pltpu.CompilerParams(dimension_semantics=("arbitrary",), vmem_limit_bytes=VMEM_LIMIT),
        name="ffn1",
    )(x, gpre, wg, wu, wd, gpost)


ROW_COLS = 3 * ATT_W + IDX_DIM
ROW_COLS_PAD = 13 * LANE
COL_ROWS = 3 * ATT_W + N_IDX


def _mix_in_body(n_valid, h_ref, g_ref, wrow_ref, wcol_ref,
                 k_ref, v_ref, ki_ref, us_ref, kb_ref, kib_ref, vtb_ref, qt_ref, qit_ref, wt_ref):
    i = pl.program_id(0)
    u = _rms(h_ref[...], g_ref[...]).astype(BF16)
    z = jnp.dot(u, wrow_ref[...], preferred_element_type=F32)
    zt = lax.dot_general(wcol_ref[...], u, (((1,), (1,)), ((), ())), preferred_element_type=F32)
    k = z[:, 0:ATT_W]
    v = z[:, ATT_W:2 * ATT_W]
    ki = z[:, 3 * ATT_W:3 * ATT_W + IDX_DIM]
    k_ref[...] = k
    v_ref[...] = v
    us_ref[...] = z[:, 2 * ATT_W:3 * ATT_W]
    ki_ref[...] = ki
    row_ok = (i * ROW_TILE + lax.broadcasted_iota(I32, (ROW_TILE, 1), 0)) < n_valid
    col_ok = (i * ROW_TILE + lax.broadcasted_iota(I32, (1, ROW_TILE), 1)) < n_valid
    kb_ref[...] = jnp.where(row_ok, k, 0.0).astype(BF16)
    kib_ref[...] = jnp.where(row_ok, ki, 0.0).astype(BF16)
    zt = jnp.where(col_ok, zt, 0.0)
    qt_ref[...] = (zt[0:ATT_W] * (HEAD_DIM ** -0.5 * LOG2E)).astype(BF16)
    vtb_ref[...] = zt[2 * ATT_W:3 * ATT_W].astype(BF16)
    wt = zt[3 * ATT_W:3 * ATT_W + N_IDX] * ((IDX_DIM ** -0.5) * (N_IDX ** -0.5))
    wt_ref[...] = wt
    for h in range(N_IDX):
        qi_h = zt[ATT_W + h * IDX_DIM:ATT_W + (h + 1) * IDX_DIM]
        qit_ref[h * IDX_DIM:(h + 1) * IDX_DIM, :] = (qi_h * wt[h:h + 1, :]).astype(BF16)


def _mix_in(h, g, wrow, wcol, n_pad):
    n = h.shape[0]
    steps = n_pad // ROW_TILE
    assert steps == pl.cdiv(n, ROW_TILE)

    def row(width):
        return pl.BlockSpec((ROW_TILE, width), lambda i: (i, 0))

    col = lambda rows: pl.BlockSpec((rows, ROW_TILE), lambda i: (0, i))
    out_shape = (
        jax.ShapeDtypeStruct((n, ATT_W), F32), jax.ShapeDtypeStruct((n, ATT_W), F32),
        jax.ShapeDtypeStruct((n, IDX_DIM), F32), jax.ShapeDtypeStruct((n, SSM_W), F32),
        jax.ShapeDtypeStruct((n_pad, ATT_W), BF16), jax.ShapeDtypeStruct((n_pad, IDX_DIM), BF16),
        jax.ShapeDtypeStruct((n_pad // ROW_TILE, ATT_W, ROW_TILE), BF16),
        jax.ShapeDtypeStruct((ATT_W, n_pad), BF16), jax.ShapeDtypeStruct((ATT_W, n_pad), BF16),
        jax.ShapeDtypeStruct((N_IDX, n_pad), F32),
    )
    out_specs = (
        row(ATT_W), row(ATT_W), row(IDX_DIM), row(SSM_W), row(ATT_W), row(IDX_DIM),
        pl.BlockSpec((None, ATT_W, ROW_TILE), lambda i: (i, 0, 0)),
        col(ATT_W), col(ATT_W), col(N_IDX),
    )
    return pl.pallas_call(
        functools.partial(_mix_in_body, n),
        grid=(steps,),
        in_specs=[row(D_MODEL), _const_spec((1, D_MODEL)), _const_spec((D_MODEL, ROW_COLS_PAD)),
                  _const_spec((COL_ROWS, D_MODEL))],
        out_specs=out_specs,
        out_shape=out_shape,
        compiler_params=pltpu.CompilerParams(dimension_semantics=("arbitrary",), vmem_limit_bytes=VMEM_LIMIT),
        name="mix_in",
    )(h, g, wrow, wcol)


def _f32_to_ordered_bits(x):
    b = lax.bitcast_convert_type(x, I32)
    return b ^ ((b >> 31) & jnp.int32(0x7FFFFFFF)) ^ jnp.int32(-2 ** 31)


def _ordered_bits_to_f32(u):
    k = u ^ jnp.int32(-2 ** 31)
    b = k ^ ((k >> 31) & jnp.int32(0x7FFFFFFF))
    return lax.bitcast_convert_type(b, F32)


def _attn_body(topk, idx_bits,
               nch_ref, nfull_ref, toff_ref,
               qit_ref, wt_ref, qt_ref, vis_ref, ki_ref, k_ref, vt_ref, bias_ref,
               o_ref,
               st_ref, cand_ref, thr_ref, cgt_ref, cge_ref, clip_ref, qicat_ref, qbd_ref, sa_ref, sb_ref,
               bma_ref, bmb_ref,
               m_ref, acc_ref, cut_ref, rowf_ref, ot_ref):
    j = pl.program_id(1)
    nch = nch_ref[j]
    toff = toff_ref[j]
    kf = float(topk)
    bias_tile_max = (bias_ref.shape[1] - KEY_CHUNK) // LANE - 1

    for h in range(N_IDX):
        qicat_ref[:, h * TQ:(h + 1) * TQ] = qit_ref[h * IDX_DIM:(h + 1) * IDX_DIM, :]
    qbd_ref[...] = jnp.zeros(qbd_ref.shape, BF16)
    for a in range(N_PAIRS):
        qbd_ref[a, 0:HEAD_DIM, 0:TQ] = qt_ref[2 * a * HEAD_DIM:(2 * a + 1) * HEAD_DIM, :]
        qbd_ref[a, HEAD_DIM:2 * HEAD_DIM, TQ:2 * TQ] = qt_ref[(2 * a + 1) * HEAD_DIM:(2 * a + 2) * HEAD_DIM, :]

    def chunk_start(c):
        return pl.multiple_of(c * KEY_CHUNK, KEY_CHUNK)

    def chunk_rows(r0):
        return r0 + lax.broadcasted_iota(I32, (KEY_CHUNK, TQ), 0)

    vis = vis_ref[...]
    cand_ref[...] = jnp.full(cand_ref.shape, -jnp.inf, F32)
    w_pos = wt_ref[...] >= 0.0
    clip_ref[0:N_IDX, :] = jnp.where(w_pos, 0.0, -jnp.inf)
    clip_ref[N_IDX:2 * N_IDX, :] = jnp.where(w_pos, jnp.inf, 0.0)

    last_batch_row = ki_ref.shape[0] - CAND_BATCH

    def score_round(masked, i, carry):
        for g in range(SCORE_ROUND):
            r0 = pl.multiple_of((i * SCORE_ROUND + g) * CAND_BATCH, CAND_BATCH)
            rel = jnp.dot(ki_ref[pl.ds(jnp.minimum(r0, last_batch_row), CAND_BATCH), :], qicat_ref[...],
                          preferred_element_type=F32)
            score = jnp.zeros((CAND_BATCH, TQ), F32)
            for h in range(N_IDX):
                score = score + jnp.clip(rel[:, h * TQ:(h + 1) * TQ], clip_ref[h:h + 1, :],
                                         clip_ref[N_IDX + h:N_IDX + h + 1, :])
            if masked:
                score = jnp.where(r0 + lax.broadcasted_iota(I32, (CAND_BATCH, TQ), 0) < vis, score, -jnp.inf)
            st_ref[pl.ds(r0, CAND_BATCH), :] = score
        return carry

    round_chunks = SCORE_ROUND * CAND_BATCH // KEY_CHUNK
    full_rounds = nfull_ref[j] // round_chunks
    lax.fori_loop(0, full_rounds, functools.partial(score_round, False), 0)
    lax.fori_loop(full_rounds, (nch + round_chunks - 1) // round_chunks, functools.partial(score_round, True), 0)

    def merge_batch(b, carry):
        r0 = pl.multiple_of(b * CAND_BATCH, CAND_BATCH)
        x = [st_ref[pl.ds(r0 + 8 * r, 8), :] for r in range(CAND_DEPTH)]
        for lo, hi in _SORT_NETWORK:
            x[lo], x[hi] = jnp.maximum(x[lo], x[hi]), jnp.minimum(x[lo], x[hi])
        g0 = pl.multiple_of(lax.rem(b, CAND_GROUPS) * CAND_BATCH, CAND_BATCH)
        t = [jnp.maximum(cand_ref[pl.ds(g0 + 8 * r, 8), :], x[CAND_DEPTH - 1 - r]) for r in range(CAND_DEPTH)]
        d = CAND_DEPTH // 2
        while d >= 1:
            for r in range(CAND_DEPTH):
                if r & d == 0:
                    t[r], t[r + d] = jnp.maximum(t[r], t[r + d]), jnp.minimum(t[r], t[r + d])
            d //= 2
        for r in range(CAND_DEPTH):
            cand_ref[pl.ds(g0 + 8 * r, 8), :] = t[r]
        return carry

    lax.fori_loop(0, nch * (KEY_CHUNK // CAND_BATCH), merge_batch, 0)

    def reduce_rows(ref, n_chunks, fn, combine, init):
        reducer = jnp.sum if combine is jnp.add else jnp.max

        def body(c, acc):
            r0 = chunk_start(c)
            val = fn(ref[pl.ds(r0, KEY_CHUNK), :], r0)
            part = reducer(val.reshape(8, KEY_CHUNK // 64, 8, TQ), axis=1)
            return combine(acc, reducer(part, axis=0))

        acc = lax.fori_loop(0, n_chunks, body, jnp.full((8, TQ), init, F32))
        return reducer(acc, axis=0, keepdims=True)

    def count(ref, n_chunks, indicator):
        return reduce_rows(ref, n_chunks, indicator, jnp.add, 0.0)

    def select(ref, n_chunks, bounds=None):
        def value_bit(i, res):
            trial = res | lax.shift_left(jnp.int32(1), 31 - i)
            thr_t = _ordered_bits_to_f32(trial)
            cnt = count(ref, n_chunks, lambda blk, r0: jnp.where(blk >= thr_t, 1.0, 0.0))
            return jnp.where(cnt >= kf, trial, res)

        if bounds is None:
            first_bit, res0 = 0, jnp.zeros((1, TQ), I32)
        else:
            lo_bits, hi_bits = _f32_to_ordered_bits(bounds[0]), _f32_to_ordered_bits(bounds[1])
            first_bit = jnp.min(lax.clz(lo_bits ^ hi_bits))
            keep = jnp.where(first_bit > 0, lax.shift_left(jnp.int32(-1), 32 - jnp.maximum(first_bit, 1)), 0)
            res0 = lo_bits & keep
        res = lax.fori_loop(first_bit, 32, value_bit, res0)
        res = jnp.where((res >> 23) == 0, jnp.int32(0x00800000), res)
        lo = _ordered_bits_to_f32(res)
        above = _ordered_bits_to_f32(res + 1)
        c_above = count(ref, n_chunks, lambda blk, r0: jnp.where(blk >= above, 1.0, 0.0))
        c_lo = count(ref, n_chunks, lambda blk, r0: jnp.where(blk >= lo, 1.0, 0.0))
        thr_ref[...] = lo
        cgt_ref[...] = c_above
        cge_ref[...] = c_lo

        def walking(state):
            return jnp.max(state[0]) > 0.0

        def walk(state):
            active, hi, c_hi = state
            v = reduce_rows(ref, n_chunks, lambda blk, r0: jnp.where(blk < hi, blk, -jnp.inf), jnp.maximum,
                            -jnp.inf)
            c_v = count(ref, n_chunks, lambda blk, r0: jnp.where(blk >= v, 1.0, 0.0))
            done = jnp.logical_and(active > 0.0, c_v >= kf)
            thr_ref[...] = jnp.where(done, v, thr_ref[...])
            cgt_ref[...] = jnp.where(done, c_hi, cgt_ref[...])
            cge_ref[...] = jnp.where(done, c_v, cge_ref[...])
            return jnp.where(done, 0.0, active), v, c_v

        surplus = jnp.where(jnp.logical_and(c_lo > kf, c_above < kf), 1.0, 0.0)
        lax.while_loop(walking, walk, (surplus, above, c_above))
        return lo

    cells = CAND_GROUPS * 8
    rank = -(-topk // cells) - 1
    bounds = None
    if rank < CAND_DEPTH:
        at_rank = [cand_ref[g * CAND_BATCH + 8 * rank:g * CAND_BATCH + 8 * rank + 8, :] for g in range(CAND_GROUPS)]
        bounds = (jnp.min(functools.reduce(jnp.minimum, at_rank), axis=0, keepdims=True),
                  jnp.max(functools.reduce(jnp.maximum, at_rank), axis=0, keepdims=True))
    cand_lo = select(cand_ref, cand_ref.shape[0] // KEY_CHUNK, bounds)
    kept_min = cand_ref[CAND_BATCH - 8:CAND_BATCH, :]
    for g in range(1, CAND_GROUPS):
        kept_min = jnp.maximum(kept_min, cand_ref[(g + 1) * CAND_BATCH - 8:(g + 1) * CAND_BATCH, :])
    unsafe = jnp.max(kept_min, axis=0, keepdims=True) >= cand_lo

    @pl.when(jnp.max(jnp.where(unsafe, 1.0, 0.0)) > 0.0)
    def _():
        select(st_ref, nch)

    thr = thr_ref[...]

    need = kf - cgt_ref[...]
    excess = cge_ref[...] - kf
    cut_ref[...] = jnp.full((1, TQ), F32_MAX, F32)
    rowf_ref[...] = lax.broadcasted_iota(I32, (KEY_CHUNK, TQ), 0).astype(F32)

    def chunk_rows_f(r0):
        return rowf_ref[...] + r0.astype(F32)

    tie_need = jnp.max(jnp.where(excess > 0.0, need, 0.0)).astype(I32)

    @pl.when(jnp.logical_and(tie_need > 0, tie_need <= TIE_SCAN_MAX))
    def _():
        def next_tied(cut_f):
            def body(c, acc):
                r0 = chunk_start(c)
                tied_row = jnp.where(st_ref[pl.ds(r0, KEY_CHUNK), :] == thr, chunk_rows_f(r0), F32_MAX)
                v = jnp.where(tied_row > cut_f, tied_row, F32_MAX)
                part = jnp.min(v.reshape(8, KEY_CHUNK // 64, 8, TQ), axis=1)
                return jnp.minimum(acc, jnp.min(part, axis=0))

            acc = lax.fori_loop(0, nch, body, jnp.full((8, TQ), F32_MAX, F32))
            return jnp.min(acc, axis=0, keepdims=True)

        def take(t, cut_f):
            return jnp.where(t.astype(F32) < need, next_tied(cut_f), cut_f)

        cut_f = lax.fori_loop(0, tie_need, take, jnp.full((1, TQ), -1.0, F32))
        cut_ref[...] = jnp.where(excess > 0.0, cut_f, F32_MAX)

    @pl.when(tie_need > TIE_SCAN_MAX)
    def _():
        def index_bit(i, cut):
            trial = cut | lax.shift_left(jnp.int32(1), idx_bits - 1 - i)
            cnt = count(st_ref, nch, lambda blk, r0: jnp.where(
                blk == thr, jnp.where(chunk_rows(r0) < trial, 1.0, 0.0), 0.0))
            return jnp.where(cnt < need, trial, cut)

        cut = lax.fori_loop(0, idx_bits, index_bit, jnp.zeros((1, TQ), I32))
        cut_ref[...] = jnp.where(excess > 0.0, cut.astype(F32), F32_MAX)

    cut_f = cut_ref[...]

    def mask_chunk(c, carry):
        r0 = chunk_start(c)
        blk = st_ref[pl.ds(r0, KEY_CHUNK), :]
        tie = jnp.where(chunk_rows_f(r0) <= cut_f, 0.0, NEG)
        st_ref[pl.ds(r0, KEY_CHUNK), :] = jnp.where(blk > thr, 0.0, jnp.where(blk == thr, tie, NEG))
        return carry

    lax.fori_loop(0, nch, mask_chunk, 0)
    st_ref[pl.ds(chunk_start(nch), KEY_CHUNK), :] = jnp.full((KEY_CHUNK, TQ), NEG, F32)

    m_ref[...] = jnp.full(m_ref.shape, NEG, F32)
    acc_ref[...] = jnp.zeros(acc_ref.shape, F32)
    last = nch - 1
    ones_rows = jnp.ones((SUM_ROWS, KEY_CHUNK), BF16)

    def logits(blk, s_ref, bmax_ref, with_bias):
        kblk = k_ref[pl.ds(chunk_start(jnp.minimum(blk, last)), KEY_CHUNK), :]
        msk = st_ref[pl.ds(chunk_start(jnp.minimum(blk, nch)), KEY_CHUNK), :]
        msk2 = jnp.concatenate([msk, msk], axis=1)
        bias_row0 = pl.multiple_of((jnp.clip(4 * blk + toff, -1, bias_tile_max) + 1) * LANE, LANE)
        for a in range(N_PAIRS):
            s2 = jnp.dot(kblk[:, a * 2 * HEAD_DIM:(a + 1) * 2 * HEAD_DIM], qbd_ref[a],
                         preferred_element_type=F32) + msk2
            if with_bias:
                s2 = s2 + bias_ref[a, pl.ds(bias_row0, KEY_CHUNK), :]
            s_ref[a] = s2
            bmax_ref[a:a + 1, :] = jnp.max(s2, axis=0, keepdims=True)

    def softmax_pv(blk, s_ref, bmax_ref):
        vblk = jnp.minimum(blk, last)
        for a in range(N_PAIRS):
            m_old = m_ref[a:a + 1, :]
            m_new = jnp.maximum(m_old, bmax_ref[a:a + 1, :])
            alpha = jnp.exp2(m_old - m_new)
            p = jnp.exp2(s_ref[a] - m_new).astype(BF16)
            v_ones = jnp.concatenate([vt_ref[vblk, a * 2 * HEAD_DIM:(a + 1) * 2 * HEAD_DIM, :], ones_rows], axis=0)
            acc_ref[a] = alpha * acc_ref[a] + jnp.dot(v_ones, p, preferred_element_type=F32)
            m_ref[a:a + 1, :] = m_new

    def block_pair(with_bias, i, carry):
        logits(2 * i + 1, sb_ref, bmb_ref, with_bias)
        softmax_pv(2 * i, sa_ref, bma_ref)
        logits(2 * i + 2, sa_ref, bma_ref, with_bias)
        softmax_pv(2 * i + 1, sb_ref, bmb_ref)
        return carry

    far_pairs = jnp.minimum(jnp.maximum(3 - toff, 0) // 4, nch) // 2
    logits(0, sa_ref, bma_ref, False)
    lax.fori_loop(0, far_pairs, functools.partial(block_pair, False), 0)
    logits(2 * far_pairs, sa_ref, bma_ref, True)
    lax.fori_loop(far_pairs, (nch + 1) // 2, functools.partial(block_pair, True), 0)

    for a in range(N_PAIRS):
        for e in range(2):
            h = 2 * a + e
            num = acc_ref[a, e * HEAD_DIM:(e + 1) * HEAD_DIM, e * TQ:(e + 1) * TQ]
            den = acc_ref[a, 2 * HEAD_DIM:2 * HEAD_DIM + 1, e * TQ:(e + 1) * TQ]
            ot_ref[h * HEAD_DIM:(h + 1) * HEAD_DIM, :] = num / den
    o_ref[...] = ot_ref[...].T


def _attention(qit, wt, qt, vis, ki, k, vt, bias, sched, topk, resident):
    s_count, _, nq = qit.shape
    nk = k.shape[1]
    n_tiles = sched[0].shape[0]
    mode = dict(pipeline_mode=pl.Buffered(1)) if resident else {}
    idx_bits = max(1, int(nk - 1).bit_length())
    logit_buf = pltpu.VMEM((N_PAIRS, KEY_CHUNK, 2 * TQ), F32)
    in_specs = [
        pl.BlockSpec((None, ATT_W, TQ), lambda s, j, *_: (s, 0, j)),
        pl.BlockSpec((None, N_IDX, TQ), lambda s, j, *_: (s, 0, j)),
        pl.BlockSpec((None, ATT_W, TQ), lambda s, j, *_: (s, 0, j)),
        pl.BlockSpec((1, TQ), lambda s, j, *_: (0, j)),
        pl.BlockSpec((None, nk, IDX_DIM), lambda s, j, *_: (s, 0, 0), **mode),
        pl.BlockSpec((None, nk, ATT_W), lambda s, j, *_: (s, 0, 0), **mode),
        pl.BlockSpec((None, nk // KEY_CHUNK, ATT_W, KEY_CHUNK), lambda s, j, *_: (s, 0, 0, 0), **mode),
        pl.BlockSpec(bias.shape, lambda s, j, *_: (0, 0, 0), pipeline_mode=pl.Buffered(1)),
    ]
    grid_spec = pltpu.PrefetchScalarGridSpec(
        num_scalar_prefetch=3,
        grid=(s_count, n_tiles),
        in_specs=in_specs,
        out_specs=pl.BlockSpec((None, TQ, ATT_W), lambda s, j, *_: (s, j, 0)),
        scratch_shapes=[
            pltpu.VMEM((nk + KEY_CHUNK, TQ), F32),
            pltpu.VMEM((CAND_GROUPS * CAND_BATCH, TQ), F32),
            pltpu.VMEM((1, TQ), F32),
            pltpu.VMEM((1, TQ), F32),
            pltpu.VMEM((1, TQ), F32),
            pltpu.VMEM((2 * N_IDX, TQ), F32),
            pltpu.VMEM((IDX_DIM, N_IDX * TQ), BF16),
            pltpu.VMEM((N_PAIRS, 2 * HEAD_DIM, 2 * TQ), BF16),
            logit_buf, logit_buf,
            pltpu.VMEM((N_PAIRS, 2 * TQ), F32),
            pltpu.VMEM((N_PAIRS, 2 * TQ), F32),
            pltpu.VMEM((N_PAIRS, 2 * TQ), F32),
            pltpu.VMEM((N_PAIRS, 2 * HEAD_DIM + SUM_ROWS, 2 * TQ), F32),
            pltpu.VMEM((1, TQ), F32),
            pltpu.VMEM((KEY_CHUNK, TQ), F32),
            pltpu.VMEM((ATT_W, TQ), F32),
        ],
    )
    return pl.pallas_call(
        functools.partial(_attn_body, topk, idx_bits),
        grid_spec=grid_spec,
        out_shape=jax.ShapeDtypeStruct((s_count, nq, ATT_W), F32),
        compiler_params=pltpu.CompilerParams(dimension_semantics=("arbitrary", "arbitrary"),
                                             vmem_limit_bytes=VMEM_LIMIT),
        name="attn",
    )(*sched, qit, wt, qt, vis, ki, k, vt, bias)


def _s5_body(tt, us_ref, s0r_ref, s0i_ref, ar_ref, ai_ref, pr_ref, pi_ref, bbr_ref, bbi_ref, cr_ref, ci_ref, d_ref,
             wglu_ref, bglu_ref, out_ref, sr_out_ref, si_out_ref, str_ref, sti_ref, xr_ref, xi_ref):
    seg = tt // S5_SEGS

    @pl.when(pl.program_id(1) == 0)
    def _():
        str_ref[...] = s0r_ref[...]
        sti_ref[...] = s0i_ref[...]

    u = us_ref[...]
    ub = u.astype(BF16)
    b_re = jnp.dot(ub, bbr_ref[...], preferred_element_type=F32)
    b_im = jnp.dot(ub, bbi_ref[...], preferred_element_type=F32)
    for c in range(S5_CHUNKS):
        xr_ref[c] = b_re[:, c * LANE:(c + 1) * LANE]
        xi_ref[c] = b_im[:, c * LANE:(c + 1) * LANE]

    def rows(ref, t):
        return jnp.stack([ref[c, pl.ds(t, S5_SEGS, stride=seg), :] for c in range(S5_CHUNKS)])

    def put(ref, t, val):
        for c in range(S5_CHUNKS):
            ref[c, pl.ds(t, S5_SEGS, stride=seg), :] = val[c]

    ar = ar_ref[...]
    ai = ai_ref[...]

    def step(t, carry):
        sr, si = carry
        nr = ar * sr - ai * si + rows(xr_ref, t)
        ni = ar * si + ai * sr + rows(xi_ref, t)
        put(xr_ref, t, nr)
        put(xi_ref, t, ni)
        return nr, ni

    zero = jnp.zeros((S5_CHUNKS, S5_SEGS, LANE), F32)
    end_r, end_i = lax.fori_loop(0, seg, step, (zero, zero))

    al_r = pr_ref[seg - 1]
    al_i = pi_ref[seg - 1]
    cur_r = str_ref[...]
    cur_i = sti_ref[...]
    start_r, start_i = [], []
    for k in range(S5_SEGS):
        start_r.append(cur_r)
        start_i.append(cur_i)
        e_r = end_r[:, k:k + 1, :]
        e_i = end_i[:, k:k + 1, :]
        cur_r, cur_i = al_r * cur_r - al_i * cur_i + e_r, al_r * cur_i + al_i * cur_r + e_i
    str_ref[...] = cur_r
    sti_ref[...] = cur_i
    sr_out_ref[...] = cur_r
    si_out_ref[...] = cur_i
    s_r = jnp.concatenate(start_r, axis=1)
    s_i = jnp.concatenate(start_i, axis=1)

    def complete(t, carry):
        p_r = pr_ref[t]
        p_i = pi_ref[t]
        put(xr_ref, t, rows(xr_ref, t) + p_r * s_r - p_i * s_i)
        put(xi_ref, t, rows(xi_ref, t) + p_r * s_i + p_i * s_r)
        return carry

    lax.fori_loop(0, seg, complete, 0)
    x_re = jnp.concatenate([xr_ref[c] for c in range(S5_CHUNKS)], axis=1)
    x_im = jnp.concatenate([xi_ref[c] for c in range(S5_CHUNKS)], axis=1)
    y = (jnp.dot(x_re.astype(BF16), cr_ref[...], preferred_element_type=F32)
         - jnp.dot(x_im.astype(BF16), ci_ref[...], preferred_element_type=F32)
         + d_ref[...] * u)
    g = jax.nn.gelu(y)
    gate = jnp.dot(g.astype(BF16), wglu_ref[...], preferred_element_type=F32) + bglu_ref[...]
    out_ref[...] = g * jax.nn.sigmoid(gate)


def _s5(us, s0r, s0i, a_re, a_im, log_a_re, log_a_im, bbr, bbi, cr, ci, d, wglu, bglu, tt):
    s_count, t_len, _ = us.shape
    assert t_len % tt == 0 and tt % S5_SEGS == 0
    seg = tt // S5_SEGS
    chunked = (S5_CHUNKS, 1, LANE)
    steps = jnp.arange(1, seg + 1, dtype=F32)[:, None]
    mag = jnp.exp(steps * log_a_re[None, :])
    pr = (mag * jnp.cos(steps * log_a_im[None, :])).reshape((seg,) + chunked)
    pi = (mag * jnp.sin(steps * log_a_im[None, :])).reshape((seg,) + chunked)
    state = pl.BlockSpec((None,) + chunked, lambda s, t: (s, 0, 0, 0))
    seq = pl.BlockSpec((None, tt, SSM_W), lambda s, t: (s, t, 0))
    state_shape = jax.ShapeDtypeStruct((s_count,) + chunked, F32)
    ssm, sr, si = pl.pallas_call(
        functools.partial(_s5_body, tt),
        grid=(s_count, t_len // tt),
        in_specs=[seq, state, state, _const_spec(chunked), _const_spec(chunked),
                  _const_spec((seg,) + chunked), _const_spec((seg,) + chunked),
                  _const_spec((SSM_W, SSM_S)), _const_spec((SSM_W, SSM_S)),
                  _const_spec((SSM_S, SSM_W)), _const_spec((SSM_S, SSM_W)),
                  _const_spec((1, SSM_W)), _const_spec((SSM_W, SSM_W)), _const_spec((1, SSM_W))],
        out_specs=(seq, state, state),
        out_shape=(jax.ShapeDtypeStruct((s_count, t_len, SSM_W), F32), state_shape, state_shape),
        scratch_shapes=[pltpu.VMEM(chunked, F32), pltpu.VMEM(chunked, F32),
                        pltpu.VMEM((S5_CHUNKS, tt, LANE), F32), pltpu.VMEM((S5_CHUNKS, tt, LANE), F32)],
        compiler_params=pltpu.CompilerParams(dimension_semantics=("arbitrary", "arbitrary"),
                                             vmem_limit_bytes=VMEM_LIMIT),
        name="s5",
    )(us, s0r.reshape((s_count,) + chunked), s0i.reshape((s_count,) + chunked), a_re.reshape(chunked),
      a_im.reshape(chunked), pr, pi, bbr, bbi, cr, ci, d, wglu, bglu)
    return ssm, sr.reshape(s_count, SSM_S), si.reshape(s_count, SSM_S)


def _mix_out_body(att_ref, ssm_ref, h_ref, woa_ref, wos_ref, gmix_ref, gpre_ref, wg_ref, wu_ref, wd_ref, gpost_ref,
                  o_ref):
    m = (jnp.dot(att_ref[...].astype(BF16), woa_ref[...], preferred_element_type=F32)
         + jnp.dot(ssm_ref[...].astype(BF16), wos_ref[...], preferred_element_type=F32))
    h2 = h_ref[...] + _rms(m, gmix_ref[...])
    y = _swiglu(_rms(h2, gpre_ref[...]), wg_ref, wu_ref, wd_ref)
    o_ref[...] = h2 + 0.5 * _rms(y, gpost_ref[...])


def _mix_out(att, ssm, h, woa, wos, gmix, gpre, wg, wu, wd, gpost):
    n = h.shape[0]
    row = lambda width: pl.BlockSpec((ROW_TILE, width), lambda i: (i, 0))
    return pl.pallas_call(
        _mix_out_body,
        grid=(pl.cdiv(n, ROW_TILE),),
        in_specs=[row(ATT_W), row(SSM_W), row(D_MODEL), _const_spec((ATT_W, D_MODEL)),
                  _const_spec((SSM_W, D_MODEL)), _const_spec((1, D_MODEL)), _const_spec((1, D_MODEL)),
                  _const_spec((D_MODEL, D_FF)), _const_spec((D_MODEL, D_FF)), _const_spec((D_FF, D_MODEL)),
                  _const_spec((1, D_MODEL))],
        out_specs=row(D_MODEL),
        out_shape=jax.ShapeDtypeStruct((n, D_MODEL), F32),
        compiler_params=pltpu.CompilerParams(dimension_semantics=("arbitrary",), vmem_limit_bytes=VMEM_LIMIT),
        name="mix_out_ffn2",
    )(att, ssm, h, woa, wos, gmix, gpre, wg, wu, wd, gpost)


def _rel_bucket_np(rel):
    half = NUM_BUCKETS // 2
    max_exact = half // 2
    n = np.abs(rel).astype(np.int64)
    nf = np.maximum(n, 1).astype(np.float64)
    large = max_exact + (np.log(nf / max_exact) / math.log(MAX_DISTANCE / max_exact)
                         * (half - max_exact)).astype(np.int64)
    large = np.minimum(large, half - 1)
    return np.where(rel > 0, half, 0) + np.where(n < max_exact, n, large)


def _pair_bias(rel_bias, n_rows, c0):
    length = n_rows + TQ
    rel = np.arange(length) - (TQ - 1) - LANE - c0
    table = (rel_bias - rel_bias[NUM_BUCKETS // 2 - 1][None, :]) * LOG2E
    t1d = table[_rel_bucket_np(rel)].T
    x = jnp.tile(t1d, (1, TQ))[:, :TQ * (length - 1)].reshape(N_HEADS, TQ, length - 1)
    b = jnp.transpose(x[:, :, TQ - 1:TQ - 1 + n_rows], (0, 2, 1)).reshape(N_PAIRS, 2, n_rows, TQ)
    return jnp.concatenate([b[:, 0], b[:, 1]], axis=-1).astype(F32)


def _prompt_schedule(n, n_pad):
    pos = np.arange(n_pad)
    vis = np.where(pos < N_META, N_META, N_META + CHUNK * ((pos - N_META) // CHUNK + 1))
    vis = np.where(pos < n, np.minimum(vis, n), 0).astype(np.int32)
    n_tiles = -(-n // TQ)
    vmax = vis[:n_tiles * TQ].reshape(n_tiles, TQ).max(axis=1)
    nch = -(-vmax // KEY_CHUNK)
    nfull = vis[:n_tiles * TQ].reshape(n_tiles, TQ).min(axis=1) // KEY_CHUNK
    toff = PROMPT_BIAS_C0 // LANE - np.arange(n_tiles)
    assert ((nch - 1) * (KEY_CHUNK // LANE) + toff).max() <= (PROMPT_BIAS_ROWS - KEY_CHUNK) // LANE - 1
    sched = tuple(jnp.asarray(a, I32) for a in (nch, nfull, toff))
    return sched, jnp.asarray(vis[None, :])


def _sample_schedule(past, ts):
    nk = past + ts
    sched = tuple(jnp.asarray([v], I32) for v in (-(-nk // KEY_CHUNK), 0, 0))
    vis = jnp.asarray(np.where(np.arange(TQ) < ts, nk, 0)[None, :], I32)
    return sched, vis


def _s5_params(lam_re, lam_im, log_step, b_re, b_im, c_re, c_im, d_skip):
    dt = jnp.exp(log_step)[:, None]
    mag = jnp.exp(lam_re * dt)
    ab_re, ab_im = mag * jnp.cos(lam_im * dt), mag * jnp.sin(lam_im * dt)
    nr, ni = ab_re - 1.0, ab_im
    den = lam_re * lam_re + lam_im * lam_im
    f_re, f_im = (nr * lam_re + ni * lam_im) / den, (ni * lam_re - nr * lam_im) / den
    bb_re = f_re[..., None] * b_re - f_im[..., None] * b_im
    bb_im = f_re[..., None] * b_im + f_im[..., None] * b_re
    eye = jnp.eye(SSM_GROUPS, dtype=F32)
    bd_in = lambda w: jnp.einsum('gpc,gh->gchp', w, eye).reshape(SSM_W, SSM_S).astype(BF16)
    bd_out = lambda w: jnp.einsum('gcp,gh->gphc', w, eye).reshape(SSM_S, SSM_W).astype(BF16)
    return (ab_re.reshape(SSM_S), ab_im.reshape(SSM_S), (lam_re * dt).reshape(SSM_S), (lam_im * dt).reshape(SSM_S),
            bd_in(bb_re), bd_in(bb_im), bd_out(c_re), bd_out(c_im), d_skip.reshape(1, SSM_W))


def kernel(x_prompt, x_sample, cache_k, cache_v, cache_kidx, state_ssm_re, state_ssm_im, meta_tokens, rel_bias,
           ffn1_g_pre, ffn1_w_gate, ffn1_w_up, ffn1_w_down, ffn1_g_post, mix_g_pre, w_in, w_out, mix_g_post,
           lam_re, lam_im, log_step, b_re, b_im, c_re, c_im, d_skip, w_glu, b_glu,
           ffn2_g_pre, ffn2_w_gate, ffn2_w_up, ffn2_w_down, ffn2_g_post):
    depth = ffn1_g_pre.shape[0]
    assert depth == 1
    bp, seq, _ = x_prompt.shape
    assert bp == 1
    ds, ts, _ = x_sample.shape
    past = cache_k.shape[2]
    n_p = N_META + seq
    n_p_pad = pl.cdiv(n_p, ROW_TILE) * ROW_TILE
    n_s = ds * ts
    assert n_s % ROW_TILE == 0 and ts <= TQ
    nk_s_pad = pl.cdiv(past + ts, KEY_CHUNK) * KEY_CHUNK
    l = 0

    row2 = lambda g: g[l].reshape(1, -1)
    w1 = (row2(ffn1_g_pre), ffn1_w_gate[l].astype(BF16), ffn1_w_up[l].astype(BF16), ffn1_w_down[l].astype(BF16),
          row2(ffn1_g_post))
    w2 = (row2(ffn2_g_pre), ffn2_w_gate[l].astype(BF16), ffn2_w_up[l].astype(BF16), ffn2_w_down[l].astype(BF16),
          row2(ffn2_g_post))
    wi = w_in[l]
    o_q, o_k, o_v, o_qi, o_ki, o_wi, o_us = np.cumsum((0, ATT_W, ATT_W, ATT_W, N_IDX * IDX_DIM, IDX_DIM, N_IDX))
    sl = lambda o, w: wi[:, o:o + w]
    wrow = jnp.concatenate([sl(o_k, ATT_W), sl(o_v, ATT_W), sl(o_us, SSM_W), sl(o_ki, IDX_DIM),
                            jnp.zeros((D_MODEL, ROW_COLS_PAD - ROW_COLS), F32)], axis=1).astype(BF16)
    wcol = jnp.concatenate([sl(o_q, ATT_W), sl(o_qi, ATT_W), sl(o_v, ATT_W), sl(o_wi, N_IDX)], axis=1).T.astype(BF16)
    woa, wos = w_out[l][:ATT_W].astype(BF16), w_out[l][ATT_W:].astype(BF16)
    s5w = _s5_params(lam_re[l], lam_im[l], log_step[l], b_re[l], b_im[l], c_re[l], c_im[l], d_skip[l])
    s5w = s5w + (w_glu[l].astype(BF16), b_glu[l].reshape(1, SSM_W))
    bias_tab = rel_bias.astype(F32)

    xp = jnp.concatenate([meta_tokens.astype(F32), x_prompt[0]], axis=0)
    hp = _ffn(xp, *w1)
    kp, vp, kip, usp, kbp, kibp, vtbp, qtp, qitp, wtp = _mix_in(hp, row2(mix_g_pre), wrow, wcol, n_p_pad)
    sched_p, vis_p = _prompt_schedule(n_p, n_p_pad)
    att_p = _attention(qitp[None], wtp[None], qtp[None], vis_p, kibp[None], kbp[None], vtbp[None],
                       _pair_bias(bias_tab, PROMPT_BIAS_ROWS, PROMPT_BIAS_C0), sched_p,
                       min(TOPK_MAX, seq // 4), True)[0]
    zero_state = jnp.zeros((1, SSM_S), F32)
    tt_p = max(t for t in range(8, 513, 8) if n_p % t == 0)
    ssm_p, srp, sip = _s5(usp[None], zero_state, zero_state, *s5w, tt_p)
    yp = _mix_out(att_p, ssm_p[0], hp, woa, wos, row2(mix_g_post), *w2)

    hs = _ffn(x_sample.reshape(n_s, D_MODEL), *w1)
    ks, vs, kis, uss, _, _, _, qts, qits, wts = _mix_in(hs, row2(mix_g_pre), wrow, wcol, n_s)
    pad_k = lambda a: jnp.pad(a, ((0, 0), (0, nk_s_pad - past - ts), (0, 0)))
    k_all = pad_k(jnp.concatenate([cache_k[l].reshape(ds, past, ATT_W), ks.reshape(ds, ts, ATT_W)], 1).astype(BF16))
    v_all = pad_k(jnp.concatenate([cache_v[l].reshape(ds, past, ATT_W), vs.reshape(ds, ts, ATT_W)], 1).astype(BF16))
    ki_all = pad_k(jnp.concatenate([cache_kidx[l], kis.reshape(ds, ts, IDX_DIM)], 1).astype(BF16))
    vt_all = jnp.transpose(v_all.reshape(ds, nk_s_pad // KEY_CHUNK, KEY_CHUNK, ATT_W), (0, 1, 3, 2))
    lanes = lambda a: jnp.pad(jnp.transpose(a.reshape(a.shape[0], ds, ts), (1, 0, 2)), ((0, 0), (0, 0), (0, TQ - ts)))
    sched_s, vis_s = _sample_schedule(past, ts)
    att_s = _attention(lanes(qits), lanes(wts), lanes(qts), vis_s, ki_all, k_all, vt_all,
                       _pair_bias(bias_tab, LANE + nk_s_pad, past), sched_s,
                       min(TOPK_MAX, (past + ts) // 4), False)
    att_s = att_s[:, :ts].reshape(n_s, ATT_W)
    ssm_s, srs, sis = _s5(uss.reshape(ds, ts, SSM_W), state_ssm_re[l].reshape(ds, SSM_S),
                          state_ssm_im[l].reshape(ds, SSM_S), *s5w, ts)
    ys = _mix_out(att_s, ssm_s.reshape(n_s, SSM_W), hs, woa, wos, row2(mix_g_post), *w2)

    heads = lambda a, b, t: a.reshape(1, b, t, N_HEADS, HEAD_DIM)
    state = lambda a, b: a.reshape(1, b, SSM_GROUPS, SSM_P)
    return (yp[N_META:][None], ys.reshape(ds, ts, D_MODEL),
            heads(kp, 1, n_p), heads(vp, 1, n_p), kip.reshape(1, 1, n_p, IDX_DIM), state(srp, 1), state(sip, 1),
            heads(ks, ds, ts), heads(vs, ds, ts), kis.reshape(1, ds, ts, IDX_DIM), state(srs, ds), state(sis, ds))
```

```python
import functools
import math

import numpy as np
import jax
import jax.numpy as jnp
from jax import lax
from jax.experimental import pallas as pl
from jax.experimental.pallas import tpu as pltpu

F32, BF16, I32 = jnp.float32, jnp.bfloat16, jnp.int32

D_MODEL = 1024
CHUNK = 64
N_META = 16
N_HEADS = 8
HEAD_DIM = 64
ATT_W = N_HEADS * HEAD_DIM
N_IDX = 8
IDX_DIM = 64
TOPK_MAX = 256
SSM_GROUPS = 32
SSM_GC = 16
SSM_W = SSM_GROUPS * SSM_GC
SSM_P = 64
SSM_S = SSM_GROUPS * SSM_P
D_FF = 2816
NUM_BUCKETS = 32
MAX_DISTANCE = 128
EPS = 1e-6

LANE = 128
VMEM_LIMIT = 62 * 1024 * 1024
ROW_TILE = 512
TQ = LANE
KEY_CHUNK = 512
PROMPT_BIAS_C0 = 512
PROMPT_BIAS_ROWS = 1280
NEG = -1e30
LOG2E = math.log2(math.e)
F32_MAX = float(np.finfo(np.float32).max)
N_PAIRS = N_HEADS // 2
SUM_ROWS = 16
CAND_DEPTH = 16
CAND_BATCH = 8 * CAND_DEPTH
CAND_GROUPS = 8
SCORE_ROUND = 8
TIE_SCAN_MAX = 8


def _batcher_network(n):
    pairs, p = [], 1
    while p < n:
        k = p
        while k >= 1:
            for j in range(k % p, n - k, 2 * k):
                for i in range(min(k, n - j - k)):
                    if (i + j) // (2 * p) == (i + j + k) // (2 * p):
                        pairs.append((i + j, i + j + k))
            k //= 2
        p *= 2
    return pairs


_SORT_NETWORK = _batcher_network(CAND_DEPTH)


def _rms(x, g):
    return x * lax.rsqrt(jnp.mean(x * x, axis=-1, keepdims=True) + EPS) * g


def _swiglu(xn, wg_ref, wu_ref, wd_ref):
    xb = xn.astype(BF16)
    a = jnp.dot(xb, wg_ref[...], preferred_element_type=F32)
    b = jnp.dot(xb, wu_ref[...], preferred_element_type=F32)
    hidden = (a * jax.nn.sigmoid(a) * b).astype(BF16)
    return jnp.dot(hidden, wd_ref[...], preferred_element_type=F32)


def _const_spec(shape):
    nd = len(shape)
    return pl.BlockSpec(shape, lambda *_: (0,) * nd, pipeline_mode=pl.Buffered(1))


def _ffn_body(x_ref, gpre_ref, wg_ref, wu_ref, wd_ref, gpost_ref, o_ref):
    x = x_ref[...]
    y = _swiglu(_rms(x, gpre_ref[...]), wg_ref, wu_ref, wd_ref)
    o_ref[...] = x + 0.5 * _rms(y, gpost_ref[...])


def _ffn(x, gpre, wg, wu, wd, gpost):
    n = x.shape[0]
    row = pl.BlockSpec((ROW_TILE, D_MODEL), lambda i: (i, 0))
    return pl.pallas_call(
        _ffn_body,
        grid=(pl.cdiv(n, ROW_TILE),),
        in_specs=[row, _const_spec((1, D_MODEL)), _const_spec((D_MODEL, D_FF)), _const_spec((D_MODEL, D_FF)),
                  _const_spec((D_FF, D_MODEL)), _const_spec((1, D_MODEL))],
        out_specs=row,
        out_shape=jax.ShapeDtypeStruct((n, D_MODEL), F32),
        compiler_params=pltpu.CompilerParams(dimension_semantics=("arbitrary",), vmem_limit_bytes=VMEM_LIMIT),
        name="ffn1",
    )(x, gpre, wg, wu, wd, gpost)


ROW_COLS = 3 * ATT_W + IDX_DIM
ROW_COLS_PAD = 13 * LANE
COL_ROWS = 3 * ATT_W + N_IDX


def _mix_in_body(n_valid, h_ref, g_ref, wrow_ref, wcol_ref,
                 k_ref, v_ref, ki_ref, us_ref, kb_ref, kib_ref, vtb_ref, qt_ref, qit_ref, wt_ref):
    i = pl.program_id(0)
    u = _rms(h_ref[...], g_ref[...]).astype(BF16)
    z = jnp.dot(u, wrow_ref[...], preferred_element_type=F32)
    zt = lax.dot_general(wcol_ref[...], u, (((1,), (1,)), ((), ())), preferred_element_type=F32)
    k = z[:, 0:ATT_W]
    v = z[:, ATT_W:2 * ATT_W]
    ki = z[:, 3 * ATT_W:3 * ATT_W + IDX_DIM]
    k_ref[...] = k
    v_ref[...] = v
    us_ref[...] = z[:, 2 * ATT_W:3 * ATT_W]
    ki_ref[...] = ki
    row_ok = (i * ROW_TILE + lax.broadcasted_iota(I32, (ROW_TILE, 1), 0)) < n_valid
    col_ok = (i * ROW_TILE + lax.broadcasted_iota(I32, (1, ROW_TILE), 1)) < n_valid
    kb_ref[...] = jnp.where(row_ok, k, 0.0).astype(BF16)
    kib_ref[...] = jnp.where(row_ok, ki, 0.0).astype(BF16)
    zt = jnp.where(col_ok, zt, 0.0)
    qt_ref[...] = (zt[0:ATT_W] * (HEAD_DIM ** -0.5 * LOG2E)).astype(BF16)
    vtb_ref[...] = zt[2 * ATT_W:3 * ATT_W].astype(BF16)
    wt = zt[3 * ATT_W:3 * ATT_W + N_IDX] * ((IDX_DIM ** -0.5) * (N_IDX ** -0.5))
    wt_ref[...] = wt
    for h in range(N_IDX):
        qi_h = zt[ATT_W + h * IDX_DIM:ATT_W + (h + 1) * IDX_DIM]
        qit_ref[h * IDX_DIM:(h + 1) * IDX_DIM, :] = (qi_h * wt[h:h + 1, :]).astype(BF16)


def _mix_in(h, g, wrow, wcol, n_pad):
    n = h.shape[0]
    steps = n_pad // ROW_TILE
    assert steps == pl.cdiv(n, ROW_TILE)

    def row(width):
        return pl.BlockSpec((ROW_TILE, width), lambda i: (i, 0))

    col = lambda rows: pl.BlockSpec((rows, ROW_TILE), lambda i: (0, i))
    out_shape = (
        jax.ShapeDtypeStruct((n, ATT_W), F32), jax.ShapeDtypeStruct((n, ATT_W), F32),
        jax.ShapeDtypeStruct((n, IDX_DIM), F32), jax.ShapeDtypeStruct((n, SSM_W), F32),
        jax.ShapeDtypeStruct((n_pad, ATT_W), BF16), jax.ShapeDtypeStruct((n_pad, IDX_DIM), BF16),
        jax.ShapeDtypeStruct((n_pad // ROW_TILE, ATT_W, ROW_TILE), BF16),
        jax.ShapeDtypeStruct((ATT_W, n_pad), BF16), jax.ShapeDtypeStruct((ATT_W, n_pad), BF16),
        jax.ShapeDtypeStruct((N_IDX, n_pad), F32),
    )
    out_specs = (
        row(ATT_W), row(ATT_W), row(IDX_DIM), row(SSM_W), row(ATT_W), row(IDX_DIM),
        pl.BlockSpec((None, ATT_W, ROW_TILE), lambda i: (i, 0, 0)),
        col(ATT_W), col(ATT_W), col(N_IDX),
    )
    return pl.pallas_call(
        functools.partial(_mix_in_body, n),
        grid=(steps,),
        in_specs=[row(D_MODEL), _const_spec((1, D_MODEL)), _const_spec((D_MODEL, ROW_COLS_PAD)),
                  _const_spec((COL_ROWS, D_MODEL))],
        out_specs=out_specs,
        out_shape=out_shape,
        compiler_params=pltpu.CompilerParams(dimension_semantics=("arbitrary",), vmem_limit_bytes=VMEM_LIMIT),
        name="mix_in",
    )(h, g, wrow, wcol)


def _ordered_bits_to_f32(u):
    k = u ^ jnp.int32(-2 ** 31)
    b = k ^ ((k >> 31) & jnp.int32(0x7FFFFFFF))
    return lax.bitcast_convert_type(b, F32)


def _attn_body(topk, idx_bits,
               nch_ref, nfull_ref, toff_ref,
               qit_ref, wt_ref, qt_ref, vis_ref, ki_ref, k_ref, vt_ref, bias_ref,
               o_ref,
               st_ref, cand_ref, thr_ref, cgt_ref, cge_ref, clip_ref, qicat_ref, qbd_ref, sa_ref, sb_ref,
               bma_ref, bmb_ref,
               m_ref, acc_ref, cut_ref, rowf_ref, ot_ref):
    j = pl.program_id(1)
    nch = nch_ref[j]
    toff = toff_ref[j]
    kf = float(topk)
    bias_tile_max = (bias_ref.shape[1] - KEY_CHUNK) // LANE - 1

    for h in range(N_IDX):
        qicat_ref[:, h * TQ:(h + 1) * TQ] = qit_ref[h * IDX_DIM:(h + 1) * IDX_DIM, :]
    qbd_ref[...] = jnp.zeros(qbd_ref.shape, BF16)
    for a in range(N_PAIRS):
        qbd_ref[a, 0:HEAD_DIM, 0:TQ] = qt_ref[2 * a * HEAD_DIM:(2 * a + 1) * HEAD_DIM, :]
        qbd_ref[a, HEAD_DIM:2 * HEAD_DIM, TQ:2 * TQ] = qt_ref[(2 * a + 1) * HEAD_DIM:(2 * a + 2) * HEAD_DIM, :]

    def chunk_start(c):
        return pl.multiple_of(c * KEY_CHUNK, KEY_CHUNK)

    def chunk_rows(r0):
        return r0 + lax.broadcasted_iota(I32, (KEY_CHUNK, TQ), 0)

    vis = vis_ref[...]
    cand_ref[...] = jnp.full(cand_ref.shape, -jnp.inf, F32)
    w_pos = wt_ref[...] >= 0.0
    clip_ref[0:N_IDX, :] = jnp.where(w_pos, 0.0, -jnp.inf)
    clip_ref[N_IDX:2 * N_IDX, :] = jnp.where(w_pos, jnp.inf, 0.0)

    last_batch_row = ki_ref.shape[0] - CAND_BATCH

    def score_round(masked, i, carry):
        for g in range(SCORE_ROUND):
            r0 = pl.multiple_of((i * SCORE_ROUND + g) * CAND_BATCH, CAND_BATCH)
            rel = jnp.dot(ki_ref[pl.ds(jnp.minimum(r0, last_batch_row), CAND_BATCH), :], qicat_ref[...],
                          preferred_element_type=F32)
            score = jnp.zeros((CAND_BATCH, TQ), F32)
            for h in range(N_IDX):
                score = score + jnp.clip(rel[:, h * TQ:(h + 1) * TQ], clip_ref[h:h + 1, :],
                                         clip_ref[N_IDX + h:N_IDX + h + 1, :])
            if masked:
                score = jnp.where(r0 + lax.broadcasted_iota(I32, (CAND_BATCH, TQ), 0) < vis, score, -jnp.inf)
            st_ref[pl.ds(r0, CAND_BATCH), :] = score
        return carry

    round_chunks = SCORE_ROUND * CAND_BATCH // KEY_CHUNK
    full_rounds = nfull_ref[j] // round_chunks
    lax.fori_loop(0, full_rounds, functools.partial(score_round, False), 0)
    lax.fori_loop(full_rounds, (nch + round_chunks - 1) // round_chunks, functools.partial(score_round, True), 0)

    def merge_batch(b, carry):
        r0 = pl.multiple_of(b * CAND_BATCH, CAND_BATCH)
        x = [st_ref[pl.ds(r0 + 8 * r, 8), :] for r in range(CAND_DEPTH)]
        for lo, hi in _SORT_NETWORK:
            x[lo], x[hi] = jnp.maximum(x[lo], x[hi]), jnp.minimum(x[lo], x[hi])
        g0 = pl.multiple_of(lax.rem(b, CAND_GROUPS) * CAND_BATCH, CAND_BATCH)
        t = [jnp.maximum(cand_ref[pl.ds(g0 + 8 * r, 8), :], x[CAND_DEPTH - 1 - r]) for r in range(CAND_DEPTH)]
        d = CAND_DEPTH // 2
        while d >= 1:
            for r in range(CAND_DEPTH):
                if r & d == 0:
                    t[r], t[r + d] = jnp.maximum(t[r], t[r + d]), jnp.minimum(t[r], t[r + d])
            d //= 2
        for r in range(CAND_DEPTH):
            cand_ref[pl.ds(g0 + 8 * r, 8), :] = t[r]
        return carry

    lax.fori_loop(0, nch * (KEY_CHUNK // CAND_BATCH), merge_batch, 0)

    def reduce_rows(ref, n_chunks, fn, combine, init):
        reducer = jnp.sum if combine is jnp.add else jnp.max

        def body(c, acc):
            r0 = chunk_start(c)
            val = fn(ref[pl.ds(r0, KEY_CHUNK), :], r0)
            part = reducer(val.reshape(8, KEY_CHUNK // 64, 8, TQ), axis=1)
            return combine(acc, reducer(part, axis=0))

        acc = lax.fori_loop(0, n_chunks, body, jnp.full((8, TQ), init, F32))
        return reducer(acc, axis=0, keepdims=True)

    def count(ref, n_chunks, indicator):
        return reduce_rows(ref, n_chunks, indicator, jnp.add, 0.0)

    def select(ref, n_chunks):
        def value_bit(i, res):
            trial = res | lax.shift_left(jnp.int32(1), 31 - i)
            thr_t = _ordered_bits_to_f32(trial)
            cnt = count(ref, n_chunks, lambda blk, r0: jnp.where(blk >= thr_t, 1.0, 0.0))
            return jnp.where(cnt >= kf, trial, res)

        res = lax.fori_loop(0, 32, value_bit, jnp.zeros((1, TQ), I32))
        res = jnp.where((res >> 23) == 0, jnp.int32(0x00800000), res)
        lo = _ordered_bits_to_f32(res)
        above = _ordered_bits_to_f32(res + 1)
        c_above = count(ref, n_chunks, lambda blk, r0: jnp.where(blk >= above, 1.0, 0.0))
        c_lo = count(ref, n_chunks, lambda blk, r0: jnp.where(blk >= lo, 1.0, 0.0))
        thr_ref[...] = lo
        cgt_ref[...] = c_above
        cge_ref[...] = c_lo

        def walking(state):
            return jnp.max(state[0]) > 0.0

        def walk(state):
            active, hi, c_hi = state
            v = reduce_rows(ref, n_chunks, lambda blk, r0: jnp.where(blk < hi, blk, -jnp.inf), jnp.maximum,
                            -jnp.inf)
            c_v = count(ref, n_chunks, lambda blk, r0: jnp.where(blk >= v, 1.0, 0.0))
            done = jnp.logical_and(active > 0.0, c_v >= kf)
            thr_ref[...] = jnp.where(done, v, thr_ref[...])
            cgt_ref[...] = jnp.where(done, c_hi, cgt_ref[...])
            cge_ref[...] = jnp.where(done, c_v, cge_ref[...])
            return jnp.where(done, 0.0, active), v, c_v

        surplus = jnp.where(jnp.logical_and(c_lo > kf, c_above < kf), 1.0, 0.0)
        lax.while_loop(walking, walk, (surplus, above, c_above))
        return lo

    cand_lo = select(cand_ref, cand_ref.shape[0] // KEY_CHUNK)
    kept_min = cand_ref[CAND_BATCH - 8:CAND_BATCH, :]
    for g in range(1, CAND_GROUPS):
        kept_min = jnp.maximum(kept_min, cand_ref[(g + 1) * CAND_BATCH - 8:(g + 1) * CAND_BATCH, :])
    unsafe = jnp.max(kept_min, axis=0, keepdims=True) >= cand_lo

    @pl.when(jnp.max(jnp.where(unsafe, 1.0, 0.0)) > 0.0)
    def _():
        select(st_ref, nch)

    thr = thr_ref[...]

    need = kf - cgt_ref[...]
    excess = cge_ref[...] - kf
    cut_ref[...] = jnp.full((1, TQ), F32_MAX, F32)
    rowf_ref[...] = lax.broadcasted_iota(I32, (KEY_CHUNK, TQ), 0).astype(F32)

    def chunk_rows_f(r0):
        return rowf_ref[...] + r0.astype(F32)

    tie_need = jnp.max(jnp.where(excess > 0.0, need, 0.0)).astype(I32)

    @pl.when(jnp.logical_and(tie_need > 0, tie_need <= TIE_SCAN_MAX))
    def _():
        def next_tied(cut_f):
            def body(c, acc):
                r0 = chunk_start(c)
                tied_row = jnp.where(st_ref[pl.ds(r0, KEY_CHUNK), :] == thr, chunk_rows_f(r0), F32_MAX)
                v = jnp.where(tied_row > cut_f, tied_row, F32_MAX)
                part = jnp.min(v.reshape(8, KEY_CHUNK // 64, 8, TQ), axis=1)
                return jnp.minimum(acc, jnp.min(part, axis=0))

            acc = lax.fori_loop(0, nch, body, jnp.full((8, TQ), F32_MAX, F32))
            return jnp.min(acc, axis=0, keepdims=True)

        def take(t, cut_f):
            return jnp.where(t.astype(F32) < need, next_tied(cut_f), cut_f)

        cut_f = lax.fori_loop(0, tie_need, take, jnp.full((1, TQ), -1.0, F32))
        cut_ref[...] = jnp.where(excess > 0.0, cut_f, F32_MAX)

    @pl.when(tie_need > TIE_SCAN_MAX)
    def _():
        def index_bit(i, cut):
            trial = cut | lax.shift_left(jnp.int32(1), idx_bits - 1 - i)
            cnt = count(st_ref, nch, lambda blk, r0: jnp.where(
                blk == thr, jnp.where(chunk_rows(r0) < trial, 1.0, 0.0), 0.0))
            return jnp.where(cnt < need, trial, cut)

        cut = lax.fori_loop(0, idx_bits, index_bit, jnp.zeros((1, TQ), I32))
        cut_ref[...] = jnp.where(excess > 0.0, cut.astype(F32), F32_MAX)

    cut_f = cut_ref[...]

    st_ref[pl.ds(chunk_start(nch), KEY_CHUNK), :] = jnp.full((KEY_CHUNK, TQ), -jnp.inf, F32)

    m_ref[...] = jnp.full(m_ref.shape, NEG, F32)
    acc_ref[...] = jnp.zeros(acc_ref.shape, F32)
    last = nch - 1
    ones_rows = jnp.ones((SUM_ROWS, KEY_CHUNK), BF16)

    def logits(blk, s_ref, bmax_ref, with_bias):
        kblk = k_ref[pl.ds(chunk_start(jnp.minimum(blk, last)), KEY_CHUNK), :]
        r0 = chunk_start(jnp.minimum(blk, nch))
        score = st_ref[pl.ds(r0, KEY_CHUNK), :]
        tie = jnp.where(chunk_rows_f(r0) <= cut_f, 0.0, NEG)
        msk = jnp.where(score > thr, 0.0, jnp.where(score == thr, tie, NEG))
        msk2 = jnp.concatenate([msk, msk], axis=1)
        bias_row0 = pl.multiple_of((jnp.clip(4 * blk + toff, -1, bias_tile_max) + 1) * LANE, LANE)
        for a in range(N_PAIRS):
            s2 = jnp.dot(kblk[:, a * 2 * HEAD_DIM:(a + 1) * 2 * HEAD_DIM], qbd_ref[a],
                         preferred_element_type=F32) + msk2
            if with_bias:
                s2 = s2 + bias_ref[a, pl.ds(bias_row0, KEY_CHUNK), :]
            s_ref[a] = s2
            bmax_ref[a:a + 1, :] = jnp.max(s2, axis=0, keepdims=True)

    def softmax_pv(blk, s_ref, bmax_ref):
        vblk = jnp.minimum(blk, last)
        for a in range(N_PAIRS):
            m_old = m_ref[a:a + 1, :]
            m_new = jnp.maximum(m_old, bmax_ref[a:a + 1, :])
            alpha = jnp.exp2(m_old - m_new)
            p = jnp.exp2(s_ref[a] - m_new).astype(BF16)
            v_ones = jnp.concatenate([vt_ref[vblk, a * 2 * HEAD_DIM:(a + 1) * 2 * HEAD_DIM, :], ones_rows], axis=0)
            acc_ref[a] = alpha * acc_ref[a] + jnp.dot(v_ones, p, preferred_element_type=F32)
            m_ref[a:a + 1, :] = m_new

    def block_pair(with_bias, i, carry):
        logits(2 * i + 1, sb_ref, bmb_ref, with_bias)
        softmax_pv(2 * i, sa_ref, bma_ref)
        logits(2 * i + 2, sa_ref, bma_ref, with_bias)
        softmax_pv(2 * i + 1, sb_ref, bmb_ref)
        return carry

    far_pairs = jnp.minimum(jnp.maximum(3 - toff, 0) // 4, nch) // 2
    logits(0, sa_ref, bma_ref, False)
    lax.fori_loop(0, far_pairs, functools.partial(block_pair, False), 0)
    logits(2 * far_pairs, sa_ref, bma_ref, True)
    lax.fori_loop(far_pairs, (nch + 1) // 2, functools.partial(block_pair, True), 0)

    for a in range(N_PAIRS):
        for e in range(2):
            h = 2 * a + e
            num = acc_ref[a, e * HEAD_DIM:(e + 1) * HEAD_DIM, e * TQ:(e + 1) * TQ]
            den = acc_ref[a, 2 * HEAD_DIM:2 * HEAD_DIM + 1, e * TQ:(e + 1) * TQ]
            ot_ref[h * HEAD_DIM:(h + 1) * HEAD_DIM, :] = num / den
    o_ref[...] = ot_ref[...].T


def _attention(qit, wt, qt, vis, ki, k, vt, bias, sched, topk, resident):
    s_count, _, nq = qit.shape
    nk = k.shape[1]
    n_tiles = sched[0].shape[0]
    mode = dict(pipeline_mode=pl.Buffered(1)) if resident else {}
    idx_bits = max(1, int(nk - 1).bit_length())
    logit_buf = pltpu.VMEM((N_PAIRS, KEY_CHUNK, 2 * TQ), F32)
    in_specs = [
        pl.BlockSpec((None, ATT_W, TQ), lambda s, j, *_: (s, 0, j)),
        pl.BlockSpec((None, N_IDX, TQ), lambda s, j, *_: (s, 0, j)),
        pl.BlockSpec((None, ATT_W, TQ), lambda s, j, *_: (s, 0, j)),
        pl.BlockSpec((1, TQ), lambda s, j, *_: (0, j)),
        pl.BlockSpec((None, nk, IDX_DIM), lambda s, j, *_: (s, 0, 0), **mode),
        pl.BlockSpec((None, nk, ATT_W), lambda s, j, *_: (s, 0, 0), **mode),
        pl.BlockSpec((None, nk // KEY_CHUNK, ATT_W, KEY_CHUNK), lambda s, j, *_: (s, 0, 0, 0), **mode),
        pl.BlockSpec(bias.shape, lambda s, j, *_: (0, 0, 0), pipeline_mode=pl.Buffered(1)),
    ]
    grid_spec = pltpu.PrefetchScalarGridSpec(
        num_scalar_prefetch=3,
        grid=(s_count, n_tiles),
        in_specs=in_specs,
        out_specs=pl.BlockSpec((None, TQ, ATT_W), lambda s, j, *_: (s, j, 0)),
        scratch_shapes=[
            pltpu.VMEM((nk + KEY_CHUNK, TQ), F32),
            pltpu.VMEM((CAND_GROUPS * CAND_BATCH, TQ), F32),
            pltpu.VMEM((1, TQ), F32),
            pltpu.VMEM((1, TQ), F32),
            pltpu.VMEM((1, TQ), F32),
            pltpu.VMEM((2 * N_IDX, TQ), F32),
            pltpu.VMEM((IDX_DIM, N_IDX * TQ), BF16),
            pltpu.VMEM((N_PAIRS, 2 * HEAD_DIM, 2 * TQ), BF16),
            logit_buf, logit_buf,
            pltpu.VMEM((N_PAIRS, 2 * TQ), F32),
            pltpu.VMEM((N_PAIRS, 2 * TQ), F32),
            pltpu.VMEM((N_PAIRS, 2 * TQ), F32),
            pltpu.VMEM((N_PAIRS, 2 * HEAD_DIM + SUM_ROWS, 2 * TQ), F32),
            pltpu.VMEM((1, TQ), F32),
            pltpu.VMEM((KEY_CHUNK, TQ), F32),
            pltpu.VMEM((ATT_W, TQ), F32),
        ],
    )
    return pl.pallas_call(
        functools.partial(_attn_body, topk, idx_bits),
        grid_spec=grid_spec,
        out_shape=jax.ShapeDtypeStruct((s_count, nq, ATT_W), F32),
        compiler_params=pltpu.CompilerParams(dimension_semantics=("arbitrary", "arbitrary"),
                                             vmem_limit_bytes=VMEM_LIMIT),
        name="attn",
    )(*sched, qit, wt, qt, vis, ki, k, vt, bias)


def _s5_body(tt, us_ref, s0r_ref, s0i_ref, ar_ref, ai_ref, bbr_ref, bbi_ref, cr_ref, ci_ref, d_ref, wglu_ref,
             bglu_ref, out_ref, sr_out_ref, si_out_ref, str_ref, sti_ref, bre_ref, bim_ref, xr_ref, xi_ref):
    @pl.when(pl.program_id(1) == 0)
    def _():
        str_ref[...] = s0r_ref[...]
        sti_ref[...] = s0i_ref[...]

    u = us_ref[...]
    ub = u.astype(BF16)
    bre_ref[...] = jnp.dot(ub, bbr_ref[...], preferred_element_type=F32)
    bim_ref[...] = jnp.dot(ub, bbi_ref[...], preferred_element_type=F32)
    ar = ar_ref[...]
    ai = ai_ref[...]

    def step(t, carry):
        sr, si = carry
        nr = ar * sr - ai * si + bre_ref[pl.ds(t, 1), :]
        ni = ar * si + ai * sr + bim_ref[pl.ds(t, 1), :]
        xr_ref[pl.ds(t, 1), :] = nr
        xi_ref[pl.ds(t, 1), :] = ni
        return nr, ni

    sr, si = lax.fori_loop(0, tt, step, (str_ref[...], sti_ref[...]))
    str_ref[...] = sr
    sti_ref[...] = si
    sr_out_ref[...] = sr
    si_out_ref[...] = si
    y = (jnp.dot(xr_ref[...].astype(BF16), cr_ref[...], preferred_element_type=F32)
         - jnp.dot(xi_ref[...].astype(BF16), ci_ref[...], preferred_element_type=F32)
         + d_ref[...] * u)
    g = jax.nn.gelu(y)
    gate = jnp.dot(g.astype(BF16), wglu_ref[...], preferred_element_type=F32) + bglu_ref[...]
    out_ref[...] = g * jax.nn.sigmoid(gate)


def _s5(us, s0r, s0i, ar, ai, bbr, bbi, cr, ci, d, wglu, bglu, tt):
    s_count, t_len, _ = us.shape
    assert t_len % tt == 0
    state = pl.BlockSpec((None, 1, SSM_S), lambda s, t: (s, 0, 0))
    seq = pl.BlockSpec((None, tt, SSM_W), lambda s, t: (s, t, 0))
    return pl.pallas_call(
        functools.partial(_s5_body, tt),
        grid=(s_count, t_len // tt),
        in_specs=[seq, state, state, _const_spec((1, SSM_S)), _const_spec((1, SSM_S)),
                  _const_spec((SSM_W, SSM_S)), _const_spec((SSM_W, SSM_S)),
                  _const_spec((SSM_S, SSM_W)), _const_spec((SSM_S, SSM_W)),
                  _const_spec((1, SSM_W)), _const_spec((SSM_W, SSM_W)), _const_spec((1, SSM_W))],
        out_specs=(seq, state, state),
        out_shape=(jax.ShapeDtypeStruct((s_count, t_len, SSM_W), F32),
                   jax.ShapeDtypeStruct((s_count, 1, SSM_S), F32),
                   jax.ShapeDtypeStruct((s_count, 1, SSM_S), F32)),
        scratch_shapes=[pltpu.VMEM((1, SSM_S), F32), pltpu.VMEM((1, SSM_S), F32),
                        pltpu.VMEM((tt, SSM_S), F32), pltpu.VMEM((tt, SSM_S), F32),
                        pltpu.VMEM((tt, SSM_S), F32), pltpu.VMEM((tt, SSM_S), F32)],
        compiler_params=pltpu.CompilerParams(dimension_semantics=("arbitrary", "arbitrary"),
                                             vmem_limit_bytes=VMEM_LIMIT),
        name="s5",
    )(us, s0r, s0i, ar, ai, bbr, bbi, cr, ci, d, wglu, bglu)


def _mix_out_body(att_ref, ssm_ref, h_ref, woa_ref, wos_ref, gmix_ref, gpre_ref, wg_ref, wu_ref, wd_ref, gpost_ref,
                  o_ref):
    m = (jnp.dot(att_ref[...].astype(BF16), woa_ref[...], preferred_element_type=F32)
         + jnp.dot(ssm_ref[...].astype(BF16), wos_ref[...], preferred_element_type=F32))
    h2 = h_ref[...] + _rms(m, gmix_ref[...])
    y = _swiglu(_rms(h2, gpre_ref[...]), wg_ref, wu_ref, wd_ref)
    o_ref[...] = h2 + 0.5 * _rms(y, gpost_ref[...])


def _mix_out(att, ssm, h, woa, wos, gmix, gpre, wg, wu, wd, gpost):
    n = h.shape[0]
    row = lambda width: pl.BlockSpec((ROW_TILE, width), lambda i: (i, 0))
    return pl.pallas_call(
        _mix_out_body,
        grid=(pl.cdiv(n, ROW_TILE),),
        in_specs=[row(ATT_W), row(SSM_W), row(D_MODEL), _const_spec((ATT_W, D_MODEL)),
                  _const_spec((SSM_W, D_MODEL)), _const_spec((1, D_MODEL)), _const_spec((1, D_MODEL)),
                  _const_spec((D_MODEL, D_FF)), _const_spec((D_MODEL, D_FF)), _const_spec((D_FF, D_MODEL)),
                  _const_spec((1, D_MODEL))],
        out_specs=row(D_MODEL),
        out_shape=jax.ShapeDtypeStruct((n, D_MODEL), F32),
        compiler_params=pltpu.CompilerParams(dimension_semantics=("arbitrary",), vmem_limit_bytes=VMEM_LIMIT),
        name="mix_out_ffn2",
    )(att, ssm, h, woa, wos, gmix, gpre, wg, wu, wd, gpost)


def _rel_bucket_np(rel):
    half = NUM_BUCKETS // 2
    max_exact = half // 2
    n = np.abs(rel).astype(np.int64)
    nf = np.maximum(n, 1).astype(np.float64)
    large = max_exact + (np.log(nf / max_exact) / math.log(MAX_DISTANCE / max_exact)
                         * (half - max_exact)).astype(np.int64)
    large = np.minimum(large, half - 1)
    return np.where(rel > 0, half, 0) + np.where(n < max_exact, n, large)


def _pair_bias(rel_bias, n_rows, c0):
    length = n_rows + TQ
    rel = np.arange(length) - (TQ - 1) - LANE - c0
    table = (rel_bias - rel_bias[NUM_BUCKETS // 2 - 1][None, :]) * LOG2E
    t1d = table[_rel_bucket_np(rel)].T
    x = jnp.tile(t1d, (1, TQ))[:, :TQ * (length - 1)].reshape(N_HEADS, TQ, length - 1)
    b = jnp.transpose(x[:, :, TQ - 1:TQ - 1 + n_rows], (0, 2, 1)).reshape(N_PAIRS, 2, n_rows, TQ)
    return jnp.concatenate([b[:, 0], b[:, 1]], axis=-1).astype(F32)


def _prompt_schedule(n, n_pad):
    pos = np.arange(n_pad)
    vis = np.where(pos < N_META, N_META, N_META + CHUNK * ((pos - N_META) // CHUNK + 1))
    vis = np.where(pos < n, np.minimum(vis, n), 0).astype(np.int32)
    n_tiles = -(-n // TQ)
    vmax = vis[:n_tiles * TQ].reshape(n_tiles, TQ).max(axis=1)
    nch = -(-vmax // KEY_CHUNK)
    nfull = vis[:n_tiles * TQ].reshape(n_tiles, TQ).min(axis=1) // KEY_CHUNK
    toff = PROMPT_BIAS_C0 // LANE - np.arange(n_tiles)
    assert ((nch - 1) * (KEY_CHUNK // LANE) + toff).max() <= (PROMPT_BIAS_ROWS - KEY_CHUNK) // LANE - 1
    sched = tuple(jnp.asarray(a, I32) for a in (nch, nfull, toff))
    return sched, jnp.asarray(vis[None, :])


def _sample_schedule(past, ts):
    nk = past + ts
    sched = tuple(jnp.asarray([v], I32) for v in (-(-nk // KEY_CHUNK), 0, 0))
    vis = jnp.asarray(np.where(np.arange(TQ) < ts, nk, 0)[None, :], I32)
    return sched, vis


def _s5_params(lam_re, lam_im, log_step, b_re, b_im, c_re, c_im, d_skip):
    dt = jnp.exp(log_step)[:, None]
    mag = jnp.exp(lam_re * dt)
    ab_re, ab_im = mag * jnp.cos(lam_im * dt), mag * jnp.sin(lam_im * dt)
    nr, ni = ab_re - 1.0, ab_im
    den = lam_re * lam_re + lam_im * lam_im
    f_re, f_im = (nr * lam_re + ni * lam_im) / den, (ni * lam_re - nr * lam_im) / den
    bb_re = f_re[..., None] * b_re - f_im[..., None] * b_im
    bb_im = f_re[..., None] * b_im + f_im[..., None] * b_re
    eye = jnp.eye(SSM_GROUPS, dtype=F32)
    bd_in = lambda w: jnp.einsum('gpc,gh->gchp', w, eye).reshape(SSM_W, SSM_S).astype(BF16)
    bd_out = lambda w: jnp.einsum('gcp,gh->gphc', w, eye).reshape(SSM_S, SSM_W).astype(BF16)
    return (ab_re.reshape(1, SSM_S), ab_im.reshape(1, SSM_S), bd_in(bb_re), bd_in(bb_im),
            bd_out(c_re), bd_out(c_im), d_skip.reshape(1, SSM_W))


def kernel(x_prompt, x_sample, cache_k, cache_v, cache_kidx, state_ssm_re, state_ssm_im, meta_tokens, rel_bias,
           ffn1_g_pre, ffn1_w_gate, ffn1_w_up, ffn1_w_down, ffn1_g_post, mix_g_pre, w_in, w_out, mix_g_post,
           lam_re, lam_im, log_step, b_re, b_im, c_re, c_im, d_skip, w_glu, b_glu,
           ffn2_g_pre, ffn2_w_gate, ffn2_w_up, ffn2_w_down, ffn2_g_post):
    depth = ffn1_g_pre.shape[0]
    assert depth == 1
    bp, seq, _ = x_prompt.shape
    assert bp == 1
    ds, ts, _ = x_sample.shape
    past = cache_k.shape[2]
    n_p = N_META + seq
    n_p_pad = pl.cdiv(n_p, ROW_TILE) * ROW_TILE
    n_s = ds * ts
    assert n_s % ROW_TILE == 0 and ts <= TQ
    nk_s_pad = pl.cdiv(past + ts, KEY_CHUNK) * KEY_CHUNK
    l = 0

    row2 = lambda g: g[l].reshape(1, -1)
    w1 = (row2(ffn1_g_pre), ffn1_w_gate[l].astype(BF16), ffn1_w_up[l].astype(BF16), ffn1_w_down[l].astype(BF16),
          row2(ffn1_g_post))
    w2 = (row2(ffn2_g_pre), ffn2_w_gate[l].astype(BF16), ffn2_w_up[l].astype(BF16), ffn2_w_down[l].astype(BF16),
          row2(ffn2_g_post))
    wi = w_in[l]
    o_q, o_k, o_v, o_qi, o_ki, o_wi, o_us = np.cumsum((0, ATT_W, ATT_W, ATT_W, N_IDX * IDX_DIM, IDX_DIM, N_IDX))
    sl = lambda o, w: wi[:, o:o + w]
    wrow = jnp.concatenate([sl(o_k, ATT_W), sl(o_v, ATT_W), sl(o_us, SSM_W), sl(o_ki, IDX_DIM),
                            jnp.zeros((D_MODEL, ROW_COLS_PAD - ROW_COLS), F32)], axis=1).astype(BF16)
    wcol = jnp.concatenate([sl(o_q, ATT_W), sl(o_qi, ATT_W), sl(o_v, ATT_W), sl(o_wi, N_IDX)], axis=1).T.astype(BF16)
    woa, wos = w_out[l][:ATT_W].astype(BF16), w_out[l][ATT_W:].astype(BF16)
    s5w = _s5_params(lam_re[l], lam_im[l], log_step[l], b_re[l], b_im[l], c_re[l], c_im[l], d_skip[l])
    s5w = s5w + (w_glu[l].astype(BF16), b_glu[l].reshape(1, SSM_W))
    bias_tab = rel_bias.astype(F32)

    xp = jnp.concatenate([meta_tokens.astype(F32), x_prompt[0]], axis=0)
    hp = _ffn(xp, *w1)
    kp, vp, kip, usp, kbp, kibp, vtbp, qtp, qitp, wtp = _mix_in(hp, row2(mix_g_pre), wrow, wcol, n_p_pad)
    sched_p, vis_p = _prompt_schedule(n_p, n_p_pad)
    att_p = _attention(qitp[None], wtp[None], qtp[None], vis_p, kibp[None], kbp[None], vtbp[None],
                       _pair_bias(bias_tab, PROMPT_BIAS_ROWS, PROMPT_BIAS_C0), sched_p,
                       min(TOPK_MAX, seq // 4), True)[0]
    zero_state = jnp.zeros((1, 1, SSM_S), F32)
    tt_p = max(t for t in range(8, 513, 8) if n_p % t == 0)
    ssm_p, srp, sip = _s5(usp[None], zero_state, zero_state, *s5w, tt_p)
    yp = _mix_out(att_p, ssm_p[0], hp, woa, wos, row2(mix_g_post), *w2)

    hs = _ffn(x_sample.reshape(n_s, D_MODEL), *w1)
    ks, vs, kis, uss, _, _, _, qts, qits, wts = _mix_in(hs, row2(mix_g_pre), wrow, wcol, n_s)
    pad_k = lambda a: jnp.pad(a, ((0, 0), (0, nk_s_pad - past - ts), (0, 0)))
    k_all = pad_k(jnp.concatenate([cache_k[l].reshape(ds, past, ATT_W), ks.reshape(ds, ts, ATT_W)], 1).astype(BF16))
    v_all = pad_k(jnp.concatenate([cache_v[l].reshape(ds, past, ATT_W), vs.reshape(ds, ts, ATT_W)], 1).astype(BF16))
    ki_all = pad_k(jnp.concatenate([cache_kidx[l], kis.reshape(ds, ts, IDX_DIM)], 1).astype(BF16))
    vt_all = jnp.transpose(v_all.reshape(ds, nk_s_pad // KEY_CHUNK, KEY_CHUNK, ATT_W), (0, 1, 3, 2))
    lanes = lambda a: jnp.pad(jnp.transpose(a.reshape(a.shape[0], ds, ts), (1, 0, 2)), ((0, 0), (0, 0), (0, TQ - ts)))
    sched_s, vis_s = _sample_schedule(past, ts)
    att_s = _attention(lanes(qits), lanes(wts), lanes(qts), vis_s, ki_all, k_all, vt_all,
                       _pair_bias(bias_tab, LANE + nk_s_pad, past), sched_s,
                       min(TOPK_MAX, (past + ts) // 4), False)
    att_s = att_s[:, :ts].reshape(n_s, ATT_W)
    ssm_s, srs, sis = _s5(uss.reshape(ds, ts, SSM_W), state_ssm_re[l].reshape(ds, 1, SSM_S),
                          state_ssm_im[l].reshape(ds, 1, SSM_S), *s5w, ts)
    ys = _mix_out(att_s, ssm_s.reshape(n_s, SSM_W), hs, woa, wos, row2(mix_g_post), *w2)

    heads = lambda a, b, t: a.reshape(1, b, t, N_HEADS, HEAD_DIM)
    state = lambda a, b: a.reshape(1, b, SSM_GROUPS, SSM_P)
    return (yp[N_META:][None], ys.reshape(ds, ts, D_MODEL),
            heads(kp, 1, n_p), heads(vp, 1, n_p), kip.reshape(1, 1, n_p, IDX_DIM), state(srp, 1), state(sip, 1),
            heads(ks, ds, ts), heads(vs, ds, ts), kis.reshape(1, ds, ts, IDX_DIM), state(srs, ds), state(sis, ds))
```

```python
import functools
import math

import numpy as np
import jax
import jax.numpy as jnp
from jax import lax
from jax.experimental import pallas as pl
from jax.experimental.pallas import tpu as pltpu

F32, BF16, I32 = jnp.float32, jnp.bfloat16, jnp.int32

D_MODEL = 1024
CHUNK = 64
N_META = 16
N_HEADS = 8
HEAD_DIM = 64
ATT_W = N_HEADS * HEAD_DIM
N_IDX = 8
IDX_DIM = 64
TOPK_MAX = 256
SSM_GROUPS = 32
SSM_GC = 16
SSM_W = SSM_GROUPS * SSM_GC
SSM_P = 64
SSM_S = SSM_GROUPS * SSM_P
D_FF = 2816
NUM_BUCKETS = 32
MAX_DISTANCE = 128
EPS = 1e-6

LANE = 128
VMEM_LIMIT = 62 * 1024 * 1024
ROW_TILE = 512
CAST_ROWS = 256
TQ = LANE
KEY_CHUNK = 512
PROMPT_BIAS_C0 = 512
PROMPT_BIAS_ROWS = 1280
NEG = -1e30
LOG2E = math.log2(math.e)
F32_MAX = float(np.finfo(np.float32).max)
N_PAIRS = N_HEADS // 2
SUM_ROWS = 16
CAND_DEPTH = 16
CAND_BATCH = 8 * CAND_DEPTH
CAND_GROUPS = 8
SCORE_ROUND = 8
TIE_SCAN_MAX = 8


def _batcher_network(n):
    pairs, p = [], 1
    while p < n:
        k = p
        while k >= 1:
            for j in range(k % p, n - k, 2 * k):
                for i in range(min(k, n - j - k)):
                    if (i + j) // (2 * p) == (i + j + k) // (2 * p):
                        pairs.append((i + j, i + j + k))
            k //= 2
        p *= 2
    return pairs


_SORT_NETWORK = _batcher_network(CAND_DEPTH)


def _rms(x, g):
    return x * lax.rsqrt(jnp.mean(x * x, axis=-1, keepdims=True) + EPS) * g


def _swiglu(xn, wg_ref, wu_ref, wd_ref):
    xb = xn.astype(BF16)
    a = jnp.dot(xb, wg_ref[...], preferred_element_type=F32)
    b = jnp.dot(xb, wu_ref[...], preferred_element_type=F32)
    hidden = (a * jax.nn.sigmoid(a) * b).astype(BF16)
    return jnp.dot(hidden, wd_ref[...], preferred_element_type=F32)


def _const_spec(shape):
    nd = len(shape)
    return pl.BlockSpec(shape, lambda *_: (0,) * nd, pipeline_mode=pl.Buffered(1))


def _cast_body(x_ref, o_ref):
    o_ref[...] = x_ref[...].astype(BF16)


def _to_bf16(w):
    rows, cols = w.shape
    tile = CAST_ROWS if rows % CAST_ROWS == 0 else rows
    spec = pl.BlockSpec((tile, cols), lambda i: (i, 0))
    return pl.pallas_call(
        _cast_body,
        grid=(rows // tile,),
        in_specs=[spec],
        out_specs=spec,
        out_shape=jax.ShapeDtypeStruct((rows, cols), BF16),
        compiler_params=pltpu.CompilerParams(dimension_semantics=("arbitrary",)),
        name="to_bf16",
    )(w)


def _ffn_body(x_ref, gpre_ref, wg_ref, wu_ref, wd_ref, gpost_ref, o_ref):
    x = x_ref[...]
    y = _swiglu(_rms(x, gpre_ref[...]), wg_ref, wu_ref, wd_ref)
    o_ref[...] = x + 0.5 * _rms(y, gpost_ref[...])


def _ffn(x, gpre, wg, wu, wd, gpost):
    n = x.shape[0]
    row = pl.BlockSpec((ROW_TILE, D_MODEL), lambda i: (i, 0))
    return pl.pallas_call(
        _ffn_body,
        grid=(pl.cdiv(n, ROW_TILE),),
        in_specs=[row, _const_spec((1, D_MODEL)), _const_spec((D_MODEL, D_FF)), _const_spec((D_MODEL, D_FF)),
                  _const_spec((D_FF, D_MODEL)), _const_spec((1, D_MODEL))],
        out_specs=row,
        out_shape=jax.ShapeDtypeStruct((n, D_MODEL), F32),
        compiler_params=pltpu.CompilerParams(dimension_semantics=("arbitrary",), vmem_limit_bytes=VMEM_LIMIT),
        name="ffn1",
    )(x, gpre, wg, wu, wd, gpost)


ROW_COLS = 3 * ATT_W + IDX_DIM
ROW_COLS_PAD = 13 * LANE
COL_ROWS = 3 * ATT_W + N_IDX


def _mix_in_body(n_valid, h_ref, g_ref, wrow_ref, wcol_ref,
                 k_ref, v_ref, ki_ref, us_ref, kb_ref, kib_ref, vtb_ref, qt_ref, qit_ref, wt_ref):
    i = pl.program_id(0)
    u = _rms(h_ref[...], g_ref[...]).astype(BF16)
    z = jnp.dot(u, wrow_ref[...], preferred_element_type=F32)
    zt = lax.dot_general(wcol_ref[...], u, (((1,), (1,)), ((), ())), preferred_element_type=F32)
    k = z[:, 0:ATT_W]
    v = z[:, ATT_W:2 * ATT_W]
    ki = z[:, 3 * ATT_W:3 * ATT_W + IDX_DIM]
    k_ref[...] = k
    v_ref[...] = v
    us_ref[...] = z[:, 2 * ATT_W:3 * ATT_W]
    ki_ref[...] = ki
    row_ok = (i * ROW_TILE + lax.broadcasted_iota(I32, (ROW_TILE, 1), 0)) < n_valid
    col_ok = (i * ROW_TILE + lax.broadcasted_iota(I32, (1, ROW_TILE), 1)) < n_valid
    kb_ref[...] = jnp.where(row_ok, k, 0.0).astype(BF16)
    kib_ref[...] = jnp.where(row_ok, ki, 0.0).astype(BF16)
    zt = jnp.where(col_ok, zt, 0.0)
    qt_ref[...] = (zt[0:ATT_W] * (HEAD_DIM ** -0.5 * LOG2E)).astype(BF16)
    vtb_ref[...] = zt[2 * ATT_W:3 * ATT_W].astype(BF16)
    wt = zt[3 * ATT_W:3 * ATT_W + N_IDX] * ((IDX_DIM ** -0.5) * (N_IDX ** -0.5))
    wt_ref[...] = wt
    for h in range(N_IDX):
        qi_h = zt[ATT_W + h * IDX_DIM:ATT_W + (h + 1) * IDX_DIM]
        qit_ref[h * IDX_DIM:(h + 1) * IDX_DIM, :] = (qi_h * wt[h:h + 1, :]).astype(BF16)


def _mix_in(h, g, wrow, wcol, n_pad):
    n = h.shape[0]
    steps = n_pad // ROW_TILE
    assert steps == pl.cdiv(n, ROW_TILE)

    def row(width):
        return pl.BlockSpec((ROW_TILE, width), lambda i: (i, 0))

    col = lambda rows: pl.BlockSpec((rows, ROW_TILE), lambda i: (0, i))
    out_shape = (
        jax.ShapeDtypeStruct((n, ATT_W), F32), jax.ShapeDtypeStruct((n, ATT_W), F32),
        jax.ShapeDtypeStruct((n, IDX_DIM), F32), jax.ShapeDtypeStruct((n, SSM_W), F32),
        jax.ShapeDtypeStruct((n_pad, ATT_W), BF16), jax.ShapeDtypeStruct((n_pad, IDX_DIM), BF16),
        jax.ShapeDtypeStruct((n_pad // ROW_TILE, ATT_W, ROW_TILE), BF16),
        jax.ShapeDtypeStruct((ATT_W, n_pad), BF16), jax.ShapeDtypeStruct((ATT_W, n_pad), BF16),
        jax.ShapeDtypeStruct((N_IDX, n_pad), F32),
    )
    out_specs = (
        row(ATT_W), row(ATT_W), row(IDX_DIM), row(SSM_W), row(ATT_W), row(IDX_DIM),
        pl.BlockSpec((None, ATT_W, ROW_TILE), lambda i: (i, 0, 0)),
        col(ATT_W), col(ATT_W), col(N_IDX),
    )
    return pl.pallas_call(
        functools.partial(_mix_in_body, n),
        grid=(steps,),
        in_specs=[row(D_MODEL), _const_spec((1, D_MODEL)), _const_spec((D_MODEL, ROW_COLS_PAD)),
                  _const_spec((COL_ROWS, D_MODEL))],
        out_specs=out_specs,
        out_shape=out_shape,
        compiler_params=pltpu.CompilerParams(dimension_semantics=("arbitrary",), vmem_limit_bytes=VMEM_LIMIT),
        name="mix_in",
    )(h, g, wrow, wcol)


def _ordered_bits_to_f32(u):
    k = u ^ jnp.int32(-2 ** 31)
    b = k ^ ((k >> 31) & jnp.int32(0x7FFFFFFF))
    return lax.bitcast_convert_type(b, F32)


def _attn_body(topk, idx_bits,
               nch_ref, nfull_ref, toff_ref,
               qit_ref, wt_ref, qt_ref, vis_ref, ki_ref, k_ref, vt_ref, bias_ref,
               o_ref,
               st_ref, cand_ref, thr_ref, cgt_ref, cge_ref, clip_ref, qicat_ref, qbd_ref, sa_ref, sb_ref,
               bma_ref, bmb_ref,
               m_ref, acc_ref, cut_ref, rowf_ref, ot_ref):
    j = pl.program_id(1)
    nch = nch_ref[j]
    toff = toff_ref[j]
    kf = float(topk)
    bias_tile_max = (bias_ref.shape[1] - KEY_CHUNK) // LANE - 1

    for h in range(N_IDX):
        qicat_ref[:, h * TQ:(h + 1) * TQ] = qit_ref[h * IDX_DIM:(h + 1) * IDX_DIM, :]

    @pl.when(jnp.logical_and(pl.program_id(0) == 0, j == 0))
    def _():
        qbd_ref[...] = jnp.zeros(qbd_ref.shape, BF16)
        rowf_ref[...] = lax.broadcasted_iota(I32, (KEY_CHUNK, TQ), 0).astype(F32)

    for a in range(N_PAIRS):
        qbd_ref[a, 0:HEAD_DIM, 0:TQ] = qt_ref[2 * a * HEAD_DIM:(2 * a + 1) * HEAD_DIM, :]
        qbd_ref[a, HEAD_DIM:2 * HEAD_DIM, TQ:2 * TQ] = qt_ref[(2 * a + 1) * HEAD_DIM:(2 * a + 2) * HEAD_DIM, :]

    def chunk_start(c):
        return pl.multiple_of(c * KEY_CHUNK, KEY_CHUNK)

    def chunk_rows(r0):
        return r0 + lax.broadcasted_iota(I32, (KEY_CHUNK, TQ), 0)

    vis = vis_ref[...]
    cand_ref[...] = jnp.full(cand_ref.shape, -jnp.inf, F32)
    w_pos = wt_ref[...] >= 0.0
    clip_ref[0:N_IDX, :] = jnp.where(w_pos, 0.0, -jnp.inf)
    clip_ref[N_IDX:2 * N_IDX, :] = jnp.where(w_pos, jnp.inf, 0.0)

    last_batch_row = ki_ref.shape[0] - CAND_BATCH

    def score_round(masked, i, carry):
        for g in range(SCORE_ROUND):
            r0 = pl.multiple_of((i * SCORE_ROUND + g) * CAND_BATCH, CAND_BATCH)
            rel = jnp.dot(ki_ref[pl.ds(jnp.minimum(r0, last_batch_row), CAND_BATCH), :], qicat_ref[...],
                          preferred_element_type=F32)
            score = jnp.zeros((CAND_BATCH, TQ), F32)
            for h in range(N_IDX):
                score = score + jnp.clip(rel[:, h * TQ:(h + 1) * TQ], clip_ref[h:h + 1, :],
                                         clip_ref[N_IDX + h:N_IDX + h + 1, :])
            if masked:
                score = jnp.where(r0 + lax.broadcasted_iota(I32, (CAND_BATCH, TQ), 0) < vis, score, -jnp.inf)
            st_ref[pl.ds(r0, CAND_BATCH), :] = score
        return carry

    round_chunks = SCORE_ROUND * CAND_BATCH // KEY_CHUNK
    full_rounds = nfull_ref[j] // round_chunks
    lax.fori_loop(0, full_rounds, functools.partial(score_round, False), 0)
    lax.fori_loop(full_rounds, (nch + round_chunks - 1) // round_chunks, functools.partial(score_round, True), 0)

    def merge_batch(b, carry):
        r0 = pl.multiple_of(b * CAND_BATCH, CAND_BATCH)
        x = [st_ref[pl.ds(r0 + 8 * r, 8), :] for r in range(CAND_DEPTH)]
        for lo, hi in _SORT_NETWORK:
            x[lo], x[hi] = jnp.maximum(x[lo], x[hi]), jnp.minimum(x[lo], x[hi])
        g0 = pl.multiple_of(lax.rem(b, CAND_GROUPS) * CAND_BATCH, CAND_BATCH)
        t = [jnp.maximum(cand_ref[pl.ds(g0 + 8 * r, 8), :], x[CAND_DEPTH - 1 - r]) for r in range(CAND_DEPTH)]
        d = CAND_DEPTH // 2
        while d >= 1:
            for r in range(CAND_DEPTH):
                if r & d == 0:
                    t[r], t[r + d] = jnp.maximum(t[r], t[r + d]), jnp.minimum(t[r], t[r + d])
            d //= 2
        for r in range(CAND_DEPTH):
            cand_ref[pl.ds(g0 + 8 * r, 8), :] = t[r]
        return carry

    lax.fori_loop(0, nch * (KEY_CHUNK // CAND_BATCH), merge_batch, 0)

    def reduce_rows(ref, n_chunks, fn, combine, init):
        reducer = jnp.sum if combine is jnp.add else jnp.max

        def body(c, acc):
            r0 = chunk_start(c)
            val = fn(ref[pl.ds(r0, KEY_CHUNK), :], r0)
            part = reducer(val.reshape(8, KEY_CHUNK // 64, 8, TQ), axis=1)
            return combine(acc, reducer(part, axis=0))

        acc = lax.fori_loop(0, n_chunks, body, jnp.full((8, TQ), init, F32))
        return reducer(acc, axis=0, keepdims=True)

    def count(ref, n_chunks, indicator):
        return reduce_rows(ref, n_chunks, indicator, jnp.add, 0.0)

    def select(ref, n_chunks):
        def value_bit(i, res):
            trial = res | lax.shift_left(jnp.int32(1), 31 - i)
            thr_t = _ordered_bits_to_f32(trial)
            cnt = count(ref, n_chunks, lambda blk, r0: jnp.where(blk >= thr_t, 1.0, 0.0))
            return jnp.where(cnt >= kf, trial, res)

        res = lax.fori_loop(0, 32, value_bit, jnp.zeros((1, TQ), I32))
        res = jnp.where((res >> 23) == 0, jnp.int32(0x00800000), res)
        lo = _ordered_bits_to_f32(res)
        above = _ordered_bits_to_f32(res + 1)
        c_above = count(ref, n_chunks, lambda blk, r0: jnp.where(blk >= above, 1.0, 0.0))
        c_lo = count(ref, n_chunks, lambda blk, r0: jnp.where(blk >= lo, 1.0, 0.0))
        thr_ref[...] = lo
        cgt_ref[...] = c_above
        cge_ref[...] = c_lo

        def walking(state):
            return jnp.max(state[0]) > 0.0

        def walk(state):
            active, hi, c_hi = state
            v = reduce_rows(ref, n_chunks, lambda blk, r0: jnp.where(blk < hi, blk, -jnp.inf), jnp.maximum,
                            -jnp.inf)
            c_v = count(ref, n_chunks, lambda blk, r0: jnp.where(blk >= v, 1.0, 0.0))
            done = jnp.logical_and(active > 0.0, c_v >= kf)
            thr_ref[...] = jnp.where(done, v, thr_ref[...])
            cgt_ref[...] = jnp.where(done, c_hi, cgt_ref[...])
            cge_ref[...] = jnp.where(done, c_v, cge_ref[...])
            return jnp.where(done, 0.0, active), v, c_v

        surplus = jnp.where(jnp.logical_and(c_lo > kf, c_above < kf), 1.0, 0.0)
        lax.while_loop(walking, walk, (surplus, above, c_above))
        return lo

    cand_lo = select(cand_ref, cand_ref.shape[0] // KEY_CHUNK)
    kept_min = cand_ref[CAND_BATCH - 8:CAND_BATCH, :]
    for g in range(1, CAND_GROUPS):
        kept_min = jnp.maximum(kept_min, cand_ref[(g + 1) * CAND_BATCH - 8:(g + 1) * CAND_BATCH, :])
    unsafe = jnp.max(kept_min, axis=0, keepdims=True) >= cand_lo

    @pl.when(jnp.max(jnp.where(unsafe, 1.0, 0.0)) > 0.0)
    def _():
        select(st_ref, nch)

    thr = thr_ref[...]

    need = kf - cgt_ref[...]
    excess = cge_ref[...] - kf
    cut_ref[...] = jnp.full((1, TQ), F32_MAX, F32)

    def chunk_rows_f(r0):
        return rowf_ref[...] + r0.astype(F32)

    tie_need = jnp.max(jnp.where(excess > 0.0, need, 0.0)).astype(I32)

    @pl.when(jnp.logical_and(tie_need > 0, tie_need <= TIE_SCAN_MAX))
    def _():
        def next_tied(cut_f):
            def body(c, acc):
                r0 = chunk_start(c)
                tied_row = jnp.where(st_ref[pl.ds(r0, KEY_CHUNK), :] == thr, chunk_rows_f(r0), F32_MAX)
                v = jnp.where(tied_row > cut_f, tied_row, F32_MAX)
                part = jnp.min(v.reshape(8, KEY_CHUNK // 64, 8, TQ), axis=1)
                return jnp.minimum(acc, jnp.min(part, axis=0))

            acc = lax.fori_loop(0, nch, body, jnp.full((8, TQ), F32_MAX, F32))
            return jnp.min(acc, axis=0, keepdims=True)

        def take(t, cut_f):
            return jnp.where(t.astype(F32) < need, next_tied(cut_f), cut_f)

        cut_f = lax.fori_loop(0, tie_need, take, jnp.full((1, TQ), -1.0, F32))
        cut_ref[...] = jnp.where(excess > 0.0, cut_f, F32_MAX)

    @pl.when(tie_need > TIE_SCAN_MAX)
    def _():
        def index_bit(i, cut):
            trial = cut | lax.shift_left(jnp.int32(1), idx_bits - 1 - i)
            cnt = count(st_ref, nch, lambda blk, r0: jnp.where(
                blk == thr, jnp.where(chunk_rows(r0) < trial, 1.0, 0.0), 0.0))
            return jnp.where(cnt < need, trial, cut)

        cut = lax.fori_loop(0, idx_bits, index_bit, jnp.zeros((1, TQ), I32))
        cut_ref[...] = jnp.where(excess > 0.0, cut.astype(F32), F32_MAX)

    cut_f = cut_ref[...]

    st_ref[pl.ds(chunk_start(nch), KEY_CHUNK), :] = jnp.full((KEY_CHUNK, TQ), -jnp.inf, F32)

    m_ref[...] = jnp.full(m_ref.shape, NEG, F32)
    acc_ref[...] = jnp.zeros(acc_ref.shape, F32)
    last = nch - 1
    ones_rows = jnp.ones((SUM_ROWS, KEY_CHUNK), BF16)

    def logits(blk, s_ref, bmax_ref, with_bias):
        kblk = k_ref[pl.ds(chunk_start(jnp.minimum(blk, last)), KEY_CHUNK), :]
        r0 = chunk_start(jnp.minimum(blk, nch))
        score = st_ref[pl.ds(r0, KEY_CHUNK), :]
        tie = jnp.where(chunk_rows_f(r0) <= cut_f, 0.0, NEG)
        msk = jnp.where(score > thr, 0.0, jnp.where(score == thr, tie, NEG))
        msk2 = jnp.concatenate([msk, msk], axis=1)
        bias_row0 = pl.multiple_of((jnp.clip(4 * blk + toff, -1, bias_tile_max) + 1) * LANE, LANE)
        for a in range(N_PAIRS):
            s2 = jnp.dot(kblk[:, a * 2 * HEAD_DIM:(a + 1) * 2 * HEAD_DIM], qbd_ref[a],
                         preferred_element_type=F32) + msk2
            if with_bias:
                s2 = s2 + bias_ref[a, pl.ds(bias_row0, KEY_CHUNK), :]
            s_ref[a] = s2
            bmax_ref[a:a + 1, :] = jnp.max(s2, axis=0, keepdims=True)

    def softmax_pv(blk, s_ref, bmax_ref):
        vblk = jnp.minimum(blk, last)
        for a in range(N_PAIRS):
            m_old = m_ref[a:a + 1, :]
            m_new = jnp.maximum(m_old, bmax_ref[a:a + 1, :])
            alpha = jnp.exp2(m_old - m_new)
            p = jnp.exp2(s_ref[a] - m_new).astype(BF16)
            v_ones = jnp.concatenate([vt_ref[vblk, a * 2 * HEAD_DIM:(a + 1) * 2 * HEAD_DIM, :], ones_rows], axis=0)
            acc_ref[a] = alpha * acc_ref[a] + jnp.dot(v_ones, p, preferred_element_type=F32)
            m_ref[a:a + 1, :] = m_new

    def block_pair(with_bias, i, carry):
        logits(2 * i + 1, sb_ref, bmb_ref, with_bias)
        softmax_pv(2 * i, sa_ref, bma_ref)
        logits(2 * i + 2, sa_ref, bma_ref, with_bias)
        softmax_pv(2 * i + 1, sb_ref, bmb_ref)
        return carry

    far_pairs = jnp.minimum(jnp.maximum(3 - toff, 0) // 4, nch) // 2
    @pl.when(far_pairs > 0)
    def _():
        logits(0, sa_ref, bma_ref, False)

    lax.fori_loop(0, far_pairs, functools.partial(block_pair, False), 0)
    logits(2 * far_pairs, sa_ref, bma_ref, True)
    lax.fori_loop(far_pairs, (nch + 1) // 2, functools.partial(block_pair, True), 0)

    for a in range(N_PAIRS):
        for e in range(2):
            h = 2 * a + e
            num = acc_ref[a, e * HEAD_DIM:(e + 1) * HEAD_DIM, e * TQ:(e + 1) * TQ]
            den = acc_ref[a, 2 * HEAD_DIM:2 * HEAD_DIM + 1, e * TQ:(e + 1) * TQ]
            ot_ref[h * HEAD_DIM:(h + 1) * HEAD_DIM, :] = num / den
    o_ref[...] = ot_ref[...].T


def _attention(qit, wt, qt, vis, ki, k, vt, bias, sched, topk, resident):
    s_count, _, nq = qit.shape
    nk = k.shape[1]
    n_tiles = sched[0].shape[0]
    mode = dict(pipeline_mode=pl.Buffered(1)) if resident else {}
    idx_bits = max(1, int(nk - 1).bit_length())
    logit_buf = pltpu.VMEM((N_PAIRS, KEY_CHUNK, 2 * TQ), F32)
    in_specs = [
        pl.BlockSpec((None, ATT_W, TQ), lambda s, j, *_: (s, 0, j)),
        pl.BlockSpec((None, N_IDX, TQ), lambda s, j, *_: (s, 0, j)),
        pl.BlockSpec((None, ATT_W, TQ), lambda s, j, *_: (s, 0, j)),
        pl.BlockSpec((1, TQ), lambda s, j, *_: (0, j)),
        pl.BlockSpec((None, nk, IDX_DIM), lambda s, j, *_: (s, 0, 0), **mode),
        pl.BlockSpec((None, nk, ATT_W), lambda s, j, *_: (s, 0, 0), **mode),
        pl.BlockSpec((None, nk // KEY_CHUNK, ATT_W, KEY_CHUNK), lambda s, j, *_: (s, 0, 0, 0), **mode),
        pl.BlockSpec(bias.shape, lambda s, j, *_: (0, 0, 0), pipeline_mode=pl.Buffered(1)),
    ]
    grid_spec = pltpu.PrefetchScalarGridSpec(
        num_scalar_prefetch=3,
        grid=(s_count, n_tiles),
        in_specs=in_specs,
        out_specs=pl.BlockSpec((None, TQ, ATT_W), lambda s, j, *_: (s, j, 0)),
        scratch_shapes=[
            pltpu.VMEM((nk + KEY_CHUNK, TQ), F32),
            pltpu.VMEM((CAND_GROUPS * CAND_BATCH, TQ), F32),
            pltpu.VMEM((1, TQ), F32),
            pltpu.VMEM((1, TQ), F32),
            pltpu.VMEM((1, TQ), F32),
            pltpu.VMEM((2 * N_IDX, TQ), F32),
            pltpu.VMEM((IDX_DIM, N_IDX * TQ), BF16),
            pltpu.VMEM((N_PAIRS, 2 * HEAD_DIM, 2 * TQ), BF16),
            logit_buf, logit_buf,
            pltpu.VMEM((N_PAIRS, 2 * TQ), F32),
            pltpu.VMEM((N_PAIRS, 2 * TQ), F32),
            pltpu.VMEM((N_PAIRS, 2 * TQ), F32),
            pltpu.VMEM((N_PAIRS, 2 * HEAD_DIM + SUM_ROWS, 2 * TQ), F32),
            pltpu.VMEM((1, TQ), F32),
            pltpu.VMEM((KEY_CHUNK, TQ), F32),
            pltpu.VMEM((ATT_W, TQ), F32),
        ],
    )
    return pl.pallas_call(
        functools.partial(_attn_body, topk, idx_bits),
        grid_spec=grid_spec,
        out_shape=jax.ShapeDtypeStruct((s_count, nq, ATT_W), F32),
        compiler_params=pltpu.CompilerParams(dimension_semantics=("arbitrary", "arbitrary"),
                                             vmem_limit_bytes=VMEM_LIMIT),
        name="attn",
    )(*sched, qit, wt, qt, vis, ki, k, vt, bias)


def _s5_body(tt, us_ref, s0r_ref, s0i_ref, ar_ref, ai_ref, bbr_ref, bbi_ref, cr_ref, ci_ref, d_ref, wglu_ref,
             bglu_ref, out_ref, sr_out_ref, si_out_ref, str_ref, sti_ref, bre_ref, bim_ref, xr_ref, xi_ref):
    @pl.when(pl.program_id(1) == 0)
    def _():
        str_ref[...] = s0r_ref[...]
        sti_ref[...] = s0i_ref[...]

    u = us_ref[...]
    ub = u.astype(BF16)
    bre_ref[...] = jnp.dot(ub, bbr_ref[...], preferred_element_type=F32)
    bim_ref[...] = jnp.dot(ub, bbi_ref[...], preferred_element_type=F32)
    ar = ar_ref[...]
    ai = ai_ref[...]

    def step(t, carry):
        sr, si = carry
        nr = ar * sr - ai * si + bre_ref[pl.ds(t, 1), :]
        ni = ar * si + ai * sr + bim_ref[pl.ds(t, 1), :]
        xr_ref[pl.ds(t, 1), :] = nr
        xi_ref[pl.ds(t, 1), :] = ni
        return nr, ni

    sr, si = lax.fori_loop(0, tt, step, (str_ref[...], sti_ref[...]))
    str_ref[...] = sr
    sti_ref[...] = si
    sr_out_ref[...] = sr
    si_out_ref[...] = si
    y = (jnp.dot(xr_ref[...].astype(BF16), cr_ref[...], preferred_element_type=F32)
         - jnp.dot(xi_ref[...].astype(BF16), ci_ref[...], preferred_element_type=F32)
         + d_ref[...] * u)
    g = jax.nn.gelu(y)
    gate = jnp.dot(g.astype(BF16), wglu_ref[...], preferred_element_type=F32) + bglu_ref[...]
    out_ref[...] = g * jax.nn.sigmoid(gate)


def _s5(us, s0r, s0i, ar, ai, bbr, bbi, cr, ci, d, wglu, bglu, tt):
    s_count, t_len, _ = us.shape
    assert t_len % tt == 0
    state = pl.BlockSpec((None, 1, SSM_S), lambda s, t: (s, 0, 0))
    seq = pl.BlockSpec((None, tt, SSM_W), lambda s, t: (s, t, 0))
    return pl.pallas_call(
        functools.partial(_s5_body, tt),
        grid=(s_count, t_len // tt),
        in_specs=[seq, state, state, _const_spec((1, SSM_S)), _const_spec((1, SSM_S)),
                  _const_spec((SSM_W, SSM_S)), _const_spec((SSM_W, SSM_S)),
                  _const_spec((SSM_S, SSM_W)), _const_spec((SSM_S, SSM_W)),
                  _const_spec((1, SSM_W)), _const_spec((SSM_W, SSM_W)), _const_spec((1, SSM_W))],
        out_specs=(seq, state, state),
        out_shape=(jax.ShapeDtypeStruct((s_count, t_len, SSM_W), F32),
                   jax.ShapeDtypeStruct((s_count, 1, SSM_S), F32),
                   jax.ShapeDtypeStruct((s_count, 1, SSM_S), F32)),
        scratch_shapes=[pltpu.VMEM((1, SSM_S), F32), pltpu.VMEM((1, SSM_S), F32),
                        pltpu.VMEM((tt, SSM_S), F32), pltpu.VMEM((tt, SSM_S), F32),
                        pltpu.VMEM((tt, SSM_S), F32), pltpu.VMEM((tt, SSM_S), F32)],
        compiler_params=pltpu.CompilerParams(dimension_semantics=("arbitrary", "arbitrary"),
                                             vmem_limit_bytes=VMEM_LIMIT),
        name="s5",
    )(us, s0r, s0i, ar, ai, bbr, bbi, cr, ci, d, wglu, bglu)


def _mix_out_body(att_ref, ssm_ref, h_ref, woa_ref, wos_ref, gmix_ref, gpre_ref, wg_ref, wu_ref, wd_ref, gpost_ref,
                  o_ref):
    m = (jnp.dot(att_ref[...].astype(BF16), woa_ref[...], preferred_element_type=F32)
         + jnp.dot(ssm_ref[...].astype(BF16), wos_ref[...], preferred_element_type=F32))
    h2 = h_ref[...] + _rms(m, gmix_ref[...])
    y = _swiglu(_rms(h2, gpre_ref[...]), wg_ref, wu_ref, wd_ref)
    o_ref[...] = h2 + 0.5 * _rms(y, gpost_ref[...])


def _mix_out(att, ssm, h, woa, wos, gmix, gpre, wg, wu, wd, gpost):
    n = h.shape[0]
    row = lambda width: pl.BlockSpec((ROW_TILE, width), lambda i: (i, 0))
    return pl.pallas_call(
        _mix_out_body,
        grid=(pl.cdiv(n, ROW_TILE),),
        in_specs=[row(ATT_W), row(SSM_W), row(D_MODEL), _const_spec((ATT_W, D_MODEL)),
                  _const_spec((SSM_W, D_MODEL)), _const_spec((1, D_MODEL)), _const_spec((1, D_MODEL)),
                  _const_spec((D_MODEL, D_FF)), _const_spec((D_MODEL, D_FF)), _const_spec((D_FF, D_MODEL)),
                  _const_spec((1, D_MODEL))],
        out_specs=row(D_MODEL),
        out_shape=jax.ShapeDtypeStruct((n, D_MODEL), F32),
        compiler_params=pltpu.CompilerParams(dimension_semantics=("arbitrary",), vmem_limit_bytes=VMEM_LIMIT),
        name="mix_out_ffn2",
    )(att, ssm, h, woa, wos, gmix, gpre, wg, wu, wd, gpost)


def _rel_bucket_np(rel):
    half = NUM_BUCKETS // 2
    max_exact = half // 2
    n = np.abs(rel).astype(np.int64)
    nf = np.maximum(n, 1).astype(np.float64)
    large = max_exact + (np.log(nf / max_exact) / math.log(MAX_DISTANCE / max_exact)
                         * (half - max_exact)).astype(np.int64)
    large = np.minimum(large, half - 1)
    return np.where(rel > 0, half, 0) + np.where(n < max_exact, n, large)


def _pair_bias(rel_bias, n_rows, c0):
    length = n_rows + TQ
    rel = np.arange(length) - (TQ - 1) - LANE - c0
    table = (rel_bias - rel_bias[NUM_BUCKETS // 2 - 1][None, :]) * LOG2E
    t1d = table[_rel_bucket_np(rel)].T
    x = jnp.tile(t1d, (1, TQ))[:, :TQ * (length - 1)].reshape(N_HEADS, TQ, length - 1)
    b = jnp.transpose(x[:, :, TQ - 1:TQ - 1 + n_rows], (0, 2, 1)).reshape(N_PAIRS, 2, n_rows, TQ)
    return jnp.concatenate([b[:, 0], b[:, 1]], axis=-1).astype(F32)


def _prompt_schedule(n, n_pad):
    pos = np.arange(n_pad)
    vis = np.where(pos < N_META, N_META, N_META + CHUNK * ((pos - N_META) // CHUNK + 1))
    vis = np.where(pos < n, np.minimum(vis, n), 0).astype(np.int32)
    n_tiles = -(-n // TQ)
    vmax = vis[:n_tiles * TQ].reshape(n_tiles, TQ).max(axis=1)
    nch = -(-vmax // KEY_CHUNK)
    nfull = vis[:n_tiles * TQ].reshape(n_tiles, TQ).min(axis=1) // KEY_CHUNK
    toff = PROMPT_BIAS_C0 // LANE - np.arange(n_tiles)
    assert ((nch - 1) * (KEY_CHUNK // LANE) + toff).max() <= (PROMPT_BIAS_ROWS - KEY_CHUNK) // LANE - 1
    sched = tuple(jnp.asarray(a, I32) for a in (nch, nfull, toff))
    return sched, jnp.asarray(vis[None, :])


def _sample_schedule(past, ts):
    nk = past + ts
    sched = tuple(jnp.asarray([v], I32) for v in (-(-nk // KEY_CHUNK), 0, 0))
    vis = jnp.asarray(np.where(np.arange(TQ) < ts, nk, 0)[None, :], I32)
    return sched, vis


def _s5_params(lam_re, lam_im, log_step, b_re, b_im, c_re, c_im, d_skip):
    dt = jnp.exp(log_step)[:, None]
    mag = jnp.exp(lam_re * dt)
    ab_re, ab_im = mag * jnp.cos(lam_im * dt), mag * jnp.sin(lam_im * dt)
    nr, ni = ab_re - 1.0, ab_im
    den = lam_re * lam_re + lam_im * lam_im
    f_re, f_im = (nr * lam_re + ni * lam_im) / den, (ni * lam_re - nr * lam_im) / den
    bb_re = f_re[..., None] * b_re - f_im[..., None] * b_im
    bb_im = f_re[..., None] * b_im + f_im[..., None] * b_re
    eye = jnp.eye(SSM_GROUPS, dtype=F32)
    bd_in = lambda w: jnp.einsum('gpc,gh->gchp', w, eye).reshape(SSM_W, SSM_S).astype(BF16)
    bd_out = lambda w: jnp.einsum('gcp,gh->gphc', w, eye).reshape(SSM_S, SSM_W).astype(BF16)
    return (ab_re.reshape(1, SSM_S), ab_im.reshape(1, SSM_S), bd_in(bb_re), bd_in(bb_im),
            bd_out(c_re), bd_out(c_im), d_skip.reshape(1, SSM_W))


def kernel(x_prompt, x_sample, cache_k, cache_v, cache_kidx, state_ssm_re, state_ssm_im, meta_tokens, rel_bias,
           ffn1_g_pre, ffn1_w_gate, ffn1_w_up, ffn1_w_down, ffn1_g_post, mix_g_pre, w_in, w_out, mix_g_post,
           lam_re, lam_im, log_step, b_re, b_im, c_re, c_im, d_skip, w_glu, b_glu,
           ffn2_g_pre, ffn2_w_gate, ffn2_w_up, ffn2_w_down, ffn2_g_post):
    depth = ffn1_g_pre.shape[0]
    assert depth == 1
    bp, seq, _ = x_prompt.shape
    assert bp == 1
    ds, ts, _ = x_sample.shape
    past = cache_k.shape[2]
    n_p = N_META + seq
    n_p_pad = pl.cdiv(n_p, ROW_TILE) * ROW_TILE
    n_s = ds * ts
    assert n_s % ROW_TILE == 0 and ts <= TQ
    nk_s_pad = pl.cdiv(past + ts, KEY_CHUNK) * KEY_CHUNK
    l = 0

    row2 = lambda g: g[l].reshape(1, -1)
    w1 = (row2(ffn1_g_pre), _to_bf16(ffn1_w_gate[l]), _to_bf16(ffn1_w_up[l]), _to_bf16(ffn1_w_down[l]),
          row2(ffn1_g_post))
    w2 = (row2(ffn2_g_pre), _to_bf16(ffn2_w_gate[l]), _to_bf16(ffn2_w_up[l]), _to_bf16(ffn2_w_down[l]),
          row2(ffn2_g_post))
    wi = w_in[l]
    o_q, o_k, o_v, o_qi, o_ki, o_wi, o_us = np.cumsum((0, ATT_W, ATT_W, ATT_W, N_IDX * IDX_DIM, IDX_DIM, N_IDX))
    sl = lambda o, w: wi[:, o:o + w]
    wrow = jnp.concatenate([sl(o_k, ATT_W), sl(o_v, ATT_W), sl(o_us, SSM_W), sl(o_ki, IDX_DIM),
                            jnp.zeros((D_MODEL, ROW_COLS_PAD - ROW_COLS), F32)], axis=1).astype(BF16)
    wcol = jnp.concatenate([sl(o_q, ATT_W), sl(o_qi, ATT_W), sl(o_v, ATT_W), sl(o_wi, N_IDX)], axis=1).T.astype(BF16)
    woa, wos = w_out[l][:ATT_W].astype(BF16), w_out[l][ATT_W:].astype(BF16)
    s5w = _s5_params(lam_re[l], lam_im[l], log_step[l], b_re[l], b_im[l], c_re[l], c_im[l], d_skip[l])
    s5w = s5w + (w_glu[l].astype(BF16), b_glu[l].reshape(1, SSM_W))
    bias_tab = rel_bias.astype(F32)

    xp = jnp.concatenate([meta_tokens.astype(F32), x_prompt[0]], axis=0)
    hp = _ffn(xp, *w1)
    kp, vp, kip, usp, kbp, kibp, vtbp, qtp, qitp, wtp = _mix_in(hp, row2(mix_g_pre), wrow, wcol, n_p_pad)
    sched_p, vis_p = _prompt_schedule(n_p, n_p_pad)
    att_p = _attention(qitp[None], wtp[None], qtp[None], vis_p, kibp[None], kbp[None], vtbp[None],
                       _pair_bias(bias_tab, PROMPT_BIAS_ROWS, PROMPT_BIAS_C0), sched_p,
                       min(TOPK_MAX, seq // 4), True)[0]
    zero_state = jnp.zeros((1, 1, SSM_S), F32)
    tt_p = max(t for t in range(8, 513, 8) if n_p % t == 0)
    ssm_p, srp, sip = _s5(usp[None], zero_state, zero_state, *s5w, tt_p)
    yp = _mix_out(att_p, ssm_p[0], hp, woa, wos, row2(mix_g_post), *w2)

    hs = _ffn(x_sample.reshape(n_s, D_MODEL), *w1)
    ks, vs, kis, uss, _, _, _, qts, qits, wts = _mix_in(hs, row2(mix_g_pre), wrow, wcol, n_s)
    pad_k = lambda a: jnp.pad(a, ((0, 0), (0, nk_s_pad - past - ts), (0, 0)))
    k_all = pad_k(jnp.concatenate([cache_k[l].reshape(ds, past, ATT_W), ks.reshape(ds, ts, ATT_W)], 1).astype(BF16))
    v_all = pad_k(jnp.concatenate([cache_v[l].reshape(ds, past, ATT_W), vs.reshape(ds, ts, ATT_W)], 1).astype(BF16))
    ki_all = pad_k(jnp.concatenate([cache_kidx[l], kis.reshape(ds, ts, IDX_DIM)], 1).astype(BF16))
    vt_all = jnp.transpose(v_all.reshape(ds, nk_s_pad // KEY_CHUNK, KEY_CHUNK, ATT_W), (0, 1, 3, 2))
    lanes = lambda a: jnp.pad(jnp.transpose(a.reshape(a.shape[0], ds, ts), (1, 0, 2)), ((0, 0), (0, 0), (0, TQ - ts)))
    sched_s, vis_s = _sample_schedule(past, ts)
    att_s = _attention(lanes(qits), lanes(wts), lanes(qts), vis_s, ki_all, k_all, vt_all,
                       _pair_bias(bias_tab, LANE + nk_s_pad, past), sched_s,
                       min(TOPK_MAX, (past + ts) // 4), False)
    att_s = att_s[:, :ts].reshape(n_s, ATT_W)
    ssm_s, srs, sis = _s5(uss.reshape(ds, ts, SSM_W), state_ssm_re[l].reshape(ds, 1, SSM_S),
                          state_ssm_im[l].reshape(ds, 1, SSM_S), *s5w, ts)
    ys = _mix_out(att_s, ssm_s.reshape(n_s, SSM_W), hs, woa, wos, row2(mix_g_post), *w2)

    heads = lambda a, b, t: a.reshape(1, b, t, N_HEADS, HEAD_DIM)
    state = lambda a, b: a.reshape(1, b, SSM_GROUPS, SSM_P)
    return (yp[N_META:][None], ys.reshape(ds, ts, D_MODEL),
            heads(kp, 1, n_p), heads(vp, 1, n_p), kip.reshape(1, 1, n_p, IDX_DIM), state(srp, 1), state(sip, 1),
            heads(ks, ds, ts), heads(vs, ds, ts), kis.reshape(1, ds, ts, IDX_DIM), state(srs, ds), state(sis, ds))
```

```python
import functools
import math

import numpy as np
import jax
import jax.numpy as jnp
from jax import lax
from jax.experimental import pallas as pl
from jax.experimental.pallas import tpu as pltpu

F32, BF16, I32 = jnp.float32, jnp.bfloat16, jnp.int32

D_MODEL = 1024
CHUNK = 64
N_META = 16
N_HEADS = 8
HEAD_DIM = 64
ATT_W = N_HEADS * HEAD_DIM
N_IDX = 8
IDX_DIM = 64
TOPK_MAX = 256
SSM_GROUPS = 32
SSM_GC = 16
SSM_W = SSM_GROUPS * SSM_GC
SSM_P = 64
SSM_S = SSM_GROUPS * SSM_P
D_FF = 2816
NUM_BUCKETS = 32
MAX_DISTANCE = 128
EPS = 1e-6

LANE = 128
VMEM_LIMIT = 62 * 1024 * 1024
ROW_TILE = 512
TQ = LANE
KEY_CHUNK = 512
PROMPT_BIAS_C0 = 512
PROMPT_BIAS_ROWS = 1280
NEG = -1e30
LOG2E = math.log2(math.e)
F32_MAX = float(np.finfo(np.float32).max)
N_PAIRS = N_HEADS // 2
SUM_ROWS = 16
CAND_DEPTH = 16
CAND_BATCH = 8 * CAND_DEPTH
CAND_GROUPS = 8
SCORE_ROUND = 8
TIE_SCAN_MAX = 8


def _batcher_network(n):
    pairs, p = [], 1
    while p < n:
        k = p
        while k >= 1:
            for j in range(k % p, n - k, 2 * k):
                for i in range(min(k, n - j - k)):
                    if (i + j) // (2 * p) == (i + j + k) // (2 * p):
                        pairs.append((i + j, i + j + k))
            k //= 2
        p *= 2
    return pairs


_SORT_NETWORK = _batcher_network(CAND_DEPTH)


def _rms(x, g):
    return x * lax.rsqrt(jnp.mean(x * x, axis=-1, keepdims=True) + EPS) * g


def _swiglu(xn, wg_ref, wu_ref, wd_ref):
    xb = xn.astype(BF16)
    a = jnp.dot(xb, wg_ref[...], preferred_element_type=F32)
    b = jnp.dot(xb, wu_ref[...], preferred_element_type=F32)
    hidden = (a * jax.nn.sigmoid(a) * b).astype(BF16)
    return jnp.dot(hidden, wd_ref[...], preferred_element_type=F32)


def _const_spec(shape):
    nd = len(shape)
    return pl.BlockSpec(shape, lambda *_: (0,) * nd, pipeline_mode=pl.Buffered(1))


def _ffn_body(x_ref, gpre_ref, wg_ref, wu_ref, wd_ref, gpost_ref, o_ref):
    x = x_ref[...]
    y = _swiglu(_rms(x, gpre_ref[...]), wg_ref, wu_ref, wd_ref)
    o_ref[...] = x + 0.5 * _rms(y, gpost_ref[...])


def _ffn(x, gpre, wg, wu, wd, gpost):
    n = x.shape[0]
    row = pl.BlockSpec((ROW_TILE, D_MODEL), lambda i: (i, 0))
    return pl.pallas_call(
        _ffn_body,
        grid=(pl.cdiv(n, ROW_TILE),),
        in_specs=[row, _const_spec((1, D_MODEL)), _const_spec((D_MODEL, D_FF)), _const_spec((D_MODEL, D_FF)),
                  _const_spec((D_FF, D_MODEL)), _const_spec((1, D_MODEL))],
        out_specs=row,
        out_shape=jax.ShapeDtypeStruct((n, D_MODEL), F32),
        compiler_params=pltpu.CompilerParams(dimension_semantics=("arbitrary",), vmem_limit_bytes=VMEM_LIMIT),
        name="ffn1",
    )(x, gpre, wg, wu, wd, gpost)


ROW_COLS = 3 * ATT_W + IDX_DIM
ROW_COLS_PAD = 13 * LANE
COL_ROWS = 3 * ATT_W + N_IDX


def _mix_in_body(n_valid, h_ref, g_ref, wrow_ref, wcol_ref,
                 k_ref, v_ref, ki_ref, us_ref, kb_ref, kib_ref, vtb_ref, qt_ref, qit_ref, wt_ref):
    i = pl.program_id(0)
    u = _rms(h_ref[...], g_ref[...]).astype(BF16)
    z = jnp.dot(u, wrow_ref[...], preferred_element_type=F32)
    zt = lax.dot_general(wcol_ref[...], u, (((1,), (1,)), ((), ())), preferred_element_type=F32)
    k = z[:, 0:ATT_W]
    v = z[:, ATT_W:2 * ATT_W]
    ki = z[:, 3 * ATT_W:3 * ATT_W + IDX_DIM]
    k_ref[...] = k
    v_ref[...] = v
    us_ref[...] = z[:, 2 * ATT_W:3 * ATT_W]
    ki_ref[...] = ki
    row_ok = (i * ROW_TILE + lax.broadcasted_iota(I32, (ROW_TILE, 1), 0)) < n_valid
    col_ok = (i * ROW_TILE + lax.broadcasted_iota(I32, (1, ROW_TILE), 1)) < n_valid
    kb_ref[...] = jnp.where(row_ok, k, 0.0).astype(BF16)
    kib_ref[...] = jnp.where(row_ok, ki, 0.0).astype(BF16)
    zt = jnp.where(col_ok, zt, 0.0)
    qt_ref[...] = (zt[0:ATT_W] * (HEAD_DIM ** -0.5 * LOG2E)).astype(BF16)
    vtb_ref[...] = zt[2 * ATT_W:3 * ATT_W].astype(BF16)
    wt = zt[3 * ATT_W:3 * ATT_W + N_IDX] * ((IDX_DIM ** -0.5) * (N_IDX ** -0.5))
    wt_ref[...] = wt
    for h in range(N_IDX):
        qi_h = zt[ATT_W + h * IDX_DIM:ATT_W + (h + 1) * IDX_DIM]
        qit_ref[h * IDX_DIM:(h + 1) * IDX_DIM, :] = (qi_h * wt[h:h + 1, :]).astype(BF16)


def _mix_in(h, g, wrow, wcol, n_pad):
    n = h.shape[0]
    steps = n_pad // ROW_TILE
    assert steps == pl.cdiv(n, ROW_TILE)

    def row(width):
        return pl.BlockSpec((ROW_TILE, width), lambda i: (i, 0))

    col = lambda rows: pl.BlockSpec((rows, ROW_TILE), lambda i: (0, i))
    out_shape = (
        jax.ShapeDtypeStruct((n, ATT_W), F32), jax.ShapeDtypeStruct((n, ATT_W), F32),
        jax.ShapeDtypeStruct((n, IDX_DIM), F32), jax.ShapeDtypeStruct((n, SSM_W), F32),
        jax.ShapeDtypeStruct((n_pad, ATT_W), BF16), jax.ShapeDtypeStruct((n_pad, IDX_DIM), BF16),
        jax.ShapeDtypeStruct((n_pad // ROW_TILE, ATT_W, ROW_TILE), BF16),
        jax.ShapeDtypeStruct((ATT_W, n_pad), BF16), jax.ShapeDtypeStruct((ATT_W, n_pad), BF16),
        jax.ShapeDtypeStruct((N_IDX, n_pad), F32),
    )
    out_specs = (
        row(ATT_W), row(ATT_W), row(IDX_DIM), row(SSM_W), row(ATT_W), row(IDX_DIM),
        pl.BlockSpec((None, ATT_W, ROW_TILE), lambda i: (i, 0, 0)),
        col(ATT_W), col(ATT_W), col(N_IDX),
    )
    return pl.pallas_call(
        functools.partial(_mix_in_body, n),
        grid=(steps,),
        in_specs=[row(D_MODEL), _const_spec((1, D_MODEL)), _const_spec((D_MODEL, ROW_COLS_PAD)),
                  _const_spec((COL_ROWS, D_MODEL))],
        out_specs=out_specs,
        out_shape=out_shape,
        compiler_params=pltpu.CompilerParams(dimension_semantics=("arbitrary",), vmem_limit_bytes=VMEM_LIMIT),
        name="mix_in",
    )(h, g, wrow, wcol)


def _ordered_bits_to_f32(u):
    k = u ^ jnp.int32(-2 ** 31)
    b = k ^ ((k >> 31) & jnp.int32(0x7FFFFFFF))
    return lax.bitcast_convert_type(b, F32)


def _attn_body(topk, idx_bits,
               nch_ref, nfull_ref, toff_ref,
               qit_ref, wt_ref, qt_ref, vis_ref, ki_ref, k_ref, vt_ref, bias_ref,
               o_ref,
               st_ref, cand_ref, thr_ref, cgt_ref, cge_ref, clip_ref, qicat_ref, qbd_ref, sa_ref, sb_ref,
               bma_ref, bmb_ref,
               m_ref, acc_ref, cut_ref, rowf_ref, ot_ref):
    j = pl.program_id(1)
    nch = nch_ref[j]
    toff = toff_ref[j]
    kf = float(topk)
    bias_tile_max = (bias_ref.shape[1] - KEY_CHUNK) // LANE - 1

    for h in range(N_IDX):
        qicat_ref[:, h * TQ:(h + 1) * TQ] = qit_ref[h * IDX_DIM:(h + 1) * IDX_DIM, :]

    @pl.when(jnp.logical_and(pl.program_id(0) == 0, j == 0))
    def _():
        qbd_ref[...] = jnp.zeros(qbd_ref.shape, BF16)
        rowf_ref[...] = lax.broadcasted_iota(I32, (KEY_CHUNK, TQ), 0).astype(F32)

    for a in range(N_PAIRS):
        qbd_ref[a, 0:HEAD_DIM, 0:TQ] = qt_ref[2 * a * HEAD_DIM:(2 * a + 1) * HEAD_DIM, :]
        qbd_ref[a, HEAD_DIM:2 * HEAD_DIM, TQ:2 * TQ] = qt_ref[(2 * a + 1) * HEAD_DIM:(2 * a + 2) * HEAD_DIM, :]

    def chunk_start(c):
        return pl.multiple_of(c * KEY_CHUNK, KEY_CHUNK)

    def chunk_rows(r0):
        return r0 + lax.broadcasted_iota(I32, (KEY_CHUNK, TQ), 0)

    vis = vis_ref[...]
    cand_ref[...] = jnp.full(cand_ref.shape, -jnp.inf, F32)
    w_pos = wt_ref[...] >= 0.0
    clip_ref[0:N_IDX, :] = jnp.where(w_pos, 0.0, -jnp.inf)
    clip_ref[N_IDX:2 * N_IDX, :] = jnp.where(w_pos, jnp.inf, 0.0)

    last_batch_row = ki_ref.shape[0] - CAND_BATCH

    def score_round(masked, i, carry):
        for g in range(SCORE_ROUND):
            r0 = pl.multiple_of((i * SCORE_ROUND + g) * CAND_BATCH, CAND_BATCH)
            rel = jnp.dot(ki_ref[pl.ds(jnp.minimum(r0, last_batch_row), CAND_BATCH), :], qicat_ref[...],
                          preferred_element_type=F32)
            score = jnp.zeros((CAND_BATCH, TQ), F32)
            for h in range(N_IDX):
                score = score + jnp.clip(rel[:, h * TQ:(h + 1) * TQ], clip_ref[h:h + 1, :],
                                         clip_ref[N_IDX + h:N_IDX + h + 1, :])
            if masked:
                score = jnp.where(r0 + lax.broadcasted_iota(I32, (CAND_BATCH, TQ), 0) < vis, score, -jnp.inf)
            st_ref[pl.ds(r0, CAND_BATCH), :] = score
        return carry

    round_chunks = SCORE_ROUND * CAND_BATCH // KEY_CHUNK
    full_rounds = nfull_ref[j] // round_chunks
    lax.fori_loop(0, full_rounds, functools.partial(score_round, False), 0)
    lax.fori_loop(full_rounds, (nch + round_chunks - 1) // round_chunks, functools.partial(score_round, True), 0)

    def merge_batch(b, carry):
        r0 = pl.multiple_of(b * CAND_BATCH, CAND_BATCH)
        x = [st_ref[pl.ds(r0 + 8 * r, 8), :] for r in range(CAND_DEPTH)]
        for lo, hi in _SORT_NETWORK:
            x[lo], x[hi] = jnp.maximum(x[lo], x[hi]), jnp.minimum(x[lo], x[hi])
        g0 = pl.multiple_of(lax.rem(b, CAND_GROUPS) * CAND_BATCH, CAND_BATCH)
        t = [jnp.maximum(cand_ref[pl.ds(g0 + 8 * r, 8), :], x[CAND_DEPTH - 1 - r]) for r in range(CAND_DEPTH)]
        d = CAND_DEPTH // 2
        while d >= 1:
            for r in range(CAND_DEPTH):
                if r & d == 0:
                    t[r], t[r + d] = jnp.maximum(t[r], t[r + d]), jnp.minimum(t[r], t[r + d])
            d //= 2
        for r in range(CAND_DEPTH):
            cand_ref[pl.ds(g0 + 8 * r, 8), :] = t[r]
        return carry

    lax.fori_loop(0, nch * (KEY_CHUNK // CAND_BATCH), merge_batch, 0)

    def reduce_rows(ref, n_chunks, fn, combine, init):
        reducer = jnp.sum if combine is jnp.add else jnp.max

        def body(c, acc):
            r0 = chunk_start(c)
            val = fn(ref[pl.ds(r0, KEY_CHUNK), :], r0)
            part = reducer(val.reshape(8, KEY_CHUNK // 64, 8, TQ), axis=1)
            return combine(acc, reducer(part, axis=0))

        acc = lax.fori_loop(0, n_chunks, body, jnp.full((8, TQ), init, F32))
        return reducer(acc, axis=0, keepdims=True)

    def count(ref, n_chunks, indicator):
        return reduce_rows(ref, n_chunks, indicator, jnp.add, 0.0)

    def select(ref, n_chunks):
        def value_bit(i, res):
            trial = res | lax.shift_left(jnp.int32(1), 31 - i)
            thr_t = _ordered_bits_to_f32(trial)
            cnt = count(ref, n_chunks, lambda blk, r0: jnp.where(blk >= thr_t, 1.0, 0.0))
            return jnp.where(cnt >= kf, trial, res)

        res = lax.fori_loop(0, 32, value_bit, jnp.zeros((1, TQ), I32))
        res = jnp.where((res >> 23) == 0, jnp.int32(0x00800000), res)
        lo = _ordered_bits_to_f32(res)
        above = _ordered_bits_to_f32(res + 1)
        c_above = count(ref, n_chunks, lambda blk, r0: jnp.where(blk >= above, 1.0, 0.0))
        c_lo = count(ref, n_chunks, lambda blk, r0: jnp.where(blk >= lo, 1.0, 0.0))
        thr_ref[...] = lo
        cgt_ref[...] = c_above
        cge_ref[...] = c_lo

        def walking(state):
            return jnp.max(state[0]) > 0.0

        def walk(state):
            active, hi, c_hi = state
            v = reduce_rows(ref, n_chunks, lambda blk, r0: jnp.where(blk < hi, blk, -jnp.inf), jnp.maximum,
                            -jnp.inf)
            c_v = count(ref, n_chunks, lambda blk, r0: jnp.where(blk >= v, 1.0, 0.0))
            done = jnp.logical_and(active > 0.0, c_v >= kf)
            thr_ref[...] = jnp.where(done, v, thr_ref[...])
            cgt_ref[...] = jnp.where(done, c_hi, cgt_ref[...])
            cge_ref[...] = jnp.where(done, c_v, cge_ref[...])
            return jnp.where(done, 0.0, active), v, c_v

        surplus = jnp.where(jnp.logical_and(c_lo > kf, c_above < kf), 1.0, 0.0)
        lax.while_loop(walking, walk, (surplus, above, c_above))
        return lo

    cand_lo = select(cand_ref, cand_ref.shape[0] // KEY_CHUNK)
    kept_min = cand_ref[CAND_BATCH - 8:CAND_BATCH, :]
    for g in range(1, CAND_GROUPS):
        kept_min = jnp.maximum(kept_min, cand_ref[(g + 1) * CAND_BATCH - 8:(g + 1) * CAND_BATCH, :])
    unsafe = jnp.max(kept_min, axis=0, keepdims=True) >= cand_lo

    @pl.when(jnp.max(jnp.where(unsafe, 1.0, 0.0)) > 0.0)
    def _():
        select(st_ref, nch)

    thr = thr_ref[...]

    need = kf - cgt_ref[...]
    excess = cge_ref[...] - kf
    cut_ref[...] = jnp.full((1, TQ), F32_MAX, F32)

    def chunk_rows_f(r0):
        return rowf_ref[...] + r0.astype(F32)

    tie_need = jnp.max(jnp.where(excess > 0.0, need, 0.0)).astype(I32)

    @pl.when(jnp.logical_and(tie_need > 0, tie_need <= TIE_SCAN_MAX))
    def _():
        def next_tied(cut_f):
            def body(c, acc):
                r0 = chunk_start(c)
                tied_row = jnp.where(st_ref[pl.ds(r0, KEY_CHUNK), :] == thr, chunk_rows_f(r0), F32_MAX)
                v = jnp.where(tied_row > cut_f, tied_row, F32_MAX)
                part = jnp.min(v.reshape(8, KEY_CHUNK // 64, 8, TQ), axis=1)
                return jnp.minimum(acc, jnp.min(part, axis=0))

            acc = lax.fori_loop(0, nch, body, jnp.full((8, TQ), F32_MAX, F32))
            return jnp.min(acc, axis=0, keepdims=True)

        def take(t, cut_f):
            return jnp.where(t.astype(F32) < need, next_tied(cut_f), cut_f)

        cut_f = lax.fori_loop(0, tie_need, take, jnp.full((1, TQ), -1.0, F32))
        cut_ref[...] = jnp.where(excess > 0.0, cut_f, F32_MAX)

    @pl.when(tie_need > TIE_SCAN_MAX)
    def _():
        def index_bit(i, cut):
            trial = cut | lax.shift_left(jnp.int32(1), idx_bits - 1 - i)
            cnt = count(st_ref, nch, lambda blk, r0: jnp.where(
                blk == thr, jnp.where(chunk_rows(r0) < trial, 1.0, 0.0), 0.0))
            return jnp.where(cnt < need, trial, cut)

        cut = lax.fori_loop(0, idx_bits, index_bit, jnp.zeros((1, TQ), I32))
        cut_ref[...] = jnp.where(excess > 0.0, cut.astype(F32), F32_MAX)

    cut_f = cut_ref[...]

    st_ref[pl.ds(chunk_start(nch), KEY_CHUNK), :] = jnp.full((KEY_CHUNK, TQ), -jnp.inf, F32)

    m_ref[...] = jnp.full(m_ref.shape, NEG, F32)
    acc_ref[...] = jnp.zeros(acc_ref.shape, F32)
    last = nch - 1
    ones_rows = jnp.ones((SUM_ROWS, KEY_CHUNK), BF16)

    def logits(blk, s_ref, bmax_ref, with_bias):
        kblk = k_ref[pl.ds(chunk_start(jnp.minimum(blk, last)), KEY_CHUNK), :]
        r0 = chunk_start(jnp.minimum(blk, nch))
        score = st_ref[pl.ds(r0, KEY_CHUNK), :]
        tie = jnp.where(chunk_rows_f(r0) <= cut_f, 0.0, NEG)
        msk = jnp.where(score > thr, 0.0, jnp.where(score == thr, tie, NEG))
        msk2 = jnp.concatenate([msk, msk], axis=1)
        bias_row0 = pl.multiple_of((jnp.clip(4 * blk + toff, -1, bias_tile_max) + 1) * LANE, LANE)
        for a in range(N_PAIRS):
            s2 = jnp.dot(kblk[:, a * 2 * HEAD_DIM:(a + 1) * 2 * HEAD_DIM], qbd_ref[a],
                         preferred_element_type=F32) + msk2
            if with_bias:
                s2 = s2 + bias_ref[a, pl.ds(bias_row0, KEY_CHUNK), :]
            s_ref[a] = s2
            bmax_ref[a:a + 1, :] = jnp.max(s2, axis=0, keepdims=True)

    def softmax_pv(blk, s_ref, bmax_ref):
        vblk = jnp.minimum(blk, last)
        for a in range(N_PAIRS):
            m_old = m_ref[a:a + 1, :]
            m_new = jnp.maximum(m_old, bmax_ref[a:a + 1, :])
            alpha = jnp.exp2(m_old - m_new)
            p = jnp.exp2(s_ref[a] - m_new).astype(BF16)
            v_ones = jnp.concatenate([vt_ref[vblk, a * 2 * HEAD_DIM:(a + 1) * 2 * HEAD_DIM, :], ones_rows], axis=0)
            acc_ref[a] = alpha * acc_ref[a] + jnp.dot(v_ones, p, preferred_element_type=F32)
            m_ref[a:a + 1, :] = m_new

    def block_pair(with_bias, i, carry):
        logits(2 * i + 1, sb_ref, bmb_ref, with_bias)
        softmax_pv(2 * i, sa_ref, bma_ref)
        logits(2 * i + 2, sa_ref, bma_ref, with_bias)
        softmax_pv(2 * i + 1, sb_ref, bmb_ref)
        return carry

    far_pairs = jnp.minimum(jnp.maximum(3 - toff, 0) // 4, nch) // 2
    @pl.when(far_pairs > 0)
    def _():
        logits(0, sa_ref, bma_ref, False)

    lax.fori_loop(0, far_pairs, functools.partial(block_pair, False), 0)
    logits(2 * far_pairs, sa_ref, bma_ref, True)
    lax.fori_loop(far_pairs, (nch + 1) // 2, functools.partial(block_pair, True), 0)

    for a in range(N_PAIRS):
        for e in range(2):
            h = 2 * a + e
            num = acc_ref[a, e * HEAD_DIM:(e + 1) * HEAD_DIM, e * TQ:(e + 1) * TQ]
            den = acc_ref[a, 2 * HEAD_DIM:2 * HEAD_DIM + 1, e * TQ:(e + 1) * TQ]
            ot_ref[h * HEAD_DIM:(h + 1) * HEAD_DIM, :] = num / den
    o_ref[...] = ot_ref[...].T


def _attention(qit, wt, qt, vis, ki, k, vt, bias, sched, topk, resident):
    s_count, _, nq = qit.shape
    nk = k.shape[1]
    n_tiles = sched[0].shape[0]
    mode = dict(pipeline_mode=pl.Buffered(1)) if resident else {}
    idx_bits = max(1, int(nk - 1).bit_length())
    logit_buf = pltpu.VMEM((N_PAIRS, KEY_CHUNK, 2 * TQ), F32)
    in_specs = [
        pl.BlockSpec((None, ATT_W, TQ), lambda s, j, *_: (s, 0, j)),
        pl.BlockSpec((None, N_IDX, TQ), lambda s, j, *_: (s, 0, j)),
        pl.BlockSpec((None, ATT_W, TQ), lambda s, j, *_: (s, 0, j)),
        pl.BlockSpec((1, TQ), lambda s, j, *_: (0, j)),
        pl.BlockSpec((None, nk, IDX_DIM), lambda s, j, *_: (s, 0, 0), **mode),
        pl.BlockSpec((None, nk, ATT_W), lambda s, j, *_: (s, 0, 0), **mode),
        pl.BlockSpec((None, nk // KEY_CHUNK, ATT_W, KEY_CHUNK), lambda s, j, *_: (s, 0, 0, 0), **mode),
        pl.BlockSpec(bias.shape, lambda s, j, *_: (0, 0, 0), pipeline_mode=pl.Buffered(1)),
    ]
    grid_spec = pltpu.PrefetchScalarGridSpec(
        num_scalar_prefetch=3,
        grid=(s_count, n_tiles),
        in_specs=in_specs,
        out_specs=pl.BlockSpec((None, TQ, ATT_W), lambda s, j, *_: (s, j, 0)),
        scratch_shapes=[
            pltpu.VMEM((nk + KEY_CHUNK, TQ), F32),
            pltpu.VMEM((CAND_GROUPS * CAND_BATCH, TQ), F32),
            pltpu.VMEM((1, TQ), F32),
            pltpu.VMEM((1, TQ), F32),
            pltpu.VMEM((1, TQ), F32),
            pltpu.VMEM((2 * N_IDX, TQ), F32),
            pltpu.VMEM((IDX_DIM, N_IDX * TQ), BF16),
            pltpu.VMEM((N_PAIRS, 2 * HEAD_DIM, 2 * TQ), BF16),
            logit_buf, logit_buf,
            pltpu.VMEM((N_PAIRS, 2 * TQ), F32),
            pltpu.VMEM((N_PAIRS, 2 * TQ), F32),
            pltpu.VMEM((N_PAIRS, 2 * TQ), F32),
            pltpu.VMEM((N_PAIRS, 2 * HEAD_DIM + SUM_ROWS, 2 * TQ), F32),
            pltpu.VMEM((1, TQ), F32),
            pltpu.VMEM((KEY_CHUNK, TQ), F32),
            pltpu.VMEM((ATT_W, TQ), F32),
        ],
    )
    return pl.pallas_call(
        functools.partial(_attn_body, topk, idx_bits),
        grid_spec=grid_spec,
        out_shape=jax.ShapeDtypeStruct((s_count, nq, ATT_W), F32),
        compiler_params=pltpu.CompilerParams(dimension_semantics=("arbitrary", "arbitrary"),
                                             vmem_limit_bytes=VMEM_LIMIT),
        name="attn",
    )(*sched, qit, wt, qt, vis, ki, k, vt, bias)


def _s5_body(tt, us_ref, s0r_ref, s0i_ref, ar_ref, ai_ref, bbr_ref, bbi_ref, cr_ref, ci_ref, d_ref, wglu_ref,
             bglu_ref, out_ref, sr_out_ref, si_out_ref, str_ref, sti_ref, bre_ref, bim_ref, xr_ref, xi_ref):
    @pl.when(pl.program_id(1) == 0)
    def _():
        str_ref[...] = s0r_ref[...]
        sti_ref[...] = s0i_ref[...]

    u = us_ref[...]
    ub = u.astype(BF16)
    bre_ref[...] = jnp.dot(ub, bbr_ref[...], preferred_element_type=F32)
    bim_ref[...] = jnp.dot(ub, bbi_ref[...], preferred_element_type=F32)
    ar = ar_ref[...]
    ai = ai_ref[...]

    def step(t, carry):
        sr, si = carry
        nr = ar * sr - ai * si + bre_ref[pl.ds(t, 1), :]
        ni = ar * si + ai * sr + bim_ref[pl.ds(t, 1), :]
        xr_ref[pl.ds(t, 1), :] = nr
        xi_ref[pl.ds(t, 1), :] = ni
        return nr, ni

    sr, si = lax.fori_loop(0, tt, step, (str_ref[...], sti_ref[...]))
    str_ref[...] = sr
    sti_ref[...] = si
    sr_out_ref[...] = sr
    si_out_ref[...] = si
    y = (jnp.dot(xr_ref[...].astype(BF16), cr_ref[...], preferred_element_type=F32)
         - jnp.dot(xi_ref[...].astype(BF16), ci_ref[...], preferred_element_type=F32)
         + d_ref[...] * u)
    g = jax.nn.gelu(y)
    gate = jnp.dot(g.astype(BF16), wglu_ref[...], preferred_element_type=F32) + bglu_ref[...]
    out_ref[...] = g * jax.nn.sigmoid(gate)


def _s5(us, s0r, s0i, ar, ai, bbr, bbi, cr, ci, d, wglu, bglu, tt):
    s_count, t_len, _ = us.shape
    assert t_len % tt == 0
    state = pl.BlockSpec((None, 1, SSM_S), lambda s, t: (s, 0, 0))
    seq = pl.BlockSpec((None, tt, SSM_W), lambda s, t: (s, t, 0))
    return pl.pallas_call(
        functools.partial(_s5_body, tt),
        grid=(s_count, t_len // tt),
        in_specs=[seq, state, state, _const_spec((1, SSM_S)), _const_spec((1, SSM_S)),
                  _const_spec((SSM_W, SSM_S)), _const_spec((SSM_W, SSM_S)),
                  _const_spec((SSM_S, SSM_W)), _const_spec((SSM_S, SSM_W)),
                  _const_spec((1, SSM_W)), _const_spec((SSM_W, SSM_W)), _const_spec((1, SSM_W))],
        out_specs=(seq, state, state),
        out_shape=(jax.ShapeDtypeStruct((s_count, t_len, SSM_W), F32),
                   jax.ShapeDtypeStruct((s_count, 1, SSM_S), F32),
                   jax.ShapeDtypeStruct((s_count, 1, SSM_S), F32)),
        scratch_shapes=[pltpu.VMEM((1, SSM_S), F32), pltpu.VMEM((1, SSM_S), F32),
                        pltpu.VMEM((tt, SSM_S), F32), pltpu.VMEM((tt, SSM_S), F32),
                        pltpu.VMEM((tt, SSM_S), F32), pltpu.VMEM((tt, SSM_S), F32)],
        compiler_params=pltpu.CompilerParams(dimension_semantics=("arbitrary", "arbitrary"),
                                             vmem_limit_bytes=VMEM_LIMIT),
        name="s5",
    )(us, s0r, s0i, ar, ai, bbr, bbi, cr, ci, d, wglu, bglu)


def _mix_out_body(att_ref, ssm_ref, h_ref, woa_ref, wos_ref, gmix_ref, gpre_ref, wg_ref, wu_ref, wd_ref, gpost_ref,
                  o_ref):
    m = (jnp.dot(att_ref[...].astype(BF16), woa_ref[...], preferred_element_type=F32)
         + jnp.dot(ssm_ref[...].astype(BF16), wos_ref[...], preferred_element_type=F32))
    h2 = h_ref[...] + _rms(m, gmix_ref[...])
    y = _swiglu(_rms(h2, gpre_ref[...]), wg_ref, wu_ref, wd_ref)
    o_ref[...] = h2 + 0.5 * _rms(y, gpost_ref[...])


def _mix_out(att, ssm, h, woa, wos, gmix, gpre, wg, wu, wd, gpost):
    n = h.shape[0]
    row = lambda width: pl.BlockSpec((ROW_TILE, width), lambda i: (i, 0))
    return pl.pallas_call(
        _mix_out_body,
        grid=(pl.cdiv(n, ROW_TILE),),
        in_specs=[row(ATT_W), row(SSM_W), row(D_MODEL), _const_spec((ATT_W, D_MODEL)),
                  _const_spec((SSM_W, D_MODEL)), _const_spec((1, D_MODEL)), _const_spec((1, D_MODEL)),
                  _const_spec((D_MODEL, D_FF)), _const_spec((D_MODEL, D_FF)), _const_spec((D_FF, D_MODEL)),
                  _const_spec((1, D_MODEL))],
        out_specs=row(D_MODEL),
        out_shape=jax.ShapeDtypeStruct((n, D_MODEL), F32),
        compiler_params=pltpu.CompilerParams(dimension_semantics=("arbitrary",), vmem_limit_bytes=VMEM_LIMIT),
        name="mix_out_ffn2",
    )(att, ssm, h, woa, wos, gmix, gpre, wg, wu, wd, gpost)


def _rel_bucket_np(rel):
    half = NUM_BUCKETS // 2
    max_exact = half // 2
    n = np.abs(rel).astype(np.int64)
    nf = np.maximum(n, 1).astype(np.float64)
    large = max_exact + (np.log(nf / max_exact) / math.log(MAX_DISTANCE / max_exact)
                         * (half - max_exact)).astype(np.int64)
    large = np.minimum(large, half - 1)
    return np.where(rel > 0, half, 0) + np.where(n < max_exact, n, large)


def _pair_bias(rel_bias, n_rows, c0):
    length = n_rows + TQ
    rel = np.arange(length) - (TQ - 1) - LANE - c0
    table = (rel_bias - rel_bias[NUM_BUCKETS // 2 - 1][None, :]) * LOG2E
    t1d = table[_rel_bucket_np(rel)].T
    x = jnp.tile(t1d, (1, TQ))[:, :TQ * (length - 1)].reshape(N_HEADS, TQ, length - 1)
    b = jnp.transpose(x[:, :, TQ - 1:TQ - 1 + n_rows], (0, 2, 1)).reshape(N_PAIRS, 2, n_rows, TQ)
    return jnp.concatenate([b[:, 0], b[:, 1]], axis=-1).astype(F32)


def _prompt_schedule(n, n_pad):
    pos = np.arange(n_pad)
    vis = np.where(pos < N_META, N_META, N_META + CHUNK * ((pos - N_META) // CHUNK + 1))
    vis = np.where(pos < n, np.minimum(vis, n), 0).astype(np.int32)
    n_tiles = -(-n // TQ)
    vmax = vis[:n_tiles * TQ].reshape(n_tiles, TQ).max(axis=1)
    nch = -(-vmax // KEY_CHUNK)
    nfull = vis[:n_tiles * TQ].reshape(n_tiles, TQ).min(axis=1) // KEY_CHUNK
    toff = PROMPT_BIAS_C0 // LANE - np.arange(n_tiles)
    assert ((nch - 1) * (KEY_CHUNK // LANE) + toff).max() <= (PROMPT_BIAS_ROWS - KEY_CHUNK) // LANE - 1
    sched = tuple(jnp.asarray(a, I32) for a in (nch, nfull, toff))
    return sched, jnp.asarray(vis[None, :])


def _sample_schedule(past, ts):
    nk = past + ts
    sched = tuple(jnp.asarray([v], I32) for v in (-(-nk // KEY_CHUNK), 0, 0))
    vis = jnp.asarray(np.where(np.arange(TQ) < ts, nk, 0)[None, :], I32)
    return sched, vis


def _s5_params(lam_re, lam_im, log_step, b_re, b_im, c_re, c_im, d_skip):
    dt = jnp.exp(log_step)[:, None]
    mag = jnp.exp(lam_re * dt)
    ab_re, ab_im = mag * jnp.cos(lam_im * dt), mag * jnp.sin(lam_im * dt)
    nr, ni = ab_re - 1.0, ab_im
    den = lam_re * lam_re + lam_im * lam_im
    f_re, f_im = (nr * lam_re + ni * lam_im) / den, (ni * lam_re - nr * lam_im) / den
    bb_re = f_re[..., None] * b_re - f_im[..., None] * b_im
    bb_im = f_re[..., None] * b_im + f_im[..., None] * b_re
    eye = jnp.eye(SSM_GROUPS, dtype=F32)
    bd_in = lambda w: jnp.einsum('gpc,gh->gchp', w, eye).reshape(SSM_W, SSM_S).astype(BF16)
    bd_out = lambda w: jnp.einsum('gcp,gh->gphc', w, eye).reshape(SSM_S, SSM_W).astype(BF16)
    return (ab_re.reshape(1, SSM_S), ab_im.reshape(1, SSM_S), bd_in(bb_re), bd_in(bb_im),
            bd_out(c_re), bd_out(c_im), d_skip.reshape(1, SSM_W))


def kernel(x_prompt, x_sample, cache_k, cache_v, cache_kidx, state_ssm_re, state_ssm_im, meta_tokens, rel_bias,
           ffn1_g_pre, ffn1_w_gate, ffn1_w_up, ffn1_w_down, ffn1_g_post, mix_g_pre, w_in, w_out, mix_g_post,
           lam_re, lam_im, log_step, b_re, b_im, c_re, c_im, d_skip, w_glu, b_glu,
           ffn2_g_pre, ffn2_w_gate, ffn2_w_up, ffn2_w_down, ffn2_g_post):
    depth = ffn1_g_pre.shape[0]
    assert depth == 1
    bp, seq, _ = x_prompt.shape
    assert bp == 1
    ds, ts, _ = x_sample.shape
    past = cache_k.shape[2]
    n_p = N_META + seq
    n_p_pad = pl.cdiv(n_p, ROW_TILE) * ROW_TILE
    n_s = ds * ts
    assert n_s % ROW_TILE == 0 and ts <= TQ
    nk_s_pad = pl.cdiv(past + ts, KEY_CHUNK) * KEY_CHUNK
    l = 0

    row2 = lambda g: g[l].reshape(1, -1)
    w1 = (row2(ffn1_g_pre), ffn1_w_gate[l].astype(BF16), ffn1_w_up[l].astype(BF16), ffn1_w_down[l].astype(BF16),
          row2(ffn1_g_post))
    w2 = (row2(ffn2_g_pre), ffn2_w_gate[l].astype(BF16), ffn2_w_up[l].astype(BF16), ffn2_w_down[l].astype(BF16),
          row2(ffn2_g_post))
    wi = w_in[l]
    o_q, o_k, o_v, o_qi, o_ki, o_wi, o_us = np.cumsum((0, ATT_W, ATT_W, ATT_W, N_IDX * IDX_DIM, IDX_DIM, N_IDX))
    sl = lambda o, w: wi[:, o:o + w]
    wrow = jnp.concatenate([sl(o_k, ATT_W), sl(o_v, ATT_W), sl(o_us, SSM_W), sl(o_ki, IDX_DIM),
                            jnp.zeros((D_MODEL, ROW_COLS_PAD - ROW_COLS), F32)], axis=1).astype(BF16)
    wcol = jnp.concatenate([sl(o_q, ATT_W), sl(o_qi, ATT_W), sl(o_v, ATT_W), sl(o_wi, N_IDX)], axis=1).T.astype(BF16)
    woa, wos = w_out[l][:ATT_W].astype(BF16), w_out[l][ATT_W:].astype(BF16)
    s5w = _s5_params(lam_re[l], lam_im[l], log_step[l], b_re[l], b_im[l], c_re[l], c_im[l], d_skip[l])
    s5w = s5w + (w_glu[l].astype(BF16), b_glu[l].reshape(1, SSM_W))
    bias_tab = rel_bias.astype(F32)

    xp = jnp.concatenate([meta_tokens.astype(F32), x_prompt[0]], axis=0)
    hp = _ffn(xp, *w1)
    kp, vp, kip, usp, kbp, kibp, vtbp, qtp, qitp, wtp = _mix_in(hp, row2(mix_g_pre), wrow, wcol, n_p_pad)
    sched_p, vis_p = _prompt_schedule(n_p, n_p_pad)
    att_p = _attention(qitp[None], wtp[None], qtp[None], vis_p, kibp[None], kbp[None], vtbp[None],
                       _pair_bias(bias_tab, PROMPT_BIAS_ROWS, PROMPT_BIAS_C0), sched_p,
                       min(TOPK_MAX, seq // 4), True)[0]
    zero_state = jnp.zeros((1, 1, SSM_S), F32)
    tt_p = max(t for t in range(8, 513, 8) if n_p % t == 0)
    ssm_p, srp, sip = _s5(usp[None], zero_state, zero_state, *s5w, tt_p)
    yp = _mix_out(att_p, ssm_p[0], hp, woa, wos, row2(mix_g_post), *w2)

    hs = _ffn(x_sample.reshape(n_s, D_MODEL), *w1)
    ks, vs, kis, uss, _, _, _, qts, qits, wts = _mix_in(hs, row2(mix_g_pre), wrow, wcol, n_s)
    pad_k = lambda a: jnp.pad(a, ((0, 0), (0, nk_s_pad - past - ts), (0, 0)))
    k_all = pad_k(jnp.concatenate([cache_k[l].reshape(ds, past, ATT_W), ks.reshape(ds, ts, ATT_W)], 1).astype(BF16))
    v_all = pad_k(jnp.concatenate([cache_v[l].reshape(ds, past, ATT_W), vs.reshape(ds, ts, ATT_W)], 1).astype(BF16))
    ki_all = pad_k(jnp.concatenate([cache_kidx[l], kis.reshape(ds, ts, IDX_DIM)], 1).astype(BF16))
    vt_all = jnp.transpose(v_all.reshape(ds, nk_s_pad // KEY_CHUNK, KEY_CHUNK, ATT_W), (0, 1, 3, 2))
    lanes = lambda a: jnp.pad(jnp.transpose(a.reshape(a.shape[0], ds, ts), (1, 0, 2)), ((0, 0), (0, 0), (0, TQ - ts)))
    sched_s, vis_s = _sample_schedule(past, ts)
    att_s = _attention(lanes(qits), lanes(wts), lanes(qts), vis_s, ki_all, k_all, vt_all,
                       _pair_bias(bias_tab, LANE + nk_s_pad, past), sched_s,
                       min(TOPK_MAX, (past + ts) // 4), False)
    att_s = att_s[:, :ts].reshape(n_s, ATT_W)
    ssm_s, srs, sis = _s5(uss.reshape(ds, ts, SSM_W), state_ssm_re[l].reshape(ds, 1, SSM_S),
                          state_ssm_im[l].reshape(ds, 1, SSM_S), *s5w, ts)
    ys = _mix_out(att_s, ssm_s.reshape(n_s, SSM_W), hs, woa, wos, row2(mix_g_post), *w2)

    heads = lambda a, b, t: a.reshape(1, b, t, N_HEADS, HEAD_DIM)
    state = lambda a, b: a.reshape(1, b, SSM_GROUPS, SSM_P)
    return (yp[N_META:][None], ys.reshape(ds, ts, D_MODEL),
            heads(kp, 1, n_p), heads(vp, 1, n_p), kip.reshape(1, 1, n_p, IDX_DIM), state(srp, 1), state(sip, 1),
            heads(ks, ds, ts), heads(vs, ds, ts), kis.reshape(1, ds, ts, IDX_DIM), state(srs, ds), state(sis, ds))
```

```python
import functools
import math

import numpy as np
import jax
import jax.numpy as jnp
from jax import lax
from jax.experimental import pallas as pl
from jax.experimental.pallas import tpu as pltpu

F32, BF16, I32 = jnp.float32, jnp.bfloat16, jnp.int32

D_MODEL = 1024
CHUNK = 64
N_META = 16
N_HEADS = 8
HEAD_DIM = 64
ATT_W = N_HEADS * HEAD_DIM
N_IDX = 8
IDX_DIM = 64
TOPK_MAX = 256
SSM_GROUPS = 32
SSM_GC = 16
SSM_W = SSM_GROUPS * SSM_GC
SSM_P = 64
SSM_S = SSM_GROUPS * SSM_P
D_FF = 2816
NUM_BUCKETS = 32
MAX_DISTANCE = 128
EPS = 1e-6

LANE = 128
VMEM_LIMIT = 62 * 1024 * 1024
ROW_TILE = 512
TQ = LANE
KEY_CHUNK = 512
PROMPT_BIAS_C0 = 512
PROMPT_BIAS_ROWS = 1280
NEG = -1e30
LOG2E = math.log2(math.e)
F32_MAX = float(np.finfo(np.float32).max)
N_PAIRS = N_HEADS // 2
SUM_ROWS = 16
CAND_DEPTH = 16
CAND_BATCH = 8 * CAND_DEPTH
CAND_GROUPS = 8
SCORE_ROUND = 8
TIE_SCAN_MAX = 8


def _batcher_network(n):
    pairs, p = [], 1
    while p < n:
        k = p
        while k >= 1:
            for j in range(k % p, n - k, 2 * k):
                for i in range(min(k, n - j - k)):
                    if (i + j) // (2 * p) == (i + j + k) // (2 * p):
                        pairs.append((i + j, i + j + k))
            k //= 2
        p *= 2
    return pairs


_SORT_NETWORK = _batcher_network(CAND_DEPTH)


def _rms(x, g):
    return x * lax.rsqrt(jnp.mean(x * x, axis=-1, keepdims=True) + EPS) * g


def _swiglu(xn, wg_ref, wu_ref, wd_ref):
    xb = xn.astype(BF16)
    a = jnp.dot(xb, wg_ref[...], preferred_element_type=F32)
    b = jnp.dot(xb, wu_ref[...], preferred_element_type=F32)
    hidden = (a * jax.nn.sigmoid(a) * b).astype(BF16)
    return jnp.dot(hidden, wd_ref[...], preferred_element_type=F32)


def _const_spec(shape):
    nd = len(shape)
    return pl.BlockSpec(shape, lambda *_: (0,) * nd, pipeline_mode=pl.Buffered(1))


def _ffn_body(x_ref, gpre_ref, wg_ref, wu_ref, wd_ref, gpost_ref, o_ref):
    x = x_ref[...]
    y = _swiglu(_rms(x, gpre_ref[...]), wg_ref, wu_ref, wd_ref)
    o_ref[...] = x + 0.5 * _rms(y, gpost_ref[...])


def _ffn(x, gpre, wg, wu, wd, gpost):
    n = x.shape[0]
    row = pl.BlockSpec((ROW_TILE, D_MODEL), lambda i: (i, 0))
    return pl.pallas_call(
        _ffn_body,
        grid=(pl.cdiv(n, ROW_TILE),),
        in_specs=[row, _const_spec((1, D_MODEL)), _const_spec((D_MODEL, D_FF)), _const_spec((D_MODEL, D_FF)),
                  _const_spec((D_FF, D_MODEL)), _const_spec((1, D_MODEL))],
        out_specs=row,
        out_shape=jax.ShapeDtypeStruct((n, D_MODEL), F32),
        compiler_params=pltpu.CompilerParams(dimension_semantics=("arbitrary",), vmem_limit_bytes=VMEM_LIMIT),
        name="ffn1",
    )(x, gpre, wg, wu, wd, gpost)


ROW_COLS = 3 * ATT_W + IDX_DIM
ROW_COLS_PAD = 13 * LANE
COL_ROWS = 3 * ATT_W + N_IDX


def _mix_in_body(n_valid, h_ref, g_ref, wrow_ref, wcol_ref,
                 k_ref, v_ref, ki_ref, us_ref, kb_ref, kib_ref, vtb_ref, qt_ref, qit_ref, wt_ref):
    i = pl.program_id(0)
    u = _rms(h_ref[...], g_ref[...]).astype(BF16)
    z = jnp.dot(u, wrow_ref[...], preferred_element_type=F32)
    zt = lax.dot_general(wcol_ref[...], u, (((1,), (1,)), ((), ())), preferred_element_type=F32)
    k = z[:, 0:ATT_W]
    v = z[:, ATT_W:2 * ATT_W]
    ki = z[:, 3 * ATT_W:3 * ATT_W + IDX_DIM]
    k_ref[...] = k
    v_ref[...] = v
    us_ref[...] = z[:, 2 * ATT_W:3 * ATT_W]
    ki_ref[...] = ki
    row_ok = (i * ROW_TILE + lax.broadcasted_iota(I32, (ROW_TILE, 1), 0)) < n_valid
    col_ok = (i * ROW_TILE + lax.broadcasted_iota(I32, (1, ROW_TILE), 1)) < n_valid
    kb_ref[...] = jnp.where(row_ok, k, 0.0).astype(BF16)
    kib_ref[...] = jnp.where(row_ok, ki, 0.0).astype(BF16)
    zt = jnp.where(col_ok, zt, 0.0)
    qt_ref[...] = (zt[0:ATT_W] * (HEAD_DIM ** -0.5 * LOG2E)).astype(BF16)
    vtb_ref[...] = zt[2 * ATT_W:3 * ATT_W].astype(BF16)
    wt = zt[3 * ATT_W:3 * ATT_W + N_IDX] * ((IDX_DIM ** -0.5) * (N_IDX ** -0.5))
    wt_ref[...] = wt
    for h in range(N_IDX):
        qi_h = zt[ATT_W + h * IDX_DIM:ATT_W + (h + 1) * IDX_DIM]
        qit_ref[h * IDX_DIM:(h + 1) * IDX_DIM, :] = (qi_h * wt[h:h + 1, :]).astype(BF16)


def _mix_in(h, g, wrow, wcol, n_pad):
    n = h.shape[0]
    steps = n_pad // ROW_TILE
    assert steps == pl.cdiv(n, ROW_TILE)

    def row(width):
        return pl.BlockSpec((ROW_TILE, width), lambda i: (i, 0))

    col = lambda rows: pl.BlockSpec((rows, ROW_TILE), lambda i: (0, i))
    out_shape = (
        jax.ShapeDtypeStruct((n, ATT_W), F32), jax.ShapeDtypeStruct((n, ATT_W), F32),
        jax.ShapeDtypeStruct((n, IDX_DIM), F32), jax.ShapeDtypeStruct((n, SSM_W), F32),
        jax.ShapeDtypeStruct((n_pad, ATT_W), BF16), jax.ShapeDtypeStruct((n_pad, IDX_DIM), BF16),
        jax.ShapeDtypeStruct((n_pad // ROW_TILE, ATT_W, ROW_TILE), BF16),
        jax.ShapeDtypeStruct((ATT_W, n_pad), BF16), jax.ShapeDtypeStruct((ATT_W, n_pad), BF16),
        jax.ShapeDtypeStruct((N_IDX, n_pad), F32),
    )
    out_specs = (
        row(ATT_W), row(ATT_W), row(IDX_DIM), row(SSM_W), row(ATT_W), row(IDX_DIM),
        pl.BlockSpec((None, ATT_W, ROW_TILE), lambda i: (i, 0, 0)),
        col(ATT_W), col(ATT_W), col(N_IDX),
    )
    return pl.pallas_call(
        functools.partial(_mix_in_body, n),
        grid=(steps,),
        in_specs=[row(D_MODEL), _const_spec((1, D_MODEL)), _const_spec((D_MODEL, ROW_COLS_PAD)),
                  _const_spec((COL_ROWS, D_MODEL))],
        out_specs=out_specs,
        out_shape=out_shape,
        compiler_params=pltpu.CompilerParams(dimension_semantics=("arbitrary",), vmem_limit_bytes=VMEM_LIMIT),
        name="mix_in",
    )(h, g, wrow, wcol)


def _ordered_bits_to_f32(u):
    k = u ^ jnp.int32(-2 ** 31)
    b = k ^ ((k >> 31) & jnp.int32(0x7FFFFFFF))
    return lax.bitcast_convert_type(b, F32)


def _attn_body(topk, idx_bits,
               nch_ref, nfull_ref, toff_ref,
               qit_ref, wt_ref, qt_ref, vis_ref, ki_ref, k_ref, vt_ref, bias_ref,
               o_ref,
               st_ref, cand_ref, thr_ref, cgt_ref, cge_ref, clip_ref, qicat_ref, qbd_ref, sa_ref, sb_ref,
               bma_ref, bmb_ref,
               m_ref, acc_ref, cut_ref, rowf_ref, ot_ref):
    j = pl.program_id(1)
    nch = nch_ref[j]
    toff = toff_ref[j]
    kf = float(topk)
    bias_tile_max = (bias_ref.shape[1] - KEY_CHUNK) // LANE - 1

    for h in range(N_IDX):
        qicat_ref[:, h * TQ:(h + 1) * TQ] = qit_ref[h * IDX_DIM:(h + 1) * IDX_DIM, :]
    qbd_ref[...] = jnp.zeros(qbd_ref.shape, BF16)
    for a in range(N_PAIRS):
        qbd_ref[a, 0:HEAD_DIM, 0:TQ] = qt_ref[2 * a * HEAD_DIM:(2 * a + 1) * HEAD_DIM, :]
        qbd_ref[a, HEAD_DIM:2 * HEAD_DIM, TQ:2 * TQ] = qt_ref[(2 * a + 1) * HEAD_DIM:(2 * a + 2) * HEAD_DIM, :]

    def chunk_start(c):
        return pl.multiple_of(c * KEY_CHUNK, KEY_CHUNK)

    def chunk_rows(r0):
        return r0 + lax.broadcasted_iota(I32, (KEY_CHUNK, TQ), 0)

    vis = vis_ref[...]
    cand_ref[...] = jnp.full(cand_ref.shape, -jnp.inf, F32)
    w_pos = wt_ref[...] >= 0.0
    clip_ref[0:N_IDX, :] = jnp.where(w_pos, 0.0, -jnp.inf)
    clip_ref[N_IDX:2 * N_IDX, :] = jnp.where(w_pos, jnp.inf, 0.0)

    last_batch_row = ki_ref.shape[0] - CAND_BATCH

    def score_round(masked, i, carry):
        for g in range(SCORE_ROUND):
            r0 = pl.multiple_of((i * SCORE_ROUND + g) * CAND_BATCH, CAND_BATCH)
            rel = jnp.dot(ki_ref[pl.ds(jnp.minimum(r0, last_batch_row), CAND_BATCH), :], qicat_ref[...],
                          preferred_element_type=F32)
            score = jnp.zeros((CAND_BATCH, TQ), F32)
            for h in range(N_IDX):
                score = score + jnp.clip(rel[:, h * TQ:(h + 1) * TQ], clip_ref[h:h + 1, :],
                                         clip_ref[N_IDX + h:N_IDX + h + 1, :])
            if masked:
                score = jnp.where(r0 + lax.broadcasted_iota(I32, (CAND_BATCH, TQ), 0) < vis, score, -jnp.inf)
            st_ref[pl.ds(r0, CAND_BATCH), :] = score
        return carry

    round_chunks = SCORE_ROUND * CAND_BATCH // KEY_CHUNK
    full_rounds = nfull_ref[j] // round_chunks
    lax.fori_loop(0, full_rounds, functools.partial(score_round, False), 0)
    lax.fori_loop(full_rounds, (nch + round_chunks - 1) // round_chunks, functools.partial(score_round, True), 0)

    def merge_batch(b, carry):
        r0 = pl.multiple_of(b * CAND_BATCH, CAND_BATCH)
        x = [st_ref[pl.ds(r0 + 8 * r, 8), :] for r in range(CAND_DEPTH)]
        for lo, hi in _SORT_NETWORK:
            x[lo], x[hi] = jnp.maximum(x[lo], x[hi]), jnp.minimum(x[lo], x[hi])
        g0 = pl.multiple_of(lax.rem(b, CAND_GROUPS) * CAND_BATCH, CAND_BATCH)
        t = [jnp.maximum(cand_ref[pl.ds(g0 + 8 * r, 8), :], x[CAND_DEPTH - 1 - r]) for r in range(CAND_DEPTH)]
        d = CAND_DEPTH // 2
        while d >= 1:
            for r in range(CAND_DEPTH):
                if r & d == 0:
                    t[r], t[r + d] = jnp.maximum(t[r], t[r + d]), jnp.minimum(t[r], t[r + d])
            d //= 2
        for r in range(CAND_DEPTH):
            cand_ref[pl.ds(g0 + 8 * r, 8), :] = t[r]
        return carry

    lax.fori_loop(0, nch * (KEY_CHUNK // CAND_BATCH), merge_batch, 0)

    def reduce_rows(ref, n_chunks, fn, combine, init):
        reducer = jnp.sum if combine is jnp.add else jnp.max

        def body(c, acc):
            r0 = chunk_start(c)
            val = fn(ref[pl.ds(r0, KEY_CHUNK), :], r0)
            part = reducer(val.reshape(8, KEY_CHUNK // 64, 8, TQ), axis=1)
            return combine(acc, reducer(part, axis=0))

        acc = lax.fori_loop(0, n_chunks, body, jnp.full((8, TQ), init, F32))
        return reducer(acc, axis=0, keepdims=True)

    def count(ref, n_chunks, indicator):
        return reduce_rows(ref, n_chunks, indicator, jnp.add, 0.0)

    def select(ref, n_chunks):
        def value_bit(i, res):
            trial = res | lax.shift_left(jnp.int32(1), 31 - i)
            thr_t = _ordered_bits_to_f32(trial)
            cnt = count(ref, n_chunks, lambda blk, r0: jnp.where(blk >= thr_t, 1.0, 0.0))
            return jnp.where(cnt >= kf, trial, res)

        res = lax.fori_loop(0, 32, value_bit, jnp.zeros((1, TQ), I32))
        res = jnp.where((res >> 23) == 0, jnp.int32(0x00800000), res)
        lo = _ordered_bits_to_f32(res)
        above = _ordered_bits_to_f32(res + 1)
        c_above = count(ref, n_chunks, lambda blk, r0: jnp.where(blk >= above, 1.0, 0.0))
        c_lo = count(ref, n_chunks, lambda blk, r0: jnp.where(blk >= lo, 1.0, 0.0))
        thr_ref[...] = lo
        cgt_ref[...] = c_above
        cge_ref[...] = c_lo

        def walking(state):
            return jnp.max(state[0]) > 0.0

        def walk(state):
            active, hi, c_hi = state
            v = reduce_rows(ref, n_chunks, lambda blk, r0: jnp.where(blk < hi, blk, -jnp.inf), jnp.maximum,
                            -jnp.inf)
            c_v = count(ref, n_chunks, lambda blk, r0: jnp.where(blk >= v, 1.0, 0.0))
            done = jnp.logical_and(active > 0.0, c_v >= kf)
            thr_ref[...] = jnp.where(done, v, thr_ref[...])
            cgt_ref[...] = jnp.where(done, c_hi, cgt_ref[...])
            cge_ref[...] = jnp.where(done, c_v, cge_ref[...])
            return jnp.where(done, 0.0, active), v, c_v

        surplus = jnp.where(jnp.logical_and(c_lo > kf, c_above < kf), 1.0, 0.0)
        lax.while_loop(walking, walk, (surplus, above, c_above))
        return lo

    cand_lo = select(cand_ref, cand_ref.shape[0] // KEY_CHUNK)
    kept_min = cand_ref[CAND_BATCH - 8:CAND_BATCH, :]
    for g in range(1, CAND_GROUPS):
        kept_min = jnp.maximum(kept_min, cand_ref[(g + 1) * CAND_BATCH - 8:(g + 1) * CAND_BATCH, :])
    unsafe = jnp.max(kept_min, axis=0, keepdims=True) >= cand_lo

    @pl.when(jnp.max(jnp.where(unsafe, 1.0, 0.0)) > 0.0)
    def _():
        select(st_ref, nch)

    thr = thr_ref[...]

    need = kf - cgt_ref[...]
    excess = cge_ref[...] - kf
    cut_ref[...] = jnp.full((1, TQ), F32_MAX, F32)
    rowf_ref[...] = lax.broadcasted_iota(I32, (KEY_CHUNK, TQ), 0).astype(F32)

    def chunk_rows_f(r0):
        return rowf_ref[...] + r0.astype(F32)

    tie_need = jnp.max(jnp.where(excess > 0.0, need, 0.0)).astype(I32)

    @pl.when(jnp.logical_and(tie_need > 0, tie_need <= TIE_SCAN_MAX))
    def _():
        def next_tied(cut_f):
            def body(c, acc):
                r0 = chunk_start(c)
                tied_row = jnp.where(st_ref[pl.ds(r0, KEY_CHUNK), :] == thr, chunk_rows_f(r0), F32_MAX)
                v = jnp.where(tied_row > cut_f, tied_row, F32_MAX)
                part = jnp.min(v.reshape(8, KEY_CHUNK // 64, 8, TQ), axis=1)
                return jnp.minimum(acc, jnp.min(part, axis=0))

            acc = lax.fori_loop(0, nch, body, jnp.full((8, TQ), F32_MAX, F32))
            return jnp.min(acc, axis=0, keepdims=True)

        def take(t, cut_f):
            return jnp.where(t.astype(F32) < need, next_tied(cut_f), cut_f)

        cut_f = lax.fori_loop(0, tie_need, take, jnp.full((1, TQ), -1.0, F32))
        cut_ref[...] = jnp.where(excess > 0.0, cut_f, F32_MAX)

    @pl.when(tie_need > TIE_SCAN_MAX)
    def _():
        def index_bit(i, cut):
            trial = cut | lax.shift_left(jnp.int32(1), idx_bits - 1 - i)
            cnt = count(st_ref, nch, lambda blk, r0: jnp.where(
                blk == thr, jnp.where(chunk_rows(r0) < trial, 1.0, 0.0), 0.0))
            return jnp.where(cnt < need, trial, cut)

        cut = lax.fori_loop(0, idx_bits, index_bit, jnp.zeros((1, TQ), I32))
        cut_ref[...] = jnp.where(excess > 0.0, cut.astype(F32), F32_MAX)

    cut_f = cut_ref[...]

    st_ref[pl.ds(chunk_start(nch), KEY_CHUNK), :] = jnp.full((KEY_CHUNK, TQ), -jnp.inf, F32)

    m_ref[...] = jnp.full(m_ref.shape, NEG, F32)
    acc_ref[...] = jnp.zeros(acc_ref.shape, F32)
    last = nch - 1
    ones_rows = jnp.ones((SUM_ROWS, KEY_CHUNK), BF16)

    def logits(blk, s_ref, bmax_ref, with_bias):
        kblk = k_ref[pl.ds(chunk_start(jnp.minimum(blk, last)), KEY_CHUNK), :]
        r0 = chunk_start(jnp.minimum(blk, nch))
        score = st_ref[pl.ds(r0, KEY_CHUNK), :]
        tie = jnp.where(chunk_rows_f(r0) <= cut_f, 0.0, NEG)
        msk = jnp.where(score > thr, 0.0, jnp.where(score == thr, tie, NEG))
        msk2 = jnp.concatenate([msk, msk], axis=1)
        bias_row0 = pl.multiple_of((jnp.clip(4 * blk + toff, -1, bias_tile_max) + 1) * LANE, LANE)
        for a in range(N_PAIRS):
            s2 = jnp.dot(kblk[:, a * 2 * HEAD_DIM:(a + 1) * 2 * HEAD_DIM], qbd_ref[a],
                         preferred_element_type=F32) + msk2
            if with_bias:
                s2 = s2 + bias_ref[a, pl.ds(bias_row0, KEY_CHUNK), :]
            s_ref[a] = s2
            bmax_ref[a:a + 1, :] = jnp.max(s2, axis=0, keepdims=True)

    def softmax_pv(blk, s_ref, bmax_ref):
        vblk = jnp.minimum(blk, last)
        for a in range(N_PAIRS):
            m_old = m_ref[a:a + 1, :]
            m_new = jnp.maximum(m_old, bmax_ref[a:a + 1, :])
            alpha = jnp.exp2(m_old - m_new)
            p = jnp.exp2(s_ref[a] - m_new).astype(BF16)
            v_ones = jnp.concatenate([vt_ref[vblk, a * 2 * HEAD_DIM:(a + 1) * 2 * HEAD_DIM, :], ones_rows], axis=0)
            acc_ref[a] = alpha * acc_ref[a] + jnp.dot(v_ones, p, preferred_element_type=F32)
            m_ref[a:a + 1, :] = m_new

    def block_pair(with_bias, i, carry):
        logits(2 * i + 1, sb_ref, bmb_ref, with_bias)
        softmax_pv(2 * i, sa_ref, bma_ref)
        logits(2 * i + 2, sa_ref, bma_ref, with_bias)
        softmax_pv(2 * i + 1, sb_ref, bmb_ref)
        return carry

    far_pairs = jnp.minimum(jnp.maximum(3 - toff, 0) // 4, nch) // 2
    logits(0, sa_ref, bma_ref, False)
    lax.fori_loop(0, far_pairs, functools.partial(block_pair, False), 0)
    logits(2 * far_pairs, sa_ref, bma_ref, True)
    lax.fori_loop(far_pairs, (nch + 1) // 2, functools.partial(block_pair, True), 0)

    for a in range(N_PAIRS):
        for e in range(2):
            h = 2 * a + e
            num = acc_ref[a, e * HEAD_DIM:(e + 1) * HEAD_DIM, e * TQ:(e + 1) * TQ]
            den = acc_ref[a, 2 * HEAD_DIM:2 * HEAD_DIM + 1, e * TQ:(e + 1) * TQ]
            ot_ref[h * HEAD_DIM:(h + 1) * HEAD_DIM, :] = num / den
    o_ref[...] = ot_ref[...].T


def _attention(qit, wt, qt, vis, ki, k, vt, bias, sched, topk, resident):
    s_count, _, nq = qit.shape
    nk = k.shape[1]
    n_tiles = sched[0].shape[0]
    mode = dict(pipeline_mode=pl.Buffered(1)) if resident else {}
    idx_bits = max(1, int(nk - 1).bit_length())
    logit_buf = pltpu.VMEM((N_PAIRS, KEY_CHUNK, 2 * TQ), F32)
    in_specs = [
        pl.BlockSpec((None, ATT_W, TQ), lambda s, j, *_: (s, 0, j)),
        pl.BlockSpec((None, N_IDX, TQ), lambda s, j, *_: (s, 0, j)),
        pl.BlockSpec((None, ATT_W, TQ), lambda s, j, *_: (s, 0, j)),
        pl.BlockSpec((1, TQ), lambda s, j, *_: (0, j)),
        pl.BlockSpec((None, nk, IDX_DIM), lambda s, j, *_: (s, 0, 0), **mode),
        pl.BlockSpec((None, nk, ATT_W), lambda s, j, *_: (s, 0, 0), **mode),
        pl.BlockSpec((None, nk // KEY_CHUNK, ATT_W, KEY_CHUNK), lambda s, j, *_: (s, 0, 0, 0), **mode),
        pl.BlockSpec(bias.shape, lambda s, j, *_: (0, 0, 0), pipeline_mode=pl.Buffered(1)),
    ]
    grid_spec = pltpu.PrefetchScalarGridSpec(
        num_scalar_prefetch=3,
        grid=(s_count, n_tiles),
        in_specs=in_specs,
        out_specs=pl.BlockSpec((None, TQ, ATT_W), lambda s, j, *_: (s, j, 0)),
        scratch_shapes=[
            pltpu.VMEM((nk + KEY_CHUNK, TQ), F32),
            pltpu.VMEM((CAND_GROUPS * CAND_BATCH, TQ), F32),
            pltpu.VMEM((1, TQ), F32),
            pltpu.VMEM((1, TQ), F32),
            pltpu.VMEM((1, TQ), F32),
            pltpu.VMEM((2 * N_IDX, TQ), F32),
            pltpu.VMEM((IDX_DIM, N_IDX * TQ), BF16),
            pltpu.VMEM((N_PAIRS, 2 * HEAD_DIM, 2 * TQ), BF16),
            logit_buf, logit_buf,
            pltpu.VMEM((N_PAIRS, 2 * TQ), F32),
            pltpu.VMEM((N_PAIRS, 2 * TQ), F32),
            pltpu.VMEM((N_PAIRS, 2 * TQ), F32),
            pltpu.VMEM((N_PAIRS, 2 * HEAD_DIM + SUM_ROWS, 2 * TQ), F32),
            pltpu.VMEM((1, TQ), F32),
            pltpu.VMEM((KEY_CHUNK, TQ), F32),
            pltpu.VMEM((ATT_W, TQ), F32),
        ],
    )
    return pl.pallas_call(
        functools.partial(_attn_body, topk, idx_bits),
        grid_spec=grid_spec,
        out_shape=jax.ShapeDtypeStruct((s_count, nq, ATT_W), F32),
        compiler_params=pltpu.CompilerParams(dimension_semantics=("arbitrary", "arbitrary"),
                                             vmem_limit_bytes=VMEM_LIMIT),
        name="attn",
    )(*sched, qit, wt, qt, vis, ki, k, vt, bias)


def _decode_keys_body(ck_ref, cv_ref, cki_ref, k_ref, v_ref, ki_ref, ko_ref, vto_ref, kio_ref):
    past = ck_ref.shape[0]
    ts = k_ref.shape[0]
    pad = jnp.zeros((KEY_CHUNK - ts, ATT_W), F32)
    ko_ref[0:past, :] = ck_ref[...].astype(BF16)
    ko_ref[past:past + KEY_CHUNK, :] = jnp.concatenate([k_ref[...], pad], axis=0).astype(BF16)
    kio_ref[0:past, :] = cki_ref[...].astype(BF16)
    kio_ref[past:past + KEY_CHUNK, :] = jnp.concatenate([ki_ref[...], pad[:, 0:IDX_DIM]], axis=0).astype(BF16)
    for b in range(past // KEY_CHUNK):
        vto_ref[b] = cv_ref[b * KEY_CHUNK:(b + 1) * KEY_CHUNK, :].T.astype(BF16)
    vto_ref[past // KEY_CHUNK] = jnp.concatenate([v_ref[...], pad], axis=0).T.astype(BF16)


def _decode_keys(cache_k, cache_v, cache_ki, k_new, v_new, ki_new):
    ds, past, _ = cache_k.shape
    ts = k_new.shape[1]
    assert past % KEY_CHUNK == 0 and ts <= KEY_CHUNK
    nk = past + KEY_CHUNK
    per_stream = lambda rows, width: pl.BlockSpec((None, rows, width), lambda s: (s, 0, 0))
    return pl.pallas_call(
        _decode_keys_body,
        grid=(ds,),
        in_specs=[per_stream(past, ATT_W), per_stream(past, ATT_W), per_stream(past, IDX_DIM),
                  per_stream(ts, ATT_W), per_stream(ts, ATT_W), per_stream(ts, IDX_DIM)],
        out_specs=(per_stream(nk, ATT_W),
                   pl.BlockSpec((None, nk // KEY_CHUNK, ATT_W, KEY_CHUNK), lambda s: (s, 0, 0, 0)),
                   per_stream(nk, IDX_DIM)),
        out_shape=(jax.ShapeDtypeStruct((ds, nk, ATT_W), BF16),
                   jax.ShapeDtypeStruct((ds, nk // KEY_CHUNK, ATT_W, KEY_CHUNK), BF16),
                   jax.ShapeDtypeStruct((ds, nk, IDX_DIM), BF16)),
        compiler_params=pltpu.CompilerParams(dimension_semantics=("arbitrary",), vmem_limit_bytes=VMEM_LIMIT),
        name="decode_keys",
    )(cache_k, cache_v, cache_ki, k_new, v_new, ki_new)


def _s5_body(tt, us_ref, s0r_ref, s0i_ref, ar_ref, ai_ref, bbr_ref, bbi_ref, cr_ref, ci_ref, d_ref, wglu_ref,
             bglu_ref, out_ref, sr_out_ref, si_out_ref, str_ref, sti_ref, bre_ref, bim_ref, xr_ref, xi_ref):
    @pl.when(pl.program_id(1) == 0)
    def _():
        str_ref[...] = s0r_ref[...]
        sti_ref[...] = s0i_ref[...]

    u = us_ref[...]
    ub = u.astype(BF16)
    bre_ref[...] = jnp.dot(ub, bbr_ref[...], preferred_element_type=F32)
    bim_ref[...] = jnp.dot(ub, bbi_ref[...], preferred_element_type=F32)
    ar = ar_ref[...]
    ai = ai_ref[...]

    def step(t, carry):
        sr, si = carry
        nr = ar * sr - ai * si + bre_ref[pl.ds(t, 1), :]
        ni = ar * si + ai * sr + bim_ref[pl.ds(t, 1), :]
        xr_ref[pl.ds(t, 1), :] = nr
        xi_ref[pl.ds(t, 1), :] = ni
        return nr, ni

    sr, si = lax.fori_loop(0, tt, step, (str_ref[...], sti_ref[...]))
    str_ref[...] = sr
    sti_ref[...] = si
    sr_out_ref[...] = sr
    si_out_ref[...] = si
    y = (jnp.dot(xr_ref[...].astype(BF16), cr_ref[...], preferred_element_type=F32)
         - jnp.dot(xi_ref[...].astype(BF16), ci_ref[...], preferred_element_type=F32)
         + d_ref[...] * u)
    g = jax.nn.gelu(y)
    gate = jnp.dot(g.astype(BF16), wglu_ref[...], preferred_element_type=F32) + bglu_ref[...]
    out_ref[...] = g * jax.nn.sigmoid(gate)


def _s5(us, s0r, s0i, ar, ai, bbr, bbi, cr, ci, d, wglu, bglu, tt):
    s_count, t_len, _ = us.shape
    assert t_len % tt == 0
    state = pl.BlockSpec((None, 1, SSM_S), lambda s, t: (s, 0, 0))
    seq = pl.BlockSpec((None, tt, SSM_W), lambda s, t: (s, t, 0))
    return pl.pallas_call(
        functools.partial(_s5_body, tt),
        grid=(s_count, t_len // tt),
        in_specs=[seq, state, state, _const_spec((1, SSM_S)), _const_spec((1, SSM_S)),
                  _const_spec((SSM_W, SSM_S)), _const_spec((SSM_W, SSM_S)),
                  _const_spec((SSM_S, SSM_W)), _const_spec((SSM_S, SSM_W)),
                  _const_spec((1, SSM_W)), _const_spec((SSM_W, SSM_W)), _const_spec((1, SSM_W))],
        out_specs=(seq, state, state),
        out_shape=(jax.ShapeDtypeStruct((s_count, t_len, SSM_W), F32),
                   jax.ShapeDtypeStruct((s_count, 1, SSM_S), F32),
                   jax.ShapeDtypeStruct((s_count, 1, SSM_S), F32)),
        scratch_shapes=[pltpu.VMEM((1, SSM_S), F32), pltpu.VMEM((1, SSM_S), F32),
                        pltpu.VMEM((tt, SSM_S), F32), pltpu.VMEM((tt, SSM_S), F32),
                        pltpu.VMEM((tt, SSM_S), F32), pltpu.VMEM((tt, SSM_S), F32)],
        compiler_params=pltpu.CompilerParams(dimension_semantics=("arbitrary", "arbitrary"),
                                             vmem_limit_bytes=VMEM_LIMIT),
        name="s5",
    )(us, s0r, s0i, ar, ai, bbr, bbi, cr, ci, d, wglu, bglu)


def _mix_out_body(att_ref, ssm_ref, h_ref, woa_ref, wos_ref, gmix_ref, gpre_ref, wg_ref, wu_ref, wd_ref, gpost_ref,
                  o_ref):
    m = (jnp.dot(att_ref[...].astype(BF16), woa_ref[...], preferred_element_type=F32)
         + jnp.dot(ssm_ref[...].astype(BF16), wos_ref[...], preferred_element_type=F32))
    h2 = h_ref[...] + _rms(m, gmix_ref[...])
    y = _swiglu(_rms(h2, gpre_ref[...]), wg_ref, wu_ref, wd_ref)
    o_ref[...] = h2 + 0.5 * _rms(y, gpost_ref[...])


def _mix_out(att, ssm, h, woa, wos, gmix, gpre, wg, wu, wd, gpost):
    n = h.shape[0]
    row = lambda width: pl.BlockSpec((ROW_TILE, width), lambda i: (i, 0))
    return pl.pallas_call(
        _mix_out_body,
        grid=(pl.cdiv(n, ROW_TILE),),
        in_specs=[row(ATT_W), row(SSM_W), row(D_MODEL), _const_spec((ATT_W, D_MODEL)),
                  _const_spec((SSM_W, D_MODEL)), _const_spec((1, D_MODEL)), _const_spec((1, D_MODEL)),
                  _const_spec((D_MODEL, D_FF)), _const_spec((D_MODEL, D_FF)), _const_spec((D_FF, D_MODEL)),
                  _const_spec((1, D_MODEL))],
        out_specs=row(D_MODEL),
        out_shape=jax.ShapeDtypeStruct((n, D_MODEL), F32),
        compiler_params=pltpu.CompilerParams(dimension_semantics=("arbitrary",), vmem_limit_bytes=VMEM_LIMIT),
        name="mix_out_ffn2",
    )(att, ssm, h, woa, wos, gmix, gpre, wg, wu, wd, gpost)


def _rel_bucket_np(rel):
    half = NUM_BUCKETS // 2
    max_exact = half // 2
    n = np.abs(rel).astype(np.int64)
    nf = np.maximum(n, 1).astype(np.float64)
    large = max_exact + (np.log(nf / max_exact) / math.log(MAX_DISTANCE / max_exact)
                         * (half - max_exact)).astype(np.int64)
    large = np.minimum(large, half - 1)
    return np.where(rel > 0, half, 0) + np.where(n < max_exact, n, large)


def _pair_bias(rel_bias, n_rows, c0):
    length = n_rows + TQ
    rel = np.arange(length) - (TQ - 1) - LANE - c0
    table = (rel_bias - rel_bias[NUM_BUCKETS // 2 - 1][None, :]) * LOG2E
    t1d = table[_rel_bucket_np(rel)].T
    x = jnp.tile(t1d, (1, TQ))[:, :TQ * (length - 1)].reshape(N_HEADS, TQ, length - 1)
    b = jnp.transpose(x[:, :, TQ - 1:TQ - 1 + n_rows], (0, 2, 1)).reshape(N_PAIRS, 2, n_rows, TQ)
    return jnp.concatenate([b[:, 0], b[:, 1]], axis=-1).astype(F32)


def _prompt_schedule(n, n_pad):
    pos = np.arange(n_pad)
    vis = np.where(pos < N_META, N_META, N_META + CHUNK * ((pos - N_META) // CHUNK + 1))
    vis = np.where(pos < n, np.minimum(vis, n), 0).astype(np.int32)
    n_tiles = -(-n // TQ)
    vmax = vis[:n_tiles * TQ].reshape(n_tiles, TQ).max(axis=1)
    nch = -(-vmax // KEY_CHUNK)
    nfull = vis[:n_tiles * TQ].reshape(n_tiles, TQ).min(axis=1) // KEY_CHUNK
    toff = PROMPT_BIAS_C0 // LANE - np.arange(n_tiles)
    assert ((nch - 1) * (KEY_CHUNK // LANE) + toff).max() <= (PROMPT_BIAS_ROWS - KEY_CHUNK) // LANE - 1
    sched = tuple(jnp.asarray(a, I32) for a in (nch, nfull, toff))
    return sched, jnp.asarray(vis[None, :])


def _sample_schedule(past, ts):
    nk = past + ts
    sched = tuple(jnp.asarray([v], I32) for v in (-(-nk // KEY_CHUNK), 0, 0))
    vis = jnp.asarray(np.where(np.arange(TQ) < ts, nk, 0)[None, :], I32)
    return sched, vis


def _s5_params(lam_re, lam_im, log_step, b_re, b_im, c_re, c_im, d_skip):
    dt = jnp.exp(log_step)[:, None]
    mag = jnp.exp(lam_re * dt)
    ab_re, ab_im = mag * jnp.cos(lam_im * dt), mag * jnp.sin(lam_im * dt)
    nr, ni = ab_re - 1.0, ab_im
    den = lam_re * lam_re + lam_im * lam_im
    f_re, f_im = (nr * lam_re + ni * lam_im) / den, (ni * lam_re - nr * lam_im) / den
    bb_re = f_re[..., None] * b_re - f_im[..., None] * b_im
    bb_im = f_re[..., None] * b_im + f_im[..., None] * b_re
    eye = jnp.eye(SSM_GROUPS, dtype=F32)
    bd_in = lambda w: jnp.einsum('gpc,gh->gchp', w, eye).reshape(SSM_W, SSM_S).astype(BF16)
    bd_out = lambda w: jnp.einsum('gcp,gh->gphc', w, eye).reshape(SSM_S, SSM_W).astype(BF16)
    return (ab_re.reshape(1, SSM_S), ab_im.reshape(1, SSM_S), bd_in(bb_re), bd_in(bb_im),
            bd_out(c_re), bd_out(c_im), d_skip.reshape(1, SSM_W))


def kernel(x_prompt, x_sample, cache_k, cache_v, cache_kidx, state_ssm_re, state_ssm_im, meta_tokens, rel_bias,
           ffn1_g_pre, ffn1_w_gate, ffn1_w_up, ffn1_w_down, ffn1_g_post, mix_g_pre, w_in, w_out, mix_g_post,
           lam_re, lam_im, log_step, b_re, b_im, c_re, c_im, d_skip, w_glu, b_glu,
           ffn2_g_pre, ffn2_w_gate, ffn2_w_up, ffn2_w_down, ffn2_g_post):
    depth = ffn1_g_pre.shape[0]
    assert depth == 1
    bp, seq, _ = x_prompt.shape
    assert bp == 1
    ds, ts, _ = x_sample.shape
    past = cache_k.shape[2]
    n_p = N_META + seq
    n_p_pad = pl.cdiv(n_p, ROW_TILE) * ROW_TILE
    n_s = ds * ts
    assert n_s % ROW_TILE == 0 and ts <= TQ
    nk_s_pad = pl.cdiv(past + ts, KEY_CHUNK) * KEY_CHUNK
    l = 0

    row2 = lambda g: g[l].reshape(1, -1)
    w1 = (row2(ffn1_g_pre), ffn1_w_gate[l].astype(BF16), ffn1_w_up[l].astype(BF16), ffn1_w_down[l].astype(BF16),
          row2(ffn1_g_post))
    w2 = (row2(ffn2_g_pre), ffn2_w_gate[l].astype(BF16), ffn2_w_up[l].astype(BF16), ffn2_w_down[l].astype(BF16),
          row2(ffn2_g_post))
    wi = w_in[l]
    o_q, o_k, o_v, o_qi, o_ki, o_wi, o_us = np.cumsum((0, ATT_W, ATT_W, ATT_W, N_IDX * IDX_DIM, IDX_DIM, N_IDX))
    sl = lambda o, w: wi[:, o:o + w]
    wrow = jnp.concatenate([sl(o_k, ATT_W), sl(o_v, ATT_W), sl(o_us, SSM_W), sl(o_ki, IDX_DIM),
                            jnp.zeros((D_MODEL, ROW_COLS_PAD - ROW_COLS), F32)], axis=1).astype(BF16)
    wcol = jnp.concatenate([sl(o_q, ATT_W), sl(o_qi, ATT_W), sl(o_v, ATT_W), sl(o_wi, N_IDX)], axis=1).T.astype(BF16)
    woa, wos = w_out[l][:ATT_W].astype(BF16), w_out[l][ATT_W:].astype(BF16)
    s5w = _s5_params(lam_re[l], lam_im[l], log_step[l], b_re[l], b_im[l], c_re[l], c_im[l], d_skip[l])
    s5w = s5w + (w_glu[l].astype(BF16), b_glu[l].reshape(1, SSM_W))
    bias_tab = rel_bias.astype(F32)

    xp = jnp.concatenate([meta_tokens.astype(F32), x_prompt[0]], axis=0)
    hp = _ffn(xp, *w1)
    kp, vp, kip, usp, kbp, kibp, vtbp, qtp, qitp, wtp = _mix_in(hp, row2(mix_g_pre), wrow, wcol, n_p_pad)
    sched_p, vis_p = _prompt_schedule(n_p, n_p_pad)
    att_p = _attention(qitp[None], wtp[None], qtp[None], vis_p, kibp[None], kbp[None], vtbp[None],
                       _pair_bias(bias_tab, PROMPT_BIAS_ROWS, PROMPT_BIAS_C0), sched_p,
                       min(TOPK_MAX, seq // 4), True)[0]
    zero_state = jnp.zeros((1, 1, SSM_S), F32)
    tt_p = max(t for t in range(8, 513, 8) if n_p % t == 0)
    ssm_p, srp, sip = _s5(usp[None], zero_state, zero_state, *s5w, tt_p)
    yp = _mix_out(att_p, ssm_p[0], hp, woa, wos, row2(mix_g_post), *w2)

    hs = _ffn(x_sample.reshape(n_s, D_MODEL), *w1)
    ks, vs, kis, uss, _, _, _, qts, qits, wts = _mix_in(hs, row2(mix_g_pre), wrow, wcol, n_s)
    k_all, vt_all, ki_all = _decode_keys(
        cache_k[l].reshape(ds, past, ATT_W), cache_v[l].reshape(ds, past, ATT_W), cache_kidx[l],
        ks.reshape(ds, ts, ATT_W), vs.reshape(ds, ts, ATT_W), kis.reshape(ds, ts, IDX_DIM))
    assert k_all.shape[1] == nk_s_pad
    lanes = lambda a: jnp.pad(jnp.transpose(a.reshape(a.shape[0], ds, ts), (1, 0, 2)), ((0, 0), (0, 0), (0, TQ - ts)))
    sched_s, vis_s = _sample_schedule(past, ts)
    att_s = _attention(lanes(qits), lanes(wts), lanes(qts), vis_s, ki_all, k_all, vt_all,
                       _pair_bias(bias_tab, LANE + nk_s_pad, past), sched_s,
                       min(TOPK_MAX, (past + ts) // 4), False)
    att_s = att_s[:, :ts].reshape(n_s, ATT_W)
    ssm_s, srs, sis = _s5(uss.reshape(ds, ts, SSM_W), state_ssm_re[l].reshape(ds, 1, SSM_S),
                          state_ssm_im[l].reshape(ds, 1, SSM_S), *s5w, ts)
    ys = _mix_out(att_s, ssm_s.reshape(n_s, SSM_W), hs, woa, wos, row2(mix_g_post), *w2)

    heads = lambda a, b, t: a.reshape(1, b, t, N_HEADS, HEAD_DIM)
    state = lambda a, b: a.reshape(1, b, SSM_GROUPS, SSM_P)
    return (yp[N_META:][None], ys.reshape(ds, ts, D_MODEL),
            heads(kp, 1, n_p), heads(vp, 1, n_p), kip.reshape(1, 1, n_p, IDX_DIM), state(srp, 1), state(sip, 1),
            heads(ks, ds, ts), heads(vs, ds, ts), kis.reshape(1, ds, ts, IDX_DIM), state(srs, ds), state(sis, ds))
```

```python
import functools
import math

import numpy as np
import jax
import jax.numpy as jnp
from jax import lax
from jax.experimental import pallas as pl
from jax.experimental.pallas import tpu as pltpu

F32, BF16, I32 = jnp.float32, jnp.bfloat16, jnp.int32

D_MODEL = 1024
CHUNK = 64
N_META = 16
N_HEADS = 8
HEAD_DIM = 64
ATT_W = N_HEADS * HEAD_DIM
N_IDX = 8
IDX_DIM = 64
TOPK_MAX = 256
SSM_GROUPS = 32
SSM_GC = 16
SSM_W = SSM_GROUPS * SSM_GC
SSM_P = 64
SSM_S = SSM_GROUPS * SSM_P
D_FF = 2816
NUM_BUCKETS = 32
MAX_DISTANCE = 128
EPS = 1e-6

LANE = 128
VMEM_LIMIT = 62 * 1024 * 1024
ROW_TILE = 512
TQ = LANE
KEY_CHUNK = 512
PROMPT_BIAS_C0 = 512
PROMPT_BIAS_ROWS = 1280
NEG = -1e30
LOG2E = math.log2(math.e)
F32_MAX = float(np.finfo(np.float32).max)
N_PAIRS = N_HEADS // 2
SUM_ROWS = 16
CAND_DEPTH = 16
CAND_BATCH = 8 * CAND_DEPTH
CAND_GROUPS = 8
SCORE_ROUND = 8
TIE_SCAN_MAX = 8


def _batcher_network(n):
    pairs, p = [], 1
    while p < n:
        k = p
        while k >= 1:
            for j in range(k % p, n - k, 2 * k):
                for i in range(min(k, n - j - k)):
                    if (i + j) // (2 * p) == (i + j + k) // (2 * p):
                        pairs.append((i + j, i + j + k))
            k //= 2
        p *= 2
    return pairs


_SORT_NETWORK = _batcher_network(CAND_DEPTH)


def _rms(x, g):
    return x * lax.rsqrt(jnp.mean(x * x, axis=-1, keepdims=True) + EPS) * g


def _swiglu(xn, wg_ref, wu_ref, wd_ref):
    xb = xn.astype(BF16)
    a = jnp.dot(xb, wg_ref[...], preferred_element_type=F32)
    b = jnp.dot(xb, wu_ref[...], preferred_element_type=F32)
    hidden = (a * jax.nn.sigmoid(a) * b).astype(BF16)
    return jnp.dot(hidden, wd_ref[...], preferred_element_type=F32)


def _const_spec(shape):
    nd = len(shape)
    return pl.BlockSpec(shape, lambda *_: (0,) * nd, pipeline_mode=pl.Buffered(1))


def _ffn_body(x_ref, gpre_ref, wg_ref, wu_ref, wd_ref, gpost_ref, o_ref):
    x = x_ref[...]
    y = _swiglu(_rms(x, gpre_ref[...]), wg_ref, wu_ref, wd_ref)
    o_ref[...] = x + 0.5 * _rms(y, gpost_ref[...])


def _ffn(x, gpre, wg, wu, wd, gpost):
    n = x.shape[0]
    row = pl.BlockSpec((ROW_TILE, D_MODEL), lambda i: (i, 0))
    return pl.pallas_call(
        _ffn_body,
        grid=(pl.cdiv(n, ROW_TILE),),
        in_specs=[row, _const_spec((1, D_MODEL)), _const_spec((D_MODEL, D_FF)), _const_spec((D_MODEL, D_FF)),
                  _const_spec((D_FF, D_MODEL)), _const_spec((1, D_MODEL))],
        out_specs=row,
        out_shape=jax.ShapeDtypeStruct((n, D_MODEL), F32),
        compiler_params=pltpu.CompilerParams(dimension_semantics=("arbitrary",), vmem_limit_bytes=VMEM_LIMIT),
        name="ffn1",
    )(x, gpre, wg, wu, wd, gpost)


ROW_COLS = 3 * ATT_W + IDX_DIM
ROW_COLS_PAD = 13 * LANE
COL_ROWS = 3 * ATT_W + N_IDX


def _mix_in_body(n_valid, h_ref, g_ref, wrow_ref, wcol_ref,
                 k_ref, v_ref, ki_ref, us_ref, kb_ref, kib_ref, vtb_ref, qt_ref, qit_ref, wt_ref):
    i = pl.program_id(0)
    u = _rms(h_ref[...], g_ref[...]).astype(BF16)
    z = jnp.dot(u, wrow_ref[...], preferred_element_type=F32)
    zt = lax.dot_general(wcol_ref[...], u, (((1,), (1,)), ((), ())), preferred_element_type=F32)
    k = z[:, 0:ATT_W]
    v = z[:, ATT_W:2 * ATT_W]
    ki = z[:, 3 * ATT_W:3 * ATT_W + IDX_DIM]
    k_ref[...] = k
    v_ref[...] = v
    us_ref[...] = z[:, 2 * ATT_W:3 * ATT_W]
    ki_ref[...] = ki
    row_ok = (i * ROW_TILE + lax.broadcasted_iota(I32, (ROW_TILE, 1), 0)) < n_valid
    col_ok = (i * ROW_TILE + lax.broadcasted_iota(I32, (1, ROW_TILE), 1)) < n_valid
    kb_ref[...] = jnp.where(row_ok, k, 0.0).astype(BF16)
    kib_ref[...] = jnp.where(row_ok, ki, 0.0).astype(BF16)
    zt = jnp.where(col_ok, zt, 0.0)
    qt_ref[...] = (zt[0:ATT_W] * (HEAD_DIM ** -0.5 * LOG2E)).astype(BF16)
    vtb_ref[...] = zt[2 * ATT_W:3 * ATT_W].astype(BF16)
    wt = zt[3 * ATT_W:3 * ATT_W + N_IDX] * ((IDX_DIM ** -0.5) * (N_IDX ** -0.5))
    wt_ref[...] = wt
    for h in range(N_IDX):
        qi_h = zt[ATT_W + h * IDX_DIM:ATT_W + (h + 1) * IDX_DIM]
        qit_ref[h * IDX_DIM:(h + 1) * IDX_DIM, :] = (qi_h * wt[h:h + 1, :]).astype(BF16)


def _mix_in(h, g, wrow, wcol, n_pad):
    n = h.shape[0]
    steps = n_pad // ROW_TILE
    assert steps == pl.cdiv(n, ROW_TILE)

    def row(width):
        return pl.BlockSpec((ROW_TILE, width), lambda i: (i, 0))

    col = lambda rows: pl.BlockSpec((rows, ROW_TILE), lambda i: (0, i))
    out_shape = (
        jax.ShapeDtypeStruct((n, ATT_W), F32), jax.ShapeDtypeStruct((n, ATT_W), F32),
        jax.ShapeDtypeStruct((n, IDX_DIM), F32), jax.ShapeDtypeStruct((n, SSM_W), F32),
        jax.ShapeDtypeStruct((n_pad, ATT_W), BF16), jax.ShapeDtypeStruct((n_pad, IDX_DIM), BF16),
        jax.ShapeDtypeStruct((n_pad // ROW_TILE, ATT_W, ROW_TILE), BF16),
        jax.ShapeDtypeStruct((ATT_W, n_pad), BF16), jax.ShapeDtypeStruct((ATT_W, n_pad), BF16),
        jax.ShapeDtypeStruct((N_IDX, n_pad), F32),
    )
    out_specs = (
        row(ATT_W), row(ATT_W), row(IDX_DIM), row(SSM_W), row(ATT_W), row(IDX_DIM),
        pl.BlockSpec((None, ATT_W, ROW_TILE), lambda i: (i, 0, 0)),
        col(ATT_W), col(ATT_W), col(N_IDX),
    )
    return pl.pallas_call(
        functools.partial(_mix_in_body, n),
        grid=(steps,),
        in_specs=[row(D_MODEL), _const_spec((1, D_MODEL)), _const_spec((D_MODEL, ROW_COLS_PAD)),
                  _const_spec((COL_ROWS, D_MODEL))],
        out_specs=out_specs,
        out_shape=out_shape,
        compiler_params=pltpu.CompilerParams(dimension_semantics=("arbitrary",), vmem_limit_bytes=VMEM_LIMIT),
        name="mix_in",
    )(h, g, wrow, wcol)


def _ordered_bits_to_f32(u):
    k = u ^ jnp.int32(-2 ** 31)
    b = k ^ ((k >> 31) & jnp.int32(0x7FFFFFFF))
    return lax.bitcast_convert_type(b, F32)


def _attn_body(topk, idx_bits, has_far,
               nch_ref, nfull_ref, toff_ref,
               qit_ref, wt_ref, qt_ref, vis_ref, ki_ref, k_ref, vt_ref, bias_ref,
               o_ref,
               st_ref, cand_ref, thr_ref, cgt_ref, cge_ref, clip_ref, qicat_ref, qbd_ref, sa_ref, sb_ref,
               bma_ref, bmb_ref,
               m_ref, acc_ref, cut_ref, rowf_ref, ot_ref):
    j = pl.program_id(1)
    nch = nch_ref[j]
    toff = toff_ref[j]
    kf = float(topk)
    bias_tile_max = (bias_ref.shape[1] - KEY_CHUNK) // LANE - 1

    for h in range(N_IDX):
        qicat_ref[:, h * TQ:(h + 1) * TQ] = qit_ref[h * IDX_DIM:(h + 1) * IDX_DIM, :]
    qbd_ref[...] = jnp.zeros(qbd_ref.shape, BF16)
    for a in range(N_PAIRS):
        qbd_ref[a, 0:HEAD_DIM, 0:TQ] = qt_ref[2 * a * HEAD_DIM:(2 * a + 1) * HEAD_DIM, :]
        qbd_ref[a, HEAD_DIM:2 * HEAD_DIM, TQ:2 * TQ] = qt_ref[(2 * a + 1) * HEAD_DIM:(2 * a + 2) * HEAD_DIM, :]

    def chunk_start(c):
        return pl.multiple_of(c * KEY_CHUNK, KEY_CHUNK)

    def chunk_rows(r0):
        return r0 + lax.broadcasted_iota(I32, (KEY_CHUNK, TQ), 0)

    vis = vis_ref[...]
    cand_ref[...] = jnp.full(cand_ref.shape, -jnp.inf, F32)
    w_pos = wt_ref[...] >= 0.0
    clip_ref[0:N_IDX, :] = jnp.where(w_pos, 0.0, -jnp.inf)
    clip_ref[N_IDX:2 * N_IDX, :] = jnp.where(w_pos, jnp.inf, 0.0)

    last_batch_row = ki_ref.shape[0] - CAND_BATCH

    def score_round(masked, i, carry):
        for g in range(SCORE_ROUND):
            r0 = pl.multiple_of((i * SCORE_ROUND + g) * CAND_BATCH, CAND_BATCH)
            rel = jnp.dot(ki_ref[pl.ds(jnp.minimum(r0, last_batch_row), CAND_BATCH), :], qicat_ref[...],
                          preferred_element_type=F32)
            score = jnp.zeros((CAND_BATCH, TQ), F32)
            for h in range(N_IDX):
                score = score + jnp.clip(rel[:, h * TQ:(h + 1) * TQ], clip_ref[h:h + 1, :],
                                         clip_ref[N_IDX + h:N_IDX + h + 1, :])
            if masked:
                score = jnp.where(r0 + lax.broadcasted_iota(I32, (CAND_BATCH, TQ), 0) < vis, score, -jnp.inf)
            st_ref[pl.ds(r0, CAND_BATCH), :] = score
        return carry

    round_chunks = SCORE_ROUND * CAND_BATCH // KEY_CHUNK
    full_rounds = nfull_ref[j] // round_chunks
    lax.fori_loop(0, full_rounds, functools.partial(score_round, False), 0)
    lax.fori_loop(full_rounds, (nch + round_chunks - 1) // round_chunks, functools.partial(score_round, True), 0)

    def merge_batch(b, carry):
        r0 = pl.multiple_of(b * CAND_BATCH, CAND_BATCH)
        x = [st_ref[pl.ds(r0 + 8 * r, 8), :] for r in range(CAND_DEPTH)]
        for lo, hi in _SORT_NETWORK:
            x[lo], x[hi] = jnp.maximum(x[lo], x[hi]), jnp.minimum(x[lo], x[hi])
        g0 = pl.multiple_of(lax.rem(b, CAND_GROUPS) * CAND_BATCH, CAND_BATCH)
        t = [jnp.maximum(cand_ref[pl.ds(g0 + 8 * r, 8), :], x[CAND_DEPTH - 1 - r]) for r in range(CAND_DEPTH)]
        d = CAND_DEPTH // 2
        while d >= 1:
            for r in range(CAND_DEPTH):
                if r & d == 0:
                    t[r], t[r + d] = jnp.maximum(t[r], t[r + d]), jnp.minimum(t[r], t[r + d])
            d //= 2
        for r in range(CAND_DEPTH):
            cand_ref[pl.ds(g0 + 8 * r, 8), :] = t[r]
        return carry

    lax.fori_loop(0, nch * (KEY_CHUNK // CAND_BATCH), merge_batch, 0)

    def reduce_rows(ref, n_chunks, fn, combine, init):
        reducer = jnp.sum if combine is jnp.add else jnp.max

        def body(c, acc):
            r0 = chunk_start(c)
            val = fn(ref[pl.ds(r0, KEY_CHUNK), :], r0)
            part = reducer(val.reshape(8, KEY_CHUNK // 64, 8, TQ), axis=1)
            return combine(acc, reducer(part, axis=0))

        acc = lax.fori_loop(0, n_chunks, body, jnp.full((8, TQ), init, F32))
        return reducer(acc, axis=0, keepdims=True)

    def count(ref, n_chunks, indicator):
        return reduce_rows(ref, n_chunks, indicator, jnp.add, 0.0)

    def select(ref, n_chunks):
        def value_bit(i, res):
            trial = res | lax.shift_left(jnp.int32(1), 31 - i)
            thr_t = _ordered_bits_to_f32(trial)
            cnt = count(ref, n_chunks, lambda blk, r0: jnp.where(blk >= thr_t, 1.0, 0.0))
            return jnp.where(cnt >= kf, trial, res)

        res = lax.fori_loop(0, 32, value_bit, jnp.zeros((1, TQ), I32))
        res = jnp.where((res >> 23) == 0, jnp.int32(0x00800000), res)
        lo = _ordered_bits_to_f32(res)
        above = _ordered_bits_to_f32(res + 1)
        c_above = count(ref, n_chunks, lambda blk, r0: jnp.where(blk >= above, 1.0, 0.0))
        c_lo = count(ref, n_chunks, lambda blk, r0: jnp.where(blk >= lo, 1.0, 0.0))
        thr_ref[...] = lo
        cgt_ref[...] = c_above
        cge_ref[...] = c_lo

        def walking(state):
            return jnp.max(state[0]) > 0.0

        def walk(state):
            active, hi, c_hi = state
            v = reduce_rows(ref, n_chunks, lambda blk, r0: jnp.where(blk < hi, blk, -jnp.inf), jnp.maximum,
                            -jnp.inf)
            c_v = count(ref, n_chunks, lambda blk, r0: jnp.where(blk >= v, 1.0, 0.0))
            done = jnp.logical_and(active > 0.0, c_v >= kf)
            thr_ref[...] = jnp.where(done, v, thr_ref[...])
            cgt_ref[...] = jnp.where(done, c_hi, cgt_ref[...])
            cge_ref[...] = jnp.where(done, c_v, cge_ref[...])
            return jnp.where(done, 0.0, active), v, c_v

        surplus = jnp.where(jnp.logical_and(c_lo > kf, c_above < kf), 1.0, 0.0)
        lax.while_loop(walking, walk, (surplus, above, c_above))
        return lo

    cand_lo = select(cand_ref, cand_ref.shape[0] // KEY_CHUNK)
    kept_min = cand_ref[CAND_BATCH - 8:CAND_BATCH, :]
    for g in range(1, CAND_GROUPS):
        kept_min = jnp.maximum(kept_min, cand_ref[(g + 1) * CAND_BATCH - 8:(g + 1) * CAND_BATCH, :])
    unsafe = jnp.max(kept_min, axis=0, keepdims=True) >= cand_lo

    @pl.when(jnp.max(jnp.where(unsafe, 1.0, 0.0)) > 0.0)
    def _():
        select(st_ref, nch)

    thr = thr_ref[...]

    need = kf - cgt_ref[...]
    excess = cge_ref[...] - kf
    cut_ref[...] = jnp.full((1, TQ), F32_MAX, F32)
    rowf_ref[...] = lax.broadcasted_iota(I32, (KEY_CHUNK, TQ), 0).astype(F32)

    def chunk_rows_f(r0):
        return rowf_ref[...] + r0.astype(F32)

    tie_need = jnp.max(jnp.where(excess > 0.0, need, 0.0)).astype(I32)

    @pl.when(jnp.logical_and(tie_need > 0, tie_need <= TIE_SCAN_MAX))
    def _():
        def next_tied(cut_f):
            def body(c, acc):
                r0 = chunk_start(c)
                tied_row = jnp.where(st_ref[pl.ds(r0, KEY_CHUNK), :] == thr, chunk_rows_f(r0), F32_MAX)
                v = jnp.where(tied_row > cut_f, tied_row, F32_MAX)
                part = jnp.min(v.reshape(8, KEY_CHUNK // 64, 8, TQ), axis=1)
                return jnp.minimum(acc, jnp.min(part, axis=0))

            acc = lax.fori_loop(0, nch, body, jnp.full((8, TQ), F32_MAX, F32))
            return jnp.min(acc, axis=0, keepdims=True)

        def take(t, cut_f):
            return jnp.where(t.astype(F32) < need, next_tied(cut_f), cut_f)

        cut_f = lax.fori_loop(0, tie_need, take, jnp.full((1, TQ), -1.0, F32))
        cut_ref[...] = jnp.where(excess > 0.0, cut_f, F32_MAX)

    @pl.when(tie_need > TIE_SCAN_MAX)
    def _():
        def index_bit(i, cut):
            trial = cut | lax.shift_left(jnp.int32(1), idx_bits - 1 - i)
            cnt = count(st_ref, nch, lambda blk, r0: jnp.where(
                blk == thr, jnp.where(chunk_rows(r0) < trial, 1.0, 0.0), 0.0))
            return jnp.where(cnt < need, trial, cut)

        cut = lax.fori_loop(0, idx_bits, index_bit, jnp.zeros((1, TQ), I32))
        cut_ref[...] = jnp.where(excess > 0.0, cut.astype(F32), F32_MAX)

    cut_f = cut_ref[...]

    st_ref[pl.ds(chunk_start(nch), KEY_CHUNK), :] = jnp.full((KEY_CHUNK, TQ), -jnp.inf, F32)

    m_ref[...] = jnp.full(m_ref.shape, NEG, F32)
    acc_ref[...] = jnp.zeros(acc_ref.shape, F32)
    last = nch - 1
    ones_rows = jnp.ones((SUM_ROWS, KEY_CHUNK), BF16)

    def logits(blk, s_ref, bmax_ref, with_bias):
        kblk = k_ref[pl.ds(chunk_start(jnp.minimum(blk, last)), KEY_CHUNK), :]
        r0 = chunk_start(jnp.minimum(blk, nch))
        score = st_ref[pl.ds(r0, KEY_CHUNK), :]
        tie = jnp.where(chunk_rows_f(r0) <= cut_f, 0.0, NEG)
        msk = jnp.where(score > thr, 0.0, jnp.where(score == thr, tie, NEG))
        msk2 = jnp.concatenate([msk, msk], axis=1)
        bias_row0 = pl.multiple_of((jnp.clip(4 * blk + toff, -1, bias_tile_max) + 1) * LANE, LANE)
        for a in range(N_PAIRS):
            s2 = jnp.dot(kblk[:, a * 2 * HEAD_DIM:(a + 1) * 2 * HEAD_DIM], qbd_ref[a],
                         preferred_element_type=F32) + msk2
            if with_bias:
                s2 = s2 + bias_ref[a, pl.ds(bias_row0, KEY_CHUNK), :]
            s_ref[a] = s2
            bmax_ref[a:a + 1, :] = jnp.max(s2, axis=0, keepdims=True)

    def softmax_pv(blk, s_ref, bmax_ref):
        vblk = jnp.minimum(blk, last)
        for a in range(N_PAIRS):
            m_old = m_ref[a:a + 1, :]
            m_new = jnp.maximum(m_old, bmax_ref[a:a + 1, :])
            alpha = jnp.exp2(m_old - m_new)
            p = jnp.exp2(s_ref[a] - m_new).astype(BF16)
            v_ones = jnp.concatenate([vt_ref[vblk, a * 2 * HEAD_DIM:(a + 1) * 2 * HEAD_DIM, :], ones_rows], axis=0)
            acc_ref[a] = alpha * acc_ref[a] + jnp.dot(v_ones, p, preferred_element_type=F32)
            m_ref[a:a + 1, :] = m_new

    def block_pair(with_bias, i, carry):
        logits(2 * i + 1, sb_ref, bmb_ref, with_bias)
        softmax_pv(2 * i, sa_ref, bma_ref)
        logits(2 * i + 2, sa_ref, bma_ref, with_bias)
        softmax_pv(2 * i + 1, sb_ref, bmb_ref)
        return carry

    if has_far:
        far_pairs = jnp.minimum(jnp.maximum(3 - toff, 0) // 4, nch) // 2
        logits(0, sa_ref, bma_ref, False)
        lax.fori_loop(0, far_pairs, functools.partial(block_pair, False), 0)
    else:
        far_pairs = 0
    logits(2 * far_pairs, sa_ref, bma_ref, True)
    lax.fori_loop(far_pairs, (nch + 1) // 2, functools.partial(block_pair, True), 0)

    for a in range(N_PAIRS):
        for e in range(2):
            h = 2 * a + e
            num = acc_ref[a, e * HEAD_DIM:(e + 1) * HEAD_DIM, e * TQ:(e + 1) * TQ]
            den = acc_ref[a, 2 * HEAD_DIM:2 * HEAD_DIM + 1, e * TQ:(e + 1) * TQ]
            ot_ref[h * HEAD_DIM:(h + 1) * HEAD_DIM, :] = num / den
    o_ref[...] = ot_ref[...].T


def _attention(qit, wt, qt, vis, ki, k, vt, bias, sched, topk, resident, has_far):
    s_count, _, nq = qit.shape
    nk = k.shape[1]
    n_tiles = sched[0].shape[0]
    mode = dict(pipeline_mode=pl.Buffered(1)) if resident else {}
    idx_bits = max(1, int(nk - 1).bit_length())
    logit_buf = pltpu.VMEM((N_PAIRS, KEY_CHUNK, 2 * TQ), F32)
    in_specs = [
        pl.BlockSpec((None, ATT_W, TQ), lambda s, j, *_: (s, 0, j)),
        pl.BlockSpec((None, N_IDX, TQ), lambda s, j, *_: (s, 0, j)),
        pl.BlockSpec((None, ATT_W, TQ), lambda s, j, *_: (s, 0, j)),
        pl.BlockSpec((1, TQ), lambda s, j, *_: (0, j)),
        pl.BlockSpec((None, nk, IDX_DIM), lambda s, j, *_: (s, 0, 0), **mode),
        pl.BlockSpec((None, nk, ATT_W), lambda s, j, *_: (s, 0, 0), **mode),
        pl.BlockSpec((None, nk // KEY_CHUNK, ATT_W, KEY_CHUNK), lambda s, j, *_: (s, 0, 0, 0), **mode),
        pl.BlockSpec(bias.shape, lambda s, j, *_: (0, 0, 0), pipeline_mode=pl.Buffered(1)),
    ]
    grid_spec = pltpu.PrefetchScalarGridSpec(
        num_scalar_prefetch=3,
        grid=(s_count, n_tiles),
        in_specs=in_specs,
        out_specs=pl.BlockSpec((None, TQ, ATT_W), lambda s, j, *_: (s, j, 0)),
        scratch_shapes=[
            pltpu.VMEM((nk + KEY_CHUNK, TQ), F32),
            pltpu.VMEM((CAND_GROUPS * CAND_BATCH, TQ), F32),
            pltpu.VMEM((1, TQ), F32),
            pltpu.VMEM((1, TQ), F32),
            pltpu.VMEM((1, TQ), F32),
            pltpu.VMEM((2 * N_IDX, TQ), F32),
            pltpu.VMEM((IDX_DIM, N_IDX * TQ), BF16),
            pltpu.VMEM((N_PAIRS, 2 * HEAD_DIM, 2 * TQ), BF16),
            logit_buf, logit_buf,
            pltpu.VMEM((N_PAIRS, 2 * TQ), F32),
            pltpu.VMEM((N_PAIRS, 2 * TQ), F32),
            pltpu.VMEM((N_PAIRS, 2 * TQ), F32),
            pltpu.VMEM((N_PAIRS, 2 * HEAD_DIM + SUM_ROWS, 2 * TQ), F32),
            pltpu.VMEM((1, TQ), F32),
            pltpu.VMEM((KEY_CHUNK, TQ), F32),
            pltpu.VMEM((ATT_W, TQ), F32),
        ],
    )
    return pl.pallas_call(
        functools.partial(_attn_body, topk, idx_bits, has_far),
        grid_spec=grid_spec,
        out_shape=jax.ShapeDtypeStruct((s_count, nq, ATT_W), F32),
        compiler_params=pltpu.CompilerParams(dimension_semantics=("arbitrary", "arbitrary"),
                                             vmem_limit_bytes=VMEM_LIMIT),
        name="attn",
    )(*sched, qit, wt, qt, vis, ki, k, vt, bias)


def _decode_keys_body(ck_ref, cv_ref, cki_ref, k_ref, v_ref, ki_ref, ko_ref, vto_ref, kio_ref):
    past = ck_ref.shape[0]
    ts = k_ref.shape[0]
    pad = jnp.zeros((KEY_CHUNK - ts, ATT_W), F32)
    ko_ref[0:past, :] = ck_ref[...].astype(BF16)
    ko_ref[past:past + KEY_CHUNK, :] = jnp.concatenate([k_ref[...], pad], axis=0).astype(BF16)
    kio_ref[0:past, :] = cki_ref[...].astype(BF16)
    kio_ref[past:past + KEY_CHUNK, :] = jnp.concatenate([ki_ref[...], pad[:, 0:IDX_DIM]], axis=0).astype(BF16)
    for b in range(past // KEY_CHUNK):
        vto_ref[b] = cv_ref[b * KEY_CHUNK:(b + 1) * KEY_CHUNK, :].T.astype(BF16)
    vto_ref[past // KEY_CHUNK] = jnp.concatenate([v_ref[...], pad], axis=0).T.astype(BF16)


def _decode_keys(cache_k, cache_v, cache_ki, k_new, v_new, ki_new):
    ds, past, _ = cache_k.shape
    ts = k_new.shape[1]
    assert past % KEY_CHUNK == 0 and ts <= KEY_CHUNK
    nk = past + KEY_CHUNK
    per_stream = lambda rows, width: pl.BlockSpec((None, rows, width), lambda s: (s, 0, 0))
    return pl.pallas_call(
        _decode_keys_body,
        grid=(ds,),
        in_specs=[per_stream(past, ATT_W), per_stream(past, ATT_W), per_stream(past, IDX_DIM),
                  per_stream(ts, ATT_W), per_stream(ts, ATT_W), per_stream(ts, IDX_DIM)],
        out_specs=(per_stream(nk, ATT_W),
                   pl.BlockSpec((None, nk // KEY_CHUNK, ATT_W, KEY_CHUNK), lambda s: (s, 0, 0, 0)),
                   per_stream(nk, IDX_DIM)),
        out_shape=(jax.ShapeDtypeStruct((ds, nk, ATT_W), BF16),
                   jax.ShapeDtypeStruct((ds, nk // KEY_CHUNK, ATT_W, KEY_CHUNK), BF16),
                   jax.ShapeDtypeStruct((ds, nk, IDX_DIM), BF16)),
        compiler_params=pltpu.CompilerParams(dimension_semantics=("arbitrary",), vmem_limit_bytes=VMEM_LIMIT),
        name="decode_keys",
    )(cache_k, cache_v, cache_ki, k_new, v_new, ki_new)


def _s5_body(tt, us_ref, s0r_ref, s0i_ref, ar_ref, ai_ref, bbr_ref, bbi_ref, cr_ref, ci_ref, d_ref, wglu_ref,
             bglu_ref, out_ref, sr_out_ref, si_out_ref, str_ref, sti_ref, bre_ref, bim_ref, xr_ref, xi_ref):
    @pl.when(pl.program_id(1) == 0)
    def _():
        str_ref[...] = s0r_ref[...]
        sti_ref[...] = s0i_ref[...]

    u = us_ref[...]
    ub = u.astype(BF16)
    bre_ref[...] = jnp.dot(ub, bbr_ref[...], preferred_element_type=F32)
    bim_ref[...] = jnp.dot(ub, bbi_ref[...], preferred_element_type=F32)
    ar = ar_ref[...]
    ai = ai_ref[...]

    def step(t, carry):
        sr, si = carry
        nr = ar * sr - ai * si + bre_ref[pl.ds(t, 1), :]
        ni = ar * si + ai * sr + bim_ref[pl.ds(t, 1), :]
        xr_ref[pl.ds(t, 1), :] = nr
        xi_ref[pl.ds(t, 1), :] = ni
        return nr, ni

    sr, si = lax.fori_loop(0, tt, step, (str_ref[...], sti_ref[...]))
    str_ref[...] = sr
    sti_ref[...] = si
    sr_out_ref[...] = sr
    si_out_ref[...] = si
    y = (jnp.dot(xr_ref[...].astype(BF16), cr_ref[...], preferred_element_type=F32)
         - jnp.dot(xi_ref[...].astype(BF16), ci_ref[...], preferred_element_type=F32)
         + d_ref[...] * u)
    g = jax.nn.gelu(y)
    gate = jnp.dot(g.astype(BF16), wglu_ref[...], preferred_element_type=F32) + bglu_ref[...]
    out_ref[...] = g * jax.nn.sigmoid(gate)


def _s5(us, s0r, s0i, ar, ai, bbr, bbi, cr, ci, d, wglu, bglu, tt):
    s_count, t_len, _ = us.shape
    assert t_len % tt == 0
    state = pl.BlockSpec((None, 1, SSM_S), lambda s, t: (s, 0, 0))
    seq = pl.BlockSpec((None, tt, SSM_W), lambda s, t: (s, t, 0))
    return pl.pallas_call(
        functools.partial(_s5_body, tt),
        grid=(s_count, t_len // tt),
        in_specs=[seq, state, state, _const_spec((1, SSM_S)), _const_spec((1, SSM_S)),
                  _const_spec((SSM_W, SSM_S)), _const_spec((SSM_W, SSM_S)),
                  _const_spec((SSM_S, SSM_W)), _const_spec((SSM_S, SSM_W)),
                  _const_spec((1, SSM_W)), _const_spec((SSM_W, SSM_W)), _const_spec((1, SSM_W))],
        out_specs=(seq, state, state),
        out_shape=(jax.ShapeDtypeStruct((s_count, t_len, SSM_W), F32),
                   jax.ShapeDtypeStruct((s_count, 1, SSM_S), F32),
                   jax.ShapeDtypeStruct((s_count, 1, SSM_S), F32)),
        scratch_shapes=[pltpu.VMEM((1, SSM_S), F32), pltpu.VMEM((1, SSM_S), F32),
                        pltpu.VMEM((tt, SSM_S), F32), pltpu.VMEM((tt, SSM_S), F32),
                        pltpu.VMEM((tt, SSM_S), F32), pltpu.VMEM((tt, SSM_S), F32)],
        compiler_params=pltpu.CompilerParams(dimension_semantics=("arbitrary", "arbitrary"),
                                             vmem_limit_bytes=VMEM_LIMIT),
        name="s5",
    )(us, s0r, s0i, ar, ai, bbr, bbi, cr, ci, d, wglu, bglu)


def _mix_out_body(att_ref, ssm_ref, h_ref, woa_ref, wos_ref, gmix_ref, gpre_ref, wg_ref, wu_ref, wd_ref, gpost_ref,
                  o_ref):
    m = (jnp.dot(att_ref[...].astype(BF16), woa_ref[...], preferred_element_type=F32)
         + jnp.dot(ssm_ref[...].astype(BF16), wos_ref[...], preferred_element_type=F32))
    h2 = h_ref[...] + _rms(m, gmix_ref[...])
    y = _swiglu(_rms(h2, gpre_ref[...]), wg_ref, wu_ref, wd_ref)
    o_ref[...] = h2 + 0.5 * _rms(y, gpost_ref[...])


def _mix_out(att, ssm, h, woa, wos, gmix, gpre, wg, wu, wd, gpost):
    n = h.shape[0]
    row = lambda width: pl.BlockSpec((ROW_TILE, width), lambda i: (i, 0))
    return pl.pallas_call(
        _mix_out_body,
        grid=(pl.cdiv(n, ROW_TILE),),
        in_specs=[row(ATT_W), row(SSM_W), row(D_MODEL), _const_spec((ATT_W, D_MODEL)),
                  _const_spec((SSM_W, D_MODEL)), _const_spec((1, D_MODEL)), _const_spec((1, D_MODEL)),
                  _const_spec((D_MODEL, D_FF)), _const_spec((D_MODEL, D_FF)), _const_spec((D_FF, D_MODEL)),
                  _const_spec((1, D_MODEL))],
        out_specs=row(D_MODEL),
        out_shape=jax.ShapeDtypeStruct((n, D_MODEL), F32),
        compiler_params=pltpu.CompilerParams(dimension_semantics=("arbitrary",), vmem_limit_bytes=VMEM_LIMIT),
        name="mix_out_ffn2",
    )(att, ssm, h, woa, wos, gmix, gpre, wg, wu, wd, gpost)


def _rel_bucket_np(rel):
    half = NUM_BUCKETS // 2
    max_exact = half // 2
    n = np.abs(rel).astype(np.int64)
    nf = np.maximum(n, 1).astype(np.float64)
    large = max_exact + (np.log(nf / max_exact) / math.log(MAX_DISTANCE / max_exact)
                         * (half - max_exact)).astype(np.int64)
    large = np.minimum(large, half - 1)
    return np.where(rel > 0, half, 0) + np.where(n < max_exact, n, large)


def _pair_bias(rel_bias, n_rows, c0):
    length = n_rows + TQ
    rel = np.arange(length) - (TQ - 1) - LANE - c0
    table = (rel_bias - rel_bias[NUM_BUCKETS // 2 - 1][None, :]) * LOG2E
    t1d = table[_rel_bucket_np(rel)].T
    x = jnp.tile(t1d, (1, TQ))[:, :TQ * (length - 1)].reshape(N_HEADS, TQ, length - 1)
    b = jnp.transpose(x[:, :, TQ - 1:TQ - 1 + n_rows], (0, 2, 1)).reshape(N_PAIRS, 2, n_rows, TQ)
    return jnp.concatenate([b[:, 0], b[:, 1]], axis=-1).astype(F32)


def _prompt_schedule(n, n_pad):
    pos = np.arange(n_pad)
    vis = np.where(pos < N_META, N_META, N_META + CHUNK * ((pos - N_META) // CHUNK + 1))
    vis = np.where(pos < n, np.minimum(vis, n), 0).astype(np.int32)
    n_tiles = -(-n // TQ)
    vmax = vis[:n_tiles * TQ].reshape(n_tiles, TQ).max(axis=1)
    nch = -(-vmax // KEY_CHUNK)
    nfull = vis[:n_tiles * TQ].reshape(n_tiles, TQ).min(axis=1) // KEY_CHUNK
    toff = PROMPT_BIAS_C0 // LANE - np.arange(n_tiles)
    assert ((nch - 1) * (KEY_CHUNK // LANE) + toff).max() <= (PROMPT_BIAS_ROWS - KEY_CHUNK) // LANE - 1
    sched = tuple(jnp.asarray(a, I32) for a in (nch, nfull, toff))
    has_far = bool((np.minimum(np.maximum(3 - toff, 0) // 4, nch) >= 2).any())
    return sched, jnp.asarray(vis[None, :]), has_far


def _sample_schedule(past, ts):
    nk = past + ts
    sched = tuple(jnp.asarray([v], I32) for v in (-(-nk // KEY_CHUNK), 0, 0))
    vis = jnp.asarray(np.where(np.arange(TQ) < ts, nk, 0)[None, :], I32)
    return sched, vis, False


def _s5_params(lam_re, lam_im, log_step, b_re, b_im, c_re, c_im, d_skip):
    dt = jnp.exp(log_step)[:, None]
    mag = jnp.exp(lam_re * dt)
    ab_re, ab_im = mag * jnp.cos(lam_im * dt), mag * jnp.sin(lam_im * dt)
    nr, ni = ab_re - 1.0, ab_im
    den = lam_re * lam_re + lam_im * lam_im
    f_re, f_im = (nr * lam_re + ni * lam_im) / den, (ni * lam_re - nr * lam_im) / den
    bb_re = f_re[..., None] * b_re - f_im[..., None] * b_im
    bb_im = f_re[..., None] * b_im + f_im[..., None] * b_re
    eye = jnp.eye(SSM_GROUPS, dtype=F32)
    bd_in = lambda w: jnp.einsum('gpc,gh->gchp', w, eye).reshape(SSM_W, SSM_S).astype(BF16)
    bd_out = lambda w: jnp.einsum('gcp,gh->gphc', w, eye).reshape(SSM_S, SSM_W).astype(BF16)
    return (ab_re.reshape(1, SSM_S), ab_im.reshape(1, SSM_S), bd_in(bb_re), bd_in(bb_im),
            bd_out(c_re), bd_out(c_im), d_skip.reshape(1, SSM_W))


def kernel(x_prompt, x_sample, cache_k, cache_v, cache_kidx, state_ssm_re, state_ssm_im, meta_tokens, rel_bias,
           ffn1_g_pre, ffn1_w_gate, ffn1_w_up, ffn1_w_down, ffn1_g_post, mix_g_pre, w_in, w_out, mix_g_post,
           lam_re, lam_im, log_step, b_re, b_im, c_re, c_im, d_skip, w_glu, b_glu,
           ffn2_g_pre, ffn2_w_gate, ffn2_w_up, ffn2_w_down, ffn2_g_post):
    depth = ffn1_g_pre.shape[0]
    assert depth == 1
    bp, seq, _ = x_prompt.shape
    assert bp == 1
    ds, ts, _ = x_sample.shape
    past = cache_k.shape[2]
    n_p = N_META + seq
    n_p_pad = pl.cdiv(n_p, ROW_TILE) * ROW_TILE
    n_s = ds * ts
    assert n_s % ROW_TILE == 0 and ts <= TQ
    nk_s_pad = pl.cdiv(past + ts, KEY_CHUNK) * KEY_CHUNK
    l = 0

    row2 = lambda g: g[l].reshape(1, -1)
    w1 = (row2(ffn1_g_pre), ffn1_w_gate[l].astype(BF16), ffn1_w_up[l].astype(BF16), ffn1_w_down[l].astype(BF16),
          row2(ffn1_g_post))
    w2 = (row2(ffn2_g_pre), ffn2_w_gate[l].astype(BF16), ffn2_w_up[l].astype(BF16), ffn2_w_down[l].astype(BF16),
          row2(ffn2_g_post))
    wi = w_in[l]
    o_q, o_k, o_v, o_qi, o_ki, o_wi, o_us = np.cumsum((0, ATT_W, ATT_W, ATT_W, N_IDX * IDX_DIM, IDX_DIM, N_IDX))
    sl = lambda o, w: wi[:, o:o + w]
    wrow = jnp.concatenate([sl(o_k, ATT_W), sl(o_v, ATT_W), sl(o_us, SSM_W), sl(o_ki, IDX_DIM),
                            jnp.zeros((D_MODEL, ROW_COLS_PAD - ROW_COLS), F32)], axis=1).astype(BF16)
    wcol = jnp.concatenate([sl(o_q, ATT_W), sl(o_qi, ATT_W), sl(o_v, ATT_W), sl(o_wi, N_IDX)], axis=1).T.astype(BF16)
    woa, wos = w_out[l][:ATT_W].astype(BF16), w_out[l][ATT_W:].astype(BF16)
    s5w = _s5_params(lam_re[l], lam_im[l], log_step[l], b_re[l], b_im[l], c_re[l], c_im[l], d_skip[l])
    s5w = s5w + (w_glu[l].astype(BF16), b_glu[l].reshape(1, SSM_W))
    bias_tab = rel_bias.astype(F32)

    xp = jnp.concatenate([meta_tokens.astype(F32), x_prompt[0]], axis=0)
    hp = _ffn(xp, *w1)
    kp, vp, kip, usp, kbp, kibp, vtbp, qtp, qitp, wtp = _mix_in(hp, row2(mix_g_pre), wrow, wcol, n_p_pad)
    sched_p, vis_p, far_p = _prompt_schedule(n_p, n_p_pad)
    att_p = _attention(qitp[None], wtp[None], qtp[None], vis_p, kibp[None], kbp[None], vtbp[None],
                       _pair_bias(bias_tab, PROMPT_BIAS_ROWS, PROMPT_BIAS_C0), sched_p,
                       min(TOPK_MAX, seq // 4), True, far_p)[0]
    zero_state = jnp.zeros((1, 1, SSM_S), F32)
    tt_p = max(t for t in range(8, 513, 8) if n_p % t == 0)
    ssm_p, srp, sip = _s5(usp[None], zero_state, zero_state, *s5w, tt_p)
    yp = _mix_out(att_p, ssm_p[0], hp, woa, wos, row2(mix_g_post), *w2)

    hs = _ffn(x_sample.reshape(n_s, D_MODEL), *w1)
    ks, vs, kis, uss, _, _, _, qts, qits, wts = _mix_in(hs, row2(mix_g_pre), wrow, wcol, n_s)
    k_all, vt_all, ki_all = _decode_keys(
        cache_k[l].reshape(ds, past, ATT_W), cache_v[l].reshape(ds, past, ATT_W), cache_kidx[l],
        ks.reshape(ds, ts, ATT_W), vs.reshape(ds, ts, ATT_W), kis.reshape(ds, ts, IDX_DIM))
    assert k_all.shape[1] == nk_s_pad
    lanes = lambda a: jnp.pad(jnp.transpose(a.reshape(a.shape[0], ds, ts), (1, 0, 2)), ((0, 0), (0, 0), (0, TQ - ts)))
    sched_s, vis_s, far_s = _sample_schedule(past, ts)
    att_s = _attention(lanes(qits), lanes(wts), lanes(qts), vis_s, ki_all, k_all, vt_all,
                       _pair_bias(bias_tab, LANE + nk_s_pad, past), sched_s,
                       min(TOPK_MAX, (past + ts) // 4), False, far_s)
    att_s = att_s[:, :ts].reshape(n_s, ATT_W)
    ssm_s, srs, sis = _s5(uss.reshape(ds, ts, SSM_W), state_ssm_re[l].reshape(ds, 1, SSM_S),
                          state_ssm_im[l].reshape(ds, 1, SSM_S), *s5w, ts)
    ys = _mix_out(att_s, ssm_s.reshape(n_s, SSM_W), hs, woa, wos, row2(mix_g_post), *w2)

    heads = lambda a, b, t: a.reshape(1, b, t, N_HEADS, HEAD_DIM)
    state = lambda a, b: a.reshape(1, b, SSM_GROUPS, SSM_P)
    return (yp[N_META:][None], ys.reshape(ds, ts, D_MODEL),
            heads(kp, 1, n_p), heads(vp, 1, n_p), kip.reshape(1, 1, n_p, IDX_DIM), state(srp, 1), state(sip, 1),
            heads(ks, ds, ts), heads(vs, ds, ts), kis.reshape(1, ds, ts, IDX_DIM), state(srs, ds), state(sis, ds))
```

```python
import functools
import math

import numpy as np
import jax
import jax.numpy as jnp
from jax import lax
from jax.experimental import pallas as pl
from jax.experimental.pallas import tpu as pltpu

F32, BF16, I32 = jnp.float32, jnp.bfloat16, jnp.int32

D_MODEL = 1024
CHUNK = 64
N_META = 16
N_HEADS = 8
HEAD_DIM = 64
ATT_W = N_HEADS * HEAD_DIM
N_IDX = 8
IDX_DIM = 64
TOPK_MAX = 256
SSM_GROUPS = 32
SSM_GC = 16
SSM_W = SSM_GROUPS * SSM_GC
SSM_P = 64
SSM_S = SSM_GROUPS * SSM_P
D_FF = 2816
NUM_BUCKETS = 32
MAX_DISTANCE = 128
EPS = 1e-6

LANE = 128
VMEM_LIMIT = 62 * 1024 * 1024
ROW_TILE = 512
TQ = LANE
KEY_CHUNK = 512
PROMPT_BIAS_C0 = 512
PROMPT_BIAS_ROWS = 1280
NEG = -1e30
LOG2E = math.log2(math.e)
F32_MAX = float(np.finfo(np.float32).max)
N_PAIRS = N_HEADS // 2
SUM_ROWS = 16
CAND_DEPTH = 16
CAND_BATCH = 8 * CAND_DEPTH
CAND_GROUPS = 8
SCORE_ROUND = 8
TIE_SCAN_MAX = 8


def _batcher_network(n):
    pairs, p = [], 1
    while p < n:
        k = p
        while k >= 1:
            for j in range(k % p, n - k, 2 * k):
                for i in range(min(k, n - j - k)):
                    if (i + j) // (2 * p) == (i + j + k) // (2 * p):
                        pairs.append((i + j, i + j + k))
            k //= 2
        p *= 2
    return pairs


_SORT_NETWORK = _batcher_network(CAND_DEPTH)


def _rms(x, g):
    return x * lax.rsqrt(jnp.mean(x * x, axis=-1, keepdims=True) + EPS) * g


def _swiglu(xn, wg_ref, wu_ref, wd_ref):
    xb = xn.astype(BF16)
    a = jnp.dot(xb, wg_ref[...], preferred_element_type=F32)
    b = jnp.dot(xb, wu_ref[...], preferred_element_type=F32)
    hidden = (a * jax.nn.sigmoid(a) * b).astype(BF16)
    return jnp.dot(hidden, wd_ref[...], preferred_element_type=F32)


def _const_spec(shape):
    nd = len(shape)
    return pl.BlockSpec(shape, lambda *_: (0,) * nd, pipeline_mode=pl.Buffered(1))


def _ffn_body(x_ref, gpre_ref, wg_ref, wu_ref, wd_ref, gpost_ref, o_ref):
    x = x_ref[...]
    y = _swiglu(_rms(x, gpre_ref[...]), wg_ref, wu_ref, wd_ref)
    o_ref[...] = x + 0.5 * _rms(y, gpost_ref[...])


def _ffn(x, gpre, wg, wu, wd, gpost):
    n = x.shape[0]
    row = pl.BlockSpec((ROW_TILE, D_MODEL), lambda i: (i, 0))
    return pl.pallas_call(
        _ffn_body,
        grid=(pl.cdiv(n, ROW_TILE),),
        in_specs=[row, _const_spec((1, D_MODEL)), _const_spec((D_MODEL, D_FF)), _const_spec((D_MODEL, D_FF)),
                  _const_spec((D_FF, D_MODEL)), _const_spec((1, D_MODEL))],
        out_specs=row,
        out_shape=jax.ShapeDtypeStruct((n, D_MODEL), F32),
        compiler_params=pltpu.CompilerParams(dimension_semantics=("arbitrary",), vmem_limit_bytes=VMEM_LIMIT),
        name="ffn1",
    )(x, gpre, wg, wu, wd, gpost)


ROW_COLS = 3 * ATT_W + IDX_DIM
ROW_COLS_PAD = 13 * LANE
COL_ROWS = 3 * ATT_W + N_IDX


def _mix_in_body(n_valid, h_ref, g_ref, wrow_ref, wcol_ref,
                 k_ref, v_ref, ki_ref, us_ref, kb_ref, kib_ref, vtb_ref, qt_ref, qit_ref, wt_ref):
    i = pl.program_id(0)
    u = _rms(h_ref[...], g_ref[...]).astype(BF16)
    z = jnp.dot(u, wrow_ref[...], preferred_element_type=F32)
    zt = lax.dot_general(wcol_ref[...], u, (((1,), (1,)), ((), ())), preferred_element_type=F32)
    k = z[:, 0:ATT_W]
    v = z[:, ATT_W:2 * ATT_W]
    ki = z[:, 3 * ATT_W:3 * ATT_W + IDX_DIM]
    k_ref[...] = k
    v_ref[...] = v
    us_ref[...] = z[:, 2 * ATT_W:3 * ATT_W]
    ki_ref[...] = ki
    row_ok = (i * ROW_TILE + lax.broadcasted_iota(I32, (ROW_TILE, 1), 0)) < n_valid
    col_ok = (i * ROW_TILE + lax.broadcasted_iota(I32, (1, ROW_TILE), 1)) < n_valid
    kb_ref[...] = jnp.where(row_ok, k, 0.0).astype(BF16)
    kib_ref[...] = jnp.where(row_ok, ki, 0.0).astype(BF16)
    zt = jnp.where(col_ok, zt, 0.0)
    qt_ref[...] = (zt[0:ATT_W] * (HEAD_DIM ** -0.5 * LOG2E)).astype(BF16)
    vtb_ref[...] = zt[2 * ATT_W:3 * ATT_W].astype(BF16)
    wt = zt[3 * ATT_W:3 * ATT_W + N_IDX] * ((IDX_DIM ** -0.5) * (N_IDX ** -0.5))
    wt_ref[...] = wt
    for h in range(N_IDX):
        qi_h = zt[ATT_W + h * IDX_DIM:ATT_W + (h + 1) * IDX_DIM]
        qit_ref[h * IDX_DIM:(h + 1) * IDX_DIM, :] = (qi_h * wt[h:h + 1, :]).astype(BF16)


def _mix_in(h, g, wrow, wcol, n_pad):
    n = h.shape[0]
    steps = n_pad // ROW_TILE
    assert steps == pl.cdiv(n, ROW_TILE)

    def row(width):
        return pl.BlockSpec((ROW_TILE, width), lambda i: (i, 0))

    col = lambda rows: pl.BlockSpec((rows, ROW_TILE), lambda i: (0, i))
    out_shape = (
        jax.ShapeDtypeStruct((n, ATT_W), F32), jax.ShapeDtypeStruct((n, ATT_W), F32),
        jax.ShapeDtypeStruct((n, IDX_DIM), F32), jax.ShapeDtypeStruct((n, SSM_W), F32),
        jax.ShapeDtypeStruct((n_pad, ATT_W), BF16), jax.ShapeDtypeStruct((n_pad, IDX_DIM), BF16),
        jax.ShapeDtypeStruct((n_pad // ROW_TILE, ATT_W, ROW_TILE), BF16),
        jax.ShapeDtypeStruct((ATT_W, n_pad), BF16), jax.ShapeDtypeStruct((ATT_W, n_pad), BF16),
        jax.ShapeDtypeStruct((N_IDX, n_pad), F32),
    )
    out_specs = (
        row(ATT_W), row(ATT_W), row(IDX_DIM), row(SSM_W), row(ATT_W), row(IDX_DIM),
        pl.BlockSpec((None, ATT_W, ROW_TILE), lambda i: (i, 0, 0)),
        col(ATT_W), col(ATT_W), col(N_IDX),
    )
    return pl.pallas_call(
        functools.partial(_mix_in_body, n),
        grid=(steps,),
        in_specs=[row(D_MODEL), _const_spec((1, D_MODEL)), _const_spec((D_MODEL, ROW_COLS_PAD)),
                  _const_spec((COL_ROWS, D_MODEL))],
        out_specs=out_specs,
        out_shape=out_shape,
        compiler_params=pltpu.CompilerParams(dimension_semantics=("arbitrary",), vmem_limit_bytes=VMEM_LIMIT),
        name="mix_in",
    )(h, g, wrow, wcol)


def _ordered_bits_to_f32(u):
    k = u ^ jnp.int32(-2 ** 31)
    b = k ^ ((k >> 31) & jnp.int32(0x7FFFFFFF))
    return lax.bitcast_convert_type(b, F32)


def _attn_body(topk, idx_bits, has_far,
               nch_ref, nfull_ref, toff_ref,
               qit_ref, wt_ref, qt_ref, vis_ref, ki_ref, k_ref, vt_ref, bias_ref,
               o_ref,
               st_ref, cand_ref, thr_ref, cgt_ref, cge_ref, clip_ref, qicat_ref, qbd_ref, sa_ref, sb_ref,
               bma_ref, bmb_ref,
               m_ref, acc_ref, cut_ref, rowf_ref, ot_ref):
    j = pl.program_id(1)
    nch = nch_ref[j]
    toff = toff_ref[j]
    kf = float(topk)
    bias_tile_max = (bias_ref.shape[1] - KEY_CHUNK) // LANE - 1

    for h in range(N_IDX):
        qicat_ref[:, h * TQ:(h + 1) * TQ] = qit_ref[h * IDX_DIM:(h + 1) * IDX_DIM, :]
    qbd_ref[...] = jnp.zeros(qbd_ref.shape, BF16)
    for a in range(N_PAIRS):
        qbd_ref[a, 0:HEAD_DIM, 0:TQ] = qt_ref[2 * a * HEAD_DIM:(2 * a + 1) * HEAD_DIM, :]
        qbd_ref[a, HEAD_DIM:2 * HEAD_DIM, TQ:2 * TQ] = qt_ref[(2 * a + 1) * HEAD_DIM:(2 * a + 2) * HEAD_DIM, :]

    def chunk_start(c):
        return pl.multiple_of(c * KEY_CHUNK, KEY_CHUNK)

    def chunk_rows(r0):
        return r0 + lax.broadcasted_iota(I32, (KEY_CHUNK, TQ), 0)

    vis = vis_ref[...]
    cand_ref[...] = jnp.full(cand_ref.shape, -jnp.inf, F32)
    w_pos = wt_ref[...] >= 0.0
    clip_ref[0:N_IDX, :] = jnp.where(w_pos, 0.0, -jnp.inf)
    clip_ref[N_IDX:2 * N_IDX, :] = jnp.where(w_pos, jnp.inf, 0.0)

    last_batch_row = ki_ref.shape[0] - CAND_BATCH

    def score_round(masked, i, carry):
        for g in range(SCORE_ROUND):
            r0 = pl.multiple_of((i * SCORE_ROUND + g) * CAND_BATCH, CAND_BATCH)
            rel = jnp.dot(ki_ref[pl.ds(jnp.minimum(r0, last_batch_row), CAND_BATCH), :], qicat_ref[...],
                          preferred_element_type=F32)
            score = jnp.zeros((CAND_BATCH, TQ), F32)
            for h in range(N_IDX):
                score = score + jnp.clip(rel[:, h * TQ:(h + 1) * TQ], clip_ref[h:h + 1, :],
                                         clip_ref[N_IDX + h:N_IDX + h + 1, :])
            if masked:
                score = jnp.where(r0 + lax.broadcasted_iota(I32, (CAND_BATCH, TQ), 0) < vis, score, -jnp.inf)
            st_ref[pl.ds(r0, CAND_BATCH), :] = score
        return carry

    round_chunks = SCORE_ROUND * CAND_BATCH // KEY_CHUNK
    full_rounds = nfull_ref[j] // round_chunks
    lax.fori_loop(0, full_rounds, functools.partial(score_round, False), 0)
    lax.fori_loop(full_rounds, (nch + round_chunks - 1) // round_chunks, functools.partial(score_round, True), 0)

    def merge_batch(b, carry):
        r0 = pl.multiple_of(b * CAND_BATCH, CAND_BATCH)
        x = [st_ref[pl.ds(r0 + 8 * r, 8), :] for r in range(CAND_DEPTH)]
        for lo, hi in _SORT_NETWORK:
            x[lo], x[hi] = jnp.maximum(x[lo], x[hi]), jnp.minimum(x[lo], x[hi])
        g0 = pl.multiple_of(lax.rem(b, CAND_GROUPS) * CAND_BATCH, CAND_BATCH)
        t = [jnp.maximum(cand_ref[pl.ds(g0 + 8 * r, 8), :], x[CAND_DEPTH - 1 - r]) for r in range(CAND_DEPTH)]
        d = CAND_DEPTH // 2
        while d >= 1:
            for r in range(CAND_DEPTH):
                if r & d == 0:
                    t[r], t[r + d] = jnp.maximum(t[r], t[r + d]), jnp.minimum(t[r], t[r + d])
            d //= 2
        for r in range(CAND_DEPTH):
            cand_ref[pl.ds(g0 + 8 * r, 8), :] = t[r]
        return carry

    lax.fori_loop(0, nch * (KEY_CHUNK // CAND_BATCH), merge_batch, 0)

    def reduce_rows(ref, n_chunks, fn, combine, init):
        reducer = jnp.sum if combine is jnp.add else jnp.max

        def body(c, acc):
            r0 = chunk_start(c)
            val = fn(ref[pl.ds(r0, KEY_CHUNK), :], r0)
            part = reducer(val.reshape(8, KEY_CHUNK // 64, 8, TQ), axis=1)
            return combine(acc, reducer(part, axis=0))

        acc = lax.fori_loop(0, n_chunks, body, jnp.full((8, TQ), init, F32), unroll=isinstance(n_chunks, int))
        return reducer(acc, axis=0, keepdims=True)

    def count(ref, n_chunks, indicator):
        return reduce_rows(ref, n_chunks, indicator, jnp.add, 0.0)

    def select(ref, n_chunks):
        def value_bit(i, res):
            trial = res | lax.shift_left(jnp.int32(1), 31 - i)
            thr_t = _ordered_bits_to_f32(trial)
            cnt = count(ref, n_chunks, lambda blk, r0: jnp.where(blk >= thr_t, 1.0, 0.0))
            return jnp.where(cnt >= kf, trial, res)

        res = lax.fori_loop(0, 32, value_bit, jnp.zeros((1, TQ), I32))
        res = jnp.where((res >> 23) == 0, jnp.int32(0x00800000), res)
        lo = _ordered_bits_to_f32(res)
        above = _ordered_bits_to_f32(res + 1)
        c_above = count(ref, n_chunks, lambda blk, r0: jnp.where(blk >= above, 1.0, 0.0))
        c_lo = count(ref, n_chunks, lambda blk, r0: jnp.where(blk >= lo, 1.0, 0.0))
        thr_ref[...] = lo
        cgt_ref[...] = c_above
        cge_ref[...] = c_lo

        def walking(state):
            return jnp.max(state[0]) > 0.0

        def walk(state):
            active, hi, c_hi = state
            v = reduce_rows(ref, n_chunks, lambda blk, r0: jnp.where(blk < hi, blk, -jnp.inf), jnp.maximum,
                            -jnp.inf)
            c_v = count(ref, n_chunks, lambda blk, r0: jnp.where(blk >= v, 1.0, 0.0))
            done = jnp.logical_and(active > 0.0, c_v >= kf)
            thr_ref[...] = jnp.where(done, v, thr_ref[...])
            cgt_ref[...] = jnp.where(done, c_hi, cgt_ref[...])
            cge_ref[...] = jnp.where(done, c_v, cge_ref[...])
            return jnp.where(done, 0.0, active), v, c_v

        surplus = jnp.where(jnp.logical_and(c_lo > kf, c_above < kf), 1.0, 0.0)
        lax.while_loop(walking, walk, (surplus, above, c_above))
        return lo

    cand_lo = select(cand_ref, cand_ref.shape[0] // KEY_CHUNK)
    kept_min = cand_ref[CAND_BATCH - 8:CAND_BATCH, :]
    for g in range(1, CAND_GROUPS):
        kept_min = jnp.maximum(kept_min, cand_ref[(g + 1) * CAND_BATCH - 8:(g + 1) * CAND_BATCH, :])
    unsafe = jnp.max(kept_min, axis=0, keepdims=True) >= cand_lo

    @pl.when(jnp.max(jnp.where(unsafe, 1.0, 0.0)) > 0.0)
    def _():
        select(st_ref, nch)

    thr = thr_ref[...]

    need = kf - cgt_ref[...]
    excess = cge_ref[...] - kf
    cut_ref[...] = jnp.full((1, TQ), F32_MAX, F32)
    rowf_ref[...] = lax.broadcasted_iota(I32, (KEY_CHUNK, TQ), 0).astype(F32)

    def chunk_rows_f(r0):
        return rowf_ref[...] + r0.astype(F32)

    tie_need = jnp.max(jnp.where(excess > 0.0, need, 0.0)).astype(I32)

    @pl.when(jnp.logical_and(tie_need > 0, tie_need <= TIE_SCAN_MAX))
    def _():
        def next_tied(cut_f):
            def body(c, acc):
                r0 = chunk_start(c)
                tied_row = jnp.where(st_ref[pl.ds(r0, KEY_CHUNK), :] == thr, chunk_rows_f(r0), F32_MAX)
                v = jnp.where(tied_row > cut_f, tied_row, F32_MAX)
                part = jnp.min(v.reshape(8, KEY_CHUNK // 64, 8, TQ), axis=1)
                return jnp.minimum(acc, jnp.min(part, axis=0))

            acc = lax.fori_loop(0, nch, body, jnp.full((8, TQ), F32_MAX, F32))
            return jnp.min(acc, axis=0, keepdims=True)

        def take(t, cut_f):
            return jnp.where(t.astype(F32) < need, next_tied(cut_f), cut_f)

        cut_f = lax.fori_loop(0, tie_need, take, jnp.full((1, TQ), -1.0, F32))
        cut_ref[...] = jnp.where(excess > 0.0, cut_f, F32_MAX)

    @pl.when(tie_need > TIE_SCAN_MAX)
    def _():
        def index_bit(i, cut):
            trial = cut | lax.shift_left(jnp.int32(1), idx_bits - 1 - i)
            cnt = count(st_ref, nch, lambda blk, r0: jnp.where(
                blk == thr, jnp.where(chunk_rows(r0) < trial, 1.0, 0.0), 0.0))
            return jnp.where(cnt < need, trial, cut)

        cut = lax.fori_loop(0, idx_bits, index_bit, jnp.zeros((1, TQ), I32))
        cut_ref[...] = jnp.where(excess > 0.0, cut.astype(F32), F32_MAX)

    cut_f = cut_ref[...]

    st_ref[pl.ds(chunk_start(nch), KEY_CHUNK), :] = jnp.full((KEY_CHUNK, TQ), -jnp.inf, F32)

    m_ref[...] = jnp.full(m_ref.shape, NEG, F32)
    acc_ref[...] = jnp.zeros(acc_ref.shape, F32)
    last = nch - 1
    ones_rows = jnp.ones((SUM_ROWS, KEY_CHUNK), BF16)

    def logits(blk, s_ref, bmax_ref, with_bias):
        kblk = k_ref[pl.ds(chunk_start(jnp.minimum(blk, last)), KEY_CHUNK), :]
        r0 = chunk_start(jnp.minimum(blk, nch))
        score = st_ref[pl.ds(r0, KEY_CHUNK), :]
        tie = jnp.where(chunk_rows_f(r0) <= cut_f, 0.0, NEG)
        msk = jnp.where(score > thr, 0.0, jnp.where(score == thr, tie, NEG))
        msk2 = jnp.concatenate([msk, msk], axis=1)
        bias_row0 = pl.multiple_of((jnp.clip(4 * blk + toff, -1, bias_tile_max) + 1) * LANE, LANE)
        for a in range(N_PAIRS):
            s2 = jnp.dot(kblk[:, a * 2 * HEAD_DIM:(a + 1) * 2 * HEAD_DIM], qbd_ref[a],
                         preferred_element_type=F32) + msk2
            if with_bias:
                s2 = s2 + bias_ref[a, pl.ds(bias_row0, KEY_CHUNK), :]
            s_ref[a] = s2
            bmax_ref[a:a + 1, :] = jnp.max(s2, axis=0, keepdims=True)

    def softmax_pv(blk, s_ref, bmax_ref):
        vblk = jnp.minimum(blk, last)
        for a in range(N_PAIRS):
            m_old = m_ref[a:a + 1, :]
            m_new = jnp.maximum(m_old, bmax_ref[a:a + 1, :])
            alpha = jnp.exp2(m_old - m_new)
            p = jnp.exp2(s_ref[a] - m_new).astype(BF16)
            v_ones = jnp.concatenate([vt_ref[vblk, a * 2 * HEAD_DIM:(a + 1) * 2 * HEAD_DIM, :], ones_rows], axis=0)
            acc_ref[a] = alpha * acc_ref[a] + jnp.dot(v_ones, p, preferred_element_type=F32)
            m_ref[a:a + 1, :] = m_new

    def block_pair(with_bias, i, carry):
        logits(2 * i + 1, sb_ref, bmb_ref, with_bias)
        softmax_pv(2 * i, sa_ref, bma_ref)
        logits(2 * i + 2, sa_ref, bma_ref, with_bias)
        softmax_pv(2 * i + 1, sb_ref, bmb_ref)
        return carry

    if has_far:
        far_pairs = jnp.minimum(jnp.maximum(3 - toff, 0) // 4, nch) // 2
        logits(0, sa_ref, bma_ref, False)
        lax.fori_loop(0, far_pairs, functools.partial(block_pair, False), 0)
    else:
        far_pairs = 0
    logits(2 * far_pairs, sa_ref, bma_ref, True)
    lax.fori_loop(far_pairs, (nch + 1) // 2, functools.partial(block_pair, True), 0)

    for a in range(N_PAIRS):
        for e in range(2):
            h = 2 * a + e
            num = acc_ref[a, e * HEAD_DIM:(e + 1) * HEAD_DIM, e * TQ:(e + 1) * TQ]
            den = acc_ref[a, 2 * HEAD_DIM:2 * HEAD_DIM + 1, e * TQ:(e + 1) * TQ]
            ot_ref[h * HEAD_DIM:(h + 1) * HEAD_DIM, :] = num / den
    o_ref[...] = ot_ref[...].T


def _attention(qit, wt, qt, vis, ki, k, vt, bias, sched, topk, resident, has_far):
    s_count, _, nq = qit.shape
    nk = k.shape[1]
    n_tiles = sched[0].shape[0]
    mode = dict(pipeline_mode=pl.Buffered(1)) if resident else {}
    idx_bits = max(1, int(nk - 1).bit_length())
    logit_buf = pltpu.VMEM((N_PAIRS, KEY_CHUNK, 2 * TQ), F32)
    in_specs = [
        pl.BlockSpec((None, ATT_W, TQ), lambda s, j, *_: (s, 0, j)),
        pl.BlockSpec((None, N_IDX, TQ), lambda s, j, *_: (s, 0, j)),
        pl.BlockSpec((None, ATT_W, TQ), lambda s, j, *_: (s, 0, j)),
        pl.BlockSpec((1, TQ), lambda s, j, *_: (0, j)),
        pl.BlockSpec((None, nk, IDX_DIM), lambda s, j, *_: (s, 0, 0), **mode),
        pl.BlockSpec((None, nk, ATT_W), lambda s, j, *_: (s, 0, 0), **mode),
        pl.BlockSpec((None, nk // KEY_CHUNK, ATT_W, KEY_CHUNK), lambda s, j, *_: (s, 0, 0, 0), **mode),
        pl.BlockSpec(bias.shape, lambda s, j, *_: (0, 0, 0), pipeline_mode=pl.Buffered(1)),
    ]
    grid_spec = pltpu.PrefetchScalarGridSpec(
        num_scalar_prefetch=3,
        grid=(s_count, n_tiles),
        in_specs=in_specs,
        out_specs=pl.BlockSpec((None, TQ, ATT_W), lambda s, j, *_: (s, j, 0)),
        scratch_shapes=[
            pltpu.VMEM((nk + KEY_CHUNK, TQ), F32),
            pltpu.VMEM((CAND_GROUPS * CAND_BATCH, TQ), F32),
            pltpu.VMEM((1, TQ), F32),
            pltpu.VMEM((1, TQ), F32),
            pltpu.VMEM((1, TQ), F32),
            pltpu.VMEM((2 * N_IDX, TQ), F32),
            pltpu.VMEM((IDX_DIM, N_IDX * TQ), BF16),
            pltpu.VMEM((N_PAIRS, 2 * HEAD_DIM, 2 * TQ), BF16),
            logit_buf, logit_buf,
            pltpu.VMEM((N_PAIRS, 2 * TQ), F32),
            pltpu.VMEM((N_PAIRS, 2 * TQ), F32),
            pltpu.VMEM((N_PAIRS, 2 * TQ), F32),
            pltpu.VMEM((N_PAIRS, 2 * HEAD_DIM + SUM_ROWS, 2 * TQ), F32),
            pltpu.VMEM((1, TQ), F32),
            pltpu.VMEM((KEY_CHUNK, TQ), F32),
            pltpu.VMEM((ATT_W, TQ), F32),
        ],
    )
    return pl.pallas_call(
        functools.partial(_attn_body, topk, idx_bits, has_far),
        grid_spec=grid_spec,
        out_shape=jax.ShapeDtypeStruct((s_count, nq, ATT_W), F32),
        compiler_params=pltpu.CompilerParams(dimension_semantics=("arbitrary", "arbitrary"),
                                             vmem_limit_bytes=VMEM_LIMIT),
        name="attn",
    )(*sched, qit, wt, qt, vis, ki, k, vt, bias)


def _decode_keys_body(ck_ref, cv_ref, cki_ref, k_ref, v_ref, ki_ref, ko_ref, vto_ref, kio_ref):
    past = ck_ref.shape[0]
    ts = k_ref.shape[0]
    pad = jnp.zeros((KEY_CHUNK - ts, ATT_W), F32)
    ko_ref[0:past, :] = ck_ref[...].astype(BF16)
    ko_ref[past:past + KEY_CHUNK, :] = jnp.concatenate([k_ref[...], pad], axis=0).astype(BF16)
    kio_ref[0:past, :] = cki_ref[...].astype(BF16)
    kio_ref[past:past + KEY_CHUNK, :] = jnp.concatenate([ki_ref[...], pad[:, 0:IDX_DIM]], axis=0).astype(BF16)
    for b in range(past // KEY_CHUNK):
        vto_ref[b] = cv_ref[b * KEY_CHUNK:(b + 1) * KEY_CHUNK, :].T.astype(BF16)
    vto_ref[past // KEY_CHUNK] = jnp.concatenate([v_ref[...], pad], axis=0).T.astype(BF16)


def _decode_keys(cache_k, cache_v, cache_ki, k_new, v_new, ki_new):
    ds, past, _ = cache_k.shape
    ts = k_new.shape[1]
    assert past % KEY_CHUNK == 0 and ts <= KEY_CHUNK
    nk = past + KEY_CHUNK
    per_stream = lambda rows, width: pl.BlockSpec((None, rows, width), lambda s: (s, 0, 0))
    return pl.pallas_call(
        _decode_keys_body,
        grid=(ds,),
        in_specs=[per_stream(past, ATT_W), per_stream(past, ATT_W), per_stream(past, IDX_DIM),
                  per_stream(ts, ATT_W), per_stream(ts, ATT_W), per_stream(ts, IDX_DIM)],
        out_specs=(per_stream(nk, ATT_W),
                   pl.BlockSpec((None, nk // KEY_CHUNK, ATT_W, KEY_CHUNK), lambda s: (s, 0, 0, 0)),
                   per_stream(nk, IDX_DIM)),
        out_shape=(jax.ShapeDtypeStruct((ds, nk, ATT_W), BF16),
                   jax.ShapeDtypeStruct((ds, nk // KEY_CHUNK, ATT_W, KEY_CHUNK), BF16),
                   jax.ShapeDtypeStruct((ds, nk, IDX_DIM), BF16)),
        compiler_params=pltpu.CompilerParams(dimension_semantics=("arbitrary",), vmem_limit_bytes=VMEM_LIMIT),
        name="decode_keys",
    )(cache_k, cache_v, cache_ki, k_new, v_new, ki_new)


def _s5_body(tt, us_ref, s0r_ref, s0i_ref, ar_ref, ai_ref, bbr_ref, bbi_ref, cr_ref, ci_ref, d_ref, wglu_ref,
             bglu_ref, out_ref, sr_out_ref, si_out_ref, str_ref, sti_ref, bre_ref, bim_ref, xr_ref, xi_ref):
    @pl.when(pl.program_id(1) == 0)
    def _():
        str_ref[...] = s0r_ref[...]
        sti_ref[...] = s0i_ref[...]

    u = us_ref[...]
    ub = u.astype(BF16)
    bre_ref[...] = jnp.dot(ub, bbr_ref[...], preferred_element_type=F32)
    bim_ref[...] = jnp.dot(ub, bbi_ref[...], preferred_element_type=F32)
    ar = ar_ref[...]
    ai = ai_ref[...]

    def step(t, carry):
        sr, si = carry
        nr = ar * sr - ai * si + bre_ref[pl.ds(t, 1), :]
        ni = ar * si + ai * sr + bim_ref[pl.ds(t, 1), :]
        xr_ref[pl.ds(t, 1), :] = nr
        xi_ref[pl.ds(t, 1), :] = ni
        return nr, ni

    sr, si = lax.fori_loop(0, tt, step, (str_ref[...], sti_ref[...]))
    str_ref[...] = sr
    sti_ref[...] = si
    sr_out_ref[...] = sr
    si_out_ref[...] = si
    y = (jnp.dot(xr_ref[...].astype(BF16), cr_ref[...], preferred_element_type=F32)
         - jnp.dot(xi_ref[...].astype(BF16), ci_ref[...], preferred_element_type=F32)
         + d_ref[...] * u)
    g = jax.nn.gelu(y)
    gate = jnp.dot(g.astype(BF16), wglu_ref[...], preferred_element_type=F32) + bglu_ref[...]
    out_ref[...] = g * jax.nn.sigmoid(gate)


def _s5(us, s0r, s0i, ar, ai, bbr, bbi, cr, ci, d, wglu, bglu, tt):
    s_count, t_len, _ = us.shape
    assert t_len % tt == 0
    state = pl.BlockSpec((None, 1, SSM_S), lambda s, t: (s, 0, 0))
    seq = pl.BlockSpec((None, tt, SSM_W), lambda s, t: (s, t, 0))
    return pl.pallas_call(
        functools.partial(_s5_body, tt),
        grid=(s_count, t_len // tt),
        in_specs=[seq, state, state, _const_spec((1, SSM_S)), _const_spec((1, SSM_S)),
                  _const_spec((SSM_W, SSM_S)), _const_spec((SSM_W, SSM_S)),
                  _const_spec((SSM_S, SSM_W)), _const_spec((SSM_S, SSM_W)),
                  _const_spec((1, SSM_W)), _const_spec((SSM_W, SSM_W)), _const_spec((1, SSM_W))],
        out_specs=(seq, state, state),
        out_shape=(jax.ShapeDtypeStruct((s_count, t_len, SSM_W), F32),
                   jax.ShapeDtypeStruct((s_count, 1, SSM_S), F32),
                   jax.ShapeDtypeStruct((s_count, 1, SSM_S), F32)),
        scratch_shapes=[pltpu.VMEM((1, SSM_S), F32), pltpu.VMEM((1, SSM_S), F32),
                        pltpu.VMEM((tt, SSM_S), F32), pltpu.VMEM((tt, SSM_S), F32),
                        pltpu.VMEM((tt, SSM_S), F32), pltpu.VMEM((tt, SSM_S), F32)],
        compiler_params=pltpu.CompilerParams(dimension_semantics=("arbitrary", "arbitrary"),
                                             vmem_limit_bytes=VMEM_LIMIT),
        name="s5",
    )(us, s0r, s0i, ar, ai, bbr, bbi, cr, ci, d, wglu, bglu)


def _mix_out_body(att_ref, ssm_ref, h_ref, woa_ref, wos_ref, gmix_ref, gpre_ref, wg_ref, wu_ref, wd_ref, gpost_ref,
                  o_ref):
    m = (jnp.dot(att_ref[...].astype(BF16), woa_ref[...], preferred_element_type=F32)
         + jnp.dot(ssm_ref[...].astype(BF16), wos_ref[...], preferred_element_type=F32))
    h2 = h_ref[...] + _rms(m, gmix_ref[...])
    y = _swiglu(_rms(h2, gpre_ref[...]), wg_ref, wu_ref, wd_ref)
    o_ref[...] = h2 + 0.5 * _rms(y, gpost_ref[...])


def _mix_out(att, ssm, h, woa, wos, gmix, gpre, wg, wu, wd, gpost):
    n = h.shape[0]
    row = lambda width: pl.BlockSpec((ROW_TILE, width), lambda i: (i, 0))
    return pl.pallas_call(
        _mix_out_body,
        grid=(pl.cdiv(n, ROW_TILE),),
        in_specs=[row(ATT_W), row(SSM_W), row(D_MODEL), _const_spec((ATT_W, D_MODEL)),
                  _const_spec((SSM_W, D_MODEL)), _const_spec((1, D_MODEL)), _const_spec((1, D_MODEL)),
                  _const_spec((D_MODEL, D_FF)), _const_spec((D_MODEL, D_FF)), _const_spec((D_FF, D_MODEL)),
                  _const_spec((1, D_MODEL))],
        out_specs=row(D_MODEL),
        out_shape=jax.ShapeDtypeStruct((n, D_MODEL), F32),
        compiler_params=pltpu.CompilerParams(dimension_semantics=("arbitrary",), vmem_limit_bytes=VMEM_LIMIT),
        name="mix_out_ffn2",
    )(att, ssm, h, woa, wos, gmix, gpre, wg, wu, wd, gpost)


def _rel_bucket_np(rel):
    half = NUM_BUCKETS // 2
    max_exact = half // 2
    n = np.abs(rel).astype(np.int64)
    nf = np.maximum(n, 1).astype(np.float64)
    large = max_exact + (np.log(nf / max_exact) / math.log(MAX_DISTANCE / max_exact)
                         * (half - max_exact)).astype(np.int64)
    large = np.minimum(large, half - 1)
    return np.where(rel > 0, half, 0) + np.where(n < max_exact, n, large)


def _pair_bias(rel_bias, n_rows, c0):
    length = n_rows + TQ
    rel = np.arange(length) - (TQ - 1) - LANE - c0
    table = (rel_bias - rel_bias[NUM_BUCKETS // 2 - 1][None, :]) * LOG2E
    t1d = table[_rel_bucket_np(rel)].T
    x = jnp.tile(t1d, (1, TQ))[:, :TQ * (length - 1)].reshape(N_HEADS, TQ, length - 1)
    b = jnp.transpose(x[:, :, TQ - 1:TQ - 1 + n_rows], (0, 2, 1)).reshape(N_PAIRS, 2, n_rows, TQ)
    return jnp.concatenate([b[:, 0], b[:, 1]], axis=-1).astype(F32)


def _prompt_schedule(n, n_pad):
    pos = np.arange(n_pad)
    vis = np.where(pos < N_META, N_META, N_META + CHUNK * ((pos - N_META) // CHUNK + 1))
    vis = np.where(pos < n, np.minimum(vis, n), 0).astype(np.int32)
    n_tiles = -(-n // TQ)
    vmax = vis[:n_tiles * TQ].reshape(n_tiles, TQ).max(axis=1)
    nch = -(-vmax // KEY_CHUNK)
    nfull = vis[:n_tiles * TQ].reshape(n_tiles, TQ).min(axis=1) // KEY_CHUNK
    toff = PROMPT_BIAS_C0 // LANE - np.arange(n_tiles)
    assert ((nch - 1) * (KEY_CHUNK // LANE) + toff).max() <= (PROMPT_BIAS_ROWS - KEY_CHUNK) // LANE - 1
    sched = tuple(jnp.asarray(a, I32) for a in (nch, nfull, toff))
    has_far = bool((np.minimum(np.maximum(3 - toff, 0) // 4, nch) >= 2).any())
    return sched, jnp.asarray(vis[None, :]), has_far


def _sample_schedule(past, ts):
    nk = past + ts
    sched = tuple(jnp.asarray([v], I32) for v in (-(-nk // KEY_CHUNK), 0, 0))
    vis = jnp.asarray(np.where(np.arange(TQ) < ts, nk, 0)[None, :], I32)
    return sched, vis, False


def _s5_params(lam_re, lam_im, log_step, b_re, b_im, c_re, c_im, d_skip):
    dt = jnp.exp(log_step)[:, None]
    mag = jnp.exp(lam_re * dt)
    ab_re, ab_im = mag * jnp.cos(lam_im * dt), mag * jnp.sin(lam_im * dt)
    nr, ni = ab_re - 1.0, ab_im
    den = lam_re * lam_re + lam_im * lam_im
    f_re, f_im = (nr * lam_re + ni * lam_im) / den, (ni * lam_re - nr * lam_im) / den
    bb_re = f_re[..., None] * b_re - f_im[..., None] * b_im
    bb_im = f_re[..., None] * b_im + f_im[..., None] * b_re
    eye = jnp.eye(SSM_GROUPS, dtype=F32)
    bd_in = lambda w: jnp.einsum('gpc,gh->gchp', w, eye).reshape(SSM_W, SSM_S).astype(BF16)
    bd_out = lambda w: jnp.einsum('gcp,gh->gphc', w, eye).reshape(SSM_S, SSM_W).astype(BF16)
    return (ab_re.reshape(1, SSM_S), ab_im.reshape(1, SSM_S), bd_in(bb_re), bd_in(bb_im),
            bd_out(c_re), bd_out(c_im), d_skip.reshape(1, SSM_W))


def kernel(x_prompt, x_sample, cache_k, cache_v, cache_kidx, state_ssm_re, state_ssm_im, meta_tokens, rel_bias,
           ffn1_g_pre, ffn1_w_gate, ffn1_w_up, ffn1_w_down, ffn1_g_post, mix_g_pre, w_in, w_out, mix_g_post,
           lam_re, lam_im, log_step, b_re, b_im, c_re, c_im, d_skip, w_glu, b_glu,
           ffn2_g_pre, ffn2_w_gate, ffn2_w_up, ffn2_w_down, ffn2_g_post):
    depth = ffn1_g_pre.shape[0]
    assert depth == 1
    bp, seq, _ = x_prompt.shape
    assert bp == 1
    ds, ts, _ = x_sample.shape
    past = cache_k.shape[2]
    n_p = N_META + seq
    n_p_pad = pl.cdiv(n_p, ROW_TILE) * ROW_TILE
    n_s = ds * ts
    assert n_s % ROW_TILE == 0 and ts <= TQ
    nk_s_pad = pl.cdiv(past + ts, KEY_CHUNK) * KEY_CHUNK
    l = 0

    row2 = lambda g: g[l].reshape(1, -1)
    w1 = (row2(ffn1_g_pre), ffn1_w_gate[l].astype(BF16), ffn1_w_up[l].astype(BF16), ffn1_w_down[l].astype(BF16),
          row2(ffn1_g_post))
    w2 = (row2(ffn2_g_pre), ffn2_w_gate[l].astype(BF16), ffn2_w_up[l].astype(BF16), ffn2_w_down[l].astype(BF16),
          row2(ffn2_g_post))
    wi = w_in[l]
    o_q, o_k, o_v, o_qi, o_ki, o_wi, o_us = np.cumsum((0, ATT_W, ATT_W, ATT_W, N_IDX * IDX_DIM, IDX_DIM, N_IDX))
    sl = lambda o, w: wi[:, o:o + w]
    wrow = jnp.concatenate([sl(o_k, ATT_W), sl(o_v, ATT_W), sl(o_us, SSM_W), sl(o_ki, IDX_DIM),
                            jnp.zeros((D_MODEL, ROW_COLS_PAD - ROW_COLS), F32)], axis=1).astype(BF16)
    wcol = jnp.concatenate([sl(o_q, ATT_W), sl(o_qi, ATT_W), sl(o_v, ATT_W), sl(o_wi, N_IDX)], axis=1).T.astype(BF16)
    woa, wos = w_out[l][:ATT_W].astype(BF16), w_out[l][ATT_W:].astype(BF16)
    s5w = _s5_params(lam_re[l], lam_im[l], log_step[l], b_re[l], b_im[l], c_re[l], c_im[l], d_skip[l])
    s5w = s5w + (w_glu[l].astype(BF16), b_glu[l].reshape(1, SSM_W))
    bias_tab = rel_bias.astype(F32)

    xp = jnp.concatenate([meta_tokens.astype(F32), x_prompt[0]], axis=0)
    hp = _ffn(xp, *w1)
    kp, vp, kip, usp, kbp, kibp, vtbp, qtp, qitp, wtp = _mix_in(hp, row2(mix_g_pre), wrow, wcol, n_p_pad)
    sched_p, vis_p, far_p = _prompt_schedule(n_p, n_p_pad)
    att_p = _attention(qitp[None], wtp[None], qtp[None], vis_p, kibp[None], kbp[None], vtbp[None],
                       _pair_bias(bias_tab, PROMPT_BIAS_ROWS, PROMPT_BIAS_C0), sched_p,
                       min(TOPK_MAX, seq // 4), True, far_p)[0]
    zero_state = jnp.zeros((1, 1, SSM_S), F32)
    tt_p = max(t for t in range(8, 513, 8) if n_p % t == 0)
    ssm_p, srp, sip = _s5(usp[None], zero_state, zero_state, *s5w, tt_p)
    yp = _mix_out(att_p, ssm_p[0], hp, woa, wos, row2(mix_g_post), *w2)

    hs = _ffn(x_sample.reshape(n_s, D_MODEL), *w1)
    ks, vs, kis, uss, _, _, _, qts, qits, wts = _mix_in(hs, row2(mix_g_pre), wrow, wcol, n_s)
    k_all, vt_all, ki_all = _decode_keys(
        cache_k[l].reshape(ds, past, ATT_W), cache_v[l].reshape(ds, past, ATT_W), cache_kidx[l],
        ks.reshape(ds, ts, ATT_W), vs.reshape(ds, ts, ATT_W), kis.reshape(ds, ts, IDX_DIM))
    assert k_all.shape[1] == nk_s_pad
    lanes = lambda a: jnp.pad(jnp.transpose(a.reshape(a.shape[0], ds, ts), (1, 0, 2)), ((0, 0), (0, 0), (0, TQ - ts)))
    sched_s, vis_s, far_s = _sample_schedule(past, ts)
    att_s = _attention(lanes(qits), lanes(wts), lanes(qts), vis_s, ki_all, k_all, vt_all,
                       _pair_bias(bias_tab, LANE + nk_s_pad, past), sched_s,
                       min(TOPK_MAX, (past + ts) // 4), False, far_s)
    att_s = att_s[:, :ts].reshape(n_s, ATT_W)
    ssm_s, srs, sis = _s5(uss.reshape(ds, ts, SSM_W), state_ssm_re[l].reshape(ds, 1, SSM_S),
                          state_ssm_im[l].reshape(ds, 1, SSM_S), *s5w, ts)
    ys = _mix_out(att_s, ssm_s.reshape(n_s, SSM_W), hs, woa, wos, row2(mix_g_post), *w2)

    heads = lambda a, b, t: a.reshape(1, b, t, N_HEADS, HEAD_DIM)
    state = lambda a, b: a.reshape(1, b, SSM_GROUPS, SSM_P)
    return (yp[N_META:][None], ys.reshape(ds, ts, D_MODEL),
            heads(kp, 1, n_p), heads(vp, 1, n_p), kip.reshape(1, 1, n_p, IDX_DIM), state(srp, 1), state(sip, 1),
            heads(ks, ds, ts), heads(vs, ds, ts), kis.reshape(1, ds, ts, IDX_DIM), state(srs, ds), state(sis, ds))
```

```python
import functools
import math

import numpy as np
import jax
import jax.numpy as jnp
from jax import lax
from jax.experimental import pallas as pl
from jax.experimental.pallas import tpu as pltpu

F32, BF16, I32 = jnp.float32, jnp.bfloat16, jnp.int32

D_MODEL = 1024
CHUNK = 64
N_META = 16
N_HEADS = 8
HEAD_DIM = 64
ATT_W = N_HEADS * HEAD_DIM
N_IDX = 8
IDX_DIM = 64
TOPK_MAX = 256
SSM_GROUPS = 32
SSM_GC = 16
SSM_W = SSM_GROUPS * SSM_GC
SSM_P = 64
SSM_S = SSM_GROUPS * SSM_P
D_FF = 2816
NUM_BUCKETS = 32
MAX_DISTANCE = 128
EPS = 1e-6

LANE = 128
VMEM_LIMIT = 62 * 1024 * 1024
ROW_TILE = 512
TQ = LANE
KEY_CHUNK = 512
PROMPT_BIAS_C0 = 512
PROMPT_BIAS_ROWS = 1280
NEG = -1e30
LOG2E = math.log2(math.e)
F32_MAX = float(np.finfo(np.float32).max)
N_PAIRS = N_HEADS // 2
SUM_ROWS = 16
CAND_DEPTH = 16
CAND_BATCH = 8 * CAND_DEPTH
CAND_GROUPS = 8
SCORE_ROUND = 8
TIE_SCAN_MAX = 8


def _batcher_network(n):
    pairs, p = [], 1
    while p < n:
        k = p
        while k >= 1:
            for j in range(k % p, n - k, 2 * k):
                for i in range(min(k, n - j - k)):
                    if (i + j) // (2 * p) == (i + j + k) // (2 * p):
                        pairs.append((i + j, i + j + k))
            k //= 2
        p *= 2
    return pairs


_SORT_NETWORK = _batcher_network(CAND_DEPTH)


def _rms(x, g):
    return x * lax.rsqrt(jnp.mean(x * x, axis=-1, keepdims=True) + EPS) * g


def _swiglu(xn, wg_ref, wu_ref, wd_ref):
    xb = xn.astype(BF16)
    a = jnp.dot(xb, wg_ref[...], preferred_element_type=F32)
    b = jnp.dot(xb, wu_ref[...], preferred_element_type=F32)
    hidden = (a * jax.nn.sigmoid(a) * b).astype(BF16)
    return jnp.dot(hidden, wd_ref[...], preferred_element_type=F32)


def _const_spec(shape):
    nd = len(shape)
    return pl.BlockSpec(shape, lambda *_: (0,) * nd, pipeline_mode=pl.Buffered(1))


def _ffn_body(x_ref, gpre_ref, wg_ref, wu_ref, wd_ref, gpost_ref, o_ref):
    x = x_ref[...]
    y = _swiglu(_rms(x, gpre_ref[...]), wg_ref, wu_ref, wd_ref)
    o_ref[...] = x + 0.5 * _rms(y, gpost_ref[...])


def _row_tile(n):
    for t in range(ROW_TILE, 15, -16):
        if n % t == 0:
            return t
    return ROW_TILE


def _ffn(x, gpre, wg, wu, wd, gpost):
    n = x.shape[0]
    tile = _row_tile(n)
    row = pl.BlockSpec((tile, D_MODEL), lambda i: (i, 0))
    return pl.pallas_call(
        _ffn_body,
        grid=(pl.cdiv(n, tile),),
        in_specs=[row, _const_spec((1, D_MODEL)), _const_spec((D_MODEL, D_FF)), _const_spec((D_MODEL, D_FF)),
                  _const_spec((D_FF, D_MODEL)), _const_spec((1, D_MODEL))],
        out_specs=row,
        out_shape=jax.ShapeDtypeStruct((n, D_MODEL), F32),
        compiler_params=pltpu.CompilerParams(dimension_semantics=("arbitrary",), vmem_limit_bytes=VMEM_LIMIT),
        name="ffn1",
    )(x, gpre, wg, wu, wd, gpost)


ROW_COLS = 3 * ATT_W + IDX_DIM
ROW_COLS_PAD = 13 * LANE
COL_ROWS = 3 * ATT_W + N_IDX


def _mix_in_body(n_valid, h_ref, g_ref, wrow_ref, wcol_ref,
                 k_ref, v_ref, ki_ref, us_ref, kb_ref, kib_ref, vtb_ref, qt_ref, qit_ref, wt_ref):
    i = pl.program_id(0)
    u = _rms(h_ref[...], g_ref[...]).astype(BF16)
    z = jnp.dot(u, wrow_ref[...], preferred_element_type=F32)
    zt = lax.dot_general(wcol_ref[...], u, (((1,), (1,)), ((), ())), preferred_element_type=F32)
    k = z[:, 0:ATT_W]
    v = z[:, ATT_W:2 * ATT_W]
    ki = z[:, 3 * ATT_W:3 * ATT_W + IDX_DIM]
    k_ref[...] = k
    v_ref[...] = v
    us_ref[...] = z[:, 2 * ATT_W:3 * ATT_W]
    ki_ref[...] = ki
    row_ok = (i * ROW_TILE + lax.broadcasted_iota(I32, (ROW_TILE, 1), 0)) < n_valid
    col_ok = (i * ROW_TILE + lax.broadcasted_iota(I32, (1, ROW_TILE), 1)) < n_valid
    kb_ref[...] = jnp.where(row_ok, k, 0.0).astype(BF16)
    kib_ref[...] = jnp.where(row_ok, ki, 0.0).astype(BF16)
    zt = jnp.where(col_ok, zt, 0.0)
    qt_ref[...] = (zt[0:ATT_W] * (HEAD_DIM ** -0.5 * LOG2E)).astype(BF16)
    vtb_ref[...] = zt[2 * ATT_W:3 * ATT_W].astype(BF16)
    wt = zt[3 * ATT_W:3 * ATT_W + N_IDX] * ((IDX_DIM ** -0.5) * (N_IDX ** -0.5))
    wt_ref[...] = wt
    for h in range(N_IDX):
        qi_h = zt[ATT_W + h * IDX_DIM:ATT_W + (h + 1) * IDX_DIM]
        qit_ref[h * IDX_DIM:(h + 1) * IDX_DIM, :] = (qi_h * wt[h:h + 1, :]).astype(BF16)


def _mix_in(h, g, wrow, wcol, n_pad):
    n = h.shape[0]
    steps = n_pad // ROW_TILE
    assert steps == pl.cdiv(n, ROW_TILE)

    def row(width):
        return pl.BlockSpec((ROW_TILE, width), lambda i: (i, 0))

    col = lambda rows: pl.BlockSpec((rows, ROW_TILE), lambda i: (0, i))
    out_shape = (
        jax.ShapeDtypeStruct((n, ATT_W), F32), jax.ShapeDtypeStruct((n, ATT_W), F32),
        jax.ShapeDtypeStruct((n, IDX_DIM), F32), jax.ShapeDtypeStruct((n, SSM_W), F32),
        jax.ShapeDtypeStruct((n_pad, ATT_W), BF16), jax.ShapeDtypeStruct((n_pad, IDX_DIM), BF16),
        jax.ShapeDtypeStruct((n_pad // ROW_TILE, ATT_W, ROW_TILE), BF16),
        jax.ShapeDtypeStruct((ATT_W, n_pad), BF16), jax.ShapeDtypeStruct((ATT_W, n_pad), BF16),
        jax.ShapeDtypeStruct((N_IDX, n_pad), F32),
    )
    out_specs = (
        row(ATT_W), row(ATT_W), row(IDX_DIM), row(SSM_W), row(ATT_W), row(IDX_DIM),
        pl.BlockSpec((None, ATT_W, ROW_TILE), lambda i: (i, 0, 0)),
        col(ATT_W), col(ATT_W), col(N_IDX),
    )
    return pl.pallas_call(
        functools.partial(_mix_in_body, n),
        grid=(steps,),
        in_specs=[row(D_MODEL), _const_spec((1, D_MODEL)), _const_spec((D_MODEL, ROW_COLS_PAD)),
                  _const_spec((COL_ROWS, D_MODEL))],
        out_specs=out_specs,
        out_shape=out_shape,
        compiler_params=pltpu.CompilerParams(dimension_semantics=("arbitrary",), vmem_limit_bytes=VMEM_LIMIT),
        name="mix_in",
    )(h, g, wrow, wcol)


def _ordered_bits_to_f32(u):
    k = u ^ jnp.int32(-2 ** 31)
    b = k ^ ((k >> 31) & jnp.int32(0x7FFFFFFF))
    return lax.bitcast_convert_type(b, F32)


def _attn_body(topk, idx_bits, has_far,
               nch_ref, nfull_ref, toff_ref,
               qit_ref, wt_ref, qt_ref, vis_ref, ki_ref, k_ref, vt_ref, bias_ref,
               o_ref,
               st_ref, cand_ref, thr_ref, cgt_ref, cge_ref, clip_ref, qicat_ref, qbd_ref, sa_ref, sb_ref,
               bma_ref, bmb_ref,
               m_ref, acc_ref, cut_ref, rowf_ref, ot_ref):
    j = pl.program_id(1)
    nch = nch_ref[j]
    toff = toff_ref[j]
    kf = float(topk)
    bias_tile_max = (bias_ref.shape[1] - KEY_CHUNK) // LANE - 1

    for h in range(N_IDX):
        qicat_ref[:, h * TQ:(h + 1) * TQ] = qit_ref[h * IDX_DIM:(h + 1) * IDX_DIM, :]
    qbd_ref[...] = jnp.zeros(qbd_ref.shape, BF16)
    for a in range(N_PAIRS):
        qbd_ref[a, 0:HEAD_DIM, 0:TQ] = qt_ref[2 * a * HEAD_DIM:(2 * a + 1) * HEAD_DIM, :]
        qbd_ref[a, HEAD_DIM:2 * HEAD_DIM, TQ:2 * TQ] = qt_ref[(2 * a + 1) * HEAD_DIM:(2 * a + 2) * HEAD_DIM, :]

    def chunk_start(c):
        return pl.multiple_of(c * KEY_CHUNK, KEY_CHUNK)

    def chunk_rows(r0):
        return r0 + lax.broadcasted_iota(I32, (KEY_CHUNK, TQ), 0)

    vis = vis_ref[...]
    cand_ref[...] = jnp.full(cand_ref.shape, -jnp.inf, F32)
    w_pos = wt_ref[...] >= 0.0
    clip_ref[0:N_IDX, :] = jnp.where(w_pos, 0.0, -jnp.inf)
    clip_ref[N_IDX:2 * N_IDX, :] = jnp.where(w_pos, jnp.inf, 0.0)

    last_batch_row = ki_ref.shape[0] - CAND_BATCH

    def score_round(masked, i, carry):
        for g in range(SCORE_ROUND):
            r0 = pl.multiple_of((i * SCORE_ROUND + g) * CAND_BATCH, CAND_BATCH)
            rel = jnp.dot(ki_ref[pl.ds(jnp.minimum(r0, last_batch_row), CAND_BATCH), :], qicat_ref[...],
                          preferred_element_type=F32)
            score = jnp.zeros((CAND_BATCH, TQ), F32)
            for h in range(N_IDX):
                score = score + jnp.clip(rel[:, h * TQ:(h + 1) * TQ], clip_ref[h:h + 1, :],
                                         clip_ref[N_IDX + h:N_IDX + h + 1, :])
            if masked:
                score = jnp.where(r0 + lax.broadcasted_iota(I32, (CAND_BATCH, TQ), 0) < vis, score, -jnp.inf)
            st_ref[pl.ds(r0, CAND_BATCH), :] = score
        return carry

    round_chunks = SCORE_ROUND * CAND_BATCH // KEY_CHUNK
    full_rounds = nfull_ref[j] // round_chunks
    lax.fori_loop(0, full_rounds, functools.partial(score_round, False), 0)
    lax.fori_loop(full_rounds, (nch + round_chunks - 1) // round_chunks, functools.partial(score_round, True), 0)

    def merge_batch(b, carry):
        r0 = pl.multiple_of(b * CAND_BATCH, CAND_BATCH)
        x = [st_ref[pl.ds(r0 + 8 * r, 8), :] for r in range(CAND_DEPTH)]
        for lo, hi in _SORT_NETWORK:
            x[lo], x[hi] = jnp.maximum(x[lo], x[hi]), jnp.minimum(x[lo], x[hi])
        g0 = pl.multiple_of(lax.rem(b, CAND_GROUPS) * CAND_BATCH, CAND_BATCH)
        t = [jnp.maximum(cand_ref[pl.ds(g0 + 8 * r, 8), :], x[CAND_DEPTH - 1 - r]) for r in range(CAND_DEPTH)]
        d = CAND_DEPTH // 2
        while d >= 1:
            for r in range(CAND_DEPTH):
                if r & d == 0:
                    t[r], t[r + d] = jnp.maximum(t[r], t[r + d]), jnp.minimum(t[r], t[r + d])
            d //= 2
        for r in range(CAND_DEPTH):
            cand_ref[pl.ds(g0 + 8 * r, 8), :] = t[r]
        return carry

    lax.fori_loop(0, nch * (KEY_CHUNK // CAND_BATCH), merge_batch, 0)

    def reduce_rows(ref, n_chunks, fn, combine, init):
        reducer = jnp.sum if combine is jnp.add else jnp.max

        def body(c, acc):
            r0 = chunk_start(c)
            val = fn(ref[pl.ds(r0, KEY_CHUNK), :], r0)
            part = reducer(val.reshape(8, KEY_CHUNK // 64, 8, TQ), axis=1)
            return combine(acc, reducer(part, axis=0))

        acc = lax.fori_loop(0, n_chunks, body, jnp.full((8, TQ), init, F32), unroll=isinstance(n_chunks, int))
        return reducer(acc, axis=0, keepdims=True)

    def count(ref, n_chunks, indicator):
        return reduce_rows(ref, n_chunks, indicator, jnp.add, 0.0)

    def select(ref, n_chunks):
        def value_bit(i, res):
            trial = res | lax.shift_left(jnp.int32(1), 31 - i)
            thr_t = _ordered_bits_to_f32(trial)
            cnt = count(ref, n_chunks, lambda blk, r0: jnp.where(blk >= thr_t, 1.0, 0.0))
            return jnp.where(cnt >= kf, trial, res)

        res = lax.fori_loop(0, 32, value_bit, jnp.zeros((1, TQ), I32))
        res = jnp.where((res >> 23) == 0, jnp.int32(0x00800000), res)
        lo = _ordered_bits_to_f32(res)
        above = _ordered_bits_to_f32(res + 1)
        c_above = count(ref, n_chunks, lambda blk, r0: jnp.where(blk >= above, 1.0, 0.0))
        c_lo = count(ref, n_chunks, lambda blk, r0: jnp.where(blk >= lo, 1.0, 0.0))
        thr_ref[...] = lo
        cgt_ref[...] = c_above
        cge_ref[...] = c_lo

        def walking(state):
            return jnp.max(state[0]) > 0.0

        def walk(state):
            active, hi, c_hi = state
            v = reduce_rows(ref, n_chunks, lambda blk, r0: jnp.where(blk < hi, blk, -jnp.inf), jnp.maximum,
                            -jnp.inf)
            c_v = count(ref, n_chunks, lambda blk, r0: jnp.where(blk >= v, 1.0, 0.0))
            done = jnp.logical_and(active > 0.0, c_v >= kf)
            thr_ref[...] = jnp.where(done, v, thr_ref[...])
            cgt_ref[...] = jnp.where(done, c_hi, cgt_ref[...])
            cge_ref[...] = jnp.where(done, c_v, cge_ref[...])
            return jnp.where(done, 0.0, active), v, c_v

        surplus = jnp.where(jnp.logical_and(c_lo > kf, c_above < kf), 1.0, 0.0)
        lax.while_loop(walking, walk, (surplus, above, c_above))
        return lo

    cand_lo = select(cand_ref, cand_ref.shape[0] // KEY_CHUNK)
    kept_min = cand_ref[CAND_BATCH - 8:CAND_BATCH, :]
    for g in range(1, CAND_GROUPS):
        kept_min = jnp.maximum(kept_min, cand_ref[(g + 1) * CAND_BATCH - 8:(g + 1) * CAND_BATCH, :])
    unsafe = jnp.max(kept_min, axis=0, keepdims=True) >= cand_lo

    @pl.when(jnp.max(jnp.where(unsafe, 1.0, 0.0)) > 0.0)
    def _():
        select(st_ref, nch)

    thr = thr_ref[...]

    need = kf - cgt_ref[...]
    excess = cge_ref[...] - kf
    cut_ref[...] = jnp.full((1, TQ), F32_MAX, F32)
    rowf_ref[...] = lax.broadcasted_iota(I32, (KEY_CHUNK, TQ), 0).astype(F32)

    def chunk_rows_f(r0):
        return rowf_ref[...] + r0.astype(F32)

    tie_need = jnp.max(jnp.where(excess > 0.0, need, 0.0)).astype(I32)

    @pl.when(jnp.logical_and(tie_need > 0, tie_need <= TIE_SCAN_MAX))
    def _():
        def next_tied(cut_f):
            def body(c, acc):
                r0 = chunk_start(c)
                tied_row = jnp.where(st_ref[pl.ds(r0, KEY_CHUNK), :] == thr, chunk_rows_f(r0), F32_MAX)
                v = jnp.where(tied_row > cut_f, tied_row, F32_MAX)
                part = jnp.min(v.reshape(8, KEY_CHUNK // 64, 8, TQ), axis=1)
                return jnp.minimum(acc, jnp.min(part, axis=0))

            acc = lax.fori_loop(0, nch, body, jnp.full((8, TQ), F32_MAX, F32))
            return jnp.min(acc, axis=0, keepdims=True)

        def take(t, cut_f):
            return jnp.where(t.astype(F32) < need, next_tied(cut_f), cut_f)

        cut_f = lax.fori_loop(0, tie_need, take, jnp.full((1, TQ), -1.0, F32))
        cut_ref[...] = jnp.where(excess > 0.0, cut_f, F32_MAX)

    @pl.when(tie_need > TIE_SCAN_MAX)
    def _():
        def index_bit(i, cut):
            trial = cut | lax.shift_left(jnp.int32(1), idx_bits - 1 - i)
            cnt = count(st_ref, nch, lambda blk, r0: jnp.where(
                blk == thr, jnp.where(chunk_rows(r0) < trial, 1.0, 0.0), 0.0))
            return jnp.where(cnt < need, trial, cut)

        cut = lax.fori_loop(0, idx_bits, index_bit, jnp.zeros((1, TQ), I32))
        cut_ref[...] = jnp.where(excess > 0.0, cut.astype(F32), F32_MAX)

    cut_f = cut_ref[...]

    st_ref[pl.ds(chunk_start(nch), KEY_CHUNK), :] = jnp.full((KEY_CHUNK, TQ), -jnp.inf, F32)

    m_ref[...] = jnp.full(m_ref.shape, NEG, F32)
    acc_ref[...] = jnp.zeros(acc_ref.shape, F32)
    last = nch - 1
    ones_rows = jnp.ones((SUM_ROWS, KEY_CHUNK), BF16)

    def logits(blk, s_ref, bmax_ref, with_bias):
        kblk = k_ref[pl.ds(chunk_start(jnp.minimum(blk, last)), KEY_CHUNK), :]
        r0 = chunk_start(jnp.minimum(blk, nch))
        score = st_ref[pl.ds(r0, KEY_CHUNK), :]
        tie = jnp.where(chunk_rows_f(r0) <= cut_f, 0.0, NEG)
        msk = jnp.where(score > thr, 0.0, jnp.where(score == thr, tie, NEG))
        msk2 = jnp.concatenate([msk, msk], axis=1)
        bias_row0 = pl.multiple_of((jnp.clip(4 * blk + toff, -1, bias_tile_max) + 1) * LANE, LANE)
        for a in range(N_PAIRS):
            s2 = jnp.dot(kblk[:, a * 2 * HEAD_DIM:(a + 1) * 2 * HEAD_DIM], qbd_ref[a],
                         preferred_element_type=F32) + msk2
            if with_bias:
                s2 = s2 + bias_ref[a, pl.ds(bias_row0, KEY_CHUNK), :]
            s_ref[a] = s2
            bmax_ref[a:a + 1, :] = jnp.max(s2, axis=0, keepdims=True)

    def softmax_pv(blk, s_ref, bmax_ref):
        vblk = jnp.minimum(blk, last)
        for a in range(N_PAIRS):
            m_old = m_ref[a:a + 1, :]
            m_new = jnp.maximum(m_old, bmax_ref[a:a + 1, :])
            alpha = jnp.exp2(m_old - m_new)
            p = jnp.exp2(s_ref[a] - m_new).astype(BF16)
            v_ones = jnp.concatenate([vt_ref[vblk, a * 2 * HEAD_DIM:(a + 1) * 2 * HEAD_DIM, :], ones_rows], axis=0)
            acc_ref[a] = alpha * acc_ref[a] + jnp.dot(v_ones, p, preferred_element_type=F32)
            m_ref[a:a + 1, :] = m_new

    def block_pair(with_bias, i, carry):
        logits(2 * i + 1, sb_ref, bmb_ref, with_bias)
        softmax_pv(2 * i, sa_ref, bma_ref)
        logits(2 * i + 2, sa_ref, bma_ref, with_bias)
        softmax_pv(2 * i + 1, sb_ref, bmb_ref)
        return carry

    if has_far:
        far_pairs = jnp.minimum(jnp.maximum(3 - toff, 0) // 4, nch) // 2
        logits(0, sa_ref, bma_ref, False)
        lax.fori_loop(0, far_pairs, functools.partial(block_pair, False), 0)
    else:
        far_pairs = 0
    logits(2 * far_pairs, sa_ref, bma_ref, True)
    lax.fori_loop(far_pairs, (nch + 1) // 2, functools.partial(block_pair, True), 0)

    for a in range(N_PAIRS):
        for e in range(2):
            h = 2 * a + e
            num = acc_ref[a, e * HEAD_DIM:(e + 1) * HEAD_DIM, e * TQ:(e + 1) * TQ]
            den = acc_ref[a, 2 * HEAD_DIM:2 * HEAD_DIM + 1, e * TQ:(e + 1) * TQ]
            ot_ref[h * HEAD_DIM:(h + 1) * HEAD_DIM, :] = num / den
    o_ref[...] = ot_ref[...].T


def _attention(qit, wt, qt, vis, ki, k, vt, bias, sched, topk, resident, has_far):
    s_count, _, nq = qit.shape
    nk = k.shape[1]
    n_tiles = sched[0].shape[0]
    mode = dict(pipeline_mode=pl.Buffered(1)) if resident else {}
    idx_bits = max(1, int(nk - 1).bit_length())
    logit_buf = pltpu.VMEM((N_PAIRS, KEY_CHUNK, 2 * TQ), F32)
    in_specs = [
        pl.BlockSpec((None, ATT_W, TQ), lambda s, j, *_: (s, 0, j)),
        pl.BlockSpec((None, N_IDX, TQ), lambda s, j, *_: (s, 0, j)),
        pl.BlockSpec((None, ATT_W, TQ), lambda s, j, *_: (s, 0, j)),
        pl.BlockSpec((1, TQ), lambda s, j, *_: (0, j)),
        pl.BlockSpec((None, nk, IDX_DIM), lambda s, j, *_: (s, 0, 0), **mode),
        pl.BlockSpec((None, nk, ATT_W), lambda s, j, *_: (s, 0, 0), **mode),
        pl.BlockSpec((None, nk // KEY_CHUNK, ATT_W, KEY_CHUNK), lambda s, j, *_: (s, 0, 0, 0), **mode),
        pl.BlockSpec(bias.shape, lambda s, j, *_: (0, 0, 0), pipeline_mode=pl.Buffered(1)),
    ]
    grid_spec = pltpu.PrefetchScalarGridSpec(
        num_scalar_prefetch=3,
        grid=(s_count, n_tiles),
        in_specs=in_specs,
        out_specs=pl.BlockSpec((None, TQ, ATT_W), lambda s, j, *_: (s, j, 0)),
        scratch_shapes=[
            pltpu.VMEM((nk + KEY_CHUNK, TQ), F32),
            pltpu.VMEM((CAND_GROUPS * CAND_BATCH, TQ), F32),
            pltpu.VMEM((1, TQ), F32),
            pltpu.VMEM((1, TQ), F32),
            pltpu.VMEM((1, TQ), F32),
            pltpu.VMEM((2 * N_IDX, TQ), F32),
            pltpu.VMEM((IDX_DIM, N_IDX * TQ), BF16),
            pltpu.VMEM((N_PAIRS, 2 * HEAD_DIM, 2 * TQ), BF16),
            logit_buf, logit_buf,
            pltpu.VMEM((N_PAIRS, 2 * TQ), F32),
            pltpu.VMEM((N_PAIRS, 2 * TQ), F32),
            pltpu.VMEM((N_PAIRS, 2 * TQ), F32),
            pltpu.VMEM((N_PAIRS, 2 * HEAD_DIM + SUM_ROWS, 2 * TQ), F32),
            pltpu.VMEM((1, TQ), F32),
            pltpu.VMEM((KEY_CHUNK, TQ), F32),
            pltpu.VMEM((ATT_W, TQ), F32),
        ],
    )
    return pl.pallas_call(
        functools.partial(_attn_body, topk, idx_bits, has_far),
        grid_spec=grid_spec,
        out_shape=jax.ShapeDtypeStruct((s_count, nq, ATT_W), F32),
        compiler_params=pltpu.CompilerParams(dimension_semantics=("arbitrary", "arbitrary"),
                                             vmem_limit_bytes=VMEM_LIMIT),
        name="attn",
    )(*sched, qit, wt, qt, vis, ki, k, vt, bias)


def _decode_keys_body(ck_ref, cv_ref, cki_ref, k_ref, v_ref, ki_ref, ko_ref, vto_ref, kio_ref):
    past = ck_ref.shape[0]
    ts = k_ref.shape[0]
    pad = jnp.zeros((KEY_CHUNK - ts, ATT_W), F32)
    ko_ref[0:past, :] = ck_ref[...].astype(BF16)
    ko_ref[past:past + KEY_CHUNK, :] = jnp.concatenate([k_ref[...], pad], axis=0).astype(BF16)
    kio_ref[0:past, :] = cki_ref[...].astype(BF16)
    kio_ref[past:past + KEY_CHUNK, :] = jnp.concatenate([ki_ref[...], pad[:, 0:IDX_DIM]], axis=0).astype(BF16)
    for b in range(past // KEY_CHUNK):
        vto_ref[b] = cv_ref[b * KEY_CHUNK:(b + 1) * KEY_CHUNK, :].T.astype(BF16)
    vto_ref[past // KEY_CHUNK] = jnp.concatenate([v_ref[...], pad], axis=0).T.astype(BF16)


def _decode_keys(cache_k, cache_v, cache_ki, k_new, v_new, ki_new):
    ds, past, _ = cache_k.shape
    ts = k_new.shape[1]
    assert past % KEY_CHUNK == 0 and ts <= KEY_CHUNK
    nk = past + KEY_CHUNK
    per_stream = lambda rows, width: pl.BlockSpec((None, rows, width), lambda s: (s, 0, 0))
    return pl.pallas_call(
        _decode_keys_body,
        grid=(ds,),
        in_specs=[per_stream(past, ATT_W), per_stream(past, ATT_W), per_stream(past, IDX_DIM),
                  per_stream(ts, ATT_W), per_stream(ts, ATT_W), per_stream(ts, IDX_DIM)],
        out_specs=(per_stream(nk, ATT_W),
                   pl.BlockSpec((None, nk // KEY_CHUNK, ATT_W, KEY_CHUNK), lambda s: (s, 0, 0, 0)),
                   per_stream(nk, IDX_DIM)),
        out_shape=(jax.ShapeDtypeStruct((ds, nk, ATT_W), BF16),
                   jax.ShapeDtypeStruct((ds, nk // KEY_CHUNK, ATT_W, KEY_CHUNK), BF16),
                   jax.ShapeDtypeStruct((ds, nk, IDX_DIM), BF16)),
        compiler_params=pltpu.CompilerParams(dimension_semantics=("arbitrary",), vmem_limit_bytes=VMEM_LIMIT),
        name="decode_keys",
    )(cache_k, cache_v, cache_ki, k_new, v_new, ki_new)


def _s5_body(tt, us_ref, s0r_ref, s0i_ref, ar_ref, ai_ref, bbr_ref, bbi_ref, cr_ref, ci_ref, d_ref, wglu_ref,
             bglu_ref, out_ref, sr_out_ref, si_out_ref, str_ref, sti_ref, bre_ref, bim_ref, xr_ref, xi_ref):
    @pl.when(pl.program_id(1) == 0)
    def _():
        str_ref[...] = s0r_ref[...]
        sti_ref[...] = s0i_ref[...]

    u = us_ref[...]
    ub = u.astype(BF16)
    bre_ref[...] = jnp.dot(ub, bbr_ref[...], preferred_element_type=F32)
    bim_ref[...] = jnp.dot(ub, bbi_ref[...], preferred_element_type=F32)
    ar = ar_ref[...]
    ai = ai_ref[...]

    def step(t, carry):
        sr, si = carry
        nr = ar * sr - ai * si + bre_ref[pl.ds(t, 1), :]
        ni = ar * si + ai * sr + bim_ref[pl.ds(t, 1), :]
        xr_ref[pl.ds(t, 1), :] = nr
        xi_ref[pl.ds(t, 1), :] = ni
        return nr, ni

    sr, si = lax.fori_loop(0, tt, step, (str_ref[...], sti_ref[...]))
    str_ref[...] = sr
    sti_ref[...] = si
    sr_out_ref[...] = sr
    si_out_ref[...] = si
    y = (jnp.dot(xr_ref[...].astype(BF16), cr_ref[...], preferred_element_type=F32)
         - jnp.dot(xi_ref[...].astype(BF16), ci_ref[...], preferred_element_type=F32)
         + d_ref[...] * u)
    g = jax.nn.gelu(y)
    gate = jnp.dot(g.astype(BF16), wglu_ref[...], preferred_element_type=F32) + bglu_ref[...]
    out_ref[...] = g * jax.nn.sigmoid(gate)


def _s5(us, s0r, s0i, ar, ai, bbr, bbi, cr, ci, d, wglu, bglu, tt):
    s_count, t_len, _ = us.shape
    assert t_len % tt == 0
    state = pl.BlockSpec((None, 1, SSM_S), lambda s, t: (s, 0, 0))
    seq = pl.BlockSpec((None, tt, SSM_W), lambda s, t: (s, t, 0))
    return pl.pallas_call(
        functools.partial(_s5_body, tt),
        grid=(s_count, t_len // tt),
        in_specs=[seq, state, state, _const_spec((1, SSM_S)), _const_spec((1, SSM_S)),
                  _const_spec((SSM_W, SSM_S)), _const_spec((SSM_W, SSM_S)),
                  _const_spec((SSM_S, SSM_W)), _const_spec((SSM_S, SSM_W)),
                  _const_spec((1, SSM_W)), _const_spec((SSM_W, SSM_W)), _const_spec((1, SSM_W))],
        out_specs=(seq, state, state),
        out_shape=(jax.ShapeDtypeStruct((s_count, t_len, SSM_W), F32),
                   jax.ShapeDtypeStruct((s_count, 1, SSM_S), F32),
                   jax.ShapeDtypeStruct((s_count, 1, SSM_S), F32)),
        scratch_shapes=[pltpu.VMEM((1, SSM_S), F32), pltpu.VMEM((1, SSM_S), F32),
                        pltpu.VMEM((tt, SSM_S), F32), pltpu.VMEM((tt, SSM_S), F32),
                        pltpu.VMEM((tt, SSM_S), F32), pltpu.VMEM((tt, SSM_S), F32)],
        compiler_params=pltpu.CompilerParams(dimension_semantics=("arbitrary", "arbitrary"),
                                             vmem_limit_bytes=VMEM_LIMIT),
        name="s5",
    )(us, s0r, s0i, ar, ai, bbr, bbi, cr, ci, d, wglu, bglu)


def _mix_out_body(att_ref, ssm_ref, h_ref, woa_ref, wos_ref, gmix_ref, gpre_ref, wg_ref, wu_ref, wd_ref, gpost_ref,
                  o_ref):
    m = (jnp.dot(att_ref[...].astype(BF16), woa_ref[...], preferred_element_type=F32)
         + jnp.dot(ssm_ref[...].astype(BF16), wos_ref[...], preferred_element_type=F32))
    h2 = h_ref[...] + _rms(m, gmix_ref[...])
    y = _swiglu(_rms(h2, gpre_ref[...]), wg_ref, wu_ref, wd_ref)
    o_ref[...] = h2 + 0.5 * _rms(y, gpost_ref[...])


def _mix_out(att, ssm, h, woa, wos, gmix, gpre, wg, wu, wd, gpost):
    n = h.shape[0]
    tile = _row_tile(n)
    row = lambda width: pl.BlockSpec((tile, width), lambda i: (i, 0))
    return pl.pallas_call(
        _mix_out_body,
        grid=(pl.cdiv(n, tile),),
        in_specs=[row(ATT_W), row(SSM_W), row(D_MODEL), _const_spec((ATT_W, D_MODEL)),
                  _const_spec((SSM_W, D_MODEL)), _const_spec((1, D_MODEL)), _const_spec((1, D_MODEL)),
                  _const_spec((D_MODEL, D_FF)), _const_spec((D_MODEL, D_FF)), _const_spec((D_FF, D_MODEL)),
                  _const_spec((1, D_MODEL))],
        out_specs=row(D_MODEL),
        out_shape=jax.ShapeDtypeStruct((n, D_MODEL), F32),
        compiler_params=pltpu.CompilerParams(dimension_semantics=("arbitrary",), vmem_limit_bytes=VMEM_LIMIT),
        name="mix_out_ffn2",
    )(att, ssm, h, woa, wos, gmix, gpre, wg, wu, wd, gpost)


def _rel_bucket_np(rel):
    half = NUM_BUCKETS // 2
    max_exact = half // 2
    n = np.abs(rel).astype(np.int64)
    nf = np.maximum(n, 1).astype(np.float64)
    large = max_exact + (np.log(nf / max_exact) / math.log(MAX_DISTANCE / max_exact)
                         * (half - max_exact)).astype(np.int64)
    large = np.minimum(large, half - 1)
    return np.where(rel > 0, half, 0) + np.where(n < max_exact, n, large)


def _pair_bias(rel_bias, n_rows, c0):
    length = n_rows + TQ
    rel = np.arange(length) - (TQ - 1) - LANE - c0
    table = (rel_bias - rel_bias[NUM_BUCKETS // 2 - 1][None, :]) * LOG2E
    t1d = table[_rel_bucket_np(rel)].T
    x = jnp.tile(t1d, (1, TQ))[:, :TQ * (length - 1)].reshape(N_HEADS, TQ, length - 1)
    b = jnp.transpose(x[:, :, TQ - 1:TQ - 1 + n_rows], (0, 2, 1)).reshape(N_PAIRS, 2, n_rows, TQ)
    return jnp.concatenate([b[:, 0], b[:, 1]], axis=-1).astype(F32)


def _prompt_schedule(n, n_pad):
    pos = np.arange(n_pad)
    vis = np.where(pos < N_META, N_META, N_META + CHUNK * ((pos - N_META) // CHUNK + 1))
    vis = np.where(pos < n, np.minimum(vis, n), 0).astype(np.int32)
    n_tiles = -(-n // TQ)
    vmax = vis[:n_tiles * TQ].reshape(n_tiles, TQ).max(axis=1)
    nch = -(-vmax // KEY_CHUNK)
    nfull = vis[:n_tiles * TQ].reshape(n_tiles, TQ).min(axis=1) // KEY_CHUNK
    toff = PROMPT_BIAS_C0 // LANE - np.arange(n_tiles)
    assert ((nch - 1) * (KEY_CHUNK // LANE) + toff).max() <= (PROMPT_BIAS_ROWS - KEY_CHUNK) // LANE - 1
    sched = tuple(jnp.asarray(a, I32) for a in (nch, nfull, toff))
    has_far = bool((np.minimum(np.maximum(3 - toff, 0) // 4, nch) >= 2).any())
    return sched, jnp.asarray(vis[None, :]), has_far


def _sample_schedule(past, ts):
    nk = past + ts
    sched = tuple(jnp.asarray([v], I32) for v in (-(-nk // KEY_CHUNK), 0, 0))
    vis = jnp.asarray(np.where(np.arange(TQ) < ts, nk, 0)[None, :], I32)
    return sched, vis, False


def _s5_params(lam_re, lam_im, log_step, b_re, b_im, c_re, c_im, d_skip):
    dt = jnp.exp(log_step)[:, None]
    mag = jnp.exp(lam_re * dt)
    ab_re, ab_im = mag * jnp.cos(lam_im * dt), mag * jnp.sin(lam_im * dt)
    nr, ni = ab_re - 1.0, ab_im
    den = lam_re * lam_re + lam_im * lam_im
    f_re, f_im = (nr * lam_re + ni * lam_im) / den, (ni * lam_re - nr * lam_im) / den
    bb_re = f_re[..., None] * b_re - f_im[..., None] * b_im
    bb_im = f_re[..., None] * b_im + f_im[..., None] * b_re
    eye = jnp.eye(SSM_GROUPS, dtype=F32)
    bd_in = lambda w: jnp.einsum('gpc,gh->gchp', w, eye).reshape(SSM_W, SSM_S).astype(BF16)
    bd_out = lambda w: jnp.einsum('gcp,gh->gphc', w, eye).reshape(SSM_S, SSM_W).astype(BF16)
    return (ab_re.reshape(1, SSM_S), ab_im.reshape(1, SSM_S), bd_in(bb_re), bd_in(bb_im),
            bd_out(c_re), bd_out(c_im), d_skip.reshape(1, SSM_W))


def kernel(x_prompt, x_sample, cache_k, cache_v, cache_kidx, state_ssm_re, state_ssm_im, meta_tokens, rel_bias,
           ffn1_g_pre, ffn1_w_gate, ffn1_w_up, ffn1_w_down, ffn1_g_post, mix_g_pre, w_in, w_out, mix_g_post,
           lam_re, lam_im, log_step, b_re, b_im, c_re, c_im, d_skip, w_glu, b_glu,
           ffn2_g_pre, ffn2_w_gate, ffn2_w_up, ffn2_w_down, ffn2_g_post):
    depth = ffn1_g_pre.shape[0]
    assert depth == 1
    bp, seq, _ = x_prompt.shape
    assert bp == 1
    ds, ts, _ = x_sample.shape
    past = cache_k.shape[2]
    n_p = N_META + seq
    n_p_pad = pl.cdiv(n_p, ROW_TILE) * ROW_TILE
    n_s = ds * ts
    assert n_s % ROW_TILE == 0 and ts <= TQ
    nk_s_pad = pl.cdiv(past + ts, KEY_CHUNK) * KEY_CHUNK
    l = 0

    row2 = lambda g: g[l].reshape(1, -1)
    w1 = (row2(ffn1_g_pre), ffn1_w_gate[l].astype(BF16), ffn1_w_up[l].astype(BF16), ffn1_w_down[l].astype(BF16),
          row2(ffn1_g_post))
    w2 = (row2(ffn2_g_pre), ffn2_w_gate[l].astype(BF16), ffn2_w_up[l].astype(BF16), ffn2_w_down[l].astype(BF16),
          row2(ffn2_g_post))
    wi = w_in[l]
    o_q, o_k, o_v, o_qi, o_ki, o_wi, o_us = np.cumsum((0, ATT_W, ATT_W, ATT_W, N_IDX * IDX_DIM, IDX_DIM, N_IDX))
    sl = lambda o, w: wi[:, o:o + w]
    wrow = jnp.concatenate([sl(o_k, ATT_W), sl(o_v, ATT_W), sl(o_us, SSM_W), sl(o_ki, IDX_DIM),
                            jnp.zeros((D_MODEL, ROW_COLS_PAD - ROW_COLS), F32)], axis=1).astype(BF16)
    wcol = jnp.concatenate([sl(o_q, ATT_W), sl(o_qi, ATT_W), sl(o_v, ATT_W), sl(o_wi, N_IDX)], axis=1).T.astype(BF16)
    woa, wos = w_out[l][:ATT_W].astype(BF16), w_out[l][ATT_W:].astype(BF16)
    s5w = _s5_params(lam_re[l], lam_im[l], log_step[l], b_re[l], b_im[l], c_re[l], c_im[l], d_skip[l])
    s5w = s5w + (w_glu[l].astype(BF16), b_glu[l].reshape(1, SSM_W))
    bias_tab = rel_bias.astype(F32)

    xp = jnp.concatenate([meta_tokens.astype(F32), x_prompt[0]], axis=0)
    hp = _ffn(xp, *w1)
    kp, vp, kip, usp, kbp, kibp, vtbp, qtp, qitp, wtp = _mix_in(hp, row2(mix_g_pre), wrow, wcol, n_p_pad)
    sched_p, vis_p, far_p = _prompt_schedule(n_p, n_p_pad)
    att_p = _attention(qitp[None], wtp[None], qtp[None], vis_p, kibp[None], kbp[None], vtbp[None],
                       _pair_bias(bias_tab, PROMPT_BIAS_ROWS, PROMPT_BIAS_C0), sched_p,
                       min(TOPK_MAX, seq // 4), True, far_p)[0]
    zero_state = jnp.zeros((1, 1, SSM_S), F32)
    tt_p = max(t for t in range(8, 513, 8) if n_p % t == 0)
    ssm_p, srp, sip = _s5(usp[None], zero_state, zero_state, *s5w, tt_p)
    yp = _mix_out(att_p, ssm_p[0], hp, woa, wos, row2(mix_g_post), *w2)

    hs = _ffn(x_sample.reshape(n_s, D_MODEL), *w1)
    ks, vs, kis, uss, _, _, _, qts, qits, wts = _mix_in(hs, row2(mix_g_pre), wrow, wcol, n_s)
    k_all, vt_all, ki_all = _decode_keys(
        cache_k[l].reshape(ds, past, ATT_W), cache_v[l].reshape(ds, past, ATT_W), cache_kidx[l],
        ks.reshape(ds, ts, ATT_W), vs.reshape(ds, ts, ATT_W), kis.reshape(ds, ts, IDX_DIM))
    assert k_all.shape[1] == nk_s_pad
    lanes = lambda a: jnp.pad(jnp.transpose(a.reshape(a.shape[0], ds, ts), (1, 0, 2)), ((0, 0), (0, 0), (0, TQ - ts)))
    sched_s, vis_s, far_s = _sample_schedule(past, ts)
    att_s = _attention(lanes(qits), lanes(wts), lanes(qts), vis_s, ki_all, k_all, vt_all,
                       _pair_bias(bias_tab, LANE + nk_s_pad, past), sched_s,
                       min(TOPK_MAX, (past + ts) // 4), False, far_s)
    att_s = att_s[:, :ts].reshape(n_s, ATT_W)
    ssm_s, srs, sis = _s5(uss.reshape(ds, ts, SSM_W), state_ssm_re[l].reshape(ds, 1, SSM_S),
                          state_ssm_im[l].reshape(ds, 1, SSM_S), *s5w, ts)
    ys = _mix_out(att_s, ssm_s.reshape(n_s, SSM_W), hs, woa, wos, row2(mix_g_post), *w2)

    heads = lambda a, b, t: a.reshape(1, b, t, N_HEADS, HEAD_DIM)
    state = lambda a, b: a.reshape(1, b, SSM_GROUPS, SSM_P)
    return (yp[N_META:][None], ys.reshape(ds, ts, D_MODEL),
            heads(kp, 1, n_p), heads(vp, 1, n_p), kip.reshape(1, 1, n_p, IDX_DIM), state(srp, 1), state(sip, 1),
            heads(ks, ds, ts), heads(vs, ds, ts), kis.reshape(1, ds, ts, IDX_DIM), state(srs, ds), state(sis, ds))
```

```python
import functools
import math

import numpy as np
import jax
import jax.numpy as jnp
from jax import lax
from jax.experimental import pallas as pl
from jax.experimental.pallas import tpu as pltpu

F32, BF16, I32 = jnp.float32, jnp.bfloat16, jnp.int32

D_MODEL = 1024
CHUNK = 64
N_META = 16
N_HEADS = 8
HEAD_DIM = 64
ATT_W = N_HEADS * HEAD_DIM
N_IDX = 8
IDX_DIM = 64
TOPK_MAX = 256
SSM_GROUPS = 32
SSM_GC = 16
SSM_W = SSM_GROUPS * SSM_GC
SSM_P = 64
SSM_S = SSM_GROUPS * SSM_P
D_FF = 2816
NUM_BUCKETS = 32
MAX_DISTANCE = 128
EPS = 1e-6

LANE = 128
VMEM_LIMIT = 62 * 1024 * 1024
ROW_TILE = 512
TQ = LANE
KEY_CHUNK = 512
PROMPT_BIAS_C0 = 512
PROMPT_BIAS_ROWS = 1280
NEG = -1e30
LOG2E = math.log2(math.e)
F32_MAX = float(np.finfo(np.float32).max)
N_PAIRS = N_HEADS // 2
SUM_ROWS = 16
CAND_DEPTH = 16
CAND_BATCH = 8 * CAND_DEPTH
CAND_GROUPS = 8
SCORE_ROUND = 8
TIE_SCAN_MAX = 8


def _batcher_network(n):
    pairs, p = [], 1
    while p < n:
        k = p
        while k >= 1:
            for j in range(k % p, n - k, 2 * k):
                for i in range(min(k, n - j - k)):
                    if (i + j) // (2 * p) == (i + j + k) // (2 * p):
                        pairs.append((i + j, i + j + k))
            k //= 2
        p *= 2
    return pairs


_SORT_NETWORK = _batcher_network(CAND_DEPTH)


def _rms(x, g):
    return x * lax.rsqrt(jnp.mean(x * x, axis=-1, keepdims=True) + EPS) * g


def _swiglu(xn, wg_ref, wu_ref, wd_ref):
    xb = xn.astype(BF16)
    a = jnp.dot(xb, wg_ref[...], preferred_element_type=F32)
    b = jnp.dot(xb, wu_ref[...], preferred_element_type=F32)
    hidden = (a * jax.nn.sigmoid(a) * b).astype(BF16)
    return jnp.dot(hidden, wd_ref[...], preferred_element_type=F32)


def _const_spec(shape):
    nd = len(shape)
    return pl.BlockSpec(shape, lambda *_: (0,) * nd, pipeline_mode=pl.Buffered(1))


def _ffn_body(x_ref, gpre_ref, wg_ref, wu_ref, wd_ref, gpost_ref, o_ref):
    x = x_ref[...]
    y = _swiglu(_rms(x, gpre_ref[...]), wg_ref, wu_ref, wd_ref)
    o_ref[...] = x + 0.5 * _rms(y, gpost_ref[...])


def _row_tile(n):
    for t in range(ROW_TILE, 15, -16):
        if n % t == 0:
            return t
    return ROW_TILE


def _ffn(x, gpre, wg, wu, wd, gpost):
    n = x.shape[0]
    tile = _row_tile(n)
    row = pl.BlockSpec((tile, D_MODEL), lambda i: (i, 0))
    return pl.pallas_call(
        _ffn_body,
        grid=(pl.cdiv(n, tile),),
        in_specs=[row, _const_spec((1, D_MODEL)), _const_spec((D_MODEL, D_FF)), _const_spec((D_MODEL, D_FF)),
                  _const_spec((D_FF, D_MODEL)), _const_spec((1, D_MODEL))],
        out_specs=row,
        out_shape=jax.ShapeDtypeStruct((n, D_MODEL), F32),
        compiler_params=pltpu.CompilerParams(dimension_semantics=("arbitrary",), vmem_limit_bytes=VMEM_LIMIT),
        name="ffn1",
    )(x, gpre, wg, wu, wd, gpost)


ROW_COLS = 3 * ATT_W + IDX_DIM
ROW_COLS_PAD = 13 * LANE
COL_ROWS = 3 * ATT_W + N_IDX


def _mix_in_body(n_valid, h_ref, g_ref, wrow_ref, wcol_ref,
                 k_ref, v_ref, ki_ref, us_ref, kb_ref, kib_ref, vtb_ref, qt_ref, qit_ref, wt_ref):
    i = pl.program_id(0)
    u = _rms(h_ref[...], g_ref[...]).astype(BF16)
    z = jnp.dot(u, wrow_ref[...], preferred_element_type=F32)
    zt = lax.dot_general(wcol_ref[...], u, (((1,), (1,)), ((), ())), preferred_element_type=F32)
    k = z[:, 0:ATT_W]
    v = z[:, ATT_W:2 * ATT_W]
    ki = z[:, 3 * ATT_W:3 * ATT_W + IDX_DIM]
    k_ref[...] = k
    v_ref[...] = v
    us_ref[...] = z[:, 2 * ATT_W:3 * ATT_W]
    ki_ref[...] = ki
    row_ok = (i * ROW_TILE + lax.broadcasted_iota(I32, (ROW_TILE, 1), 0)) < n_valid
    col_ok = (i * ROW_TILE + lax.broadcasted_iota(I32, (1, ROW_TILE), 1)) < n_valid
    kb_ref[...] = jnp.where(row_ok, k, 0.0).astype(BF16)
    kib_ref[...] = jnp.where(row_ok, ki, 0.0).astype(BF16)
    zt = jnp.where(col_ok, zt, 0.0)
    qt_ref[...] = (zt[0:ATT_W] * (HEAD_DIM ** -0.5 * LOG2E)).astype(BF16)
    vtb_ref[...] = zt[2 * ATT_W:3 * ATT_W].astype(BF16)
    wt = zt[3 * ATT_W:3 * ATT_W + N_IDX] * ((IDX_DIM ** -0.5) * (N_IDX ** -0.5))
    wt_ref[...] = wt
    for h in range(N_IDX):
        qi_h = zt[ATT_W + h * IDX_DIM:ATT_W + (h + 1) * IDX_DIM]
        qit_ref[h * IDX_DIM:(h + 1) * IDX_DIM, :] = (qi_h * wt[h:h + 1, :]).astype(BF16)


def _mix_in(h, g, wrow, wcol, n_pad):
    n = h.shape[0]
    steps = n_pad // ROW_TILE
    assert steps == pl.cdiv(n, ROW_TILE)

    def row(width):
        return pl.BlockSpec((ROW_TILE, width), lambda i: (i, 0))

    col = lambda rows: pl.BlockSpec((rows, ROW_TILE), lambda i: (0, i))
    out_shape = (
        jax.ShapeDtypeStruct((n, ATT_W), F32), jax.ShapeDtypeStruct((n, ATT_W), F32),
        jax.ShapeDtypeStruct((n, IDX_DIM), F32), jax.ShapeDtypeStruct((n, SSM_W), F32),
        jax.ShapeDtypeStruct((n_pad, ATT_W), BF16), jax.ShapeDtypeStruct((n_pad, IDX_DIM), BF16),
        jax.ShapeDtypeStruct((n_pad // ROW_TILE, ATT_W, ROW_TILE), BF16),
        jax.ShapeDtypeStruct((ATT_W, n_pad), BF16), jax.ShapeDtypeStruct((ATT_W, n_pad), BF16),
        jax.ShapeDtypeStruct((N_IDX, n_pad), F32),
    )
    out_specs = (
        row(ATT_W), row(ATT_W), row(IDX_DIM), row(SSM_W), row(ATT_W), row(IDX_DIM),
        pl.BlockSpec((None, ATT_W, ROW_TILE), lambda i: (i, 0, 0)),
        col(ATT_W), col(ATT_W), col(N_IDX),
    )
    return pl.pallas_call(
        functools.partial(_mix_in_body, n),
        grid=(steps,),
        in_specs=[row(D_MODEL), _const_spec((1, D_MODEL)), _const_spec((D_MODEL, ROW_COLS_PAD)),
                  _const_spec((COL_ROWS, D_MODEL))],
        out_specs=out_specs,
        out_shape=out_shape,
        compiler_params=pltpu.CompilerParams(dimension_semantics=("arbitrary",), vmem_limit_bytes=VMEM_LIMIT),
        name="mix_in",
    )(h, g, wrow, wcol)


def _ordered_bits_to_f32(u):
    k = u ^ jnp.int32(-2 ** 31)
    b = k ^ ((k >> 31) & jnp.int32(0x7FFFFFFF))
    return lax.bitcast_convert_type(b, F32)


def _attn_body(topk, idx_bits, has_far,
               nch_ref, nfull_ref, toff_ref,
               qit_ref, wt_ref, qt_ref, vis_ref, ki_ref, k_ref, vt_ref, bias_ref,
               o_ref,
               st_ref, cand_ref, thr_ref, cgt_ref, cge_ref, clip_ref, qicat_ref, qbd_ref, sa_ref, sb_ref,
               bma_ref, bmb_ref,
               m_ref, acc_ref, cut_ref, rowf_ref, ot_ref):
    j = pl.program_id(1)
    nch = nch_ref[j]
    toff = toff_ref[j]
    kf = float(topk)
    bias_tile_max = (bias_ref.shape[1] - KEY_CHUNK) // LANE - 1

    for h in range(N_IDX):
        qicat_ref[:, h * TQ:(h + 1) * TQ] = qit_ref[h * IDX_DIM:(h + 1) * IDX_DIM, :]
    qbd_ref[...] = jnp.zeros(qbd_ref.shape, BF16)
    for a in range(N_PAIRS):
        qbd_ref[a, 0:HEAD_DIM, 0:TQ] = qt_ref[2 * a * HEAD_DIM:(2 * a + 1) * HEAD_DIM, :]
        qbd_ref[a, HEAD_DIM:2 * HEAD_DIM, TQ:2 * TQ] = qt_ref[(2 * a + 1) * HEAD_DIM:(2 * a + 2) * HEAD_DIM, :]

    def chunk_start(c):
        return pl.multiple_of(c * KEY_CHUNK, KEY_CHUNK)

    def chunk_rows(r0):
        return r0 + lax.broadcasted_iota(I32, (KEY_CHUNK, TQ), 0)

    vis = vis_ref[...]
    cand_ref[...] = jnp.full(cand_ref.shape, -jnp.inf, F32)
    w_pos = wt_ref[...] >= 0.0
    clip_ref[0:N_IDX, :] = jnp.where(w_pos, 0.0, -jnp.inf)
    clip_ref[N_IDX:2 * N_IDX, :] = jnp.where(w_pos, jnp.inf, 0.0)

    last_batch_row = ki_ref.shape[0] - CAND_BATCH

    def score_round(masked, i, carry):
        for g in range(SCORE_ROUND):
            r0 = pl.multiple_of((i * SCORE_ROUND + g) * CAND_BATCH, CAND_BATCH)
            rel = jnp.dot(ki_ref[pl.ds(jnp.minimum(r0, last_batch_row), CAND_BATCH), :], qicat_ref[...],
                          preferred_element_type=F32)
            score = jnp.zeros((CAND_BATCH, TQ), F32)
            for h in range(N_IDX):
                score = score + jnp.clip(rel[:, h * TQ:(h + 1) * TQ], clip_ref[h:h + 1, :],
                                         clip_ref[N_IDX + h:N_IDX + h + 1, :])
            if masked:
                score = jnp.where(r0 + lax.broadcasted_iota(I32, (CAND_BATCH, TQ), 0) < vis, score, -jnp.inf)
            st_ref[pl.ds(r0, CAND_BATCH), :] = score
        return carry

    round_chunks = SCORE_ROUND * CAND_BATCH // KEY_CHUNK
    full_rounds = nfull_ref[j] // round_chunks
    lax.fori_loop(0, full_rounds, functools.partial(score_round, False), 0)
    lax.fori_loop(full_rounds, (nch + round_chunks - 1) // round_chunks, functools.partial(score_round, True), 0)

    def merge_batch(b, carry):
        r0 = pl.multiple_of(b * CAND_BATCH, CAND_BATCH)
        x = [st_ref[pl.ds(r0 + 8 * r, 8), :] for r in range(CAND_DEPTH)]
        for lo, hi in _SORT_NETWORK:
            x[lo], x[hi] = jnp.maximum(x[lo], x[hi]), jnp.minimum(x[lo], x[hi])
        g0 = pl.multiple_of(lax.rem(b, CAND_GROUPS) * CAND_BATCH, CAND_BATCH)
        t = [jnp.maximum(cand_ref[pl.ds(g0 + 8 * r, 8), :], x[CAND_DEPTH - 1 - r]) for r in range(CAND_DEPTH)]
        d = CAND_DEPTH // 2
        while d >= 1:
            for r in range(CAND_DEPTH):
                if r & d == 0:
                    t[r], t[r + d] = jnp.maximum(t[r], t[r + d]), jnp.minimum(t[r], t[r + d])
            d //= 2
        for r in range(CAND_DEPTH):
            cand_ref[pl.ds(g0 + 8 * r, 8), :] = t[r]
        return carry

    lax.fori_loop(0, nch * (KEY_CHUNK // CAND_BATCH), merge_batch, 0)

    def reduce_rows(ref, n_chunks, fn, combine, init):
        reducer = jnp.sum if combine is jnp.add else jnp.max

        def body(c, acc):
            r0 = chunk_start(c)
            val = fn(ref[pl.ds(r0, KEY_CHUNK), :], r0)
            part = reducer(val.reshape(8, KEY_CHUNK // 64, 8, TQ), axis=1)
            return combine(acc, reducer(part, axis=0))

        acc = lax.fori_loop(0, n_chunks, body, jnp.full((8, TQ), init, F32), unroll=isinstance(n_chunks, int))
        return reducer(acc, axis=0, keepdims=True)

    def count(ref, n_chunks, indicator):
        return reduce_rows(ref, n_chunks, indicator, jnp.add, 0.0)

    def select(ref, n_chunks):
        def value_bit(i, res):
            trial = res | lax.shift_left(jnp.int32(1), 31 - i)
            thr_t = _ordered_bits_to_f32(trial)
            cnt = count(ref, n_chunks, lambda blk, r0: jnp.where(blk >= thr_t, 1.0, 0.0))
            return jnp.where(cnt >= kf, trial, res)

        res = lax.fori_loop(0, 32, value_bit, jnp.zeros((1, TQ), I32))
        res = jnp.where((res >> 23) == 0, jnp.int32(0x00800000), res)
        lo = _ordered_bits_to_f32(res)
        above = _ordered_bits_to_f32(res + 1)
        c_above = count(ref, n_chunks, lambda blk, r0: jnp.where(blk >= above, 1.0, 0.0))
        c_lo = count(ref, n_chunks, lambda blk, r0: jnp.where(blk >= lo, 1.0, 0.0))
        thr_ref[...] = lo
        cgt_ref[...] = c_above
        cge_ref[...] = c_lo

        def walking(state):
            return jnp.max(state[0]) > 0.0

        def walk(state):
            active, hi, c_hi = state
            v = reduce_rows(ref, n_chunks, lambda blk, r0: jnp.where(blk < hi, blk, -jnp.inf), jnp.maximum,
                            -jnp.inf)
            c_v = count(ref, n_chunks, lambda blk, r0: jnp.where(blk >= v, 1.0, 0.0))
            done = jnp.logical_and(active > 0.0, c_v >= kf)
            thr_ref[...] = jnp.where(done, v, thr_ref[...])
            cgt_ref[...] = jnp.where(done, c_hi, cgt_ref[...])
            cge_ref[...] = jnp.where(done, c_v, cge_ref[...])
            return jnp.where(done, 0.0, active), v, c_v

        surplus = jnp.where(jnp.logical_and(c_lo > kf, c_above < kf), 1.0, 0.0)
        lax.while_loop(walking, walk, (surplus, above, c_above))
        return lo

    cand_lo = select(cand_ref, cand_ref.shape[0] // KEY_CHUNK)
    kept_min = cand_ref[CAND_BATCH - 8:CAND_BATCH, :]
    for g in range(1, CAND_GROUPS):
        kept_min = jnp.maximum(kept_min, cand_ref[(g + 1) * CAND_BATCH - 8:(g + 1) * CAND_BATCH, :])
    unsafe = jnp.max(kept_min, axis=0, keepdims=True) >= cand_lo

    @pl.when(jnp.max(jnp.where(unsafe, 1.0, 0.0)) > 0.0)
    def _():
        select(st_ref, nch)

    thr = thr_ref[...]

    need = kf - cgt_ref[...]
    excess = cge_ref[...] - kf
    cut_ref[...] = jnp.full((1, TQ), F32_MAX, F32)
    rowf_ref[...] = lax.broadcasted_iota(I32, (KEY_CHUNK, TQ), 0).astype(F32)

    def chunk_rows_f(r0):
        return rowf_ref[...] + r0.astype(F32)

    tie_need = jnp.max(jnp.where(excess > 0.0, need, 0.0)).astype(I32)

    @pl.when(jnp.logical_and(tie_need > 0, tie_need <= TIE_SCAN_MAX))
    def _():
        def next_tied(cut_f):
            def body(c, acc):
                r0 = chunk_start(c)
                tied_row = jnp.where(st_ref[pl.ds(r0, KEY_CHUNK), :] == thr, chunk_rows_f(r0), F32_MAX)
                v = jnp.where(tied_row > cut_f, tied_row, F32_MAX)
                part = jnp.min(v.reshape(8, KEY_CHUNK // 64, 8, TQ), axis=1)
                return jnp.minimum(acc, jnp.min(part, axis=0))

            acc = lax.fori_loop(0, nch, body, jnp.full((8, TQ), F32_MAX, F32))
            return jnp.min(acc, axis=0, keepdims=True)

        def take(t, cut_f):
            return jnp.where(t.astype(F32) < need, next_tied(cut_f), cut_f)

        cut_f = lax.fori_loop(0, tie_need, take, jnp.full((1, TQ), -1.0, F32))
        cut_ref[...] = jnp.where(excess > 0.0, cut_f, F32_MAX)

    @pl.when(tie_need > TIE_SCAN_MAX)
    def _():
        def index_bit(i, cut):
            trial = cut | lax.shift_left(jnp.int32(1), idx_bits - 1 - i)
            cnt = count(st_ref, nch, lambda blk, r0: jnp.where(
                blk == thr, jnp.where(chunk_rows(r0) < trial, 1.0, 0.0), 0.0))
            return jnp.where(cnt < need, trial, cut)

        cut = lax.fori_loop(0, idx_bits, index_bit, jnp.zeros((1, TQ), I32))
        cut_ref[...] = jnp.where(excess > 0.0, cut.astype(F32), F32_MAX)

    cut_f = cut_ref[...]

    st_ref[pl.ds(chunk_start(nch), KEY_CHUNK), :] = jnp.full((KEY_CHUNK, TQ), -jnp.inf, F32)

    m_ref[...] = jnp.full(m_ref.shape, NEG, F32)
    acc_ref[...] = jnp.zeros(acc_ref.shape, F32)
    last = nch - 1
    ones_rows = jnp.ones((SUM_ROWS, KEY_CHUNK), BF16)

    def logits(blk, s_ref, bmax_ref, with_bias):
        kblk = k_ref[pl.ds(chunk_start(jnp.minimum(blk, last)), KEY_CHUNK), :]
        r0 = chunk_start(jnp.minimum(blk, nch))
        score = st_ref[pl.ds(r0, KEY_CHUNK), :]
        tie = jnp.where(chunk_rows_f(r0) <= cut_f, 0.0, NEG)
        msk = jnp.where(score > thr, 0.0, jnp.where(score == thr, tie, NEG))
        msk2 = jnp.concatenate([msk, msk], axis=1)
        bias_row0 = pl.multiple_of((jnp.clip(4 * blk + toff, -1, bias_tile_max) + 1) * LANE, LANE)
        for a in range(N_PAIRS):
            s2 = jnp.dot(kblk[:, a * 2 * HEAD_DIM:(a + 1) * 2 * HEAD_DIM], qbd_ref[a],
                         preferred_element_type=F32) + msk2
            if with_bias:
                s2 = s2 + bias_ref[a, pl.ds(bias_row0, KEY_CHUNK), :]
            s_ref[a] = s2
            bmax_ref[a:a + 1, :] = jnp.max(s2, axis=0, keepdims=True)

    def softmax_pv(blk, s_ref, bmax_ref):
        vblk = jnp.minimum(blk, last)
        for a in range(N_PAIRS):
            m_old = m_ref[a:a + 1, :]
            m_new = jnp.maximum(m_old, bmax_ref[a:a + 1, :])
            alpha = jnp.exp2(m_old - m_new)
            p = jnp.exp2(s_ref[a] - m_new).astype(BF16)
            v_ones = jnp.concatenate([vt_ref[vblk, a * 2 * HEAD_DIM:(a + 1) * 2 * HEAD_DIM, :], ones_rows], axis=0)
            acc_ref[a] = alpha * acc_ref[a] + jnp.dot(v_ones, p, preferred_element_type=F32)
            m_ref[a:a + 1, :] = m_new

    def block_pair(with_bias, i, carry):
        logits(2 * i + 1, sb_ref, bmb_ref, with_bias)
        softmax_pv(2 * i, sa_ref, bma_ref)
        logits(2 * i + 2, sa_ref, bma_ref, with_bias)
        softmax_pv(2 * i + 1, sb_ref, bmb_ref)
        return carry

    if has_far:
        far_pairs = jnp.minimum(jnp.maximum(3 - toff, 0) // 4, nch) // 2
        logits(0, sa_ref, bma_ref, False)
        lax.fori_loop(0, far_pairs, functools.partial(block_pair, False), 0)
    else:
        far_pairs = 0
    logits(2 * far_pairs, sa_ref, bma_ref, True)
    lax.fori_loop(far_pairs, nch // 2, functools.partial(block_pair, True), 0)

    @pl.when(nch % 2 == 1)
    def _():
        softmax_pv(last, sa_ref, bma_ref)

    for a in range(N_PAIRS):
        for e in range(2):
            h = 2 * a + e
            num = acc_ref[a, e * HEAD_DIM:(e + 1) * HEAD_DIM, e * TQ:(e + 1) * TQ]
            den = acc_ref[a, 2 * HEAD_DIM:2 * HEAD_DIM + 1, e * TQ:(e + 1) * TQ]
            ot_ref[h * HEAD_DIM:(h + 1) * HEAD_DIM, :] = num / den
    o_ref[...] = ot_ref[...].T


def _attention(qit, wt, qt, vis, ki, k, vt, bias, sched, topk, resident, has_far):
    s_count, _, nq = qit.shape
    nk = k.shape[1]
    n_tiles = sched[0].shape[0]
    mode = dict(pipeline_mode=pl.Buffered(1)) if resident else {}
    idx_bits = max(1, int(nk - 1).bit_length())
    logit_buf = pltpu.VMEM((N_PAIRS, KEY_CHUNK, 2 * TQ), F32)
    in_specs = [
        pl.BlockSpec((None, ATT_W, TQ), lambda s, j, *_: (s, 0, j)),
        pl.BlockSpec((None, N_IDX, TQ), lambda s, j, *_: (s, 0, j)),
        pl.BlockSpec((None, ATT_W, TQ), lambda s, j, *_: (s, 0, j)),
        pl.BlockSpec((1, TQ), lambda s, j, *_: (0, j)),
        pl.BlockSpec((None, nk, IDX_DIM), lambda s, j, *_: (s, 0, 0), **mode),
        pl.BlockSpec((None, nk, ATT_W), lambda s, j, *_: (s, 0, 0), **mode),
        pl.BlockSpec((None, nk // KEY_CHUNK, ATT_W, KEY_CHUNK), lambda s, j, *_: (s, 0, 0, 0), **mode),
        pl.BlockSpec(bias.shape, lambda s, j, *_: (0, 0, 0), pipeline_mode=pl.Buffered(1)),
    ]
    grid_spec = pltpu.PrefetchScalarGridSpec(
        num_scalar_prefetch=3,
        grid=(s_count, n_tiles),
        in_specs=in_specs,
        out_specs=pl.BlockSpec((None, TQ, ATT_W), lambda s, j, *_: (s, j, 0)),
        scratch_shapes=[
            pltpu.VMEM((nk + KEY_CHUNK, TQ), F32),
            pltpu.VMEM((CAND_GROUPS * CAND_BATCH, TQ), F32),
            pltpu.VMEM((1, TQ), F32),
            pltpu.VMEM((1, TQ), F32),
            pltpu.VMEM((1, TQ), F32),
            pltpu.VMEM((2 * N_IDX, TQ), F32),
            pltpu.VMEM((IDX_DIM, N_IDX * TQ), BF16),
            pltpu.VMEM((N_PAIRS, 2 * HEAD_DIM, 2 * TQ), BF16),
            logit_buf, logit_buf,
            pltpu.VMEM((N_PAIRS, 2 * TQ), F32),
            pltpu.VMEM((N_PAIRS, 2 * TQ), F32),
            pltpu.VMEM((N_PAIRS, 2 * TQ), F32),
            pltpu.VMEM((N_PAIRS, 2 * HEAD_DIM + SUM_ROWS, 2 * TQ), F32),
            pltpu.VMEM((1, TQ), F32),
            pltpu.VMEM((KEY_CHUNK, TQ), F32),
            pltpu.VMEM((ATT_W, TQ), F32),
        ],
    )
    return pl.pallas_call(
        functools.partial(_attn_body, topk, idx_bits, has_far),
        grid_spec=grid_spec,
        out_shape=jax.ShapeDtypeStruct((s_count, nq, ATT_W), F32),
        compiler_params=pltpu.CompilerParams(dimension_semantics=("arbitrary", "arbitrary"),
                                             vmem_limit_bytes=VMEM_LIMIT),
        name="attn",
    )(*sched, qit, wt, qt, vis, ki, k, vt, bias)


def _decode_keys_body(ck_ref, cv_ref, cki_ref, k_ref, v_ref, ki_ref, ko_ref, vto_ref, kio_ref):
    past = ck_ref.shape[0]
    ts = k_ref.shape[0]
    pad = jnp.zeros((KEY_CHUNK - ts, ATT_W), F32)
    ko_ref[0:past, :] = ck_ref[...].astype(BF16)
    ko_ref[past:past + KEY_CHUNK, :] = jnp.concatenate([k_ref[...], pad], axis=0).astype(BF16)
    kio_ref[0:past, :] = cki_ref[...].astype(BF16)
    kio_ref[past:past + KEY_CHUNK, :] = jnp.concatenate([ki_ref[...], pad[:, 0:IDX_DIM]], axis=0).astype(BF16)
    for b in range(past // KEY_CHUNK):
        vto_ref[b] = cv_ref[b * KEY_CHUNK:(b + 1) * KEY_CHUNK, :].T.astype(BF16)
    vto_ref[past // KEY_CHUNK] = jnp.concatenate([v_ref[...], pad], axis=0).T.astype(BF16)


def _decode_keys(cache_k, cache_v, cache_ki, k_new, v_new, ki_new):
    ds, past, _ = cache_k.shape
    ts = k_new.shape[1]
    assert past % KEY_CHUNK == 0 and ts <= KEY_CHUNK
    nk = past + KEY_CHUNK
    per_stream = lambda rows, width: pl.BlockSpec((None, rows, width), lambda s: (s, 0, 0))
    return pl.pallas_call(
        _decode_keys_body,
        grid=(ds,),
        in_specs=[per_stream(past, ATT_W), per_stream(past, ATT_W), per_stream(past, IDX_DIM),
                  per_stream(ts, ATT_W), per_stream(ts, ATT_W), per_stream(ts, IDX_DIM)],
        out_specs=(per_stream(nk, ATT_W),
                   pl.BlockSpec((None, nk // KEY_CHUNK, ATT_W, KEY_CHUNK), lambda s: (s, 0, 0, 0)),
                   per_stream(nk, IDX_DIM)),
        out_shape=(jax.ShapeDtypeStruct((ds, nk, ATT_W), BF16),
                   jax.ShapeDtypeStruct((ds, nk // KEY_CHUNK, ATT_W, KEY_CHUNK), BF16),
                   jax.ShapeDtypeStruct((ds, nk, IDX_DIM), BF16)),
        compiler_params=pltpu.CompilerParams(dimension_semantics=("arbitrary",), vmem_limit_bytes=VMEM_LIMIT),
        name="decode_keys",
    )(cache_k, cache_v, cache_ki, k_new, v_new, ki_new)


def _s5_body(tt, us_ref, s0r_ref, s0i_ref, ar_ref, ai_ref, bbr_ref, bbi_ref, cr_ref, ci_ref, d_ref, wglu_ref,
             bglu_ref, out_ref, sr_out_ref, si_out_ref, str_ref, sti_ref, bre_ref, bim_ref, xr_ref, xi_ref):
    @pl.when(pl.program_id(1) == 0)
    def _():
        str_ref[...] = s0r_ref[...]
        sti_ref[...] = s0i_ref[...]

    u = us_ref[...]
    ub = u.astype(BF16)
    bre_ref[...] = jnp.dot(ub, bbr_ref[...], preferred_element_type=F32)
    bim_ref[...] = jnp.dot(ub, bbi_ref[...], preferred_element_type=F32)
    ar = ar_ref[...]
    ai = ai_ref[...]

    def step(t, carry):
        sr, si = carry
        nr = ar * sr - ai * si + bre_ref[pl.ds(t, 1), :]
        ni = ar * si + ai * sr + bim_ref[pl.ds(t, 1), :]
        xr_ref[pl.ds(t, 1), :] = nr
        xi_ref[pl.ds(t, 1), :] = ni
        return nr, ni

    sr, si = lax.fori_loop(0, tt, step, (str_ref[...], sti_ref[...]))
    str_ref[...] = sr
    sti_ref[...] = si
    sr_out_ref[...] = sr
    si_out_ref[...] = si
    y = (jnp.dot(xr_ref[...].astype(BF16), cr_ref[...], preferred_element_type=F32)
         - jnp.dot(xi_ref[...].astype(BF16), ci_ref[...], preferred_element_type=F32)
         + d_ref[...] * u)
    g = jax.nn.gelu(y)
    gate = jnp.dot(g.astype(BF16), wglu_ref[...], preferred_element_type=F32) + bglu_ref[...]
    out_ref[...] = g * jax.nn.sigmoid(gate)


def _s5(us, s0r, s0i, ar, ai, bbr, bbi, cr, ci, d, wglu, bglu, tt):
    s_count, t_len, _ = us.shape
    assert t_len % tt == 0
    state = pl.BlockSpec((None, 1, SSM_S), lambda s, t: (s, 0, 0))
    seq = pl.BlockSpec((None, tt, SSM_W), lambda s, t: (s, t, 0))
    return pl.pallas_call(
        functools.partial(_s5_body, tt),
        grid=(s_count, t_len // tt),
        in_specs=[seq, state, state, _const_spec((1, SSM_S)), _const_spec((1, SSM_S)),
                  _const_spec((SSM_W, SSM_S)), _const_spec((SSM_W, SSM_S)),
                  _const_spec((SSM_S, SSM_W)), _const_spec((SSM_S, SSM_W)),
                  _const_spec((1, SSM_W)), _const_spec((SSM_W, SSM_W)), _const_spec((1, SSM_W))],
        out_specs=(seq, state, state),
        out_shape=(jax.ShapeDtypeStruct((s_count, t_len, SSM_W), F32),
                   jax.ShapeDtypeStruct((s_count, 1, SSM_S), F32),
                   jax.ShapeDtypeStruct((s_count, 1, SSM_S), F32)),
        scratch_shapes=[pltpu.VMEM((1, SSM_S), F32), pltpu.VMEM((1, SSM_S), F32),
                        pltpu.VMEM((tt, SSM_S), F32), pltpu.VMEM((tt, SSM_S), F32),
                        pltpu.VMEM((tt, SSM_S), F32), pltpu.VMEM((tt, SSM_S), F32)],
        compiler_params=pltpu.CompilerParams(dimension_semantics=("arbitrary", "arbitrary"),
                                             vmem_limit_bytes=VMEM_LIMIT),
        name="s5",
    )(us, s0r, s0i, ar, ai, bbr, bbi, cr, ci, d, wglu, bglu)


def _mix_out_body(att_ref, ssm_ref, h_ref, woa_ref, wos_ref, gmix_ref, gpre_ref, wg_ref, wu_ref, wd_ref, gpost_ref,
                  o_ref):
    m = (jnp.dot(att_ref[...].astype(BF16), woa_ref[...], preferred_element_type=F32)
         + jnp.dot(ssm_ref[...].astype(BF16), wos_ref[...], preferred_element_type=F32))
    h2 = h_ref[...] + _rms(m, gmix_ref[...])
    y = _swiglu(_rms(h2, gpre_ref[...]), wg_ref, wu_ref, wd_ref)
    o_ref[...] = h2 + 0.5 * _rms(y, gpost_ref[...])


def _mix_out(att, ssm, h, woa, wos, gmix, gpre, wg, wu, wd, gpost):
    n = h.shape[0]
    tile = _row_tile(n)
    row = lambda width: pl.BlockSpec((tile, width), lambda i: (i, 0))
    return pl.pallas_call(
        _mix_out_body,
        grid=(pl.cdiv(n, tile),),
        in_specs=[row(ATT_W), row(SSM_W), row(D_MODEL), _const_spec((ATT_W, D_MODEL)),
                  _const_spec((SSM_W, D_MODEL)), _const_spec((1, D_MODEL)), _const_spec((1, D_MODEL)),
                  _const_spec((D_MODEL, D_FF)), _const_spec((D_MODEL, D_FF)), _const_spec((D_FF, D_MODEL)),
                  _const_spec((1, D_MODEL))],
        out_specs=row(D_MODEL),
        out_shape=jax.ShapeDtypeStruct((n, D_MODEL), F32),
        compiler_params=pltpu.CompilerParams(dimension_semantics=("arbitrary",), vmem_limit_bytes=VMEM_LIMIT),
        name="mix_out_ffn2",
    )(att, ssm, h, woa, wos, gmix, gpre, wg, wu, wd, gpost)


def _rel_bucket_np(rel):
    half = NUM_BUCKETS // 2
    max_exact = half // 2
    n = np.abs(rel).astype(np.int64)
    nf = np.maximum(n, 1).astype(np.float64)
    large = max_exact + (np.log(nf / max_exact) / math.log(MAX_DISTANCE / max_exact)
                         * (half - max_exact)).astype(np.int64)
    large = np.minimum(large, half - 1)
    return np.where(rel > 0, half, 0) + np.where(n < max_exact, n, large)


def _pair_bias(rel_bias, n_rows, c0):
    length = n_rows + TQ
    rel = np.arange(length) - (TQ - 1) - LANE - c0
    table = (rel_bias - rel_bias[NUM_BUCKETS // 2 - 1][None, :]) * LOG2E
    t1d = table[_rel_bucket_np(rel)].T
    x = jnp.tile(t1d, (1, TQ))[:, :TQ * (length - 1)].reshape(N_HEADS, TQ, length - 1)
    b = jnp.transpose(x[:, :, TQ - 1:TQ - 1 + n_rows], (0, 2, 1)).reshape(N_PAIRS, 2, n_rows, TQ)
    return jnp.concatenate([b[:, 0], b[:, 1]], axis=-1).astype(F32)


def _prompt_schedule(n, n_pad):
    pos = np.arange(n_pad)
    vis = np.where(pos < N_META, N_META, N_META + CHUNK * ((pos - N_META) // CHUNK + 1))
    vis = np.where(pos < n, np.minimum(vis, n), 0).astype(np.int32)
    n_tiles = -(-n // TQ)
    vmax = vis[:n_tiles * TQ].reshape(n_tiles, TQ).max(axis=1)
    nch = -(-vmax // KEY_CHUNK)
    nfull = vis[:n_tiles * TQ].reshape(n_tiles, TQ).min(axis=1) // KEY_CHUNK
    toff = PROMPT_BIAS_C0 // LANE - np.arange(n_tiles)
    assert ((nch - 1) * (KEY_CHUNK // LANE) + toff).max() <= (PROMPT_BIAS_ROWS - KEY_CHUNK) // LANE - 1
    sched = tuple(jnp.asarray(a, I32) for a in (nch, nfull, toff))
    has_far = bool((np.minimum(np.maximum(3 - toff, 0) // 4, nch) >= 2).any())
    return sched, jnp.asarray(vis[None, :]), has_far


def _sample_schedule(past, ts):
    nk = past + ts
    sched = tuple(jnp.asarray([v], I32) for v in (-(-nk // KEY_CHUNK), 0, 0))
    vis = jnp.asarray(np.where(np.arange(TQ) < ts, nk, 0)[None, :], I32)
    return sched, vis, False


def _s5_params(lam_re, lam_im, log_step, b_re, b_im, c_re, c_im, d_skip):
    dt = jnp.exp(log_step)[:, None]
    mag = jnp.exp(lam_re * dt)
    ab_re, ab_im = mag * jnp.cos(lam_im * dt), mag * jnp.sin(lam_im * dt)
    nr, ni = ab_re - 1.0, ab_im
    den = lam_re * lam_re + lam_im * lam_im
    f_re, f_im = (nr * lam_re + ni * lam_im) / den, (ni * lam_re - nr * lam_im) / den
    bb_re = f_re[..., None] * b_re - f_im[..., None] * b_im
    bb_im = f_re[..., None] * b_im + f_im[..., None] * b_re
    eye = jnp.eye(SSM_GROUPS, dtype=F32)
    bd_in = lambda w: jnp.einsum('gpc,gh->gchp', w, eye).reshape(SSM_W, SSM_S).astype(BF16)
    bd_out = lambda w: jnp.einsum('gcp,gh->gphc', w, eye).reshape(SSM_S, SSM_W).astype(BF16)
    return (ab_re.reshape(1, SSM_S), ab_im.reshape(1, SSM_S), bd_in(bb_re), bd_in(bb_im),
            bd_out(c_re), bd_out(c_im), d_skip.reshape(1, SSM_W))


def kernel(x_prompt, x_sample, cache_k, cache_v, cache_kidx, state_ssm_re, state_ssm_im, meta_tokens, rel_bias,
           ffn1_g_pre, ffn1_w_gate, ffn1_w_up, ffn1_w_down, ffn1_g_post, mix_g_pre, w_in, w_out, mix_g_post,
           lam_re, lam_im, log_step, b_re, b_im, c_re, c_im, d_skip, w_glu, b_glu,
           ffn2_g_pre, ffn2_w_gate, ffn2_w_up, ffn2_w_down, ffn2_g_post):
    depth = ffn1_g_pre.shape[0]
    assert depth == 1
    bp, seq, _ = x_prompt.shape
    assert bp == 1
    ds, ts, _ = x_sample.shape
    past = cache_k.shape[2]
    n_p = N_META + seq
    n_p_pad = pl.cdiv(n_p, ROW_TILE) * ROW_TILE
    n_s = ds * ts
    assert n_s % ROW_TILE == 0 and ts <= TQ
    nk_s_pad = pl.cdiv(past + ts, KEY_CHUNK) * KEY_CHUNK
    l = 0

    row2 = lambda g: g[l].reshape(1, -1)
    w1 = (row2(ffn1_g_pre), ffn1_w_gate[l].astype(BF16), ffn1_w_up[l].astype(BF16), ffn1_w_down[l].astype(BF16),
          row2(ffn1_g_post))
    w2 = (row2(ffn2_g_pre), ffn2_w_gate[l].astype(BF16), ffn2_w_up[l].astype(BF16), ffn2_w_down[l].astype(BF16),
          row2(ffn2_g_post))
    wi = w_in[l]
    o_q, o_k, o_v, o_qi, o_ki, o_wi, o_us = np.cumsum((0, ATT_W, ATT_W, ATT_W, N_IDX * IDX_DIM, IDX_DIM, N_IDX))
    sl = lambda o, w: wi[:, o:o + w]
    wrow = jnp.concatenate([sl(o_k, ATT_W), sl(o_v, ATT_W), sl(o_us, SSM_W), sl(o_ki, IDX_DIM),
                            jnp.zeros((D_MODEL, ROW_COLS_PAD - ROW_COLS), F32)], axis=1).astype(BF16)
    wcol = jnp.concatenate([sl(o_q, ATT_W), sl(o_qi, ATT_W), sl(o_v, ATT_W), sl(o_wi, N_IDX)], axis=1).T.astype(BF16)
    woa, wos = w_out[l][:ATT_W].astype(BF16), w_out[l][ATT_W:].astype(BF16)
    s5w = _s5_params(lam_re[l], lam_im[l], log_step[l], b_re[l], b_im[l], c_re[l], c_im[l], d_skip[l])
    s5w = s5w + (w_glu[l].astype(BF16), b_glu[l].reshape(1, SSM_W))
    bias_tab = rel_bias.astype(F32)

    xp = jnp.concatenate([meta_tokens.astype(F32), x_prompt[0]], axis=0)
    hp = _ffn(xp, *w1)
    kp, vp, kip, usp, kbp, kibp, vtbp, qtp, qitp, wtp = _mix_in(hp, row2(mix_g_pre), wrow, wcol, n_p_pad)
    sched_p, vis_p, far_p = _prompt_schedule(n_p, n_p_pad)
    att_p = _attention(qitp[None], wtp[None], qtp[None], vis_p, kibp[None], kbp[None], vtbp[None],
                       _pair_bias(bias_tab, PROMPT_BIAS_ROWS, PROMPT_BIAS_C0), sched_p,
                       min(TOPK_MAX, seq // 4), True, far_p)[0]
    zero_state = jnp.zeros((1, 1, SSM_S), F32)
    tt_p = max(t for t in range(8, 513, 8) if n_p % t == 0)
    ssm_p, srp, sip = _s5(usp[None], zero_state, zero_state, *s5w, tt_p)
    yp = _mix_out(att_p, ssm_p[0], hp, woa, wos, row2(mix_g_post), *w2)

    hs = _ffn(x_sample.reshape(n_s, D_MODEL), *w1)
    ks, vs, kis, uss, _, _, _, qts, qits, wts = _mix_in(hs, row2(mix_g_pre), wrow, wcol, n_s)
    k_all, vt_all, ki_all = _decode_keys(
        cache_k[l].reshape(ds, past, ATT_W), cache_v[l].reshape(ds, past, ATT_W), cache_kidx[l],
        ks.reshape(ds, ts, ATT_W), vs.reshape(ds, ts, ATT_W), kis.reshape(ds, ts, IDX_DIM))
    assert k_all.shape[1] == nk_s_pad
    lanes = lambda a: jnp.pad(jnp.transpose(a.reshape(a.shape[0], ds, ts), (1, 0, 2)), ((0, 0), (0, 0), (0, TQ - ts)))
    sched_s, vis_s, far_s = _sample_schedule(past, ts)
    att_s = _attention(lanes(qits), lanes(wts), lanes(qts), vis_s, ki_all, k_all, vt_all,
                       _pair_bias(bias_tab, LANE + nk_s_pad, past), sched_s,
                       min(TOPK_MAX, (past + ts) // 4), False, far_s)
    att_s = att_s[:, :ts].reshape(n_s, ATT_W)
    ssm_s, srs, sis = _s5(uss.reshape(ds, ts, SSM_W), state_ssm_re[l].reshape(ds, 1, SSM_S),
                          state_ssm_im[l].reshape(ds, 1, SSM_S), *s5w, ts)
    ys = _mix_out(att_s, ssm_s.reshape(n_s, SSM_W), hs, woa, wos, row2(mix_g_post), *w2)

    heads = lambda a, b, t: a.reshape(1, b, t, N_HEADS, HEAD_DIM)
    state = lambda a, b: a.reshape(1, b, SSM_GROUPS, SSM_P)
    return (yp[N_META:][None], ys.reshape(ds, ts, D_MODEL),
            heads(kp, 1, n_p), heads(vp, 1, n_p), kip.reshape(1, 1, n_p, IDX_DIM), state(srp, 1), state(sip, 1),
            heads(ks, ds, ts), heads(vs, ds, ts), kis.reshape(1, ds, ts, IDX_DIM), state(srs, ds), state(sis, ds))
```

```python
import functools
import math

import numpy as np
import jax
import jax.numpy as jnp
from jax import lax
from jax.experimental import pallas as pl
from jax.experimental.pallas import tpu as pltpu

F32, BF16, I32 = jnp.float32, jnp.bfloat16, jnp.int32

D_MODEL = 1024
CHUNK = 64
N_META = 16
N_HEADS = 8
HEAD_DIM = 64
ATT_W = N_HEADS * HEAD_DIM
N_IDX = 8
IDX_DIM = 64
TOPK_MAX = 256
SSM_GROUPS = 32
SSM_GC = 16
SSM_W = SSM_GROUPS * SSM_GC
SSM_P = 64
SSM_S = SSM_GROUPS * SSM_P
D_FF = 2816
NUM_BUCKETS = 32
MAX_DISTANCE = 128
EPS = 1e-6

LANE = 128
VMEM_LIMIT = 62 * 1024 * 1024
ROW_TILE = 512
TQ = LANE
KEY_CHUNK = 512
PROMPT_BIAS_C0 = 512
PROMPT_BIAS_ROWS = 1280
NEG = -1e30
LOG2E = math.log2(math.e)
F32_MAX = float(np.finfo(np.float32).max)
N_PAIRS = N_HEADS // 2
SUM_ROWS = 16
CAND_DEPTH = 16
CAND_BATCH = 8 * CAND_DEPTH
CAND_GROUPS = 8
SCORE_ROUND = 8
TIE_SCAN_MAX = 8


def _batcher_network(n):
    pairs, p = [], 1
    while p < n:
        k = p
        while k >= 1:
            for j in range(k % p, n - k, 2 * k):
                for i in range(min(k, n - j - k)):
                    if (i + j) // (2 * p) == (i + j + k) // (2 * p):
                        pairs.append((i + j, i + j + k))
            k //= 2
        p *= 2
    return pairs


_SORT_NETWORK = _batcher_network(CAND_DEPTH)


def _rms(x, g):
    return x * lax.rsqrt(jnp.mean(x * x, axis=-1, keepdims=True) + EPS) * g


def _swiglu(xn, wg_ref, wu_ref, wd_ref):
    xb = xn.astype(BF16)
    a = jnp.dot(xb, wg_ref[...], preferred_element_type=F32)
    b = jnp.dot(xb, wu_ref[...], preferred_element_type=F32)
    hidden = (a * jax.nn.sigmoid(a) * b).astype(BF16)
    return jnp.dot(hidden, wd_ref[...], preferred_element_type=F32)


def _const_spec(shape):
    nd = len(shape)
    return pl.BlockSpec(shape, lambda *_: (0,) * nd, pipeline_mode=pl.Buffered(1))


def _ffn_body(x_ref, gpre_ref, wg_ref, wu_ref, wd_ref, gpost_ref, o_ref):
    x = x_ref[...]
    y = _swiglu(_rms(x, gpre_ref[...]), wg_ref, wu_ref, wd_ref)
    o_ref[...] = x + 0.5 * _rms(y, gpost_ref[...])


def _row_tile(n):
    for t in range(ROW_TILE, 15, -16):
        if n % t == 0:
            return t
    return ROW_TILE


def _ffn(x, gpre, wg, wu, wd, gpost):
    n = x.shape[0]
    tile = _row_tile(n)
    row = pl.BlockSpec((tile, D_MODEL), lambda i: (i, 0))
    return pl.pallas_call(
        _ffn_body,
        grid=(pl.cdiv(n, tile),),
        in_specs=[row, _const_spec((1, D_MODEL)), _const_spec((D_MODEL, D_FF)), _const_spec((D_MODEL, D_FF)),
                  _const_spec((D_FF, D_MODEL)), _const_spec((1, D_MODEL))],
        out_specs=row,
        out_shape=jax.ShapeDtypeStruct((n, D_MODEL), F32),
        compiler_params=pltpu.CompilerParams(dimension_semantics=("arbitrary",), vmem_limit_bytes=VMEM_LIMIT),
        name="ffn1",
    )(x, gpre, wg, wu, wd, gpost)


ROW_COLS = 3 * ATT_W + IDX_DIM
ROW_COLS_PAD = 13 * LANE
COL_ROWS = 3 * ATT_W + N_IDX


def _mix_in_body(n_valid, h_ref, g_ref, wrow_ref, wcol_ref,
                 k_ref, v_ref, ki_ref, us_ref, kb_ref, kib_ref, vtb_ref, qt_ref, qit_ref, wt_ref):
    i = pl.program_id(0)
    u = _rms(h_ref[...], g_ref[...]).astype(BF16)
    z = jnp.dot(u, wrow_ref[...], preferred_element_type=F32)
    zt = lax.dot_general(wcol_ref[...], u, (((1,), (1,)), ((), ())), preferred_element_type=F32)
    k = z[:, 0:ATT_W]
    v = z[:, ATT_W:2 * ATT_W]
    ki = z[:, 3 * ATT_W:3 * ATT_W + IDX_DIM]
    k_ref[...] = k
    v_ref[...] = v
    us_ref[...] = z[:, 2 * ATT_W:3 * ATT_W]
    ki_ref[...] = ki
    row_ok = (i * ROW_TILE + lax.broadcasted_iota(I32, (ROW_TILE, 1), 0)) < n_valid
    col_ok = (i * ROW_TILE + lax.broadcasted_iota(I32, (1, ROW_TILE), 1)) < n_valid
    kb_ref[...] = jnp.where(row_ok, k, 0.0).astype(BF16)
    kib_ref[...] = jnp.where(row_ok, ki, 0.0).astype(BF16)
    zt = jnp.where(col_ok, zt, 0.0)
    qt_ref[...] = (zt[0:ATT_W] * (HEAD_DIM ** -0.5 * LOG2E)).astype(BF16)
    vtb_ref[...] = zt[2 * ATT_W:3 * ATT_W].astype(BF16)
    wt = zt[3 * ATT_W:3 * ATT_W + N_IDX] * ((IDX_DIM ** -0.5) * (N_IDX ** -0.5))
    wt_ref[...] = wt
    for h in range(N_IDX):
        qi_h = zt[ATT_W + h * IDX_DIM:ATT_W + (h + 1) * IDX_DIM]
        qit_ref[h * IDX_DIM:(h + 1) * IDX_DIM, :] = (qi_h * wt[h:h + 1, :]).astype(BF16)


def _mix_in(h, g, wrow, wcol, n_pad):
    n = h.shape[0]
    steps = n_pad // ROW_TILE
    assert steps == pl.cdiv(n, ROW_TILE)

    def row(width):
        return pl.BlockSpec((ROW_TILE, width), lambda i: (i, 0))

    col = lambda rows: pl.BlockSpec((rows, ROW_TILE), lambda i: (0, i))
    out_shape = (
        jax.ShapeDtypeStruct((n, ATT_W), F32), jax.ShapeDtypeStruct((n, ATT_W), F32),
        jax.ShapeDtypeStruct((n, IDX_DIM), F32), jax.ShapeDtypeStruct((n, SSM_W), F32),
        jax.ShapeDtypeStruct((n_pad, ATT_W), BF16), jax.ShapeDtypeStruct((n_pad, IDX_DIM), BF16),
        jax.ShapeDtypeStruct((n_pad // ROW_TILE, ATT_W, ROW_TILE), BF16),
        jax.ShapeDtypeStruct((ATT_W, n_pad), BF16), jax.ShapeDtypeStruct((ATT_W, n_pad), BF16),
        jax.ShapeDtypeStruct((N_IDX, n_pad), F32),
    )
    out_specs = (
        row(ATT_W), row(ATT_W), row(IDX_DIM), row(SSM_W), row(ATT_W), row(IDX_DIM),
        pl.BlockSpec((None, ATT_W, ROW_TILE), lambda i: (i, 0, 0)),
        col(ATT_W), col(ATT_W), col(N_IDX),
    )
    return pl.pallas_call(
        functools.partial(_mix_in_body, n),
        grid=(steps,),
        in_specs=[row(D_MODEL), _const_spec((1, D_MODEL)), _const_spec((D_MODEL, ROW_COLS_PAD)),
                  _const_spec((COL_ROWS, D_MODEL))],
        out_specs=out_specs,
        out_shape=out_shape,
        compiler_params=pltpu.CompilerParams(dimension_semantics=("arbitrary",), vmem_limit_bytes=VMEM_LIMIT),
        name="mix_in",
    )(h, g, wrow, wcol)


def _ordered_bits_to_f32(u):
    k = u ^ jnp.int32(-2 ** 31)
    b = k ^ ((k >> 31) & jnp.int32(0x7FFFFFFF))
    return lax.bitcast_convert_type(b, F32)


def _attn_body(topk, idx_bits, has_far,
               nch_ref, nfull_ref, toff_ref,
               qit_ref, wt_ref, qt_ref, vis_ref, ki_ref, k_ref, vt_ref, bias_ref,
               o_ref,
               st_ref, cand_ref, sa_ref, sb_ref, qicat_ref, qbd_ref, acc_ref, rowf_ref, ot_ref,
               clip_ref, bma_ref, bmb_ref, m_ref, thr_ref, cgt_ref, cge_ref, cut_ref):
    j = pl.program_id(1)
    nch = nch_ref[j]
    toff = toff_ref[j]
    kf = float(topk)
    bias_tile_max = (bias_ref.shape[1] - KEY_CHUNK) // LANE - 1

    for h in range(N_IDX):
        qicat_ref[:, h * TQ:(h + 1) * TQ] = qit_ref[h * IDX_DIM:(h + 1) * IDX_DIM, :]
    qbd_ref[...] = jnp.zeros(qbd_ref.shape, BF16)
    for a in range(N_PAIRS):
        qbd_ref[a, 0:HEAD_DIM, 0:TQ] = qt_ref[2 * a * HEAD_DIM:(2 * a + 1) * HEAD_DIM, :]
        qbd_ref[a, HEAD_DIM:2 * HEAD_DIM, TQ:2 * TQ] = qt_ref[(2 * a + 1) * HEAD_DIM:(2 * a + 2) * HEAD_DIM, :]

    def chunk_start(c):
        return pl.multiple_of(c * KEY_CHUNK, KEY_CHUNK)

    def chunk_rows(r0):
        return r0 + lax.broadcasted_iota(I32, (KEY_CHUNK, TQ), 0)

    vis = vis_ref[...]
    cand_ref[...] = jnp.full(cand_ref.shape, -jnp.inf, F32)
    w_pos = wt_ref[...] >= 0.0
    clip_ref[0:N_IDX, :] = jnp.where(w_pos, 0.0, -jnp.inf)
    clip_ref[N_IDX:2 * N_IDX, :] = jnp.where(w_pos, jnp.inf, 0.0)

    last_batch_row = ki_ref.shape[0] - CAND_BATCH

    def score_round(masked, i, carry):
        for g in range(SCORE_ROUND):
            r0 = pl.multiple_of((i * SCORE_ROUND + g) * CAND_BATCH, CAND_BATCH)
            rel = jnp.dot(ki_ref[pl.ds(jnp.minimum(r0, last_batch_row), CAND_BATCH), :], qicat_ref[...],
                          preferred_element_type=F32)
            score = jnp.zeros((CAND_BATCH, TQ), F32)
            for h in range(N_IDX):
                score = score + jnp.clip(rel[:, h * TQ:(h + 1) * TQ], clip_ref[h:h + 1, :],
                                         clip_ref[N_IDX + h:N_IDX + h + 1, :])
            if masked:
                score = jnp.where(r0 + lax.broadcasted_iota(I32, (CAND_BATCH, TQ), 0) < vis, score, -jnp.inf)
            st_ref[pl.ds(r0, CAND_BATCH), :] = score
        return carry

    round_chunks = SCORE_ROUND * CAND_BATCH // KEY_CHUNK
    full_rounds = nfull_ref[j] // round_chunks
    lax.fori_loop(0, full_rounds, functools.partial(score_round, False), 0)
    lax.fori_loop(full_rounds, (nch + round_chunks - 1) // round_chunks, functools.partial(score_round, True), 0)

    def merge_batch(b, carry):
        r0 = pl.multiple_of(b * CAND_BATCH, CAND_BATCH)
        x = [st_ref[pl.ds(r0 + 8 * r, 8), :] for r in range(CAND_DEPTH)]
        for lo, hi in _SORT_NETWORK:
            x[lo], x[hi] = jnp.maximum(x[lo], x[hi]), jnp.minimum(x[lo], x[hi])
        g0 = pl.multiple_of(lax.rem(b, CAND_GROUPS) * CAND_BATCH, CAND_BATCH)
        t = [jnp.maximum(cand_ref[pl.ds(g0 + 8 * r, 8), :], x[CAND_DEPTH - 1 - r]) for r in range(CAND_DEPTH)]
        d = CAND_DEPTH // 2
        while d >= 1:
            for r in range(CAND_DEPTH):
                if r & d == 0:
                    t[r], t[r + d] = jnp.maximum(t[r], t[r + d]), jnp.minimum(t[r], t[r + d])
            d //= 2
        for r in range(CAND_DEPTH):
            cand_ref[pl.ds(g0 + 8 * r, 8), :] = t[r]
        return carry

    lax.fori_loop(0, nch * (KEY_CHUNK // CAND_BATCH), merge_batch, 0)

    def reduce_rows(ref, n_chunks, fn, combine, init):
        reducer = jnp.sum if combine is jnp.add else jnp.max

        def body(c, acc):
            r0 = chunk_start(c)
            val = fn(ref[pl.ds(r0, KEY_CHUNK), :], r0)
            part = reducer(val.reshape(8, KEY_CHUNK // 64, 8, TQ), axis=1)
            return combine(acc, reducer(part, axis=0))

        acc = lax.fori_loop(0, n_chunks, body, jnp.full((8, TQ), init, F32), unroll=isinstance(n_chunks, int))
        return reducer(acc, axis=0, keepdims=True)

    def count(ref, n_chunks, indicator):
        return reduce_rows(ref, n_chunks, indicator, jnp.add, 0.0)

    def select(ref, n_chunks):
        def value_bit(i, res):
            trial = res | lax.shift_left(jnp.int32(1), 31 - i)
            thr_t = _ordered_bits_to_f32(trial)
            cnt = count(ref, n_chunks, lambda blk, r0: jnp.where(blk >= thr_t, 1.0, 0.0))
            return jnp.where(cnt >= kf, trial, res)

        res = lax.fori_loop(0, 32, value_bit, jnp.zeros((1, TQ), I32))
        res = jnp.where((res >> 23) == 0, jnp.int32(0x00800000), res)
        lo = _ordered_bits_to_f32(res)
        above = _ordered_bits_to_f32(res + 1)
        c_above = count(ref, n_chunks, lambda blk, r0: jnp.where(blk >= above, 1.0, 0.0))
        c_lo = count(ref, n_chunks, lambda blk, r0: jnp.where(blk >= lo, 1.0, 0.0))
        thr_ref[...] = lo
        cgt_ref[...] = c_above
        cge_ref[...] = c_lo

        def walking(state):
            return jnp.max(state[0]) > 0.0

        def walk(state):
            active, hi, c_hi = state
            v = reduce_rows(ref, n_chunks, lambda blk, r0: jnp.where(blk < hi, blk, -jnp.inf), jnp.maximum,
                            -jnp.inf)
            c_v = count(ref, n_chunks, lambda blk, r0: jnp.where(blk >= v, 1.0, 0.0))
            done = jnp.logical_and(active > 0.0, c_v >= kf)
            thr_ref[...] = jnp.where(done, v, thr_ref[...])
            cgt_ref[...] = jnp.where(done, c_hi, cgt_ref[...])
            cge_ref[...] = jnp.where(done, c_v, cge_ref[...])
            return jnp.where(done, 0.0, active), v, c_v

        surplus = jnp.where(jnp.logical_and(c_lo > kf, c_above < kf), 1.0, 0.0)
        lax.while_loop(walking, walk, (surplus, above, c_above))
        return lo

    cand_lo = select(cand_ref, cand_ref.shape[0] // KEY_CHUNK)
    kept_min = cand_ref[CAND_BATCH - 8:CAND_BATCH, :]
    for g in range(1, CAND_GROUPS):
        kept_min = jnp.maximum(kept_min, cand_ref[(g + 1) * CAND_BATCH - 8:(g + 1) * CAND_BATCH, :])
    unsafe = jnp.max(kept_min, axis=0, keepdims=True) >= cand_lo

    @pl.when(jnp.max(jnp.where(unsafe, 1.0, 0.0)) > 0.0)
    def _():
        select(st_ref, nch)

    thr = thr_ref[...]

    need = kf - cgt_ref[...]
    excess = cge_ref[...] - kf
    cut_ref[...] = jnp.full((1, TQ), F32_MAX, F32)
    rowf_ref[...] = lax.broadcasted_iota(I32, (KEY_CHUNK, TQ), 0).astype(F32)

    def chunk_rows_f(r0):
        return rowf_ref[...] + r0.astype(F32)

    tie_need = jnp.max(jnp.where(excess > 0.0, need, 0.0)).astype(I32)

    @pl.when(jnp.logical_and(tie_need > 0, tie_need <= TIE_SCAN_MAX))
    def _():
        def next_tied(cut_f):
            def body(c, acc):
                r0 = chunk_start(c)
                tied_row = jnp.where(st_ref[pl.ds(r0, KEY_CHUNK), :] == thr, chunk_rows_f(r0), F32_MAX)
                v = jnp.where(tied_row > cut_f, tied_row, F32_MAX)
                part = jnp.min(v.reshape(8, KEY_CHUNK // 64, 8, TQ), axis=1)
                return jnp.minimum(acc, jnp.min(part, axis=0))

            acc = lax.fori_loop(0, nch, body, jnp.full((8, TQ), F32_MAX, F32))
            return jnp.min(acc, axis=0, keepdims=True)

        def take(t, cut_f):
            return jnp.where(t.astype(F32) < need, next_tied(cut_f), cut_f)

        cut_f = lax.fori_loop(0, tie_need, take, jnp.full((1, TQ), -1.0, F32))
        cut_ref[...] = jnp.where(excess > 0.0, cut_f, F32_MAX)

    @pl.when(tie_need > TIE_SCAN_MAX)
    def _():
        def index_bit(i, cut):
            trial = cut | lax.shift_left(jnp.int32(1), idx_bits - 1 - i)
            cnt = count(st_ref, nch, lambda blk, r0: jnp.where(
                blk == thr, jnp.where(chunk_rows(r0) < trial, 1.0, 0.0), 0.0))
            return jnp.where(cnt < need, trial, cut)

        cut = lax.fori_loop(0, idx_bits, index_bit, jnp.zeros((1, TQ), I32))
        cut_ref[...] = jnp.where(excess > 0.0, cut.astype(F32), F32_MAX)

    cut_f = cut_ref[...]

    st_ref[pl.ds(chunk_start(nch), KEY_CHUNK), :] = jnp.full((KEY_CHUNK, TQ), -jnp.inf, F32)

    m_ref[...] = jnp.full(m_ref.shape, NEG, F32)
    acc_ref[...] = jnp.zeros(acc_ref.shape, F32)
    last = nch - 1
    ones_rows = jnp.ones((SUM_ROWS, KEY_CHUNK), BF16)

    def logits(blk, s_ref, bmax_ref, with_bias):
        kblk = k_ref[pl.ds(chunk_start(jnp.minimum(blk, last)), KEY_CHUNK), :]
        r0 = chunk_start(jnp.minimum(blk, nch))
        score = st_ref[pl.ds(r0, KEY_CHUNK), :]
        tie = jnp.where(chunk_rows_f(r0) <= cut_f, 0.0, NEG)
        msk = jnp.where(score > thr, 0.0, jnp.where(score == thr, tie, NEG))
        msk2 = jnp.concatenate([msk, msk], axis=1)
        bias_row0 = pl.multiple_of((jnp.clip(4 * blk + toff, -1, bias_tile_max) + 1) * LANE, LANE)
        for a in range(N_PAIRS):
            s2 = jnp.dot(kblk[:, a * 2 * HEAD_DIM:(a + 1) * 2 * HEAD_DIM], qbd_ref[a],
                         preferred_element_type=F32) + msk2
            if with_bias:
                s2 = s2 + bias_ref[a, pl.ds(bias_row0, KEY_CHUNK), :]
            s_ref[a] = s2
            bmax_ref[a:a + 1, :] = jnp.max(s2, axis=0, keepdims=True)

    def softmax_pv(blk, s_ref, bmax_ref):
        vblk = jnp.minimum(blk, last)
        for a in range(N_PAIRS):
            m_old = m_ref[a:a + 1, :]
            m_new = jnp.maximum(m_old, bmax_ref[a:a + 1, :])
            alpha = jnp.exp2(m_old - m_new)
            p = jnp.exp2(s_ref[a] - m_new).astype(BF16)
            v_ones = jnp.concatenate([vt_ref[vblk, a * 2 * HEAD_DIM:(a + 1) * 2 * HEAD_DIM, :], ones_rows], axis=0)
            acc_ref[a] = alpha * acc_ref[a] + jnp.dot(v_ones, p, preferred_element_type=F32)
            m_ref[a:a + 1, :] = m_new

    def block_pair(with_bias, i, carry):
        logits(2 * i + 1, sb_ref, bmb_ref, with_bias)
        softmax_pv(2 * i, sa_ref, bma_ref)
        logits(2 * i + 2, sa_ref, bma_ref, with_bias)
        softmax_pv(2 * i + 1, sb_ref, bmb_ref)
        return carry

    if has_far:
        far_pairs = jnp.minimum(jnp.maximum(3 - toff, 0) // 4, nch) // 2
        logits(0, sa_ref, bma_ref, False)
        lax.fori_loop(0, far_pairs, functools.partial(block_pair, False), 0)
    else:
        far_pairs = 0
    logits(2 * far_pairs, sa_ref, bma_ref, True)
    lax.fori_loop(far_pairs, nch // 2, functools.partial(block_pair, True), 0)

    @pl.when(nch % 2 == 1)
    def _():
        softmax_pv(last, sa_ref, bma_ref)

    for a in range(N_PAIRS):
        for e in range(2):
            h = 2 * a + e
            num = acc_ref[a, e * HEAD_DIM:(e + 1) * HEAD_DIM, e * TQ:(e + 1) * TQ]
            den = acc_ref[a, 2 * HEAD_DIM:2 * HEAD_DIM + 1, e * TQ:(e + 1) * TQ]
            ot_ref[h * HEAD_DIM:(h + 1) * HEAD_DIM, :] = num / den
    o_ref[...] = ot_ref[...].T


def _attention(qit, wt, qt, vis, ki, k, vt, bias, sched, topk, resident, has_far):
    s_count, _, nq = qit.shape
    nk = k.shape[1]
    n_tiles = sched[0].shape[0]
    mode = dict(pipeline_mode=pl.Buffered(1)) if resident else {}
    idx_bits = max(1, int(nk - 1).bit_length())
    logit_buf = pltpu.VMEM((N_PAIRS, KEY_CHUNK, 2 * TQ), F32)
    in_specs = [
        pl.BlockSpec((None, ATT_W, TQ), lambda s, j, *_: (s, 0, j)),
        pl.BlockSpec((None, N_IDX, TQ), lambda s, j, *_: (s, 0, j)),
        pl.BlockSpec((None, ATT_W, TQ), lambda s, j, *_: (s, 0, j)),
        pl.BlockSpec((1, TQ), lambda s, j, *_: (0, j)),
        pl.BlockSpec((None, nk, IDX_DIM), lambda s, j, *_: (s, 0, 0), **mode),
        pl.BlockSpec((None, nk, ATT_W), lambda s, j, *_: (s, 0, 0), **mode),
        pl.BlockSpec((None, nk // KEY_CHUNK, ATT_W, KEY_CHUNK), lambda s, j, *_: (s, 0, 0, 0), **mode),
        pl.BlockSpec(bias.shape, lambda s, j, *_: (0, 0, 0), pipeline_mode=pl.Buffered(1)),
    ]
    grid_spec = pltpu.PrefetchScalarGridSpec(
        num_scalar_prefetch=3,
        grid=(s_count, n_tiles),
        in_specs=in_specs,
        out_specs=pl.BlockSpec((None, TQ, ATT_W), lambda s, j, *_: (s, j, 0)),
        scratch_shapes=[
            pltpu.VMEM((nk + KEY_CHUNK, TQ), F32),
            pltpu.VMEM((CAND_GROUPS * CAND_BATCH, TQ), F32),
            logit_buf, logit_buf,
            pltpu.VMEM((IDX_DIM, N_IDX * TQ), BF16),
            pltpu.VMEM((N_PAIRS, 2 * HEAD_DIM, 2 * TQ), BF16),
            pltpu.VMEM((N_PAIRS, 2 * HEAD_DIM + SUM_ROWS, 2 * TQ), F32),
            pltpu.VMEM((KEY_CHUNK, TQ), F32),
            pltpu.VMEM((ATT_W, TQ), F32),
            pltpu.VMEM((2 * N_IDX, TQ), F32),
            pltpu.VMEM((N_PAIRS, 2 * TQ), F32),
            pltpu.VMEM((N_PAIRS, 2 * TQ), F32),
            pltpu.VMEM((N_PAIRS, 2 * TQ), F32),
            pltpu.VMEM((1, TQ), F32),
            pltpu.VMEM((1, TQ), F32),
            pltpu.VMEM((1, TQ), F32),
            pltpu.VMEM((1, TQ), F32),
        ],
    )
    return pl.pallas_call(
        functools.partial(_attn_body, topk, idx_bits, has_far),
        grid_spec=grid_spec,
        out_shape=jax.ShapeDtypeStruct((s_count, nq, ATT_W), F32),
        compiler_params=pltpu.CompilerParams(dimension_semantics=("arbitrary", "arbitrary"),
                                             vmem_limit_bytes=VMEM_LIMIT),
        name="attn",
    )(*sched, qit, wt, qt, vis, ki, k, vt, bias)


def _decode_keys_body(ck_ref, cv_ref, cki_ref, k_ref, v_ref, ki_ref, ko_ref, vto_ref, kio_ref):
    past = ck_ref.shape[0]
    ts = k_ref.shape[0]
    pad = jnp.zeros((KEY_CHUNK - ts, ATT_W), F32)
    ko_ref[0:past, :] = ck_ref[...].astype(BF16)
    ko_ref[past:past + KEY_CHUNK, :] = jnp.concatenate([k_ref[...], pad], axis=0).astype(BF16)
    kio_ref[0:past, :] = cki_ref[...].astype(BF16)
    kio_ref[past:past + KEY_CHUNK, :] = jnp.concatenate([ki_ref[...], pad[:, 0:IDX_DIM]], axis=0).astype(BF16)
    for b in range(past // KEY_CHUNK):
        vto_ref[b] = cv_ref[b * KEY_CHUNK:(b + 1) * KEY_CHUNK, :].T.astype(BF16)
    vto_ref[past // KEY_CHUNK] = jnp.concatenate([v_ref[...], pad], axis=0).T.astype(BF16)


def _decode_keys(cache_k, cache_v, cache_ki, k_new, v_new, ki_new):
    ds, past, _ = cache_k.shape
    ts = k_new.shape[1]
    assert past % KEY_CHUNK == 0 and ts <= KEY_CHUNK
    nk = past + KEY_CHUNK
    per_stream = lambda rows, width: pl.BlockSpec((None, rows, width), lambda s: (s, 0, 0))
    return pl.pallas_call(
        _decode_keys_body,
        grid=(ds,),
        in_specs=[per_stream(past, ATT_W), per_stream(past, ATT_W), per_stream(past, IDX_DIM),
                  per_stream(ts, ATT_W), per_stream(ts, ATT_W), per_stream(ts, IDX_DIM)],
        out_specs=(per_stream(nk, ATT_W),
                   pl.BlockSpec((None, nk // KEY_CHUNK, ATT_W, KEY_CHUNK), lambda s: (s, 0, 0, 0)),
                   per_stream(nk, IDX_DIM)),
        out_shape=(jax.ShapeDtypeStruct((ds, nk, ATT_W), BF16),
                   jax.ShapeDtypeStruct((ds, nk // KEY_CHUNK, ATT_W, KEY_CHUNK), BF16),
                   jax.ShapeDtypeStruct((ds, nk, IDX_DIM), BF16)),
        compiler_params=pltpu.CompilerParams(dimension_semantics=("arbitrary",), vmem_limit_bytes=VMEM_LIMIT),
        name="decode_keys",
    )(cache_k, cache_v, cache_ki, k_new, v_new, ki_new)


def _s5_body(tt, us_ref, s0r_ref, s0i_ref, ar_ref, ai_ref, bbr_ref, bbi_ref, cr_ref, ci_ref, d_ref, wglu_ref,
             bglu_ref, out_ref, sr_out_ref, si_out_ref, str_ref, sti_ref, bre_ref, bim_ref, xr_ref, xi_ref):
    @pl.when(pl.program_id(1) == 0)
    def _():
        str_ref[...] = s0r_ref[...]
        sti_ref[...] = s0i_ref[...]

    u = us_ref[...]
    ub = u.astype(BF16)
    bre_ref[...] = jnp.dot(ub, bbr_ref[...], preferred_element_type=F32)
    bim_ref[...] = jnp.dot(ub, bbi_ref[...], preferred_element_type=F32)
    ar = ar_ref[...]
    ai = ai_ref[...]

    def step(t, carry):
        sr, si = carry
        nr = ar * sr - ai * si + bre_ref[pl.ds(t, 1), :]
        ni = ar * si + ai * sr + bim_ref[pl.ds(t, 1), :]
        xr_ref[pl.ds(t, 1), :] = nr
        xi_ref[pl.ds(t, 1), :] = ni
        return nr, ni

    sr, si = lax.fori_loop(0, tt, step, (str_ref[...], sti_ref[...]))
    str_ref[...] = sr
    sti_ref[...] = si
    sr_out_ref[...] = sr
    si_out_ref[...] = si
    y = (jnp.dot(xr_ref[...].astype(BF16), cr_ref[...], preferred_element_type=F32)
         - jnp.dot(xi_ref[...].astype(BF16), ci_ref[...], preferred_element_type=F32)
         + d_ref[...] * u)
    g = jax.nn.gelu(y)
    gate = jnp.dot(g.astype(BF16), wglu_ref[...], preferred_element_type=F32) + bglu_ref[...]
    out_ref[...] = g * jax.nn.sigmoid(gate)


def _s5(us, s0r, s0i, ar, ai, bbr, bbi, cr, ci, d, wglu, bglu, tt):
    s_count, t_len, _ = us.shape
    assert t_len % tt == 0
    state = pl.BlockSpec((None, 1, SSM_S), lambda s, t: (s, 0, 0))
    seq = pl.BlockSpec((None, tt, SSM_W), lambda s, t: (s, t, 0))
    return pl.pallas_call(
        functools.partial(_s5_body, tt),
        grid=(s_count, t_len // tt),
        in_specs=[seq, state, state, _const_spec((1, SSM_S)), _const_spec((1, SSM_S)),
                  _const_spec((SSM_W, SSM_S)), _const_spec((SSM_W, SSM_S)),
                  _const_spec((SSM_S, SSM_W)), _const_spec((SSM_S, SSM_W)),
                  _const_spec((1, SSM_W)), _const_spec((SSM_W, SSM_W)), _const_spec((1, SSM_W))],
        out_specs=(seq, state, state),
        out_shape=(jax.ShapeDtypeStruct((s_count, t_len, SSM_W), F32),
                   jax.ShapeDtypeStruct((s_count, 1, SSM_S), F32),
                   jax.ShapeDtypeStruct((s_count, 1, SSM_S), F32)),
        scratch_shapes=[pltpu.VMEM((1, SSM_S), F32), pltpu.VMEM((1, SSM_S), F32),
                        pltpu.VMEM((tt, SSM_S), F32), pltpu.VMEM((tt, SSM_S), F32),
                        pltpu.VMEM((tt, SSM_S), F32), pltpu.VMEM((tt, SSM_S), F32)],
        compiler_params=pltpu.CompilerParams(dimension_semantics=("arbitrary", "arbitrary"),
                                             vmem_limit_bytes=VMEM_LIMIT),
        name="s5",
    )(us, s0r, s0i, ar, ai, bbr, bbi, cr, ci, d, wglu, bglu)


def _mix_out_body(att_ref, ssm_ref, h_ref, woa_ref, wos_ref, gmix_ref, gpre_ref, wg_ref, wu_ref, wd_ref, gpost_ref,
                  o_ref):
    m = (jnp.dot(att_ref[...].astype(BF16), woa_ref[...], preferred_element_type=F32)
         + jnp.dot(ssm_ref[...].astype(BF16), wos_ref[...], preferred_element_type=F32))
    h2 = h_ref[...] + _rms(m, gmix_ref[...])
    y = _swiglu(_rms(h2, gpre_ref[...]), wg_ref, wu_ref, wd_ref)
    o_ref[...] = h2 + 0.5 * _rms(y, gpost_ref[...])


def _mix_out(att, ssm, h, woa, wos, gmix, gpre, wg, wu, wd, gpost):
    n = h.shape[0]
    tile = _row_tile(n)
    row = lambda width: pl.BlockSpec((tile, width), lambda i: (i, 0))
    return pl.pallas_call(
        _mix_out_body,
        grid=(pl.cdiv(n, tile),),
        in_specs=[row(ATT_W), row(SSM_W), row(D_MODEL), _const_spec((ATT_W, D_MODEL)),
                  _const_spec((SSM_W, D_MODEL)), _const_spec((1, D_MODEL)), _const_spec((1, D_MODEL)),
                  _const_spec((D_MODEL, D_FF)), _const_spec((D_MODEL, D_FF)), _const_spec((D_FF, D_MODEL)),
                  _const_spec((1, D_MODEL))],
        out_specs=row(D_MODEL),
        out_shape=jax.ShapeDtypeStruct((n, D_MODEL), F32),
        compiler_params=pltpu.CompilerParams(dimension_semantics=("arbitrary",), vmem_limit_bytes=VMEM_LIMIT),
        name="mix_out_ffn2",
    )(att, ssm, h, woa, wos, gmix, gpre, wg, wu, wd, gpost)


def _rel_bucket_np(rel):
    half = NUM_BUCKETS // 2
    max_exact = half // 2
    n = np.abs(rel).astype(np.int64)
    nf = np.maximum(n, 1).astype(np.float64)
    large = max_exact + (np.log(nf / max_exact) / math.log(MAX_DISTANCE / max_exact)
                         * (half - max_exact)).astype(np.int64)
    large = np.minimum(large, half - 1)
    return np.where(rel > 0, half, 0) + np.where(n < max_exact, n, large)


def _pair_bias(rel_bias, n_rows, c0):
    length = n_rows + TQ
    rel = np.arange(length) - (TQ - 1) - LANE - c0
    table = (rel_bias - rel_bias[NUM_BUCKETS // 2 - 1][None, :]) * LOG2E
    t1d = table[_rel_bucket_np(rel)].T
    x = jnp.tile(t1d, (1, TQ))[:, :TQ * (length - 1)].reshape(N_HEADS, TQ, length - 1)
    b = jnp.transpose(x[:, :, TQ - 1:TQ - 1 + n_rows], (0, 2, 1)).reshape(N_PAIRS, 2, n_rows, TQ)
    return jnp.concatenate([b[:, 0], b[:, 1]], axis=-1).astype(F32)


def _prompt_schedule(n, n_pad):
    pos = np.arange(n_pad)
    vis = np.where(pos < N_META, N_META, N_META + CHUNK * ((pos - N_META) // CHUNK + 1))
    vis = np.where(pos < n, np.minimum(vis, n), 0).astype(np.int32)
    n_tiles = -(-n // TQ)
    vmax = vis[:n_tiles * TQ].reshape(n_tiles, TQ).max(axis=1)
    nch = -(-vmax // KEY_CHUNK)
    nfull = vis[:n_tiles * TQ].reshape(n_tiles, TQ).min(axis=1) // KEY_CHUNK
    toff = PROMPT_BIAS_C0 // LANE - np.arange(n_tiles)
    assert ((nch - 1) * (KEY_CHUNK // LANE) + toff).max() <= (PROMPT_BIAS_ROWS - KEY_CHUNK) // LANE - 1
    sched = tuple(jnp.asarray(a, I32) for a in (nch, nfull, toff))
    has_far = bool((np.minimum(np.maximum(3 - toff, 0) // 4, nch) >= 2).any())
    return sched, jnp.asarray(vis[None, :]), has_far


def _sample_schedule(past, ts):
    nk = past + ts
    sched = tuple(jnp.asarray([v], I32) for v in (-(-nk // KEY_CHUNK), 0, 0))
    vis = jnp.asarray(np.where(np.arange(TQ) < ts, nk, 0)[None, :], I32)
    return sched, vis, False


def _s5_params(lam_re, lam_im, log_step, b_re, b_im, c_re, c_im, d_skip):
    dt = jnp.exp(log_step)[:, None]
    mag = jnp.exp(lam_re * dt)
    ab_re, ab_im = mag * jnp.cos(lam_im * dt), mag * jnp.sin(lam_im * dt)
    nr, ni = ab_re - 1.0, ab_im
    den = lam_re * lam_re + lam_im * lam_im
    f_re, f_im = (nr * lam_re + ni * lam_im) / den, (ni * lam_re - nr * lam_im) / den
    bb_re = f_re[..., None] * b_re - f_im[..., None] * b_im
    bb_im = f_re[..., None] * b_im + f_im[..., None] * b_re
    eye = jnp.eye(SSM_GROUPS, dtype=F32)
    bd_in = lambda w: jnp.einsum('gpc,gh->gchp', w, eye).reshape(SSM_W, SSM_S).astype(BF16)
    bd_out = lambda w: jnp.einsum('gcp,gh->gphc', w, eye).reshape(SSM_S, SSM_W).astype(BF16)
    return (ab_re.reshape(1, SSM_S), ab_im.reshape(1, SSM_S), bd_in(bb_re), bd_in(bb_im),
            bd_out(c_re), bd_out(c_im), d_skip.reshape(1, SSM_W))


def kernel(x_prompt, x_sample, cache_k, cache_v, cache_kidx, state_ssm_re, state_ssm_im, meta_tokens, rel_bias,
           ffn1_g_pre, ffn1_w_gate, ffn1_w_up, ffn1_w_down, ffn1_g_post, mix_g_pre, w_in, w_out, mix_g_post,
           lam_re, lam_im, log_step, b_re, b_im, c_re, c_im, d_skip, w_glu, b_glu,
           ffn2_g_pre, ffn2_w_gate, ffn2_w_up, ffn2_w_down, ffn2_g_post):
    depth = ffn1_g_pre.shape[0]
    assert depth == 1
    bp, seq, _ = x_prompt.shape
    assert bp == 1
    ds, ts, _ = x_sample.shape
    past = cache_k.shape[2]
    n_p = N_META + seq
    n_p_pad = pl.cdiv(n_p, ROW_TILE) * ROW_TILE
    n_s = ds * ts
    assert n_s % ROW_TILE == 0 and ts <= TQ
    nk_s_pad = pl.cdiv(past + ts, KEY_CHUNK) * KEY_CHUNK
    l = 0

    row2 = lambda g: g[l].reshape(1, -1)
    w1 = (row2(ffn1_g_pre), ffn1_w_gate[l].astype(BF16), ffn1_w_up[l].astype(BF16), ffn1_w_down[l].astype(BF16),
          row2(ffn1_g_post))
    w2 = (row2(ffn2_g_pre), ffn2_w_gate[l].astype(BF16), ffn2_w_up[l].astype(BF16), ffn2_w_down[l].astype(BF16),
          row2(ffn2_g_post))
    wi = w_in[l]
    o_q, o_k, o_v, o_qi, o_ki, o_wi, o_us = np.cumsum((0, ATT_W, ATT_W, ATT_W, N_IDX * IDX_DIM, IDX_DIM, N_IDX))
    sl = lambda o, w: wi[:, o:o + w]
    wrow = jnp.concatenate([sl(o_k, ATT_W), sl(o_v, ATT_W), sl(o_us, SSM_W), sl(o_ki, IDX_DIM),
                            jnp.zeros((D_MODEL, ROW_COLS_PAD - ROW_COLS), F32)], axis=1).astype(BF16)
    wcol = jnp.concatenate([sl(o_q, ATT_W), sl(o_qi, ATT_W), sl(o_v, ATT_W), sl(o_wi, N_IDX)], axis=1).T.astype(BF16)
    woa, wos = w_out[l][:ATT_W].astype(BF16), w_out[l][ATT_W:].astype(BF16)
    s5w = _s5_params(lam_re[l], lam_im[l], log_step[l], b_re[l], b_im[l], c_re[l], c_im[l], d_skip[l])
    s5w = s5w + (w_glu[l].astype(BF16), b_glu[l].reshape(1, SSM_W))
    bias_tab = rel_bias.astype(F32)

    xp = jnp.concatenate([meta_tokens.astype(F32), x_prompt[0]], axis=0)
    hp = _ffn(xp, *w1)
    kp, vp, kip, usp, kbp, kibp, vtbp, qtp, qitp, wtp = _mix_in(hp, row2(mix_g_pre), wrow, wcol, n_p_pad)
    sched_p, vis_p, far_p = _prompt_schedule(n_p, n_p_pad)
    att_p = _attention(qitp[None], wtp[None], qtp[None], vis_p, kibp[None], kbp[None], vtbp[None],
                       _pair_bias(bias_tab, PROMPT_BIAS_ROWS, PROMPT_BIAS_C0), sched_p,
                       min(TOPK_MAX, seq // 4), True, far_p)[0]
    zero_state = jnp.zeros((1, 1, SSM_S), F32)
    tt_p = max(t for t in range(8, 513, 8) if n_p % t == 0)
    ssm_p, srp, sip = _s5(usp[None], zero_state, zero_state, *s5w, tt_p)
    yp = _mix_out(att_p, ssm_p[0], hp, woa, wos, row2(mix_g_post), *w2)

    hs = _ffn(x_sample.reshape(n_s, D_MODEL), *w1)
    ks, vs, kis, uss, _, _, _, qts, qits, wts = _mix_in(hs, row2(mix_g_pre), wrow, wcol, n_s)
    k_all, vt_all, ki_all = _decode_keys(
        cache_k[l].reshape(ds, past, ATT_W), cache_v[l].reshape(ds, past, ATT_W), cache_kidx[l],
        ks.reshape(ds, ts, ATT_W), vs.reshape(ds, ts, ATT_W), kis.reshape(ds, ts, IDX_DIM))
    assert k_all.shape[1] == nk_s_pad
    lanes = lambda a: jnp.pad(jnp.transpose(a.reshape(a.shape[0], ds, ts), (1, 0, 2)), ((0, 0), (0, 0), (0, TQ - ts)))
    sched_s, vis_s, far_s = _sample_schedule(past, ts)
    att_s = _attention(lanes(qits), lanes(wts), lanes(qts), vis_s, ki_all, k_all, vt_all,
                       _pair_bias(bias_tab, LANE + nk_s_pad, past), sched_s,
                       min(TOPK_MAX, (past + ts) // 4), False, far_s)
    att_s = att_s[:, :ts].reshape(n_s, ATT_W)
    ssm_s, srs, sis = _s5(uss.reshape(ds, ts, SSM_W), state_ssm_re[l].reshape(ds, 1, SSM_S),
                          state_ssm_im[l].reshape(ds, 1, SSM_S), *s5w, ts)
    ys = _mix_out(att_s, ssm_s.reshape(n_s, SSM_W), hs, woa, wos, row2(mix_g_post), *w2)

    heads = lambda a, b, t: a.reshape(1, b, t, N_HEADS, HEAD_DIM)
    state = lambda a, b: a.reshape(1, b, SSM_GROUPS, SSM_P)
    return (yp[N_META:][None], ys.reshape(ds, ts, D_MODEL),
            heads(kp, 1, n_p), heads(vp, 1, n_p), kip.reshape(1, 1, n_p, IDX_DIM), state(srp, 1), state(sip, 1),
            heads(ks, ds, ts), heads(vs, ds, ts), kis.reshape(1, ds, ts, IDX_DIM), state(srs, ds), state(sis, ds))
```

```python
import functools
import math

import numpy as np
import jax
import jax.numpy as jnp
from jax import lax
from jax.experimental import pallas as pl
from jax.experimental.pallas import tpu as pltpu

F32, BF16, I32 = jnp.float32, jnp.bfloat16, jnp.int32

D_MODEL = 1024
CHUNK = 64
N_META = 16
N_HEADS = 8
HEAD_DIM = 64
ATT_W = N_HEADS * HEAD_DIM
N_IDX = 8
IDX_DIM = 64
TOPK_MAX = 256
SSM_GROUPS = 32
SSM_GC = 16
SSM_W = SSM_GROUPS * SSM_GC
SSM_P = 64
SSM_S = SSM_GROUPS * SSM_P
D_FF = 2816
NUM_BUCKETS = 32
MAX_DISTANCE = 128
EPS = 1e-6

LANE = 128
VMEM_LIMIT = 62 * 1024 * 1024
ROW_TILE = 512
TQ = LANE
KEY_CHUNK = 512
PROMPT_BIAS_C0 = 512
PROMPT_BIAS_ROWS = 1280
NEG = -1e30
LOG2E = math.log2(math.e)
F32_MAX = float(np.finfo(np.float32).max)
N_PAIRS = N_HEADS // 2
SUM_ROWS = 16
CAND_DEPTH = 16
CAND_BATCH = 8 * CAND_DEPTH
CAND_GROUPS = 8
SCORE_ROUND = 8
TIE_SCAN_MAX = 8


def _batcher_network(n):
    pairs, p = [], 1
    while p < n:
        k = p
        while k >= 1:
            for j in range(k % p, n - k, 2 * k):
                for i in range(min(k, n - j - k)):
                    if (i + j) // (2 * p) == (i + j + k) // (2 * p):
                        pairs.append((i + j, i + j + k))
            k //= 2
        p *= 2
    return pairs


_SORT_NETWORK = _batcher_network(CAND_DEPTH)


def _rms(x, g):
    return x * lax.rsqrt(jnp.mean(x * x, axis=-1, keepdims=True) + EPS) * g


def _swiglu(xn, wg_ref, wu_ref, wd_ref):
    xb = xn.astype(BF16)
    a = jnp.dot(xb, wg_ref[...], preferred_element_type=F32)
    b = jnp.dot(xb, wu_ref[...], preferred_element_type=F32)
    hidden = (a * jax.nn.sigmoid(a) * b).astype(BF16)
    return jnp.dot(hidden, wd_ref[...], preferred_element_type=F32)


def _const_spec(shape):
    nd = len(shape)
    return pl.BlockSpec(shape, lambda *_: (0,) * nd, pipeline_mode=pl.Buffered(1))


def _ffn_body(x_ref, gpre_ref, wg_ref, wu_ref, wd_ref, gpost_ref, o_ref):
    x = x_ref[...]
    y = _swiglu(_rms(x, gpre_ref[...]), wg_ref, wu_ref, wd_ref)
    o_ref[...] = x + 0.5 * _rms(y, gpost_ref[...])


def _row_tile(n):
    for t in range(ROW_TILE, 15, -16):
        if n % t == 0:
            return t
    return ROW_TILE


def _ffn(x, gpre, wg, wu, wd, gpost):
    n = x.shape[0]
    tile = _row_tile(n)
    row = pl.BlockSpec((tile, D_MODEL), lambda i: (i, 0))
    return pl.pallas_call(
        _ffn_body,
        grid=(pl.cdiv(n, tile),),
        in_specs=[row, _const_spec((1, D_MODEL)), _const_spec((D_MODEL, D_FF)), _const_spec((D_MODEL, D_FF)),
                  _const_spec((D_FF, D_MODEL)), _const_spec((1, D_MODEL))],
        out_specs=row,
        out_shape=jax.ShapeDtypeStruct((n, D_MODEL), F32),
        compiler_params=pltpu.CompilerParams(dimension_semantics=("arbitrary",), vmem_limit_bytes=VMEM_LIMIT),
        name="ffn1",
    )(x, gpre, wg, wu, wd, gpost)


ROW_COLS = 3 * ATT_W + IDX_DIM
ROW_COLS_PAD = 13 * LANE
COL_ROWS = 3 * ATT_W + N_IDX


def _mix_in_body(n_valid, h_ref, g_ref, wrow_ref, wcol_ref,
                 k_ref, v_ref, ki_ref, us_ref, kb_ref, kib_ref, vtb_ref, qt_ref, qit_ref, wt_ref):
    i = pl.program_id(0)
    u = _rms(h_ref[...], g_ref[...]).astype(BF16)
    z = jnp.dot(u, wrow_ref[...], preferred_element_type=F32)
    zt = lax.dot_general(wcol_ref[...], u, (((1,), (1,)), ((), ())), preferred_element_type=F32)
    k = z[:, 0:ATT_W]
    v = z[:, ATT_W:2 * ATT_W]
    ki = z[:, 3 * ATT_W:3 * ATT_W + IDX_DIM]
    k_ref[...] = k
    v_ref[...] = v
    us_ref[...] = z[:, 2 * ATT_W:3 * ATT_W]
    ki_ref[...] = ki
    row_ok = (i * ROW_TILE + lax.broadcasted_iota(I32, (ROW_TILE, 1), 0)) < n_valid
    col_ok = (i * ROW_TILE + lax.broadcasted_iota(I32, (1, ROW_TILE), 1)) < n_valid
    kb_ref[...] = jnp.where(row_ok, k, 0.0).astype(BF16)
    kib_ref[...] = jnp.where(row_ok, ki, 0.0).astype(BF16)
    zt = jnp.where(col_ok, zt, 0.0)
    qt_ref[...] = (zt[0:ATT_W] * (HEAD_DIM ** -0.5 * LOG2E)).astype(BF16)
    vtb_ref[...] = zt[2 * ATT_W:3 * ATT_W].astype(BF16)
    wt = zt[3 * ATT_W:3 * ATT_W + N_IDX] * ((IDX_DIM ** -0.5) * (N_IDX ** -0.5))
    wt_ref[...] = wt
    for h in range(N_IDX):
        qi_h = zt[ATT_W + h * IDX_DIM:ATT_W + (h + 1) * IDX_DIM]
        qit_ref[h * IDX_DIM:(h + 1) * IDX_DIM, :] = (qi_h * wt[h:h + 1, :]).astype(BF16)


def _mix_in(h, g, wrow, wcol, n_pad):
    n = h.shape[0]
    steps = n_pad // ROW_TILE
    assert steps == pl.cdiv(n, ROW_TILE)

    def row(width):
        return pl.BlockSpec((ROW_TILE, width), lambda i: (i, 0))

    col = lambda rows: pl.BlockSpec((rows, ROW_TILE), lambda i: (0, i))
    out_shape = (
        jax.ShapeDtypeStruct((n, ATT_W), F32), jax.ShapeDtypeStruct((n, ATT_W), F32),
        jax.ShapeDtypeStruct((n, IDX_DIM), F32), jax.ShapeDtypeStruct((n, SSM_W), F32),
        jax.ShapeDtypeStruct((n_pad, ATT_W), BF16), jax.ShapeDtypeStruct((n_pad, IDX_DIM), BF16),
        jax.ShapeDtypeStruct((n_pad // ROW_TILE, ATT_W, ROW_TILE), BF16),
        jax.ShapeDtypeStruct((ATT_W, n_pad), BF16), jax.ShapeDtypeStruct((ATT_W, n_pad), BF16),
        jax.ShapeDtypeStruct((N_IDX, n_pad), F32),
    )
    out_specs = (
        row(ATT_W), row(ATT_W), row(IDX_DIM), row(SSM_W), row(ATT_W), row(IDX_DIM),
        pl.BlockSpec((None, ATT_W, ROW_TILE), lambda i: (i, 0, 0)),
        col(ATT_W), col(ATT_W), col(N_IDX),
    )
    return pl.pallas_call(
        functools.partial(_mix_in_body, n),
        grid=(steps,),
        in_specs=[row(D_MODEL), _const_spec((1, D_MODEL)), _const_spec((D_MODEL, ROW_COLS_PAD)),
                  _const_spec((COL_ROWS, D_MODEL))],
        out_specs=out_specs,
        out_shape=out_shape,
        compiler_params=pltpu.CompilerParams(dimension_semantics=("arbitrary",), vmem_limit_bytes=VMEM_LIMIT),
        name="mix_in",
    )(h, g, wrow, wcol)


def _ordered_bits_to_f32(u):
    k = u ^ jnp.int32(-2 ** 31)
    b = k ^ ((k >> 31) & jnp.int32(0x7FFFFFFF))
    return lax.bitcast_convert_type(b, F32)


def _attn_body(topk, idx_bits, has_far,
               nch_ref, nfull_ref, toff_ref,
               qit_ref, wt_ref, qt_ref, vis_ref, ki_ref, k_ref, vt_ref, bias_ref,
               o_ref,
               st_ref, cand_ref, thr_ref, cgt_ref, cge_ref, clip_ref, qicat_ref, qbd_ref, sa_ref, sb_ref,
               bma_ref, bmb_ref,
               m_ref, acc_ref, cut_ref, rowf_ref, ot_ref):
    j = pl.program_id(1)
    nch = nch_ref[j]
    toff = toff_ref[j]
    kf = float(topk)
    bias_tile_max = (bias_ref.shape[1] - KEY_CHUNK) // LANE - 1

    for h in range(N_IDX):
        qicat_ref[:, h * TQ:(h + 1) * TQ] = qit_ref[h * IDX_DIM:(h + 1) * IDX_DIM, :]
    @pl.when(jnp.logical_and(pl.program_id(0) == 0, j == 0))
    def _():
        qbd_ref[...] = jnp.zeros(qbd_ref.shape, BF16)
        rowf_ref[...] = lax.broadcasted_iota(I32, (KEY_CHUNK, TQ), 0).astype(F32)

    for a in range(N_PAIRS):
        qbd_ref[a, 0:HEAD_DIM, 0:TQ] = qt_ref[2 * a * HEAD_DIM:(2 * a + 1) * HEAD_DIM, :]
        qbd_ref[a, HEAD_DIM:2 * HEAD_DIM, TQ:2 * TQ] = qt_ref[(2 * a + 1) * HEAD_DIM:(2 * a + 2) * HEAD_DIM, :]

    def chunk_start(c):
        return pl.multiple_of(c * KEY_CHUNK, KEY_CHUNK)

    def chunk_rows(r0):
        return r0 + lax.broadcasted_iota(I32, (KEY_CHUNK, TQ), 0)

    vis = vis_ref[...]
    cand_ref[...] = jnp.full(cand_ref.shape, -jnp.inf, F32)
    w_pos = wt_ref[...] >= 0.0
    clip_ref[0:N_IDX, :] = jnp.where(w_pos, 0.0, -jnp.inf)
    clip_ref[N_IDX:2 * N_IDX, :] = jnp.where(w_pos, jnp.inf, 0.0)

    last_batch_row = ki_ref.shape[0] - CAND_BATCH

    def score_round(masked, i, carry):
        for g in range(SCORE_ROUND):
            r0 = pl.multiple_of((i * SCORE_ROUND + g) * CAND_BATCH, CAND_BATCH)
            rel = jnp.dot(ki_ref[pl.ds(jnp.minimum(r0, last_batch_row), CAND_BATCH), :], qicat_ref[...],
                          preferred_element_type=F32)
            score = jnp.zeros((CAND_BATCH, TQ), F32)
            for h in range(N_IDX):
                score = score + jnp.clip(rel[:, h * TQ:(h + 1) * TQ], clip_ref[h:h + 1, :],
                                         clip_ref[N_IDX + h:N_IDX + h + 1, :])
            if masked:
                score = jnp.where(r0 + lax.broadcasted_iota(I32, (CAND_BATCH, TQ), 0) < vis, score, -jnp.inf)
            st_ref[pl.ds(r0, CAND_BATCH), :] = score
        return carry

    round_chunks = SCORE_ROUND * CAND_BATCH // KEY_CHUNK
    full_rounds = nfull_ref[j] // round_chunks
    lax.fori_loop(0, full_rounds, functools.partial(score_round, False), 0)
    lax.fori_loop(full_rounds, (nch + round_chunks - 1) // round_chunks, functools.partial(score_round, True), 0)

    def merge_batch(b, carry):
        r0 = pl.multiple_of(b * CAND_BATCH, CAND_BATCH)
        x = [st_ref[pl.ds(r0 + 8 * r, 8), :] for r in range(CAND_DEPTH)]
        for lo, hi in _SORT_NETWORK:
            x[lo], x[hi] = jnp.maximum(x[lo], x[hi]), jnp.minimum(x[lo], x[hi])
        g0 = pl.multiple_of(lax.rem(b, CAND_GROUPS) * CAND_BATCH, CAND_BATCH)
        t = [jnp.maximum(cand_ref[pl.ds(g0 + 8 * r, 8), :], x[CAND_DEPTH - 1 - r]) for r in range(CAND_DEPTH)]
        d = CAND_DEPTH // 2
        while d >= 1:
            for r in range(CAND_DEPTH):
                if r & d == 0:
                    t[r], t[r + d] = jnp.maximum(t[r], t[r + d]), jnp.minimum(t[r], t[r + d])
            d //= 2
        for r in range(CAND_DEPTH):
            cand_ref[pl.ds(g0 + 8 * r, 8), :] = t[r]
        return carry

    lax.fori_loop(0, nch * (KEY_CHUNK // CAND_BATCH), merge_batch, 0)

    def reduce_rows(ref, n_chunks, fn, combine, init):
        reducer = jnp.sum if combine is jnp.add else jnp.max

        def body(c, acc):
            r0 = chunk_start(c)
            val = fn(ref[pl.ds(r0, KEY_CHUNK), :], r0)
            part = reducer(val.reshape(8, KEY_CHUNK // 64, 8, TQ), axis=1)
            return combine(acc, reducer(part, axis=0))

        acc = lax.fori_loop(0, n_chunks, body, jnp.full((8, TQ), init, F32), unroll=isinstance(n_chunks, int))
        return reducer(acc, axis=0, keepdims=True)

    def count(ref, n_chunks, indicator):
        return reduce_rows(ref, n_chunks, indicator, jnp.add, 0.0)

    def select(ref, n_chunks):
        def value_bit(i, res):
            trial = res | lax.shift_left(jnp.int32(1), 31 - i)
            thr_t = _ordered_bits_to_f32(trial)
            cnt = count(ref, n_chunks, lambda blk, r0: jnp.where(blk >= thr_t, 1.0, 0.0))
            return jnp.where(cnt >= kf, trial, res)

        res = lax.fori_loop(0, 32, value_bit, jnp.zeros((1, TQ), I32))
        res = jnp.where((res >> 23) == 0, jnp.int32(0x00800000), res)
        lo = _ordered_bits_to_f32(res)
        above = _ordered_bits_to_f32(res + 1)
        c_above = count(ref, n_chunks, lambda blk, r0: jnp.where(blk >= above, 1.0, 0.0))
        c_lo = count(ref, n_chunks, lambda blk, r0: jnp.where(blk >= lo, 1.0, 0.0))
        thr_ref[...] = lo
        cgt_ref[...] = c_above
        cge_ref[...] = c_lo

        def walking(state):
            return jnp.max(state[0]) > 0.0

        def walk(state):
            active, hi, c_hi = state
            v = reduce_rows(ref, n_chunks, lambda blk, r0: jnp.where(blk < hi, blk, -jnp.inf), jnp.maximum,
                            -jnp.inf)
            c_v = count(ref, n_chunks, lambda blk, r0: jnp.where(blk >= v, 1.0, 0.0))
            done = jnp.logical_and(active > 0.0, c_v >= kf)
            thr_ref[...] = jnp.where(done, v, thr_ref[...])
            cgt_ref[...] = jnp.where(done, c_hi, cgt_ref[...])
            cge_ref[...] = jnp.where(done, c_v, cge_ref[...])
            return jnp.where(done, 0.0, active), v, c_v

        surplus = jnp.where(jnp.logical_and(c_lo > kf, c_above < kf), 1.0, 0.0)
        lax.while_loop(walking, walk, (surplus, above, c_above))
        return lo

    cand_lo = select(cand_ref, cand_ref.shape[0] // KEY_CHUNK)
    kept_min = cand_ref[CAND_BATCH - 8:CAND_BATCH, :]
    for g in range(1, CAND_GROUPS):
        kept_min = jnp.maximum(kept_min, cand_ref[(g + 1) * CAND_BATCH - 8:(g + 1) * CAND_BATCH, :])
    unsafe = jnp.max(kept_min, axis=0, keepdims=True) >= cand_lo

    @pl.when(jnp.max(jnp.where(unsafe, 1.0, 0.0)) > 0.0)
    def _():
        select(st_ref, nch)

    thr = thr_ref[...]

    need = kf - cgt_ref[...]
    excess = cge_ref[...] - kf
    cut_ref[...] = jnp.full((1, TQ), F32_MAX, F32)

    def chunk_rows_f(r0):
        return rowf_ref[...] + r0.astype(F32)

    tie_need = jnp.max(jnp.where(excess > 0.0, need, 0.0)).astype(I32)

    @pl.when(jnp.logical_and(tie_need > 0, tie_need <= TIE_SCAN_MAX))
    def _():
        def next_tied(cut_f):
            def body(c, acc):
                r0 = chunk_start(c)
                tied_row = jnp.where(st_ref[pl.ds(r0, KEY_CHUNK), :] == thr, chunk_rows_f(r0), F32_MAX)
                v = jnp.where(tied_row > cut_f, tied_row, F32_MAX)
                part = jnp.min(v.reshape(8, KEY_CHUNK // 64, 8, TQ), axis=1)
                return jnp.minimum(acc, jnp.min(part, axis=0))

            acc = lax.fori_loop(0, nch, body, jnp.full((8, TQ), F32_MAX, F32))
            return jnp.min(acc, axis=0, keepdims=True)

        def take(t, cut_f):
            return jnp.where(t.astype(F32) < need, next_tied(cut_f), cut_f)

        cut_f = lax.fori_loop(0, tie_need, take, jnp.full((1, TQ), -1.0, F32))
        cut_ref[...] = jnp.where(excess > 0.0, cut_f, F32_MAX)

    @pl.when(tie_need > TIE_SCAN_MAX)
    def _():
        def index_bit(i, cut):
            trial = cut | lax.shift_left(jnp.int32(1), idx_bits - 1 - i)
            cnt = count(st_ref, nch, lambda blk, r0: jnp.where(
                blk == thr, jnp.where(chunk_rows(r0) < trial, 1.0, 0.0), 0.0))
            return jnp.where(cnt < need, trial, cut)

        cut = lax.fori_loop(0, idx_bits, index_bit, jnp.zeros((1, TQ), I32))
        cut_ref[...] = jnp.where(excess > 0.0, cut.astype(F32), F32_MAX)

    cut_f = cut_ref[...]

    st_ref[pl.ds(chunk_start(nch), KEY_CHUNK), :] = jnp.full((KEY_CHUNK, TQ), -jnp.inf, F32)

    m_ref[...] = jnp.full(m_ref.shape, NEG, F32)
    acc_ref[...] = jnp.zeros(acc_ref.shape, F32)
    last = nch - 1
    ones_rows = jnp.ones((SUM_ROWS, KEY_CHUNK), BF16)

    def logits(blk, s_ref, bmax_ref, with_bias):
        kblk = k_ref[pl.ds(chunk_start(jnp.minimum(blk, last)), KEY_CHUNK), :]
        r0 = chunk_start(jnp.minimum(blk, nch))
        score = st_ref[pl.ds(r0, KEY_CHUNK), :]
        tie = jnp.where(chunk_rows_f(r0) <= cut_f, 0.0, NEG)
        msk = jnp.where(score > thr, 0.0, jnp.where(score == thr, tie, NEG))
        msk2 = jnp.concatenate([msk, msk], axis=1)
        bias_row0 = pl.multiple_of((jnp.clip(4 * blk + toff, -1, bias_tile_max) + 1) * LANE, LANE)
        for a in range(N_PAIRS):
            s2 = jnp.dot(kblk[:, a * 2 * HEAD_DIM:(a + 1) * 2 * HEAD_DIM], qbd_ref[a],
                         preferred_element_type=F32) + msk2
            if with_bias:
                s2 = s2 + bias_ref[a, pl.ds(bias_row0, KEY_CHUNK), :]
            s_ref[a] = s2
            bmax_ref[a:a + 1, :] = jnp.max(s2, axis=0, keepdims=True)

    def softmax_pv(blk, s_ref, bmax_ref):
        vblk = jnp.minimum(blk, last)
        for a in range(N_PAIRS):
            m_old = m_ref[a:a + 1, :]
            m_new = jnp.maximum(m_old, bmax_ref[a:a + 1, :])
            alpha = jnp.exp2(m_old - m_new)
            p = jnp.exp2(s_ref[a] - m_new).astype(BF16)
            v_ones = jnp.concatenate([vt_ref[vblk, a * 2 * HEAD_DIM:(a + 1) * 2 * HEAD_DIM, :], ones_rows], axis=0)
            acc_ref[a] = alpha * acc_ref[a] + jnp.dot(v_ones, p, preferred_element_type=F32)
            m_ref[a:a + 1, :] = m_new

    def block_pair(with_bias, i, carry):
        logits(2 * i + 1, sb_ref, bmb_ref, with_bias)
        softmax_pv(2 * i, sa_ref, bma_ref)
        logits(2 * i + 2, sa_ref, bma_ref, with_bias)
        softmax_pv(2 * i + 1, sb_ref, bmb_ref)
        return carry

    if has_far:
        far_pairs = jnp.minimum(jnp.maximum(3 - toff, 0) // 4, nch) // 2
        logits(0, sa_ref, bma_ref, False)
        lax.fori_loop(0, far_pairs, functools.partial(block_pair, False), 0)
    else:
        far_pairs = 0
    logits(2 * far_pairs, sa_ref, bma_ref, True)
    lax.fori_loop(far_pairs, nch // 2, functools.partial(block_pair, True), 0)

    @pl.when(nch % 2 == 1)
    def _():
        softmax_pv(last, sa_ref, bma_ref)

    for a in range(N_PAIRS):
        for e in range(2):
            h = 2 * a + e
            num = acc_ref[a, e * HEAD_DIM:(e + 1) * HEAD_DIM, e * TQ:(e + 1) * TQ]
            den = acc_ref[a, 2 * HEAD_DIM:2 * HEAD_DIM + 1, e * TQ:(e + 1) * TQ]
            ot_ref[h * HEAD_DIM:(h + 1) * HEAD_DIM, :] = num / den
    o_ref[...] = ot_ref[...].T


def _attention(qit, wt, qt, vis, ki, k, vt, bias, sched, topk, resident, has_far):
    s_count, _, nq = qit.shape
    nk = k.shape[1]
    n_tiles = sched[0].shape[0]
    mode = dict(pipeline_mode=pl.Buffered(1)) if resident else {}
    idx_bits = max(1, int(nk - 1).bit_length())
    logit_buf = pltpu.VMEM((N_PAIRS, KEY_CHUNK, 2 * TQ), F32)
    in_specs = [
        pl.BlockSpec((None, ATT_W, TQ), lambda s, j, *_: (s, 0, j)),
        pl.BlockSpec((None, N_IDX, TQ), lambda s, j, *_: (s, 0, j)),
        pl.BlockSpec((None, ATT_W, TQ), lambda s, j, *_: (s, 0, j)),
        pl.BlockSpec((1, TQ), lambda s, j, *_: (0, j)),
        pl.BlockSpec((None, nk, IDX_DIM), lambda s, j, *_: (s, 0, 0), **mode),
        pl.BlockSpec((None, nk, ATT_W), lambda s, j, *_: (s, 0, 0), **mode),
        pl.BlockSpec((None, nk // KEY_CHUNK, ATT_W, KEY_CHUNK), lambda s, j, *_: (s, 0, 0, 0), **mode),
        pl.BlockSpec(bias.shape, lambda s, j, *_: (0, 0, 0), pipeline_mode=pl.Buffered(1)),
    ]
    grid_spec = pltpu.PrefetchScalarGridSpec(
        num_scalar_prefetch=3,
        grid=(s_count, n_tiles),
        in_specs=in_specs,
        out_specs=pl.BlockSpec((None, TQ, ATT_W), lambda s, j, *_: (s, j, 0)),
        scratch_shapes=[
            pltpu.VMEM((nk + KEY_CHUNK, TQ), F32),
            pltpu.VMEM((CAND_GROUPS * CAND_BATCH, TQ), F32),
            pltpu.VMEM((1, TQ), F32),
            pltpu.VMEM((1, TQ), F32),
            pltpu.VMEM((1, TQ), F32),
            pltpu.VMEM((2 * N_IDX, TQ), F32),
            pltpu.VMEM((IDX_DIM, N_IDX * TQ), BF16),
            pltpu.VMEM((N_PAIRS, 2 * HEAD_DIM, 2 * TQ), BF16),
            logit_buf, logit_buf,
            pltpu.VMEM((N_PAIRS, 2 * TQ), F32),
            pltpu.VMEM((N_PAIRS, 2 * TQ), F32),
            pltpu.VMEM((N_PAIRS, 2 * TQ), F32),
            pltpu.VMEM((N_PAIRS, 2 * HEAD_DIM + SUM_ROWS, 2 * TQ), F32),
            pltpu.VMEM((1, TQ), F32),
            pltpu.VMEM((KEY_CHUNK, TQ), F32),
            pltpu.VMEM((ATT_W, TQ), F32),
        ],
    )
    return pl.pallas_call(
        functools.partial(_attn_body, topk, idx_bits, has_far),
        grid_spec=grid_spec,
        out_shape=jax.ShapeDtypeStruct((s_count, nq, ATT_W), F32),
        compiler_params=pltpu.CompilerParams(dimension_semantics=("arbitrary", "arbitrary"),
                                             vmem_limit_bytes=VMEM_LIMIT),
        name="attn",
    )(*sched, qit, wt, qt, vis, ki, k, vt, bias)


def _decode_keys_body(ck_ref, cv_ref, cki_ref, k_ref, v_ref, ki_ref, ko_ref, vto_ref, kio_ref):
    past = ck_ref.shape[0]
    ts = k_ref.shape[0]
    pad = jnp.zeros((KEY_CHUNK - ts, ATT_W), F32)
    ko_ref[0:past, :] = ck_ref[...].astype(BF16)
    ko_ref[past:past + KEY_CHUNK, :] = jnp.concatenate([k_ref[...], pad], axis=0).astype(BF16)
    kio_ref[0:past, :] = cki_ref[...].astype(BF16)
    kio_ref[past:past + KEY_CHUNK, :] = jnp.concatenate([ki_ref[...], pad[:, 0:IDX_DIM]], axis=0).astype(BF16)
    for b in range(past // KEY_CHUNK):
        vto_ref[b] = cv_ref[b * KEY_CHUNK:(b + 1) * KEY_CHUNK, :].T.astype(BF16)
    vto_ref[past // KEY_CHUNK] = jnp.concatenate([v_ref[...], pad], axis=0).T.astype(BF16)


def _decode_keys(cache_k, cache_v, cache_ki, k_new, v_new, ki_new):
    ds, past, _ = cache_k.shape
    ts = k_new.shape[1]
    assert past % KEY_CHUNK == 0 and ts <= KEY_CHUNK
    nk = past + KEY_CHUNK
    per_stream = lambda rows, width: pl.BlockSpec((None, rows, width), lambda s: (s, 0, 0))
    return pl.pallas_call(
        _decode_keys_body,
        grid=(ds,),
        in_specs=[per_stream(past, ATT_W), per_stream(past, ATT_W), per_stream(past, IDX_DIM),
                  per_stream(ts, ATT_W), per_stream(ts, ATT_W), per_stream(ts, IDX_DIM)],
        out_specs=(per_stream(nk, ATT_W),
                   pl.BlockSpec((None, nk // KEY_CHUNK, ATT_W, KEY_CHUNK), lambda s: (s, 0, 0, 0)),
                   per_stream(nk, IDX_DIM)),
        out_shape=(jax.ShapeDtypeStruct((ds, nk, ATT_W), BF16),
                   jax.ShapeDtypeStruct((ds, nk // KEY_CHUNK, ATT_W, KEY_CHUNK), BF16),
                   jax.ShapeDtypeStruct((ds, nk, IDX_DIM), BF16)),
        compiler_params=pltpu.CompilerParams(dimension_semantics=("arbitrary",), vmem_limit_bytes=VMEM_LIMIT),
        name="decode_keys",
    )(cache_k, cache_v, cache_ki, k_new, v_new, ki_new)


def _s5_body(tt, us_ref, s0r_ref, s0i_ref, ar_ref, ai_ref, bbr_ref, bbi_ref, cr_ref, ci_ref, d_ref, wglu_ref,
             bglu_ref, out_ref, sr_out_ref, si_out_ref, str_ref, sti_ref, bre_ref, bim_ref, xr_ref, xi_ref):
    @pl.when(pl.program_id(1) == 0)
    def _():
        str_ref[...] = s0r_ref[...]
        sti_ref[...] = s0i_ref[...]

    u = us_ref[...]
    ub = u.astype(BF16)
    bre_ref[...] = jnp.dot(ub, bbr_ref[...], preferred_element_type=F32)
    bim_ref[...] = jnp.dot(ub, bbi_ref[...], preferred_element_type=F32)
    ar = ar_ref[...]
    ai = ai_ref[...]

    def step(t, carry):
        sr, si = carry
        nr = ar * sr - ai * si + bre_ref[pl.ds(t, 1), :]
        ni = ar * si + ai * sr + bim_ref[pl.ds(t, 1), :]
        xr_ref[pl.ds(t, 1), :] = nr
        xi_ref[pl.ds(t, 1), :] = ni
        return nr, ni

    sr, si = lax.fori_loop(0, tt, step, (str_ref[...], sti_ref[...]))
    str_ref[...] = sr
    sti_ref[...] = si
    sr_out_ref[...] = sr
    si_out_ref[...] = si
    y = (jnp.dot(xr_ref[...].astype(BF16), cr_ref[...], preferred_element_type=F32)
         - jnp.dot(xi_ref[...].astype(BF16), ci_ref[...], preferred_element_type=F32)
         + d_ref[...] * u)
    g = jax.nn.gelu(y)
    gate = jnp.dot(g.astype(BF16), wglu_ref[...], preferred_element_type=F32) + bglu_ref[...]
    out_ref[...] = g * jax.nn.sigmoid(gate)


def _s5(us, s0r, s0i, ar, ai, bbr, bbi, cr, ci, d, wglu, bglu, tt):
    s_count, t_len, _ = us.shape
    assert t_len % tt == 0
    state = pl.BlockSpec((None, 1, SSM_S), lambda s, t: (s, 0, 0))
    seq = pl.BlockSpec((None, tt, SSM_W), lambda s, t: (s, t, 0))
    return pl.pallas_call(
        functools.partial(_s5_body, tt),
        grid=(s_count, t_len // tt),
        in_specs=[seq, state, state, _const_spec((1, SSM_S)), _const_spec((1, SSM_S)),
                  _const_spec((SSM_W, SSM_S)), _const_spec((SSM_W, SSM_S)),
                  _const_spec((SSM_S, SSM_W)), _const_spec((SSM_S, SSM_W)),
                  _const_spec((1, SSM_W)), _const_spec((SSM_W, SSM_W)), _const_spec((1, SSM_W))],
        out_specs=(seq, state, state),
        out_shape=(jax.ShapeDtypeStruct((s_count, t_len, SSM_W), F32),
                   jax.ShapeDtypeStruct((s_count, 1, SSM_S), F32),
                   jax.ShapeDtypeStruct((s_count, 1, SSM_S), F32)),
        scratch_shapes=[pltpu.VMEM((1, SSM_S), F32), pltpu.VMEM((1, SSM_S), F32),
                        pltpu.VMEM((tt, SSM_S), F32), pltpu.VMEM((tt, SSM_S), F32),
                        pltpu.VMEM((tt, SSM_S), F32), pltpu.VMEM((tt, SSM_S), F32)],
        compiler_params=pltpu.CompilerParams(dimension_semantics=("arbitrary", "arbitrary"),
                                             vmem_limit_bytes=VMEM_LIMIT),
        name="s5",
    )(us, s0r, s0i, ar, ai, bbr, bbi, cr, ci, d, wglu, bglu)


def _mix_out_body(att_ref, ssm_ref, h_ref, woa_ref, wos_ref, gmix_ref, gpre_ref, wg_ref, wu_ref, wd_ref, gpost_ref,
                  o_ref):
    m = (jnp.dot(att_ref[...].astype(BF16), woa_ref[...], preferred_element_type=F32)
         + jnp.dot(ssm_ref[...].astype(BF16), wos_ref[...], preferred_element_type=F32))
    h2 = h_ref[...] + _rms(m, gmix_ref[...])
    y = _swiglu(_rms(h2, gpre_ref[...]), wg_ref, wu_ref, wd_ref)
    o_ref[...] = h2 + 0.5 * _rms(y, gpost_ref[...])


def _mix_out(att, ssm, h, woa, wos, gmix, gpre, wg, wu, wd, gpost):
    n = h.shape[0]
    tile = _row_tile(n)
    row = lambda width: pl.BlockSpec((tile, width), lambda i: (i, 0))
    return pl.pallas_call(
        _mix_out_body,
        grid=(pl.cdiv(n, tile),),
        in_specs=[row(ATT_W), row(SSM_W), row(D_MODEL), _const_spec((ATT_W, D_MODEL)),
                  _const_spec((SSM_W, D_MODEL)), _const_spec((1, D_MODEL)), _const_spec((1, D_MODEL)),
                  _const_spec((D_MODEL, D_FF)), _const_spec((D_MODEL, D_FF)), _const_spec((D_FF, D_MODEL)),
                  _const_spec((1, D_MODEL))],
        out_specs=row(D_MODEL),
        out_shape=jax.ShapeDtypeStruct((n, D_MODEL), F32),
        compiler_params=pltpu.CompilerParams(dimension_semantics=("arbitrary",), vmem_limit_bytes=VMEM_LIMIT),
        name="mix_out_ffn2",
    )(att, ssm, h, woa, wos, gmix, gpre, wg, wu, wd, gpost)


def _rel_bucket_np(rel):
    half = NUM_BUCKETS // 2
    max_exact = half // 2
    n = np.abs(rel).astype(np.int64)
    nf = np.maximum(n, 1).astype(np.float64)
    large = max_exact + (np.log(nf / max_exact) / math.log(MAX_DISTANCE / max_exact)
                         * (half - max_exact)).astype(np.int64)
    large = np.minimum(large, half - 1)
    return np.where(rel > 0, half, 0) + np.where(n < max_exact, n, large)


def _pair_bias(rel_bias, n_rows, c0):
    length = n_rows + TQ
    rel = np.arange(length) - (TQ - 1) - LANE - c0
    table = (rel_bias - rel_bias[NUM_BUCKETS // 2 - 1][None, :]) * LOG2E
    t1d = table[_rel_bucket_np(rel)].T
    x = jnp.tile(t1d, (1, TQ))[:, :TQ * (length - 1)].reshape(N_HEADS, TQ, length - 1)
    b = jnp.transpose(x[:, :, TQ - 1:TQ - 1 + n_rows], (0, 2, 1)).reshape(N_PAIRS, 2, n_rows, TQ)
    return jnp.concatenate([b[:, 0], b[:, 1]], axis=-1).astype(F32)


def _prompt_schedule(n, n_pad):
    pos = np.arange(n_pad)
    vis = np.where(pos < N_META, N_META, N_META + CHUNK * ((pos - N_META) // CHUNK + 1))
    vis = np.where(pos < n, np.minimum(vis, n), 0).astype(np.int32)
    n_tiles = -(-n // TQ)
    vmax = vis[:n_tiles * TQ].reshape(n_tiles, TQ).max(axis=1)
    nch = -(-vmax // KEY_CHUNK)
    nfull = vis[:n_tiles * TQ].reshape(n_tiles, TQ).min(axis=1) // KEY_CHUNK
    toff = PROMPT_BIAS_C0 // LANE - np.arange(n_tiles)
    assert ((nch - 1) * (KEY_CHUNK // LANE) + toff).max() <= (PROMPT_BIAS_ROWS - KEY_CHUNK) // LANE - 1
    sched = tuple(jnp.asarray(a, I32) for a in (nch, nfull, toff))
    has_far = bool((np.minimum(np.maximum(3 - toff, 0) // 4, nch) >= 2).any())
    return sched, jnp.asarray(vis[None, :]), has_far


def _sample_schedule(past, ts):
    nk = past + ts
    sched = tuple(jnp.asarray([v], I32) for v in (-(-nk // KEY_CHUNK), 0, 0))
    vis = jnp.asarray(np.where(np.arange(TQ) < ts, nk, 0)[None, :], I32)
    return sched, vis, False


def _s5_params(lam_re, lam_im, log_step, b_re, b_im, c_re, c_im, d_skip):
    dt = jnp.exp(log_step)[:, None]
    mag = jnp.exp(lam_re * dt)
    ab_re, ab_im = mag * jnp.cos(lam_im * dt), mag * jnp.sin(lam_im * dt)
    nr, ni = ab_re - 1.0, ab_im
    den = lam_re * lam_re + lam_im * lam_im
    f_re, f_im = (nr * lam_re + ni * lam_im) / den, (ni * lam_re - nr * lam_im) / den
    bb_re = f_re[..., None] * b_re - f_im[..., None] * b_im
    bb_im = f_re[..., None] * b_im + f_im[..., None] * b_re
    eye = jnp.eye(SSM_GROUPS, dtype=F32)
    bd_in = lambda w: jnp.einsum('gpc,gh->gchp', w, eye).reshape(SSM_W, SSM_S).astype(BF16)
    bd_out = lambda w: jnp.einsum('gcp,gh->gphc', w, eye).reshape(SSM_S, SSM_W).astype(BF16)
    return (ab_re.reshape(1, SSM_S), ab_im.reshape(1, SSM_S), bd_in(bb_re), bd_in(bb_im),
            bd_out(c_re), bd_out(c_im), d_skip.reshape(1, SSM_W))


def kernel(x_prompt, x_sample, cache_k, cache_v, cache_kidx, state_ssm_re, state_ssm_im, meta_tokens, rel_bias,
           ffn1_g_pre, ffn1_w_gate, ffn1_w_up, ffn1_w_down, ffn1_g_post, mix_g_pre, w_in, w_out, mix_g_post,
           lam_re, lam_im, log_step, b_re, b_im, c_re, c_im, d_skip, w_glu, b_glu,
           ffn2_g_pre, ffn2_w_gate, ffn2_w_up, ffn2_w_down, ffn2_g_post):
    depth = ffn1_g_pre.shape[0]
    assert depth == 1
    bp, seq, _ = x_prompt.shape
    assert bp == 1
    ds, ts, _ = x_sample.shape
    past = cache_k.shape[2]
    n_p = N_META + seq
    n_p_pad = pl.cdiv(n_p, ROW_TILE) * ROW_TILE
    n_s = ds * ts
    assert n_s % ROW_TILE == 0 and ts <= TQ
    nk_s_pad = pl.cdiv(past + ts, KEY_CHUNK) * KEY_CHUNK
    l = 0

    row2 = lambda g: g[l].reshape(1, -1)
    w1 = (row2(ffn1_g_pre), ffn1_w_gate[l].astype(BF16), ffn1_w_up[l].astype(BF16), ffn1_w_down[l].astype(BF16),
          row2(ffn1_g_post))
    w2 = (row2(ffn2_g_pre), ffn2_w_gate[l].astype(BF16), ffn2_w_up[l].astype(BF16), ffn2_w_down[l].astype(BF16),
          row2(ffn2_g_post))
    wi = w_in[l]
    o_q, o_k, o_v, o_qi, o_ki, o_wi, o_us = np.cumsum((0, ATT_W, ATT_W, ATT_W, N_IDX * IDX_DIM, IDX_DIM, N_IDX))
    sl = lambda o, w: wi[:, o:o + w]
    wrow = jnp.concatenate([sl(o_k, ATT_W), sl(o_v, ATT_W), sl(o_us, SSM_W), sl(o_ki, IDX_DIM),
                            jnp.zeros((D_MODEL, ROW_COLS_PAD - ROW_COLS), F32)], axis=1).astype(BF16)
    wcol = jnp.concatenate([sl(o_q, ATT_W), sl(o_qi, ATT_W), sl(o_v, ATT_W), sl(o_wi, N_IDX)], axis=1).T.astype(BF16)
    woa, wos = w_out[l][:ATT_W].astype(BF16), w_out[l][ATT_W:].astype(BF16)
    s5w = _s5_params(lam_re[l], lam_im[l], log_step[l], b_re[l], b_im[l], c_re[l], c_im[l], d_skip[l])
    s5w = s5w + (w_glu[l].astype(BF16), b_glu[l].reshape(1, SSM_W))
    bias_tab = rel_bias.astype(F32)

    xp = jnp.concatenate([meta_tokens.astype(F32), x_prompt[0]], axis=0)
    hp = _ffn(xp, *w1)
    kp, vp, kip, usp, kbp, kibp, vtbp, qtp, qitp, wtp = _mix_in(hp, row2(mix_g_pre), wrow, wcol, n_p_pad)
    sched_p, vis_p, far_p = _prompt_schedule(n_p, n_p_pad)
    att_p = _attention(qitp[None], wtp[None], qtp[None], vis_p, kibp[None], kbp[None], vtbp[None],
                       _pair_bias(bias_tab, PROMPT_BIAS_ROWS, PROMPT_BIAS_C0), sched_p,
                       min(TOPK_MAX, seq // 4), True, far_p)[0]
    zero_state = jnp.zeros((1, 1, SSM_S), F32)
    tt_p = max(t for t in range(8, 513, 8) if n_p % t == 0)
    ssm_p, srp, sip = _s5(usp[None], zero_state, zero_state, *s5w, tt_p)
    yp = _mix_out(att_p, ssm_p[0], hp, woa, wos, row2(mix_g_post), *w2)

    hs = _ffn(x_sample.reshape(n_s, D_MODEL), *w1)
    ks, vs, kis, uss, _, _, _, qts, qits, wts = _mix_in(hs, row2(mix_g_pre), wrow, wcol, n_s)
    k_all, vt_all, ki_all = _decode_keys(
        cache_k[l].reshape(ds, past, ATT_W), cache_v[l].reshape(ds, past, ATT_W), cache_kidx[l],
        ks.reshape(ds, ts, ATT_W), vs.reshape(ds, ts, ATT_W), kis.reshape(ds, ts, IDX_DIM))
    assert k_all.shape[1] == nk_s_pad
    lanes = lambda a: jnp.pad(jnp.transpose(a.reshape(a.shape[0], ds, ts), (1, 0, 2)), ((0, 0), (0, 0), (0, TQ - ts)))
    sched_s, vis_s, far_s = _sample_schedule(past, ts)
    att_s = _attention(lanes(qits), lanes(wts), lanes(qts), vis_s, ki_all, k_all, vt_all,
                       _pair_bias(bias_tab, LANE + nk_s_pad, past), sched_s,
                       min(TOPK_MAX, (past + ts) // 4), False, far_s)
    att_s = att_s[:, :ts].reshape(n_s, ATT_W)
    ssm_s, srs, sis = _s5(uss.reshape(ds, ts, SSM_W), state_ssm_re[l].reshape(ds, 1, SSM_S),
                          state_ssm_im[l].reshape(ds, 1, SSM_S), *s5w, ts)
    ys = _mix_out(att_s, ssm_s.reshape(n_s, SSM_W), hs, woa, wos, row2(mix_g_post), *w2)

    heads = lambda a, b, t: a.reshape(1, b, t, N_HEADS, HEAD_DIM)
    state = lambda a, b: a.reshape(1, b, SSM_GROUPS, SSM_P)
    return (yp[N_META:][None], ys.reshape(ds, ts, D_MODEL),
            heads(kp, 1, n_p), heads(vp, 1, n_p), kip.reshape(1, 1, n_p, IDX_DIM), state(srp, 1), state(sip, 1),
            heads(ks, ds, ts), heads(vs, ds, ts), kis.reshape(1, ds, ts, IDX_DIM), state(srs, ds), state(sis, ds))
```
